```python
import jax, jax.numpy as jnp
from jax import lax
import numpy as np

D_MODEL = 1024
BATCH = 8
SEQ = 2048
DEPTH = 2
DEC_BATCH = 128
DEC_SEQ = 8
PAST_LEN = 16384
PAGE_SIZE = 128

N_AB = (DEPTH + 1) // 2
N_C = DEPTH // 2
N_SUB = 3
D_MIX = D_MODEL
MLSTM_WIDTH = D_MIX // 2
MLSTM_HEADS = 4
MLSTM_HD = MLSTM_WIDTH // MLSTM_HEADS
MLSTM_CHUNK = 128
RG_WIDTH = D_MIX - MLSTM_WIDTH
RG_BLOCKS = 8
RG_BD = RG_WIDTH // RG_BLOCKS
RG_CONV = 4
RG_C = 8.0
D_IN_AB = 4 * MLSTM_WIDTH + 2 * MLSTM_HEADS + 2 * RG_WIDTH
AB_SPLITS = (MLSTM_WIDTH, 2 * MLSTM_WIDTH, 3 * MLSTM_WIDTH, 4 * MLSTM_WIDTH,
             4 * MLSTM_WIDTH + MLSTM_HEADS, 4 * MLSTM_WIDTH + 2 * MLSTM_HEADS,
             4 * MLSTM_WIDTH + 2 * MLSTM_HEADS + RG_WIDTH)
RWKV_HS = 64
RWKV_HEADS = D_MODEL // RWKV_HS
RWKV_DECAY_LORA = 64
RWKV_A_LORA = 64
RWKV_GATE_LORA = 128
RWKV_LN_EPS = 64e-5
D_FF = ((8 * D_MODEL // 3 + 127) // 128) * 128
ALPHA = (2.0 * DEPTH) ** 0.25
BETA = (8.0 * DEPTH) ** -0.25
LN_EPS = 1e-5
HEAD_NORM_EPS = 1e-6

kernel_name = 'hybrid_mlstm_rglru_rwkv7_decoder_step'


def layer_norm(x, g, b):
    xf = x.astype(jnp.float32)
    mu = jnp.mean(xf, axis=-1, keepdims=True)
    var = jnp.mean(jnp.square(xf - mu), axis=-1, keepdims=True)
    return ((xf - mu) * lax.rsqrt(var + LN_EPS) * g + b).astype(x.dtype)


def modulate(x, shift, scale):
    return x * (1.0 + scale[:, None, :]) + shift[:, None, :]


def post_norm_residual(x, y, gate, res_w, g, b):
    return layer_norm(ALPHA * x + res_w * (1.0 + gate[:, None, :]) * y, g, b)


def swiglu(h, w1, w3, w2):
    return (jax.nn.silu(h @ w1) * (h @ w3)) @ w2


def mlstm_chunk(carry, inp):
    C0, n0, m0 = carry
    q, k, v, ig, lf = inp
    L = q.shape[2]
    b = jnp.cumsum(lf, axis=-1)
    causal = jnp.tril(jnp.ones((L, L), dtype=bool))
    dmat = jnp.where(causal, b[..., :, None] - b[..., None, :] + ig[..., None, :], -jnp.inf)
    m_inter = b + m0[..., None]
    m = jnp.maximum(m_inter, jnp.max(dmat, axis=-1))
    w_inter = jnp.exp(m_inter - m)
    scores = jnp.einsum('bhtd,bhsd->bhts', q, k) * jnp.exp(dmat - m[..., None])
    num = jnp.einsum('bhts,bhsd->bhtd', scores, v) + w_inter[..., None] * jnp.einsum('bhvk,bhtk->bhtv', C0, q)
    den = jnp.sum(scores, axis=-1) + w_inter * jnp.einsum('bhk,bhtk->bht', n0, q)
    h = num / jnp.maximum(jnp.abs(den), jnp.exp(-m))[..., None]
    m_end = m[..., -1]
    w_state = jnp.exp(b[..., -1] + m0 - m_end)
    w_rows = jnp.exp(b[..., -1:] - b + ig - m_end[..., None])
    C_new = w_state[..., None, None] * C0 + jnp.einsum('bhs,bhsv,bhsk->bhvk', w_rows, v, k)
    n_new = w_state[..., None] * n0 + jnp.einsum('bhs,bhsk->bhk', w_rows, k)
    return (C_new, n_new, m_end), h


def mlstm(q, k, v, ig, lf, C0, n0, m0):
    B, S = q.shape[0], q.shape[1]
    L = MLSTM_CHUNK if S % MLSTM_CHUNK == 0 else S
    nc = S // L

    def chunks(t):
        t = t.astype(jnp.float32).reshape((B, nc, L) + t.shape[2:])
        return jnp.moveaxis(jnp.moveaxis(t, 3, 2), 1, 0)

    carry0 = (C0.astype(jnp.float32), n0.astype(jnp.float32), m0.astype(jnp.float32))
    (C1, n1, m1), h = lax.scan(mlstm_chunk, carry0,
                               (chunks(q), chunks(k), chunks(v), chunks(ig), chunks(lf)))
    h = jnp.swapaxes(jnp.moveaxis(h, 0, 1), 2, 3).reshape(B, S, MLSTM_HEADS, MLSTM_HD)
    return h, C1, n1, m1


def head_norm(h, g):
    mu = jnp.mean(h, axis=-1, keepdims=True)
    var = jnp.mean(jnp.square(h - mu), axis=-1, keepdims=True)
    hn = (h - mu) * lax.rsqrt(var + HEAD_NORM_EPS)
    return hn.reshape(h.shape[0], h.shape[1], -1) * g


def causal_dwconv(xr, buf, w, b):
    S = xr.shape[1]
    xp = jnp.concatenate([buf.astype(xr.dtype), xr], axis=1)
    out = sum(w[j] * xp[:, j:j + S] for j in range(RG_CONV)) + b
    return out, xp[:, S:].astype(jnp.float32)


def lin_combine(left, right):
    a1, b1 = left
    a2, b2 = right
    return a1 * a2, a2 * b1 + b2


def rg_lru(xc, h0, w_a, b_a, w_x, b_x, lam):
    B, S, W = xc.shape
    xb = xc.reshape(B, S, RG_BLOCKS, RG_BD)
    r = jax.nn.sigmoid(jnp.einsum('bsgi,gij->bsgj', xb, w_a).reshape(B, S, W) + b_a).astype(jnp.float32)
    i = jax.nn.sigmoid(jnp.einsum('bsgi,gij->bsgj', xb, w_x).reshape(B, S, W) + b_x).astype(jnp.float32)
    log_a = -RG_C * jax.nn.softplus(-lam.astype(jnp.float32)) * r
    a = jnp.exp(log_a)
    u = jnp.sqrt(-jnp.expm1(2.0 * log_a)) * i * xc.astype(jnp.float32)
    u = u.at[:, 0].add(a[:, 0] * h0.astype(jnp.float32))
    _, h = lax.associative_scan(lin_combine, (a, u), axis=1)
    return h, h[:, -1]


def ab_mixer(h, C0, n0, m0, rh0, buf0, w_in, b_gates, mnorm_g, conv_w, conv_b,
             w_a, b_a, w_x, b_x, lam, w_out):
    B, S, _ = h.shape
    z = h @ w_in
    q, k, v, o, ig, fg, xr, gr = jnp.split(z, AB_SPLITS, axis=-1)
    heads = lambda t: t.reshape(B, S, MLSTM_HEADS, MLSTM_HD)
    ig = ig.astype(jnp.float32) + b_gates[0]
    lf = jax.nn.log_sigmoid(fg.astype(jnp.float32) + b_gates[1])
    hm, C1, n1, m1 = mlstm(heads(q), heads(k) * (MLSTM_HD ** -0.5), heads(v), ig, lf, C0, n0, m0)
    hm = head_norm(hm, mnorm_g) * jax.nn.sigmoid(o.astype(jnp.float32))
    xc, buf1 = causal_dwconv(xr, buf0, conv_w, conv_b)
    hr, rh1 = rg_lru(xc, rh0, w_a, b_a, w_x, b_x, lam)
    hr = hr.astype(h.dtype) * jax.nn.gelu(gr)
    y = jnp.concatenate([hm.astype(h.dtype), hr], axis=-1) @ w_out
    return y, (C1, n1, m1, rh1, buf1)


def rwkv7_step(S_, inp):
    r_t, dec_t, k_t, v_t, kk_t, a_t = inp
    sa = jnp.einsum('bhvk,bhk->bhv', S_, -kk_t)
    S_ = (S_ * dec_t[:, :, None, :] + sa[..., None] * (kk_t * a_t)[:, :, None, :]
          + v_t[..., None] * k_t[:, :, None, :])
    return S_, jnp.einsum('bhvk,bhk->bhv', S_, r_t)


def rwkv7_mixer(h, wkv0, prev0, mu, wr, wk, wv, w0, w1, w2, a0, a1, a2, g1, g2,
                k_k, k_a, r_k, lnx_g, lnx_b, wo):
    B, S, D = h.shape
    f32 = jnp.float32
    h_prev = jnp.concatenate([prev0[:, None].astype(h.dtype), h[:, :-1]], axis=1)
    dx = h_prev - h
    xr, xw, xk, xv, xa, xg = [h + dx * mu[j] for j in range(6)]
    heads = lambda t: t.astype(f32).reshape(B, S, RWKV_HEADS, RWKV_HS)
    r = heads(xr @ wr)
    k = heads(xk @ wk)
    v = heads(xv @ wv)
    w_log = -jax.nn.softplus(-(w0 + jnp.tanh(xw @ w1) @ w2).astype(f32)) - 0.5
    decay = heads(jnp.exp(-jnp.exp(w_log)))
    a = heads(jax.nn.sigmoid(a0 + (xa @ a1) @ a2))
    g = jax.nn.sigmoid(xg @ g1) @ g2
    kk = k * k_k.astype(f32).reshape(RWKV_HEADS, RWKV_HS)
    kk = kk / jnp.maximum(jnp.sqrt(jnp.sum(kk * kk, axis=-1, keepdims=True)), 1e-12)
    k = k * (1.0 + (a - 1.0) * k_a.astype(f32).reshape(RWKV_HEADS, RWKV_HS))
    tm = lambda t: jnp.moveaxis(t, 1, 0)
    wkv1, y = lax.scan(rwkv7_step, wkv0.astype(f32), (tm(r), tm(decay), tm(k), tm(v), tm(kk), tm(a)))
    y = jnp.moveaxis(y, 0, 1)
    ym = jnp.mean(y, axis=-1, keepdims=True)
    yv = jnp.mean(jnp.square(y - ym), axis=-1, keepdims=True)
    y = ((y - ym) * lax.rsqrt(yv + RWKV_LN_EPS)).reshape(B, S, D) * lnx_g + lnx_b
    bonus = jnp.sum(r * k * r_k.astype(f32), axis=-1, keepdims=True) * v
    y = y + bonus.reshape(B, S, D)
    out = (y.astype(h.dtype) * g) @ wo
    return out, (wkv1, h[:, -1].astype(f32))


def run_trunk(x, c, states, p):
    mC, mn, mm, rh, rconv, wkv, shift = states
    B = x.shape[0]
    new_ab, new_c = [], []
    for layer in range(DEPTH):
        mod = (jax.nn.silu(c) @ p['ada_w'][layer] + p['ada_b'][layer]).reshape(B, 3 * N_SUB, D_MODEL)
        lg, lb = p['ln_g'][layer], p['ln_b'][layer]
        y = swiglu(modulate(x, mod[:, 0], mod[:, 1]),
                   p['ffn_w1'][layer, 0], p['ffn_w3'][layer, 0], p['ffn_w2'][layer, 0])
        x = post_norm_residual(x, y, mod[:, 2], 0.5, lg[0], lb[0])
        hin = modulate(x, mod[:, 3], mod[:, 4])
        j = layer // 2
        if layer % 2 == 0:
            y, st = ab_mixer(hin, mC[j], mn[j], mm[j], rh[j], rconv[j],
                             p['ab_w_in'][j], p['mlstm_b_gates'][j], p['mlstm_norm_g'][j],
                             p['rg_conv_w'][j], p['rg_conv_b'][j], p['rg_w_a'][j], p['rg_b_a'][j],
                             p['rg_w_x'][j], p['rg_b_x'][j], p['rg_lambda'][j], p['ab_w_out'][j])
            new_ab.append(st)
        else:
            y, st = rwkv7_mixer(hin, wkv[j], shift[j], p['rw_mu'][j], p['rw_wr'][j], p['rw_wk'][j],
                                p['rw_wv'][j], p['rw_w0'][j], p['rw_w1'][j], p['rw_w2'][j],
                                p['rw_a0'][j], p['rw_a1'][j], p['rw_a2'][j], p['rw_g1'][j], p['rw_g2'][j],
                                p['rw_k_k'][j], p['rw_k_a'][j], p['rw_r_k'][j],
                                p['rw_lnx_g'][j], p['rw_lnx_b'][j], p['rw_wo'][j])
            new_c.append(st)
        x = post_norm_residual(x, y, mod[:, 5], 1.0, lg[1], lb[1])
        y = swiglu(modulate(x, mod[:, 6], mod[:, 7]),
                   p['ffn_w1'][layer, 1], p['ffn_w3'][layer, 1], p['ffn_w2'][layer, 1])
        x = post_norm_residual(x, y, mod[:, 8], 0.5, lg[2], lb[2])
    stk = lambda sts, i: jnp.stack([s[i] for s in sts], axis=0)
    return x, (stk(new_ab, 0), stk(new_ab, 1), stk(new_ab, 2), stk(new_ab, 3), stk(new_ab, 4),
               stk(new_c, 0), stk(new_c, 1))


def zero_states(B):
    f = jnp.float32
    return (jnp.zeros((N_AB, B, MLSTM_HEADS, MLSTM_HD, MLSTM_HD), f),
            jnp.zeros((N_AB, B, MLSTM_HEADS, MLSTM_HD), f),
            jnp.zeros((N_AB, B, MLSTM_HEADS), f),
            jnp.zeros((N_AB, B, RG_WIDTH), f),
            jnp.zeros((N_AB, B, RG_CONV - 1, RG_WIDTH), f),
            jnp.zeros((N_C, B, RWKV_HEADS, RWKV_HS, RWKV_HS), f),
            jnp.zeros((N_C, B, D_MODEL), f))


def setup_inputs(seed: int = 0) -> dict:
    key = jax.random.key(seed)
    it = iter(jax.random.split(key, 64))
    f32 = jnp.float32

    def nrm(shape, scale):
        return jax.random.normal(next(it), shape, f32) * scale

    def unif(shape, lo, hi):
        return jax.random.uniform(next(it), shape, f32, lo, hi)

    D = D_MODEL
    inv = D ** -0.5
    d = {}
    d['x_prompt'] = nrm((BATCH, SEQ, D), 1.0)
    d['x_sample'] = nrm((DEC_BATCH, DEC_SEQ, D), 1.0)
    d['c_prompt'] = nrm((BATCH, D), 1.0)
    d['c_sample'] = nrm((DEC_BATCH, D), 1.0)
    d['state_mlstm_C'] = nrm((N_AB, DEC_BATCH, MLSTM_HEADS, MLSTM_HD, MLSTM_HD), 0.1)
    d['state_mlstm_n'] = nrm((N_AB, DEC_BATCH, MLSTM_HEADS, MLSTM_HD), 0.1)
    d['state_mlstm_m'] = unif((N_AB, DEC_BATCH, MLSTM_HEADS), 0.0, 3.0)
    d['state_rglru_h'] = nrm((N_AB, DEC_BATCH, RG_WIDTH), 0.5)
    d['state_rglru_conv'] = nrm((N_AB, DEC_BATCH, RG_CONV - 1, RG_WIDTH), 0.5)
    d['state_rwkv_wkv'] = nrm((N_C, DEC_BATCH, RWKV_HEADS, RWKV_HS, RWKV_HS), 0.1)
    d['state_rwkv_shift'] = nrm((N_C, DEC_BATCH, D), 1.0)
    d['ada_w'] = nrm((DEPTH, D, 3 * N_SUB * D), 0.5 * inv)
    d['ada_b'] = nrm((DEPTH, 3 * N_SUB * D), 0.02)
    d['ln_g'] = 1.0 + nrm((DEPTH, N_SUB, D), 0.05)
    d['ln_b'] = nrm((DEPTH, N_SUB, D), 0.02)
    d['ffn_w1'] = nrm((DEPTH, 2, D, D_FF), inv)
    d['ffn_w3'] = nrm((DEPTH, 2, D, D_FF), inv)
    d['ffn_w2'] = nrm((DEPTH, 2, D_FF, D), BETA * D_FF ** -0.5)
    d['ab_w_in'] = nrm((N_AB, D, D_IN_AB), inv)
    d['mlstm_b_gates'] = jnp.stack([nrm((N_AB, MLSTM_HEADS), 0.1),
                                    unif((N_AB, MLSTM_HEADS), 3.0, 6.0)], axis=1)
    d['mlstm_norm_g'] = 1.0 + nrm((N_AB, MLSTM_WIDTH), 0.05)
    d['rg_conv_w'] = nrm((N_AB, RG_CONV, RG_WIDTH), RG_CONV ** -0.5)
    d['rg_conv_b'] = nrm((N_AB, RG_WIDTH), 0.02)
    d['rg_w_a'] = nrm((N_AB, RG_BLOCKS, RG_BD, RG_BD), RG_BD ** -0.5)
    d['rg_b_a'] = nrm((N_AB, RG_WIDTH), 0.02)
    d['rg_w_x'] = nrm((N_AB, RG_BLOCKS, RG_BD, RG_BD), RG_BD ** -0.5)
    d['rg_b_x'] = nrm((N_AB, RG_WIDTH), 0.02)
    a_base = unif((N_AB, RG_WIDTH), 0.9, 0.999) ** (1.0 / RG_C)
    d['rg_lambda'] = jnp.log(a_base) - jnp.log1p(-a_base)
    d['ab_w_out'] = nrm((N_AB, D_MIX, D), BETA * D_MIX ** -0.5)
    d['rw_mu'] = unif((N_C, 6, D), 0.0, 1.0)
    d['rw_wr'] = nrm((N_C, D, D), inv)
    d['rw_wk'] = nrm((N_C, D, D), inv)
    d['rw_wv'] = nrm((N_C, D, D), inv)
    d['rw_w0'] = unif((N_C, D), -5.0, 1.0)
    d['rw_w1'] = nrm((N_C, D, RWKV_DECAY_LORA), inv)
    d['rw_w2'] = nrm((N_C, RWKV_DECAY_LORA, D), 0.1)
    d['rw_a0'] = nrm((N_C, D), 0.1)
    d['rw_a1'] = nrm((N_C, D, RWKV_A_LORA), inv)
    d['rw_a2'] = nrm((N_C, RWKV_A_LORA, D), 0.1)
    d['rw_g1'] = nrm((N_C, D, RWKV_GATE_LORA), inv)
    d['rw_g2'] = nrm((N_C, RWKV_GATE_LORA, D), RWKV_GATE_LORA ** -0.5)
    d['rw_k_k'] = 0.85 + nrm((N_C, D), 0.05)
    d['rw_k_a'] = 1.0 + nrm((N_C, D), 0.05)
    d['rw_r_k'] = nrm((N_C, RWKV_HEADS, RWKV_HS), 0.1)
    d['rw_lnx_g'] = 1.0 + nrm((N_C, D), 0.05)
    d['rw_lnx_b'] = nrm((N_C, D), 0.02)
    d['rw_wo'] = nrm((N_C, D, D), BETA * inv)
    return d


def reference(x_prompt, x_sample, c_prompt, c_sample,
              state_mlstm_C, state_mlstm_n, state_mlstm_m, state_rglru_h, state_rglru_conv,
              state_rwkv_wkv, state_rwkv_shift,
              ada_w, ada_b, ln_g, ln_b, ffn_w1, ffn_w3, ffn_w2,
              ab_w_in, mlstm_b_gates, mlstm_norm_g, rg_conv_w, rg_conv_b, rg_w_a, rg_b_a,
              rg_w_x, rg_b_x, rg_lambda, ab_w_out,
              rw_mu, rw_wr, rw_wk, rw_wv, rw_w0, rw_w1, rw_w2, rw_a0, rw_a1, rw_a2,
              rw_g1, rw_g2, rw_k_k, rw_k_a, rw_r_k, rw_lnx_g, rw_lnx_b, rw_wo):
    p = dict(ada_w=ada_w, ada_b=ada_b, ln_g=ln_g, ln_b=ln_b, ffn_w1=ffn_w1, ffn_w3=ffn_w3,
             ffn_w2=ffn_w2, ab_w_in=ab_w_in, mlstm_b_gates=mlstm_b_gates, mlstm_norm_g=mlstm_norm_g,
             rg_conv_w=rg_conv_w, rg_conv_b=rg_conv_b, rg_w_a=rg_w_a, rg_b_a=rg_b_a, rg_w_x=rg_w_x,
             rg_b_x=rg_b_x, rg_lambda=rg_lambda, ab_w_out=ab_w_out, rw_mu=rw_mu, rw_wr=rw_wr,
             rw_wk=rw_wk, rw_wv=rw_wv, rw_w0=rw_w0, rw_w1=rw_w1, rw_w2=rw_w2, rw_a0=rw_a0,
             rw_a1=rw_a1, rw_a2=rw_a2, rw_g1=rw_g1, rw_g2=rw_g2, rw_k_k=rw_k_k, rw_k_a=rw_k_a,
             rw_r_k=rw_r_k, rw_lnx_g=rw_lnx_g, rw_lnx_b=rw_lnx_b, rw_wo=rw_wo)
    y_prompt, sp = run_trunk(x_prompt, c_prompt, zero_states(x_prompt.shape[0]), p)
    y_sample, ss = run_trunk(x_sample, c_sample,
                             (state_mlstm_C, state_mlstm_n, state_mlstm_m, state_rglru_h,
                              state_rglru_conv, state_rwkv_wkv, state_rwkv_shift), p)
    p_mC, p_mn, p_mm, p_rh, p_rconv, p_wkv, p_shift = sp
    s_mC, s_mn, s_mm, s_rh, s_rconv, s_wkv, s_shift = ss
    return (y_prompt, y_sample,
            p_mC, p_mn, p_mm, p_rh, p_rconv, p_wkv, p_shift,
            s_mC, s_mn, s_mm, s_rh, s_rconv, s_wkv, s_shift)
```

```python
import functools

import jax
import jax.numpy as jnp
from jax import lax
from jax.experimental import pallas as pl
from jax.experimental.pallas import tpu as pltpu

D_MODEL = 1024
DEPTH = 2
N_SUB = 3
MLSTM_WIDTH = 512
MLSTM_HEADS = 4
MLSTM_HD = 128
MLSTM_CHUNK = 128
RG_WIDTH = 512
RG_BLOCKS = 8
RG_BD = 64
RG_CONV = 4
RG_C = 8.0
RWKV_HS = 64
RWKV_HEADS = 16
RWKV_LN_EPS = 64e-5
RWKV_CHUNK = 64
D_FF = 2816
ALPHA = (2.0 * DEPTH) ** 0.25
LN_EPS = 1e-5
HEAD_NORM_EPS = 1e-6

LANES = 128
SUBLANES = 8
VMEM_LIMIT_BYTES = 56 * 1024 * 1024
Z_COLS = 3200
LORA_PAD = 128

BF = jnp.bfloat16
F32 = jnp.float32

NT_DIMS = (((1,), (1,)), ((), ()))
TN_DIMS = (((0,), (0,)), ((), ()))


def _cparams(sem):
    return pltpu.CompilerParams(dimension_semantics=sem, vmem_limit_bytes=VMEM_LIMIT_BYTES)


def _dot(a, b):
    return jnp.dot(a, b, preferred_element_type=F32)


def _dot_nt(a, b):
    return lax.dot_general(a, b, NT_DIMS, preferred_element_type=F32)


def _dot_tn(a, b):
    return lax.dot_general(a, b, TN_DIMS, preferred_element_type=F32)


def _layer_norm(z, g, b):
    mu = jnp.mean(z, axis=-1, keepdims=True)
    zc = z - mu
    var = jnp.mean(zc * zc, axis=-1, keepdims=True)
    return zc * lax.rsqrt(var + LN_EPS) * g + b


def _row_tile(B, S, rows):
    if S >= rows:
        assert S % rows == 0
        return 1, rows
    bb = min(B, rows // S)
    assert B % bb == 0
    return bb, S


def _split2(x):
    hi = x.astype(BF)
    lo = (x - hi.astype(F32)).astype(BF)
    return hi, lo


def _split3(x):
    hi = x.astype(BF)
    r1 = x - hi.astype(F32)
    mid = r1.astype(BF)
    lo = (r1 - mid.astype(F32)).astype(BF)
    return hi, mid, lo


def _adaln_kernel(c_ref, w_ref, b_ref, o_ref):
    h = jax.nn.silu(c_ref[...]).astype(BF)
    o_ref[...] = _dot(h, w_ref[...].astype(BF)) + b_ref[...]


def adaln(c_all, ada_w, ada_b):
    Bc = c_all.shape[0]
    n_out = ada_w.shape[-1]
    tn = 1024
    return pl.pallas_call(
        _adaln_kernel,
        out_shape=jax.ShapeDtypeStruct((DEPTH, Bc, n_out), F32),
        grid=(DEPTH, n_out // tn),
        in_specs=[
            pl.BlockSpec((Bc, D_MODEL), lambda l, j: (0, 0)),
            pl.BlockSpec((None, D_MODEL, tn), lambda l, j: (l, 0, j)),
            pl.BlockSpec((None, 1, tn), lambda l, j: (l, 0, j)),
        ],
        out_specs=pl.BlockSpec((None, Bc, tn), lambda l, j: (l, 0, j)),
        compiler_params=_cparams(("parallel", "parallel")),
        name="adaln",
    )(c_all, ada_w, ada_b.reshape(DEPTH, 1, n_out))


def _mod_spec(bb, k):
    return pl.BlockSpec((bb, None, 1, D_MODEL), lambda i, *_: (i, k, 0, 0))


def _ffn_kernel(x_ref, sh_ref, sc_ref, gt_ref, w1_ref, w3_ref, w2_ref, lg_ref, lb_ref,
                o_ref, h_scr, acc_scr, *, nj, res_w):
    j = pl.program_id(2)
    bb, ts, _ = x_ref.shape

    @pl.when(j == 0)
    def _():
        h = x_ref[...] * (1.0 + sc_ref[...]) + sh_ref[...]
        h_scr[...] = h.reshape(bb * ts, D_MODEL).astype(BF)

    h = h_scr[...]
    a = _dot(h, w1_ref[...])
    b = _dot(h, w3_ref[...])
    g = (jax.nn.silu(a) * b).astype(BF)
    y = _dot(g, w2_ref[...])

    @pl.when(j == 0)
    def _():
        acc_scr[...] = y

    @pl.when(j > 0)
    def _():
        acc_scr[...] += y

    @pl.when(j == nj - 1)
    def _():
        yy = acc_scr[...].reshape(bb, ts, D_MODEL)
        z = ALPHA * x_ref[...] + (res_w * (1.0 + gt_ref[...])) * yy
        o_ref[...] = _layer_norm(z, lg_ref[...], lb_ref[...])


def ffn_block(x, mod, sub, w1, w3, w2, lg, lb):
    B, S, _ = x.shape
    bb, ts = _row_tile(B, S, 1024)
    tf = 256
    nj = D_FF // tf
    grid = (B // bb, S // ts, nj)
    xmap = lambda i, s, j: (i, s, 0)
    return pl.pallas_call(
        functools.partial(_ffn_kernel, nj=nj, res_w=0.5),
        out_shape=jax.ShapeDtypeStruct(x.shape, F32),
        grid=grid,
        in_specs=[
            pl.BlockSpec((bb, ts, D_MODEL), xmap),
            _mod_spec(bb, 3 * sub), _mod_spec(bb, 3 * sub + 1), _mod_spec(bb, 3 * sub + 2),
            pl.BlockSpec((D_MODEL, tf), lambda i, s, j: (0, j)),
            pl.BlockSpec((D_MODEL, tf), lambda i, s, j: (0, j)),
            pl.BlockSpec((tf, D_MODEL), lambda i, s, j: (j, 0)),
            pl.BlockSpec((1, D_MODEL), lambda i, s, j: (0, 0)),
            pl.BlockSpec((1, D_MODEL), lambda i, s, j: (0, 0)),
        ],
        out_specs=pl.BlockSpec((bb, ts, D_MODEL), xmap),
        scratch_shapes=[pltpu.VMEM((bb * ts, D_MODEL), BF), pltpu.VMEM((bb * ts, D_MODEL), F32)],
        compiler_params=_cparams(("parallel", "parallel", "arbitrary")),
        name="ffn_block",
    )(x, mod, mod, mod, w1, w3, w2, lg, lb)


def _modmm_kernel(x_ref, sh_ref, sc_ref, w_ref, o_ref, h_scr):
    j = pl.program_id(2)
    bb, ts, _ = x_ref.shape

    @pl.when(j == 0)
    def _():
        h = x_ref[...] * (1.0 + sc_ref[...]) + sh_ref[...]
        h_scr[...] = h.reshape(bb * ts, D_MODEL).astype(BF)

    o_ref[...] = _dot(h_scr[...], w_ref[...]).reshape(o_ref.shape)


def mod_matmul(x, mod, w):
    B, S, _ = x.shape
    n_out = w.shape[1]
    bb, ts = _row_tile(B, S, 512)
    tn = 640
    assert n_out % tn == 0
    xmap = lambda i, s, j: (i, s, 0)
    return pl.pallas_call(
        _modmm_kernel,
        out_shape=jax.ShapeDtypeStruct((B, S, n_out), F32),
        grid=(B // bb, S // ts, n_out // tn),
        in_specs=[
            pl.BlockSpec((bb, ts, D_MODEL), xmap),
            _mod_spec(bb, 3), _mod_spec(bb, 4),
            pl.BlockSpec((D_MODEL, tn), lambda i, s, j: (0, j)),
        ],
        out_specs=pl.BlockSpec((bb, ts, tn), lambda i, s, j: (i, s, j)),
        scratch_shapes=[pltpu.VMEM((bb * ts, D_MODEL), BF)],
        compiler_params=_cparams(("parallel", "parallel", "arbitrary")),
        name="mod_matmul",
    )(x, mod, mod, w)


def _log_sigmoid(x):
    return jnp.minimum(x, 0.0) - jnp.log1p(jnp.exp(-jnp.abs(x)))


def _mlstm_kernel(m0_ref, bg_ref, q_ref, k_ref, v_ref, o_ref, gc_ref, gr_ref, c0_ref, n0_ref, g_ref,
                  h_out, c_out, n_out, m_out, c_scr, n_scr, m_scr, *, L, nc):
    b = pl.program_id(0)
    h = pl.program_id(1)
    c = pl.program_id(2)

    @pl.when(c == 0)
    def _():
        c_scr[...] = c0_ref[...]
        n_scr[...] = n0_ref[...]
        m_scr[...] = jnp.full((1, 1), m0_ref[b * MLSTM_HEADS + h], F32)

    C0 = c_scr[...]
    n0 = n_scr[...]
    m0 = m_scr[...]
    b_i = bg_ref[h]
    b_f = bg_ref[MLSTM_HEADS + h]
    gc = gc_ref[...]
    gr = gr_ref[...]
    ig_c = gc[:, 0:1] + b_i
    lf_c = _log_sigmoid(gc[:, 1:2] + b_f)
    ig_r = gr[0:1, :] + b_i
    lf_r = _log_sigmoid(gr[1:2, :] + b_f)

    row = lax.broadcasted_iota(jnp.int32, (L, L), 0)
    col = lax.broadcasted_iota(jnp.int32, (L, L), 1)
    causal = col <= row
    b_c = jnp.sum(jnp.where(causal, lf_r, 0.0), axis=1, keepdims=True)
    b_r = jnp.sum(jnp.where(row <= col, lf_c, 0.0), axis=0, keepdims=True)
    dmat = jnp.where(causal, b_c - b_r + ig_r, -jnp.inf)
    m_inter = b_c + m0
    m = jnp.maximum(m_inter, jnp.max(dmat, axis=1, keepdims=True))
    w_inter = jnp.exp(m_inter - m)
    p = jnp.exp(dmat - m)

    q = q_ref[...]
    k = k_ref[...] * (MLSTM_HD ** -0.5)
    v = v_ref[...]
    qb = q.astype(BF)
    kb = k.astype(BF)
    scores = _dot_nt(qb, kb) * p
    num = _dot(scores.astype(BF), v.astype(BF)) + w_inter * _dot_nt(qb, C0.astype(BF))
    den = jnp.sum(scores, axis=1, keepdims=True) + w_inter * jnp.sum(q * n0, axis=1, keepdims=True)
    hh = num / jnp.maximum(jnp.abs(den), jnp.exp(-m))

    m_end = m[L - 1:L, :]
    b_end = b_c[L - 1:L, :]
    w_state = jnp.exp(b_end + m0 - m_end)
    w_rows = jnp.exp(b_end - b_c + ig_c - m_end)
    c_new = w_state * C0 + _dot_tn((w_rows * v).astype(BF), kb)
    n_new = w_state * n0 + jnp.sum(w_rows * k, axis=0, keepdims=True)
    c_scr[...] = c_new
    n_scr[...] = n_new
    m_scr[...] = m_end

    mu = jnp.mean(hh, axis=1, keepdims=True)
    hc = hh - mu
    var = jnp.mean(hc * hc, axis=1, keepdims=True)
    hn = hc * lax.rsqrt(var + HEAD_NORM_EPS)
    h_out[...] = hn * g_ref[...] * jax.nn.sigmoid(o_ref[...])

    @pl.when(c == nc - 1)
    def _():
        c_out[...] = c_new
        n_out[...] = n_new
        m_out[...] = jnp.broadcast_to(m_end, (1, LANES))


def mlstm_mixer(z, gates, C0, n0, m0, b_gates, norm_g):
    B, S, _ = z.shape
    H = MLSTM_HEADS
    L = MLSTM_CHUNK if S % MLSTM_CHUNK == 0 else S
    nc = S // L
    gcol = gates.reshape(B, S, 2, H).transpose(0, 3, 1, 2)
    grow = gates.reshape(B, S, 2, H).transpose(0, 3, 2, 1)
    zspec = lambda off: pl.BlockSpec((None, L, MLSTM_HD), lambda b, h, c: (b, c, off + h))
    smem = pl.BlockSpec(memory_space=pltpu.SMEM)
    hd_spec = pl.BlockSpec((None, None, 1, MLSTM_HD), lambda b, h, c: (b, h, 0, 0))
    c_spec = pl.BlockSpec((None, None, MLSTM_HD, MLSTM_HD), lambda b, h, c: (b, h, 0, 0))
    hm, C1, n1, m1 = pl.pallas_call(
        functools.partial(_mlstm_kernel, L=L, nc=nc),
        out_shape=(
            jax.ShapeDtypeStruct((B, S, MLSTM_WIDTH), F32),
            jax.ShapeDtypeStruct((B, H, MLSTM_HD, MLSTM_HD), F32),
            jax.ShapeDtypeStruct((B, H, 1, MLSTM_HD), F32),
            jax.ShapeDtypeStruct((B, H, 1, LANES), F32),
        ),
        grid=(B, H, nc),
        in_specs=[
            smem, smem,
            zspec(0), zspec(H), zspec(2 * H), zspec(3 * H),
            pl.BlockSpec((None, None, L, 2), lambda b, h, c: (b, h, c, 0)),
            pl.BlockSpec((None, None, 2, L), lambda b, h, c: (b, h, 0, c)),
            c_spec, hd_spec,
            pl.BlockSpec((1, MLSTM_HD), lambda b, h, c: (0, h)),
        ],
        out_specs=(
            pl.BlockSpec((None, L, MLSTM_HD), lambda b, h, c: (b, c, h)),
            c_spec, hd_spec,
            pl.BlockSpec((None, None, 1, LANES), lambda b, h, c: (b, h, 0, 0)),
        ),
        scratch_shapes=[pltpu.VMEM((MLSTM_HD, MLSTM_HD), F32), pltpu.VMEM((1, MLSTM_HD), F32),
                        pltpu.VMEM((1, 1), F32)],
        compiler_params=_cparams(("parallel", "parallel", "arbitrary")),
        name="mlstm",
    )(m0.reshape(B * H), b_gates.reshape(2 * H), z, z, z, z, gcol, grow,
      C0, n0.reshape(B, H, 1, MLSTM_HD), norm_g.reshape(1, MLSTM_WIDTH))
    return hm, C1, n1.reshape(B, H, MLSTM_HD), m1[:, :, 0, 0]


CONV_PAD = SUBLANES


def _expm1(y):
    u = jnp.exp(y)
    small = jnp.where(u == 1.0, y, (u - 1.0) * y / jnp.log(u))
    return jnp.where(jnp.abs(y) > 0.5, u - 1.0, small)


def _rglru_kernel(xr_ref, gr_ref, buf0_ref, h0_ref, cw_ref, cb_ref, wa_ref, wx_ref, ba_ref, bx_ref,
                  lam_ref, hr_out, h_out, buf_out, xp_scr, a_scr, u_scr, hs_scr, h_scr, *, ts, ns):
    s = pl.program_id(1)
    lo = CONV_PAD - (RG_CONV - 1)

    @pl.when(s == 0)
    def _():
        xp_scr[lo:CONV_PAD, :] = buf0_ref[...]
        h_scr[...] = h0_ref[...]

    xp_scr[CONV_PAD:CONV_PAD + ts, :] = xr_ref[...]
    cw = cw_ref[...]
    xc = cb_ref[...] + cw[0:1, :] * xp_scr[lo:lo + ts, :]
    for j in range(1, RG_CONV):
        xc = xc + cw[j:j + 1, :] * xp_scr[lo + j:lo + j + ts, :]
    hist = xp_scr[ts + lo:ts + CONV_PAD, :]
    xp_scr[lo:CONV_PAD, :] = hist

    xb = xc.astype(BF)
    r = jax.nn.sigmoid(_dot(xb, wa_ref[...]) + ba_ref[...])
    i = jax.nn.sigmoid(_dot(xb, wx_ref[...]) + bx_ref[...])
    lam = lam_ref[...]
    softplus_neg = jnp.maximum(-lam, 0.0) + jnp.log1p(jnp.exp(-jnp.abs(lam)))
    log_a = (-RG_C * softplus_neg) * r
    a_scr[...] = jnp.exp(log_a)
    u_scr[...] = jnp.sqrt(-_expm1(2.0 * log_a)) * i * xc

    def body(blk, h):
        r0 = pl.multiple_of(blk * SUBLANES, SUBLANES)
        a8 = a_scr[pl.ds(r0, SUBLANES), :]
        u8 = u_scr[pl.ds(r0, SUBLANES), :]
        rows = []
        for t in range(SUBLANES):
            h = a8[t:t + 1, :] * h + u8[t:t + 1, :]
            rows.append(h)
        hs_scr[pl.ds(r0, SUBLANES), :] = jnp.concatenate(rows, axis=0)
        return h

    h_last = lax.fori_loop(0, ts // SUBLANES, body, h_scr[...])
    h_scr[...] = h_last
    hr_out[...] = hs_scr[...] * jax.nn.gelu(gr_ref[...])

    @pl.when(s == ns - 1)
    def _():
        h_out[...] = h_last
        buf_out[...] = hist


def rglru_mixer(z, h0, buf0, conv_w, conv_b, wa_bd, wx_bd, b_a, b_x, lam):
    B, S, _ = z.shape
    W = RG_WIDTH
    ts = min(S, 512)
    ns = S // ts
    xr_blk = (4 * MLSTM_WIDTH) // W
    row = lambda a: a.reshape(1, W)
    full = lambda shp: pl.BlockSpec(shp, lambda b, s: (0,) * len(shp))
    hr, h1, buf1 = pl.pallas_call(
        functools.partial(_rglru_kernel, ts=ts, ns=ns),
        out_shape=(
            jax.ShapeDtypeStruct((B, S, W), F32),
            jax.ShapeDtypeStruct((B, 1, W), F32),
            jax.ShapeDtypeStruct((B, RG_CONV - 1, W), F32),
        ),
        grid=(B, ns),
        in_specs=[
            pl.BlockSpec((None, ts, W), lambda b, s: (b, s, xr_blk)),
            pl.BlockSpec((None, ts, W), lambda b, s: (b, s, xr_blk + 1)),
            pl.BlockSpec((None, RG_CONV - 1, W), lambda b, s: (b, 0, 0)),
            pl.BlockSpec((None, 1, W), lambda b, s: (b, 0, 0)),
            full((RG_CONV, W)), full((1, W)), full((W, W)), full((W, W)),
            full((1, W)), full((1, W)), full((1, W)),
        ],
        out_specs=(
            pl.BlockSpec((None, ts, W), lambda b, s: (b, s, 0)),
            pl.BlockSpec((None, 1, W), lambda b, s: (b, 0, 0)),
            pl.BlockSpec((None, RG_CONV - 1, W), lambda b, s: (b, 0, 0)),
        ),
        scratch_shapes=[pltpu.VMEM((CONV_PAD + ts, W), F32), pltpu.VMEM((ts, W), F32),
                        pltpu.VMEM((ts, W), F32), pltpu.VMEM((ts, W), F32), pltpu.VMEM((1, W), F32)],
        compiler_params=_cparams(("parallel", "arbitrary")),
        name="rglru",
    )(z, z, buf0, h0.reshape(B, 1, W), conv_w, row(conv_b), wa_bd, wx_bd, row(b_a), row(b_x), row(lam))
    return hr, h1.reshape(B, W), buf1


def _proj2_post_kernel(x_ref, gt_ref, a1_ref, a2_ref, w1_ref, w2_ref, lg_ref, lb_ref, o_ref):
    bb, ts, _ = x_ref.shape
    a1 = a1_ref[...].reshape(bb * ts, -1).astype(BF)
    a2 = a2_ref[...].reshape(bb * ts, -1).astype(BF)
    y = _dot(a1, w1_ref[...]) + _dot(a2, w2_ref[...])
    z = ALPHA * x_ref[...] + (1.0 + gt_ref[...]) * y.reshape(bb, ts, D_MODEL)
    o_ref[...] = _layer_norm(z, lg_ref[...], lb_ref[...])


def proj2_post(x, mod, a1, a2, w1, w2, lg, lb):
    B, S, _ = x.shape
    bb, ts = _row_tile(B, S, 512)
    K1, K2 = a1.shape[-1], a2.shape[-1]
    xmap = lambda i, s: (i, s, 0)
    full = lambda shp: pl.BlockSpec(shp, lambda i, s: (0,) * len(shp))
    return pl.pallas_call(
        _proj2_post_kernel,
        out_shape=jax.ShapeDtypeStruct(x.shape, F32),
        grid=(B // bb, S // ts),
        in_specs=[
            pl.BlockSpec((bb, ts, D_MODEL), xmap), _mod_spec(bb, 5),
            pl.BlockSpec((bb, ts, K1), xmap), pl.BlockSpec((bb, ts, K2), xmap),
            full((K1, D_MODEL)), full((K2, D_MODEL)), full((1, D_MODEL)), full((1, D_MODEL)),
        ],
        out_specs=pl.BlockSpec((bb, ts, D_MODEL), xmap),
        compiler_params=_cparams(("parallel", "parallel")),
        name="proj2_post",
    )(x, mod, a1, a2, w1, w2, lg, lb)


def _seg_sum(x, e_ref, et_ref):
    e = e_ref[...]
    et = et_ref[...]
    hi, lo = _split2(x)
    s = _dot(hi, e) + _dot(lo, e)
    shi, slo = _split2(s)
    return _dot(shi, et) + _dot(slo, et)


def _rwkv_pre_kernel(x_ref, sh_ref, sc_ref, prev0_ref, mu_ref, wr_ref, wk_ref, wv_ref, w0_ref, w1_ref,
                     w2_ref, a0_ref, a1_ref, a2_ref, g1_ref, g2_ref, kk_ref, ka_ref, e_ref, et_ref,
                     r_out, lw_out, k_out, v_out, kn_out, a_out, g_out, shift_out, carry_scr, *, ns):
    s = pl.program_id(1)
    bb, ts, _ = x_ref.shape
    tm = bb * ts

    @pl.when(s == 0)
    def _():
        carry_scr[...] = prev0_ref[...]

    h = x_ref[...] * (1.0 + sc_ref[...]) + sh_ref[...]
    t_idx = lax.broadcasted_iota(jnp.int32, h.shape, 1)
    h_prev = jnp.where(t_idx == 0, carry_scr[...], pltpu.roll(h, 1, 1))
    last = h[:, ts - 1:ts, :]
    carry_scr[...] = last
    dx = (h_prev - h).reshape(tm, D_MODEL)
    h2 = h.reshape(tm, D_MODEL)
    mu = mu_ref[...]
    mix = lambda j: (h2 + dx * mu[j:j + 1, :]).astype(BF)
    xr, xw, xk, xv, xa, xg = [mix(j) for j in range(6)]

    r = _dot(xr, wr_ref[...])
    k = _dot(xk, wk_ref[...])
    v = _dot(xv, wv_ref[...])
    wl = w0_ref[...] + _dot(jnp.tanh(_dot(xw, w1_ref[...])).astype(BF), w2_ref[...])
    w_log = -(jnp.maximum(-wl, 0.0) + jnp.log1p(jnp.exp(-jnp.abs(wl)))) - 0.5
    a = jax.nn.sigmoid(a0_ref[...] + _dot(_dot(xa, a1_ref[...]).astype(BF), a2_ref[...]))
    g = _dot(jax.nn.sigmoid(_dot(xg, g1_ref[...])).astype(BF), g2_ref[...])
    kk = k * kk_ref[...]
    nrm = jnp.sqrt(_seg_sum(kk * kk, e_ref, et_ref))
    kn = kk / jnp.maximum(nrm, 1e-12)
    k = k * (1.0 + (a - 1.0) * ka_ref[...])

    shp = x_ref.shape
    r_out[...] = r.reshape(shp)
    lw_out[...] = (-jnp.exp(w_log)).reshape(shp)
    k_out[...] = k.reshape(shp)
    v_out[...] = v.reshape(shp)
    kn_out[...] = kn.reshape(shp)
    a_out[...] = a.reshape(shp)
    g_out[...] = g.reshape(shp)

    @pl.when(s == ns - 1)
    def _():
        shift_out[...] = last


def rwkv_pre(x, mod, prev0, P):
    B, S, _ = x.shape
    bb, ts = _row_tile(B, S, 256)
    ns = S // ts
    xmap = lambda i, s: (i, s, 0)
    full = lambda shp: pl.BlockSpec(shp, lambda i, s: (0,) * len(shp))
    tok = pl.BlockSpec((bb, ts, D_MODEL), xmap)
    st = pl.BlockSpec((bb, 1, D_MODEL), lambda i, s: (i, 0, 0))
    sq, lo_in, lo_out, vec = (D_MODEL, D_MODEL), (D_MODEL, LORA_PAD), (LORA_PAD, D_MODEL), (1, D_MODEL)
    outs = pl.pallas_call(
        functools.partial(_rwkv_pre_kernel, ns=ns),
        out_shape=tuple([jax.ShapeDtypeStruct(x.shape, F32)] * 7
                        + [jax.ShapeDtypeStruct((B, 1, D_MODEL), F32)]),
        grid=(B // bb, ns),
        in_specs=[
            tok, _mod_spec(bb, 3), _mod_spec(bb, 4), st, full((6, D_MODEL)),
            full(sq), full(sq), full(sq), full(vec), full(lo_in), full(lo_out),
            full(vec), full(lo_in), full(lo_out), full(lo_in), full(lo_out),
            full(vec), full(vec), full((D_MODEL, LANES)), full((LANES, D_MODEL)),
        ],
        out_specs=tuple([tok] * 7 + [st]),
        scratch_shapes=[pltpu.VMEM((bb, 1, D_MODEL), F32)],
        compiler_params=_cparams(("parallel", "arbitrary")),
        name="rwkv_pre",
    )(x, mod, mod, prev0.reshape(B, 1, D_MODEL), P['mu'], P['wr'], P['wk'], P['wv'], P['w0'], P['w1'],
      P['w2'], P['a0'], P['a1'], P['a2'], P['g1'], P['g2'], P['k_k'], P['k_a'], P['seg'], P['seg_t'])
    return outs


def _stack_heads(x, first):
    return jnp.concatenate([jnp.where(first, x, 0.0), jnp.where(first, 0.0, x)], axis=0)


def _wkv_kernel(r_ref, lw_ref, k_ref, v_ref, kn_ref, a_ref, s0_ref, y_out, s_out, sp_scr, *, L, nc):
    c = pl.program_id(2)
    hs = RWKV_HS
    L2 = 2 * L

    @pl.when(c == 0)
    def _():
        zero = jnp.zeros((hs, hs), F32)
        sp_scr[...] = jnp.concatenate(
            [jnp.concatenate([s0_ref[0], zero], axis=1), jnp.concatenate([zero, s0_ref[1]], axis=1)], axis=0)

    sp = sp_scr[...]
    lw = lw_ref[...]
    row = lax.broadcasted_iota(jnp.int32, (L, L), 0)
    col = lax.broadcasted_iota(jnp.int32, (L, L), 1)
    tri = jnp.where(col <= row, 1.0, 0.0).astype(BF)
    lhi, lmid, llo = _split3(lw)
    c_incl = _dot(tri, lhi) + _dot(tri, lmid) + _dot(tri, llo)
    c_excl = c_incl - lw
    c_end = c_incl[L - 1:L, :]
    e_inv = jnp.exp(-c_incl)
    kn = kn_ref[...]
    first = lax.broadcasted_iota(jnp.int32, (L, LANES), 1) < hs
    la = _stack_heads(-kn * jnp.exp(c_excl), first).astype(BF)
    lr = _stack_heads(r_ref[...] * jnp.exp(c_incl), first).astype(BF)
    rb = _stack_heads(kn * a_ref[...] * e_inv, first).astype(BF)
    rk = _stack_heads(k_ref[...] * e_inv, first).astype(BF)
    vs = _stack_heads(v_ref[...], first).astype(BF)

    row2 = lax.broadcasted_iota(jnp.int32, (L2, L2), 0)
    col2 = lax.broadcasted_iota(jnp.int32, (L2, L2), 1)
    same = (row2 >= L) == (col2 >= L)
    strict = same & (col2 < row2)
    incl = same & (col2 <= row2)
    n_ab = jnp.where(strict, _dot_nt(la, rb), 0.0)
    n_ak = jnp.where(strict, _dot_nt(la, rk), 0.0)
    m_rb = jnp.where(incl, _dot_nt(lr, rb), 0.0)
    m_rk = jnp.where(incl, _dot_nt(lr, rk), 0.0)

    eye = jnp.where(row2 == col2, 1.0, 0.0)
    xp = n_ab
    tinv = eye + xp
    lev = 2
    while lev < L:
        xb = xp.astype(BF)
        xp = _dot(xb, xb)
        tinv = tinv + _dot(tinv.astype(BF), xp.astype(BF))
        lev *= 2

    spb = sp.astype(BF)
    rhs = _dot_nt(la, spb) + _dot(n_ak.astype(BF), vs)
    u = _dot(tinv.astype(BF), rhs.astype(BF))
    ub = u.astype(BF)
    ys = _dot_nt(lr, spb) + _dot(m_rb.astype(BF), ub) + _dot(m_rk.astype(BF), vs)
    y_out[...] = ys[0:L, :] + ys[L:L2, :]
    sp_new = (sp + _dot_tn(ub, rb) + _dot_tn(vs, rk)) * jnp.exp(c_end)
    sp_scr[...] = sp_new

    @pl.when(c == nc - 1)
    def _():
        s_out[0] = sp_new[0:hs, 0:hs]
        s_out[1] = sp_new[hs:2 * hs, hs:2 * hs]


def wkv_recurrence(r, lw, k, v, kn, a, s0):
    B, S, _ = r.shape
    L = RWKV_CHUNK if S % RWKV_CHUNK == 0 else S
    assert L & (L - 1) == 0 and L % SUBLANES == 0
    nc = S // L
    npair = RWKV_HEADS // 2
    tok = pl.BlockSpec((None, L, LANES), lambda b, p, c: (b, c, p))
    st = pl.BlockSpec((None, 2, RWKV_HS, RWKV_HS), lambda b, p, c: (b, p, 0, 0))
    return pl.pallas_call(
        functools.partial(_wkv_kernel, L=L, nc=nc),
        out_shape=(jax.ShapeDtypeStruct(r.shape, F32), jax.ShapeDtypeStruct(s0.shape, F32)),
        grid=(B, npair, nc),
        in_specs=[tok] * 6 + [st],
        out_specs=(tok, st),
        scratch_shapes=[pltpu.VMEM((2 * RWKV_HS, 2 * RWKV_HS), F32)],
        compiler_params=_cparams(("parallel", "parallel", "arbitrary")),
        name="wkv",
    )(r, lw, k, v, kn, a, s0)


def _rwkv_post_kernel(x_ref, gt_ref, y_ref, r_ref, k_ref, v_ref, g_ref, rk_ref, lxg_ref, lxb_ref,
                      wo_ref, e_ref, et_ref, lg_ref, lb_ref, o_ref):
    bb, ts, _ = x_ref.shape
    flat = lambda ref: ref[...].reshape(bb * ts, D_MODEL)
    y = flat(y_ref)
    inv_n = 1.0 / RWKV_HS
    yc = y - _seg_sum(y, e_ref, et_ref) * inv_n
    yv = _seg_sum(yc * yc, e_ref, et_ref) * inv_n
    yn = yc * lax.rsqrt(yv + RWKV_LN_EPS) * lxg_ref[...] + lxb_ref[...]
    bonus = _seg_sum(flat(r_ref) * flat(k_ref) * rk_ref[...], e_ref, et_ref) * flat(v_ref)
    out = _dot(((yn + bonus) * flat(g_ref)).astype(BF), wo_ref[...])
    z = ALPHA * x_ref[...] + (1.0 + gt_ref[...]) * out.reshape(bb, ts, D_MODEL)
    o_ref[...] = _layer_norm(z, lg_ref[...], lb_ref[...])


def rwkv_post(x, mod, y, r, k, v, g, P, lg, lb):
    B, S, _ = x.shape
    bb, ts = _row_tile(B, S, 256)
    xmap = lambda i, s: (i, s, 0)
    full = lambda shp: pl.BlockSpec(shp, lambda i, s: (0,) * len(shp))
    tok = pl.BlockSpec((bb, ts, D_MODEL), xmap)
    vec = (1, D_MODEL)
    return pl.pallas_call(
        _rwkv_post_kernel,
        out_shape=jax.ShapeDtypeStruct(x.shape, F32),
        grid=(B // bb, S // ts),
        in_specs=[tok, _mod_spec(bb, 5), tok, tok, tok, tok, tok, full(vec), full(vec), full(vec),
                  full((D_MODEL, D_MODEL)), full((D_MODEL, LANES)), full((LANES, D_MODEL)),
                  full(vec), full(vec)],
        out_specs=tok,
        compiler_params=_cparams(("parallel", "parallel")),
        name="rwkv_post",
    )(x, mod, y, r, k, v, g, P['r_k'], P['lnx_g'], P['lnx_b'], P['wo'], P['seg'], P['seg_t'], lg, lb)


def _block_diag(w):
    G, n, _ = w.shape
    eye = jnp.eye(G, dtype=w.dtype)
    return (eye[:, None, :, None] * w[:, :, None, :]).reshape(G * n, G * n)


def _pad_cols(w, n):
    return jnp.pad(w, ((0, 0), (0, n - w.shape[1])))


def _pad_rows(w, n):
    return jnp.pad(w, ((0, n - w.shape[0]), (0, 0)))


def _prep_ab(j, p):
    w_in = p['ab_w_in'][j]
    n_main = 4 * MLSTM_WIDTH
    n_gate = 2 * MLSTM_HEADS
    w_all = jnp.concatenate([w_in[:, :n_main], w_in[:, n_main + n_gate:],
                             w_in[:, n_main:n_main + n_gate]], axis=1)
    return dict(
        w_in=_pad_cols(w_all, Z_COLS).astype(BF),
        b_gates=p['mlstm_b_gates'][j], norm_g=p['mlstm_norm_g'][j],
        conv_w=p['rg_conv_w'][j], conv_b=p['rg_conv_b'][j],
        wa=_block_diag(p['rg_w_a'][j]).astype(BF), wx=_block_diag(p['rg_w_x'][j]).astype(BF),
        b_a=p['rg_b_a'][j], b_x=p['rg_b_x'][j], lam=p['rg_lambda'][j],
        w_out_m=p['ab_w_out'][j][:MLSTM_WIDTH].astype(BF), w_out_r=p['ab_w_out'][j][MLSTM_WIDTH:].astype(BF),
    )


def _prep_rwkv(j, p):
    vec = lambda a: a.reshape(1, D_MODEL)
    head = jnp.arange(D_MODEL) // RWKV_HS
    seg = (head[:, None] == jnp.arange(LANES)[None, :]).astype(BF)
    return dict(
        mu=p['rw_mu'][j],
        wr=p['rw_wr'][j].astype(BF), wk=p['rw_wk'][j].astype(BF), wv=p['rw_wv'][j].astype(BF),
        w0=vec(p['rw_w0'][j]), w1=_pad_cols(p['rw_w1'][j], LORA_PAD).astype(BF),
        w2=_pad_rows(p['rw_w2'][j], LORA_PAD).astype(BF),
        a0=vec(p['rw_a0'][j]), a1=_pad_cols(p['rw_a1'][j], LORA_PAD).astype(BF),
        a2=_pad_rows(p['rw_a2'][j], LORA_PAD).astype(BF),
        g1=_pad_cols(p['rw_g1'][j], LORA_PAD).astype(BF), g2=_pad_rows(p['rw_g2'][j], LORA_PAD).astype(BF),
        k_k=vec(p['rw_k_k'][j]), k_a=vec(p['rw_k_a'][j]), r_k=vec(p['rw_r_k'][j]),
        lnx_g=vec(p['rw_lnx_g'][j]), lnx_b=vec(p['rw_lnx_b'][j]), wo=p['rw_wo'][j].astype(BF),
        seg=seg, seg_t=seg.T,
    )


def ab_mixer(x, mod, st, A, lg, lb):
    mC, mn, mm, rh, rconv = st
    z = mod_matmul(x, mod, A['w_in'])
    gates = z[:, :, 6 * MLSTM_WIDTH:6 * MLSTM_WIDTH + 2 * MLSTM_HEADS]
    hm, C1, n1, m1 = mlstm_mixer(z, gates, mC, mn, mm, A['b_gates'], A['norm_g'])
    hr, rh1, buf1 = rglru_mixer(z, rh, rconv, A['conv_w'], A['conv_b'], A['wa'], A['wx'],
                                A['b_a'], A['b_x'], A['lam'])
    x = proj2_post(x, mod, hm, hr, A['w_out_m'], A['w_out_r'], lg, lb)
    return x, (C1, n1, m1, rh1, buf1)


def rwkv_mixer(x, mod, st, R, lg, lb):
    wkv0, prev0 = st
    r, lw, k, v, kn, a, g, shift = rwkv_pre(x, mod, prev0, R)
    y, wkv1 = wkv_recurrence(r, lw, k, v, kn, a, wkv0)
    x = rwkv_post(x, mod, y, r, k, v, g, R, lg, lb)
    return x, (wkv1, shift.reshape(shift.shape[0], D_MODEL))


def run_trunk(x, mods, states, W):
    mC, mn, mm, rh, rconv, wkv, shift = states
    new_ab, new_c = [], []
    for layer in range(DEPTH):
        mod = mods[layer]
        lg = lambda i: W['ln_g'][layer, i].reshape(1, D_MODEL)
        lb = lambda i: W['ln_b'][layer, i].reshape(1, D_MODEL)
        f = W['ffn'][layer]
        x = ffn_block(x, mod, 0, f[0][0], f[0][1], f[0][2], lg(0), lb(0))
        j = layer // 2
        if layer % 2 == 0:
            x, st = ab_mixer(x, mod, (mC[j], mn[j], mm[j], rh[j], rconv[j]), W['ab'][j], lg(1), lb(1))
            new_ab.append(st)
        else:
            x, st = rwkv_mixer(x, mod, (wkv[j], shift[j]), W['rwkv'][j], lg(1), lb(1))
            new_c.append(st)
        x = ffn_block(x, mod, 2, f[1][0], f[1][1], f[1][2], lg(2), lb(2))
    stk = lambda sts, i: jnp.stack([s[i] for s in sts], axis=0)
    return x, (stk(new_ab, 0), stk(new_ab, 1), stk(new_ab, 2), stk(new_ab, 3), stk(new_ab, 4),
               stk(new_c, 0), stk(new_c, 1))


def _zero_states(B):
    n_ab, n_c = (DEPTH + 1) // 2, DEPTH // 2
    return (jnp.zeros((n_ab, B, MLSTM_HEADS, MLSTM_HD, MLSTM_HD), F32),
            jnp.zeros((n_ab, B, MLSTM_HEADS, MLSTM_HD), F32),
            jnp.zeros((n_ab, B, MLSTM_HEADS), F32),
            jnp.zeros((n_ab, B, RG_WIDTH), F32),
            jnp.zeros((n_ab, B, RG_CONV - 1, RG_WIDTH), F32),
            jnp.zeros((n_c, B, RWKV_HEADS, RWKV_HS, RWKV_HS), F32),
            jnp.zeros((n_c, B, D_MODEL), F32))


def kernel(x_prompt, x_sample, c_prompt, c_sample, state_mlstm_C, state_mlstm_n, state_mlstm_m, state_rglru_h, state_rglru_conv, state_rwkv_wkv, state_rwkv_shift, ada_w, ada_b, ln_g, ln_b, ffn_w1, ffn_w3, ffn_w2, ab_w_in, mlstm_b_gates, mlstm_norm_g, rg_conv_w, rg_conv_b, rg_w_a, rg_b_a, rg_w_x, rg_b_x, rg_lambda, ab_w_out, rw_mu, rw_wr, rw_wk, rw_wv, rw_w0, rw_w1, rw_w2, rw_a0, rw_a1, rw_a2, rw_g1, rw_g2, rw_k_k, rw_k_a, rw_r_k, rw_lnx_g, rw_lnx_b, rw_wo):
    p = dict(ab_w_in=ab_w_in, mlstm_b_gates=mlstm_b_gates, mlstm_norm_g=mlstm_norm_g,
             rg_conv_w=rg_conv_w, rg_conv_b=rg_conv_b, rg_w_a=rg_w_a, rg_b_a=rg_b_a, rg_w_x=rg_w_x,
             rg_b_x=rg_b_x, rg_lambda=rg_lambda, ab_w_out=ab_w_out, rw_mu=rw_mu, rw_wr=rw_wr,
             rw_wk=rw_wk, rw_wv=rw_wv, rw_w0=rw_w0, rw_w1=rw_w1, rw_w2=rw_w2, rw_a0=rw_a0,
             rw_a1=rw_a1, rw_a2=rw_a2, rw_g1=rw_g1, rw_g2=rw_g2, rw_k_k=rw_k_k, rw_k_a=rw_k_a,
             rw_r_k=rw_r_k, rw_lnx_g=rw_lnx_g, rw_lnx_b=rw_lnx_b, rw_wo=rw_wo)
    W = dict(
        ln_g=ln_g, ln_b=ln_b,
        ffn=[[(ffn_w1[l, i].astype(BF), ffn_w3[l, i].astype(BF), ffn_w2[l, i].astype(BF))
              for i in range(2)] for l in range(DEPTH)],
        ab=[_prep_ab(j, p) for j in range((DEPTH + 1) // 2)],
        rwkv=[_prep_rwkv(j, p) for j in range(DEPTH // 2)],
    )
    Bp, Bs = x_prompt.shape[0], x_sample.shape[0]
    mod_all = adaln(jnp.concatenate([c_prompt, c_sample], axis=0), ada_w, ada_b)
    mods_p = [mod_all[l, :Bp].reshape(Bp, 3 * N_SUB, 1, D_MODEL) for l in range(DEPTH)]
    mods_s = [mod_all[l, Bp:].reshape(Bs, 3 * N_SUB, 1, D_MODEL) for l in range(DEPTH)]
    y_prompt, sp = run_trunk(x_prompt, mods_p, _zero_states(Bp), W)
    y_sample, ss = run_trunk(x_sample, mods_s,
                             (state_mlstm_C, state_mlstm_n, state_mlstm_m, state_rglru_h,
                              state_rglru_conv, state_rwkv_wkv, state_rwkv_shift), W)
    return (y_prompt, y_sample) + tuple(sp) + tuple(ss)
```

```python
import functools

import jax
import jax.numpy as jnp
from jax import lax
from jax.experimental import pallas as pl
from jax.experimental.pallas import tpu as pltpu

D_MODEL = 1024
DEPTH = 2
N_SUB = 3
MLSTM_WIDTH = 512
MLSTM_HEADS = 4
MLSTM_HD = 128
MLSTM_CHUNK = 128
RG_WIDTH = 512
RG_BLOCKS = 8
RG_BD = 64
RG_CONV = 4
RG_C = 8.0
RWKV_HS = 64
RWKV_HEADS = 16
RWKV_LN_EPS = 64e-5
RWKV_CHUNK = 64
D_FF = 2816
ALPHA = (2.0 * DEPTH) ** 0.25
LN_EPS = 1e-5
HEAD_NORM_EPS = 1e-6

LANES = 128
SUBLANES = 8
VMEM_LIMIT_BYTES = 56 * 1024 * 1024
Z_COLS = 3200
LORA_PAD = 128

BF = jnp.bfloat16
F32 = jnp.float32

NT_DIMS = (((1,), (1,)), ((), ()))
TN_DIMS = (((0,), (0,)), ((), ()))


def _cparams(sem):
    return pltpu.CompilerParams(dimension_semantics=sem, vmem_limit_bytes=VMEM_LIMIT_BYTES)


def _dot(a, b):
    return jnp.dot(a, b, preferred_element_type=F32)


def _dot_nt(a, b):
    return lax.dot_general(a, b, NT_DIMS, preferred_element_type=F32)


def _dot_tn(a, b):
    return lax.dot_general(a, b, TN_DIMS, preferred_element_type=F32)


def _layer_norm(z, g, b):
    mu = jnp.mean(z, axis=-1, keepdims=True)
    zc = z - mu
    var = jnp.mean(zc * zc, axis=-1, keepdims=True)
    return zc * lax.rsqrt(var + LN_EPS) * g + b


def _row_tile(B, S, rows):
    if S >= rows:
        assert S % rows == 0
        return 1, rows
    bb = min(B, rows // S)
    assert B % bb == 0
    return bb, S


def _split2(x):
    hi = x.astype(BF)
    lo = (x - hi.astype(F32)).astype(BF)
    return hi, lo


def _split3(x):
    hi = x.astype(BF)
    r1 = x - hi.astype(F32)
    mid = r1.astype(BF)
    lo = (r1 - mid.astype(F32)).astype(BF)
    return hi, mid, lo


def _adaln_kernel(c_ref, w_ref, b_ref, o_ref):
    h = jax.nn.silu(c_ref[...]).astype(BF)
    o_ref[...] = _dot(h, w_ref[...].astype(BF)) + b_ref[...]


def adaln(c_all, ada_w, ada_b):
    Bc = c_all.shape[0]
    n_out = ada_w.shape[-1]
    tn = 1024
    return pl.pallas_call(
        _adaln_kernel,
        out_shape=jax.ShapeDtypeStruct((DEPTH, Bc, n_out), F32),
        grid=(DEPTH, n_out // tn),
        in_specs=[
            pl.BlockSpec((Bc, D_MODEL), lambda l, j: (0, 0)),
            pl.BlockSpec((None, D_MODEL, tn), lambda l, j: (l, 0, j)),
            pl.BlockSpec((None, 1, tn), lambda l, j: (l, 0, j)),
        ],
        out_specs=pl.BlockSpec((None, Bc, tn), lambda l, j: (l, 0, j)),
        compiler_params=_cparams(("parallel", "parallel")),
        name="adaln",
    )(c_all, ada_w, ada_b.reshape(DEPTH, 1, n_out))


def _mod_spec(bb, k):
    return pl.BlockSpec((bb, None, 1, D_MODEL), lambda i, *_: (i, k, 0, 0))


def _ffn_kernel(x_ref, sh_ref, sc_ref, gt_ref, w1_ref, w3_ref, w2_ref, lg_ref, lb_ref,
                o_ref, h_scr, acc_scr, *, nj, res_w):
    j = pl.program_id(2)
    bb, ts, _ = x_ref.shape

    @pl.when(j == 0)
    def _():
        h = x_ref[...] * (1.0 + sc_ref[...]) + sh_ref[...]
        h_scr[...] = h.reshape(bb * ts, D_MODEL).astype(BF)

    h = h_scr[...]
    a = _dot(h, w1_ref[...])
    b = _dot(h, w3_ref[...])
    g = (jax.nn.silu(a) * b).astype(BF)
    y = _dot(g, w2_ref[...])

    @pl.when(j == 0)
    def _():
        acc_scr[...] = y

    @pl.when(j > 0)
    def _():
        acc_scr[...] += y

    @pl.when(j == nj - 1)
    def _():
        yy = acc_scr[...].reshape(bb, ts, D_MODEL)
        z = ALPHA * x_ref[...] + (res_w * (1.0 + gt_ref[...])) * yy
        o_ref[...] = _layer_norm(z, lg_ref[...], lb_ref[...])


def ffn_block(x, mod, sub, w1, w3, w2, lg, lb):
    B, S, _ = x.shape
    bb, ts = _row_tile(B, S, 1024)
    tf = 256
    nj = D_FF // tf
    grid = (B // bb, S // ts, nj)
    xmap = lambda i, s, j: (i, s, 0)
    return pl.pallas_call(
        functools.partial(_ffn_kernel, nj=nj, res_w=0.5),
        out_shape=jax.ShapeDtypeStruct(x.shape, F32),
        grid=grid,
        in_specs=[
            pl.BlockSpec((bb, ts, D_MODEL), xmap),
            _mod_spec(bb, 3 * sub), _mod_spec(bb, 3 * sub + 1), _mod_spec(bb, 3 * sub + 2),
            pl.BlockSpec((D_MODEL, tf), lambda i, s, j: (0, j)),
            pl.BlockSpec((D_MODEL, tf), lambda i, s, j: (0, j)),
            pl.BlockSpec((tf, D_MODEL), lambda i, s, j: (j, 0)),
            pl.BlockSpec((1, D_MODEL), lambda i, s, j: (0, 0)),
            pl.BlockSpec((1, D_MODEL), lambda i, s, j: (0, 0)),
        ],
        out_specs=pl.BlockSpec((bb, ts, D_MODEL), xmap),
        scratch_shapes=[pltpu.VMEM((bb * ts, D_MODEL), BF), pltpu.VMEM((bb * ts, D_MODEL), F32)],
        compiler_params=_cparams(("parallel", "parallel", "arbitrary")),
        name="ffn_block",
    )(x, mod, mod, mod, w1, w3, w2, lg, lb)


def _modmm_kernel(x_ref, sh_ref, sc_ref, w_ref, o_ref, h_scr):
    j = pl.program_id(2)
    bb, ts, _ = x_ref.shape

    @pl.when(j == 0)
    def _():
        h = x_ref[...] * (1.0 + sc_ref[...]) + sh_ref[...]
        h_scr[...] = h.reshape(bb * ts, D_MODEL).astype(BF)

    o_ref[...] = _dot(h_scr[...], w_ref[...]).reshape(o_ref.shape)


def mod_matmul(x, mod, w):
    B, S, _ = x.shape
    n_out = w.shape[1]
    bb, ts = _row_tile(B, S, 512)
    tn = 640
    assert n_out % tn == 0
    xmap = lambda i, s, j: (i, s, 0)
    return pl.pallas_call(
        _modmm_kernel,
        out_shape=jax.ShapeDtypeStruct((B, S, n_out), F32),
        grid=(B // bb, S // ts, n_out // tn),
        in_specs=[
            pl.BlockSpec((bb, ts, D_MODEL), xmap),
            _mod_spec(bb, 3), _mod_spec(bb, 4),
            pl.BlockSpec((D_MODEL, tn), lambda i, s, j: (0, j)),
        ],
        out_specs=pl.BlockSpec((bb, ts, tn), lambda i, s, j: (i, s, j)),
        scratch_shapes=[pltpu.VMEM((bb * ts, D_MODEL), BF)],
        compiler_params=_cparams(("parallel", "parallel", "arbitrary")),
        name="mod_matmul",
    )(x, mod, mod, w)


def _log_sigmoid(x):
    return jnp.minimum(x, 0.0) - jnp.log1p(jnp.exp(-jnp.abs(x)))


def _lockstep(gens):
    results = [None] * len(gens)
    live = list(range(len(gens)))
    while live:
        still = []
        for idx in live:
            try:
                next(gens[idx])
                still.append(idx)
            except StopIteration as stop:
                results[idx] = stop.value
        live = still
    return results


def _mlstm_head_chunk(q, k, v, o, gc, gr, b_i, b_f, C0, n0, m0, g, L):
    ig_c = gc[:, 0:1] + b_i
    lf_c = _log_sigmoid(gc[:, 1:2] + b_f)
    ig_r = gr[0:1, :] + b_i
    lf_r = _log_sigmoid(gr[1:2, :] + b_f)

    row = lax.broadcasted_iota(jnp.int32, (L, L), 0)
    col = lax.broadcasted_iota(jnp.int32, (L, L), 1)
    causal = col <= row
    b_c = jnp.sum(jnp.where(causal, lf_r, 0.0), axis=1, keepdims=True)
    b_r = jnp.sum(jnp.where(row <= col, lf_c, 0.0), axis=0, keepdims=True)
    dmat = jnp.where(causal, b_c - b_r + ig_r, -jnp.inf)
    m_inter = b_c + m0
    m = jnp.maximum(m_inter, jnp.max(dmat, axis=1, keepdims=True))
    w_inter = jnp.exp(m_inter - m)
    p = jnp.exp(dmat - m)

    k = k * (MLSTM_HD ** -0.5)
    qb = q.astype(BF)
    kb = k.astype(BF)
    yield
    s_raw = _dot_nt(qb, kb)
    q_c0 = _dot_nt(qb, C0.astype(BF))
    yield
    scores = s_raw * p
    num = _dot(scores.astype(BF), v.astype(BF)) + w_inter * q_c0
    den = jnp.sum(scores, axis=1, keepdims=True) + w_inter * jnp.sum(q * n0, axis=1, keepdims=True)
    hh = num / jnp.maximum(jnp.abs(den), jnp.exp(-m))

    m_end = m[L - 1:L, :]
    b_end = b_c[L - 1:L, :]
    w_state = jnp.exp(b_end + m0 - m_end)
    w_rows = jnp.exp(b_end - b_c + ig_c - m_end)
    yield
    c_new = w_state * C0 + _dot_tn((w_rows * v).astype(BF), kb)
    n_new = w_state * n0 + jnp.sum(w_rows * k, axis=0, keepdims=True)

    mu = jnp.mean(hh, axis=1, keepdims=True)
    hc = hh - mu
    var = jnp.mean(hc * hc, axis=1, keepdims=True)
    hn = hc * lax.rsqrt(var + HEAD_NORM_EPS)
    return hn * g * jax.nn.sigmoid(o), c_new, n_new, m_end


def _mlstm_kernel(m0_ref, bg_ref, q_ref, k_ref, v_ref, o_ref, gc_ref, gr_ref, c0_ref, n0_ref, g_ref,
                  h_out, c_out, n_out, m_out, c_scr, n_scr, m_scr, *, L, nc, bb):
    ib = pl.program_id(0)
    c = pl.program_id(1)
    H = MLSTM_HEADS

    @pl.when(c == 0)
    def _():
        c_scr[...] = c0_ref[...]
        n_scr[...] = n0_ref[...]

        def init(i, carry):
            for h in range(H):
                m_scr[i, h] = jnp.full((1, LANES), m0_ref[(ib * bb + i) * H + h], F32)
            return carry

        lax.fori_loop(0, bb, init, 0)

    def seq(i, carry):
        sls = [slice(h * MLSTM_HD, (h + 1) * MLSTM_HD) for h in range(H)]
        heads = _lockstep([_mlstm_head_chunk(
            q_ref[i, :, sls[h]], k_ref[i, :, sls[h]], v_ref[i, :, sls[h]], o_ref[i, :, sls[h]], gc_ref[i, h],
            gr_ref[i, h], bg_ref[h], bg_ref[H + h], c_scr[i, h], n_scr[i, h], m_scr[i, h][:, 0:1],
            g_ref[:, sls[h]], L) for h in range(H)])
        for h, (out, c_new, n_new, m_end) in enumerate(heads):
            h_out[i, :, sls[h]] = out
            c_scr[i, h] = c_new
            n_scr[i, h] = n_new
            m_scr[i, h] = jnp.broadcast_to(m_end, (1, LANES))
        return carry

    lax.fori_loop(0, bb, seq, 0)

    @pl.when(c == nc - 1)
    def _():
        c_out[...] = c_scr[...]
        n_out[...] = n_scr[...]
        m_out[...] = m_scr[...]


def mlstm_mixer(z, gates, C0, n0, m0, b_gates, norm_g):
    B, S, _ = z.shape
    H = MLSTM_HEADS
    W = MLSTM_WIDTH
    L = MLSTM_CHUNK if S % MLSTM_CHUNK == 0 else S
    nc = S // L
    bb = 1 if nc > 1 else min(B, 8)
    gcol = gates.reshape(B, S, 2, H).transpose(0, 3, 1, 2)
    grow = gates.reshape(B, S, 2, H).transpose(0, 3, 2, 1)
    zspec = lambda blk: pl.BlockSpec((bb, L, W), lambda b, c: (b, c, blk))
    smem = pl.BlockSpec(memory_space=pltpu.SMEM)
    hd_spec = pl.BlockSpec((bb, H, 1, MLSTM_HD), lambda b, c: (b, 0, 0, 0))
    c_spec = pl.BlockSpec((bb, H, MLSTM_HD, MLSTM_HD), lambda b, c: (b, 0, 0, 0))
    hm, C1, n1, m1 = pl.pallas_call(
        functools.partial(_mlstm_kernel, L=L, nc=nc, bb=bb),
        out_shape=(
            jax.ShapeDtypeStruct((B, S, W), F32),
            jax.ShapeDtypeStruct((B, H, MLSTM_HD, MLSTM_HD), F32),
            jax.ShapeDtypeStruct((B, H, 1, MLSTM_HD), F32),
            jax.ShapeDtypeStruct((B, H, 1, LANES), F32),
        ),
        grid=(B // bb, nc),
        in_specs=[
            smem, smem,
            zspec(0), zspec(1), zspec(2), zspec(3),
            pl.BlockSpec((bb, H, L, 2), lambda b, c: (b, 0, c, 0)),
            pl.BlockSpec((bb, H, 2, L), lambda b, c: (b, 0, 0, c)),
            c_spec, hd_spec,
            pl.BlockSpec((1, W), lambda b, c: (0, 0)),
        ],
        out_specs=(pl.BlockSpec((bb, L, W), lambda b, c: (b, c, 0)), c_spec, hd_spec, hd_spec),
        scratch_shapes=[pltpu.VMEM((bb, H, MLSTM_HD, MLSTM_HD), F32), pltpu.VMEM((bb, H, 1, MLSTM_HD), F32),
                        pltpu.VMEM((bb, H, 1, LANES), F32)],
        compiler_params=_cparams(("parallel", "arbitrary")),
        name="mlstm",
    )(m0.reshape(B * H), b_gates.reshape(2 * H), z, z, z, z, gcol, grow,
      C0, n0.reshape(B, H, 1, MLSTM_HD), norm_g.reshape(1, W))
    return hm, C1, n1.reshape(B, H, MLSTM_HD), m1[:, :, 0, 0]


CONV_PAD = SUBLANES


def _expm1(y):
    u = jnp.exp(y)
    small = jnp.where(u == 1.0, y, (u - 1.0) * y / jnp.log(u))
    return jnp.where(jnp.abs(y) > 0.5, u - 1.0, small)


def _rglru_kernel(xr_ref, gr_ref, buf0_ref, h0_ref, cw_ref, cb_ref, wa_ref, wx_ref, ba_ref, bx_ref,
                  lam_ref, hr_out, h_out, buf_out, xp_scr, a_scr, u_scr, hs_scr, h_scr, *, ts, ns):
    s = pl.program_id(1)
    lo = CONV_PAD - (RG_CONV - 1)

    @pl.when(s == 0)
    def _():
        xp_scr[lo:CONV_PAD, :] = buf0_ref[...]
        h_scr[...] = h0_ref[...]

    xp_scr[CONV_PAD:CONV_PAD + ts, :] = xr_ref[...]
    cw = cw_ref[...]
    xc = cb_ref[...] + cw[0:1, :] * xp_scr[lo:lo + ts, :]
    for j in range(1, RG_CONV):
        xc = xc + cw[j:j + 1, :] * xp_scr[lo + j:lo + j + ts, :]
    hist = xp_scr[ts + lo:ts + CONV_PAD, :]
    xp_scr[lo:CONV_PAD, :] = hist

    xb = xc.astype(BF)
    r = jax.nn.sigmoid(_dot(xb, wa_ref[...]) + ba_ref[...])
    i = jax.nn.sigmoid(_dot(xb, wx_ref[...]) + bx_ref[...])
    lam = lam_ref[...]
    softplus_neg = jnp.maximum(-lam, 0.0) + jnp.log1p(jnp.exp(-jnp.abs(lam)))
    log_a = (-RG_C * softplus_neg) * r
    a_scr[...] = jnp.exp(log_a)
    u_scr[...] = jnp.sqrt(-_expm1(2.0 * log_a)) * i * xc

    def body(blk, h):
        r0 = pl.multiple_of(blk * SUBLANES, SUBLANES)
        a8 = a_scr[pl.ds(r0, SUBLANES), :]
        u8 = u_scr[pl.ds(r0, SUBLANES), :]
        rows = []
        for t in range(SUBLANES):
            h = a8[t:t + 1, :] * h + u8[t:t + 1, :]
            rows.append(h)
        hs_scr[pl.ds(r0, SUBLANES), :] = jnp.concatenate(rows, axis=0)
        return h

    h_last = lax.fori_loop(0, ts // SUBLANES, body, h_scr[...])
    h_scr[...] = h_last
    hr_out[...] = hs_scr[...] * jax.nn.gelu(gr_ref[...])

    @pl.when(s == ns - 1)
    def _():
        h_out[...] = h_last
        buf_out[...] = hist


def rglru_mixer(z, h0, buf0, conv_w, conv_b, wa_bd, wx_bd, b_a, b_x, lam):
    B, S, _ = z.shape
    W = RG_WIDTH
    ts = min(S, 512)
    ns = S // ts
    xr_blk = (4 * MLSTM_WIDTH) // W
    row = lambda a: a.reshape(1, W)
    full = lambda shp: pl.BlockSpec(shp, lambda b, s: (0,) * len(shp))
    hr, h1, buf1 = pl.pallas_call(
        functools.partial(_rglru_kernel, ts=ts, ns=ns),
        out_shape=(
            jax.ShapeDtypeStruct((B, S, W), F32),
            jax.ShapeDtypeStruct((B, 1, W), F32),
            jax.ShapeDtypeStruct((B, RG_CONV - 1, W), F32),
        ),
        grid=(B, ns),
        in_specs=[
            pl.BlockSpec((None, ts, W), lambda b, s: (b, s, xr_blk)),
            pl.BlockSpec((None, ts, W), lambda b, s: (b, s, xr_blk + 1)),
            pl.BlockSpec((None, RG_CONV - 1, W), lambda b, s: (b, 0, 0)),
            pl.BlockSpec((None, 1, W), lambda b, s: (b, 0, 0)),
            full((RG_CONV, W)), full((1, W)), full((W, W)), full((W, W)),
            full((1, W)), full((1, W)), full((1, W)),
        ],
        out_specs=(
            pl.BlockSpec((None, ts, W), lambda b, s: (b, s, 0)),
            pl.BlockSpec((None, 1, W), lambda b, s: (b, 0, 0)),
            pl.BlockSpec((None, RG_CONV - 1, W), lambda b, s: (b, 0, 0)),
        ),
        scratch_shapes=[pltpu.VMEM((CONV_PAD + ts, W), F32), pltpu.VMEM((ts, W), F32),
                        pltpu.VMEM((ts, W), F32), pltpu.VMEM((ts, W), F32), pltpu.VMEM((1, W), F32)],
        compiler_params=_cparams(("parallel", "arbitrary")),
        name="rglru",
    )(z, z, buf0, h0.reshape(B, 1, W), conv_w, row(conv_b), wa_bd, wx_bd, row(b_a), row(b_x), row(lam))
    return hr, h1.reshape(B, W), buf1


def _proj2_post_kernel(x_ref, gt_ref, a1_ref, a2_ref, w1_ref, w2_ref, lg_ref, lb_ref, o_ref):
    bb, ts, _ = x_ref.shape
    a1 = a1_ref[...].reshape(bb * ts, -1).astype(BF)
    a2 = a2_ref[...].reshape(bb * ts, -1).astype(BF)
    y = _dot(a1, w1_ref[...]) + _dot(a2, w2_ref[...])
    z = ALPHA * x_ref[...] + (1.0 + gt_ref[...]) * y.reshape(bb, ts, D_MODEL)
    o_ref[...] = _layer_norm(z, lg_ref[...], lb_ref[...])


def proj2_post(x, mod, a1, a2, w1, w2, lg, lb):
    B, S, _ = x.shape
    bb, ts = _row_tile(B, S, 512)
    K1, K2 = a1.shape[-1], a2.shape[-1]
    xmap = lambda i, s: (i, s, 0)
    full = lambda shp: pl.BlockSpec(shp, lambda i, s: (0,) * len(shp))
    return pl.pallas_call(
        _proj2_post_kernel,
        out_shape=jax.ShapeDtypeStruct(x.shape, F32),
        grid=(B // bb, S // ts),
        in_specs=[
            pl.BlockSpec((bb, ts, D_MODEL), xmap), _mod_spec(bb, 5),
            pl.BlockSpec((bb, ts, K1), xmap), pl.BlockSpec((bb, ts, K2), xmap),
            full((K1, D_MODEL)), full((K2, D_MODEL)), full((1, D_MODEL)), full((1, D_MODEL)),
        ],
        out_specs=pl.BlockSpec((bb, ts, D_MODEL), xmap),
        compiler_params=_cparams(("parallel", "parallel")),
        name="proj2_post",
    )(x, mod, a1, a2, w1, w2, lg, lb)


def _seg_sum(x, e_ref, et_ref):
    e = e_ref[...]
    et = et_ref[...]
    hi, lo = _split2(x)
    s = _dot(hi, e) + _dot(lo, e)
    shi, slo = _split2(s)
    return _dot(shi, et) + _dot(slo, et)


def _rwkv_pre_kernel(x_ref, sh_ref, sc_ref, prev0_ref, mu_ref, wr_ref, wk_ref, wv_ref, w0_ref, w1_ref,
                     w2_ref, a0_ref, a1_ref, a2_ref, g1_ref, g2_ref, kk_ref, ka_ref, e_ref, et_ref,
                     r_out, lw_out, k_out, v_out, kn_out, a_out, g_out, shift_out, carry_scr, *, ns):
    s = pl.program_id(1)
    bb, ts, _ = x_ref.shape
    tm = bb * ts

    @pl.when(s == 0)
    def _():
        carry_scr[...] = prev0_ref[...]

    h = x_ref[...] * (1.0 + sc_ref[...]) + sh_ref[...]
    t_idx = lax.broadcasted_iota(jnp.int32, h.shape, 1)
    h_prev = jnp.where(t_idx == 0, carry_scr[...], pltpu.roll(h, 1, 1))
    last = h[:, ts - 1:ts, :]
    carry_scr[...] = last
    dx = (h_prev - h).reshape(tm, D_MODEL)
    h2 = h.reshape(tm, D_MODEL)
    mu = mu_ref[...]
    mix = lambda j: (h2 + dx * mu[j:j + 1, :]).astype(BF)
    xr, xw, xk, xv, xa, xg = [mix(j) for j in range(6)]

    r = _dot(xr, wr_ref[...])
    k = _dot(xk, wk_ref[...])
    v = _dot(xv, wv_ref[...])
    wl = w0_ref[...] + _dot(jnp.tanh(_dot(xw, w1_ref[...])).astype(BF), w2_ref[...])
    w_log = -(jnp.maximum(-wl, 0.0) + jnp.log1p(jnp.exp(-jnp.abs(wl)))) - 0.5
    a = jax.nn.sigmoid(a0_ref[...] + _dot(_dot(xa, a1_ref[...]).astype(BF), a2_ref[...]))
    g = _dot(jax.nn.sigmoid(_dot(xg, g1_ref[...])).astype(BF), g2_ref[...])
    kk = k * kk_ref[...]
    nrm = jnp.sqrt(_seg_sum(kk * kk, e_ref, et_ref))
    kn = kk / jnp.maximum(nrm, 1e-12)
    k = k * (1.0 + (a - 1.0) * ka_ref[...])

    shp = x_ref.shape
    r_out[...] = r.reshape(shp)
    lw_out[...] = (-jnp.exp(w_log)).reshape(shp)
    k_out[...] = k.reshape(shp)
    v_out[...] = v.reshape(shp)
    kn_out[...] = kn.reshape(shp)
    a_out[...] = a.reshape(shp)
    g_out[...] = g.reshape(shp)

    @pl.when(s == ns - 1)
    def _():
        shift_out[...] = last


def rwkv_pre(x, mod, prev0, P):
    B, S, _ = x.shape
    bb, ts = _row_tile(B, S, 256)
    ns = S // ts
    xmap = lambda i, s: (i, s, 0)
    full = lambda shp: pl.BlockSpec(shp, lambda i, s: (0,) * len(shp))
    tok = pl.BlockSpec((bb, ts, D_MODEL), xmap)
    st = pl.BlockSpec((bb, 1, D_MODEL), lambda i, s: (i, 0, 0))
    sq, lo_in, lo_out, vec = (D_MODEL, D_MODEL), (D_MODEL, LORA_PAD), (LORA_PAD, D_MODEL), (1, D_MODEL)
    outs = pl.pallas_call(
        functools.partial(_rwkv_pre_kernel, ns=ns),
        out_shape=tuple([jax.ShapeDtypeStruct(x.shape, F32)] * 7
                        + [jax.ShapeDtypeStruct((B, 1, D_MODEL), F32)]),
        grid=(B // bb, ns),
        in_specs=[
            tok, _mod_spec(bb, 3), _mod_spec(bb, 4), st, full((6, D_MODEL)),
            full(sq), full(sq), full(sq), full(vec), full(lo_in), full(lo_out),
            full(vec), full(lo_in), full(lo_out), full(lo_in), full(lo_out),
            full(vec), full(vec), full((D_MODEL, LANES)), full((LANES, D_MODEL)),
        ],
        out_specs=tuple([tok] * 7 + [st]),
        scratch_shapes=[pltpu.VMEM((bb, 1, D_MODEL), F32)],
        compiler_params=_cparams(("parallel", "arbitrary")),
        name="rwkv_pre",
    )(x, mod, mod, prev0.reshape(B, 1, D_MODEL), P['mu'], P['wr'], P['wk'], P['wv'], P['w0'], P['w1'],
      P['w2'], P['a0'], P['a1'], P['a2'], P['g1'], P['g2'], P['k_k'], P['k_a'], P['seg'], P['seg_t'])
    return outs


def _stack_heads(x, first):
    return jnp.concatenate([jnp.where(first, x, 0.0), jnp.where(first, 0.0, x)], axis=0)


def _wkv_pair_chunks(toks, sps, consts, L):
    tri, first, strict, incl, eye = consts
    n = len(toks)
    rng = range(n)
    splits = [_split3(t[1]) for t in toks]
    c_incl = [_dot(tri, s[0]) + _dot(tri, s[1]) + _dot(tri, s[2]) for s in splits]
    e_inv = [jnp.exp(-c) for c in c_incl]
    la = [_stack_heads(-toks[p][4] * jnp.exp(c_incl[p] - toks[p][1]), first).astype(BF) for p in rng]
    lr = [_stack_heads(toks[p][0] * jnp.exp(c_incl[p]), first).astype(BF) for p in rng]
    rb = [_stack_heads(toks[p][4] * toks[p][5] * e_inv[p], first).astype(BF) for p in rng]
    rk = [_stack_heads(toks[p][2] * e_inv[p], first).astype(BF) for p in rng]
    vs = [_stack_heads(toks[p][3], first).astype(BF) for p in rng]

    n_ab = [jnp.where(strict, _dot_nt(la[p], rb[p]), 0.0) for p in rng]
    n_ak = [jnp.where(strict, _dot_nt(la[p], rk[p]), 0.0).astype(BF) for p in rng]
    m_rb = [jnp.where(incl, _dot_nt(lr[p], rb[p]), 0.0).astype(BF) for p in rng]
    m_rk = [jnp.where(incl, _dot_nt(lr[p], rk[p]), 0.0).astype(BF) for p in rng]

    xp = n_ab
    tinv = [eye + x for x in xp]
    lev = 2
    while lev < L:
        xb = [x.astype(BF) for x in xp]
        xp = [_dot(x, x) for x in xb]
        tinv = [tinv[p] + _dot(tinv[p].astype(BF), xp[p].astype(BF)) for p in rng]
        lev *= 2

    spb = [s.astype(BF) for s in sps]
    rhs = [_dot_nt(la[p], spb[p]) + _dot(n_ak[p], vs[p]) for p in rng]
    ub = [_dot(tinv[p].astype(BF), rhs[p].astype(BF)).astype(BF) for p in rng]
    ys = [_dot_nt(lr[p], spb[p]) + _dot(m_rb[p], ub[p]) + _dot(m_rk[p], vs[p]) for p in rng]
    sp_new = [(sps[p] + _dot_tn(ub[p], rb[p]) + _dot_tn(vs[p], rk[p])) * jnp.exp(c_incl[p][L - 1:L, :])
              for p in rng]
    return [(ys[p][0:L, :] + ys[p][L:2 * L, :], sp_new[p]) for p in rng]


def _wkv_kernel(r_ref, lw_ref, k_ref, v_ref, kn_ref, a_ref, s0_ref, y_out, s_out, sp_scr, *, L, nc, bb):
    c = pl.program_id(1)
    hs = RWKV_HS
    npair = RWKV_HEADS // 2
    L2 = 2 * L

    @pl.when(c == 0)
    def _():
        zero = jnp.zeros((hs, hs), F32)

        def init(i, carry):
            for p in range(npair):
                top = jnp.concatenate([s0_ref[i, 2 * p], zero], axis=1)
                bot = jnp.concatenate([zero, s0_ref[i, 2 * p + 1]], axis=1)
                sp_scr[i, p] = jnp.concatenate([top, bot], axis=0)
            return carry

        lax.fori_loop(0, bb, init, 0)

    row = lax.broadcasted_iota(jnp.int32, (L, L), 0)
    col = lax.broadcasted_iota(jnp.int32, (L, L), 1)
    tri = jnp.where(col <= row, 1.0, 0.0).astype(BF)
    first = lax.broadcasted_iota(jnp.int32, (L, LANES), 1) < hs
    row2 = lax.broadcasted_iota(jnp.int32, (L2, L2), 0)
    col2 = lax.broadcasted_iota(jnp.int32, (L2, L2), 1)
    same = (row2 >= L) == (col2 >= L)
    consts = (tri, first, same & (col2 < row2), same & (col2 <= row2), jnp.where(row2 == col2, 1.0, 0.0))

    def seq(i, carry):
        sls = [slice(p * LANES, (p + 1) * LANES) for p in range(npair)]
        toks = [tuple(ref[i, :, sl] for ref in (r_ref, lw_ref, k_ref, v_ref, kn_ref, a_ref)) for sl in sls]
        outs = _wkv_pair_chunks(toks, [sp_scr[i, p] for p in range(npair)], consts, L)
        for p, (y, sp_new) in enumerate(outs):
            y_out[i, :, sls[p]] = y
            sp_scr[i, p] = sp_new
        return carry

    lax.fori_loop(0, bb, seq, 0)

    @pl.when(c == nc - 1)
    def _():
        def fin(i, carry):
            for p in range(npair):
                sp = sp_scr[i, p]
                s_out[i, 2 * p] = sp[0:hs, 0:hs]
                s_out[i, 2 * p + 1] = sp[hs:2 * hs, hs:2 * hs]
            return carry

        lax.fori_loop(0, bb, fin, 0)


def wkv_recurrence(r, lw, k, v, kn, a, s0):
    B, S, _ = r.shape
    L = RWKV_CHUNK if S % RWKV_CHUNK == 0 else S
    assert L & (L - 1) == 0 and L % SUBLANES == 0
    nc = S // L
    bb = 1 if nc > 1 else min(B, 8)
    npair = RWKV_HEADS // 2
    tok = pl.BlockSpec((bb, L, D_MODEL), lambda b, c: (b, c, 0))
    st = pl.BlockSpec((bb, RWKV_HEADS, RWKV_HS, RWKV_HS), lambda b, c: (b, 0, 0, 0))
    return pl.pallas_call(
        functools.partial(_wkv_kernel, L=L, nc=nc, bb=bb),
        out_shape=(jax.ShapeDtypeStruct(r.shape, F32), jax.ShapeDtypeStruct(s0.shape, F32)),
        grid=(B // bb, nc),
        in_specs=[tok] * 6 + [st],
        out_specs=(tok, st),
        scratch_shapes=[pltpu.VMEM((bb, npair, 2 * RWKV_HS, 2 * RWKV_HS), F32)],
        compiler_params=_cparams(("parallel", "arbitrary")),
        name="wkv",
    )(r, lw, k, v, kn, a, s0)


def _rwkv_post_kernel(x_ref, gt_ref, y_ref, r_ref, k_ref, v_ref, g_ref, rk_ref, lxg_ref, lxb_ref,
                      wo_ref, e_ref, et_ref, lg_ref, lb_ref, o_ref):
    bb, ts, _ = x_ref.shape
    flat = lambda ref: ref[...].reshape(bb * ts, D_MODEL)
    y = flat(y_ref)
    inv_n = 1.0 / RWKV_HS
    yc = y - _seg_sum(y, e_ref, et_ref) * inv_n
    yv = _seg_sum(yc * yc, e_ref, et_ref) * inv_n
    yn = yc * lax.rsqrt(yv + RWKV_LN_EPS) * lxg_ref[...] + lxb_ref[...]
    bonus = _seg_sum(flat(r_ref) * flat(k_ref) * rk_ref[...], e_ref, et_ref) * flat(v_ref)
    out = _dot(((yn + bonus) * flat(g_ref)).astype(BF), wo_ref[...])
    z = ALPHA * x_ref[...] + (1.0 + gt_ref[...]) * out.reshape(bb, ts, D_MODEL)
    o_ref[...] = _layer_norm(z, lg_ref[...], lb_ref[...])


def rwkv_post(x, mod, y, r, k, v, g, P, lg, lb):
    B, S, _ = x.shape
    bb, ts = _row_tile(B, S, 256)
    xmap = lambda i, s: (i, s, 0)
    full = lambda shp: pl.BlockSpec(shp, lambda i, s: (0,) * len(shp))
    tok = pl.BlockSpec((bb, ts, D_MODEL), xmap)
    vec = (1, D_MODEL)
    return pl.pallas_call(
        _rwkv_post_kernel,
        out_shape=jax.ShapeDtypeStruct(x.shape, F32),
        grid=(B // bb, S // ts),
        in_specs=[tok, _mod_spec(bb, 5), tok, tok, tok, tok, tok, full(vec), full(vec), full(vec),
                  full((D_MODEL, D_MODEL)), full((D_MODEL, LANES)), full((LANES, D_MODEL)),
                  full(vec), full(vec)],
        out_specs=tok,
        compiler_params=_cparams(("parallel", "parallel")),
        name="rwkv_post",
    )(x, mod, y, r, k, v, g, P['r_k'], P['lnx_g'], P['lnx_b'], P['wo'], P['seg'], P['seg_t'], lg, lb)


def _block_diag(w):
    G, n, _ = w.shape
    eye = jnp.eye(G, dtype=w.dtype)
    return (eye[:, None, :, None] * w[:, :, None, :]).reshape(G * n, G * n)


def _pad_cols(w, n):
    return jnp.pad(w, ((0, 0), (0, n - w.shape[1])))


def _pad_rows(w, n):
    return jnp.pad(w, ((0, n - w.shape[0]), (0, 0)))


def _prep_ab(j, p):
    w_in = p['ab_w_in'][j]
    n_main = 4 * MLSTM_WIDTH
    n_gate = 2 * MLSTM_HEADS
    w_all = jnp.concatenate([w_in[:, :n_main], w_in[:, n_main + n_gate:],
                             w_in[:, n_main:n_main + n_gate]], axis=1)
    return dict(
        w_in=_pad_cols(w_all, Z_COLS).astype(BF),
        b_gates=p['mlstm_b_gates'][j], norm_g=p['mlstm_norm_g'][j],
        conv_w=p['rg_conv_w'][j], conv_b=p['rg_conv_b'][j],
        wa=_block_diag(p['rg_w_a'][j]).astype(BF), wx=_block_diag(p['rg_w_x'][j]).astype(BF),
        b_a=p['rg_b_a'][j], b_x=p['rg_b_x'][j], lam=p['rg_lambda'][j],
        w_out_m=p['ab_w_out'][j][:MLSTM_WIDTH].astype(BF), w_out_r=p['ab_w_out'][j][MLSTM_WIDTH:].astype(BF),
    )


def _prep_rwkv(j, p):
    vec = lambda a: a.reshape(1, D_MODEL)
    head = jnp.arange(D_MODEL) // RWKV_HS
    seg = (head[:, None] == jnp.arange(LANES)[None, :]).astype(BF)
    return dict(
        mu=p['rw_mu'][j],
        wr=p['rw_wr'][j].astype(BF), wk=p['rw_wk'][j].astype(BF), wv=p['rw_wv'][j].astype(BF),
        w0=vec(p['rw_w0'][j]), w1=_pad_cols(p['rw_w1'][j], LORA_PAD).astype(BF),
        w2=_pad_rows(p['rw_w2'][j], LORA_PAD).astype(BF),
        a0=vec(p['rw_a0'][j]), a1=_pad_cols(p['rw_a1'][j], LORA_PAD).astype(BF),
        a2=_pad_rows(p['rw_a2'][j], LORA_PAD).astype(BF),
        g1=_pad_cols(p['rw_g1'][j], LORA_PAD).astype(BF), g2=_pad_rows(p['rw_g2'][j], LORA_PAD).astype(BF),
        k_k=vec(p['rw_k_k'][j]), k_a=vec(p['rw_k_a'][j]), r_k=vec(p['rw_r_k'][j]),
        lnx_g=vec(p['rw_lnx_g'][j]), lnx_b=vec(p['rw_lnx_b'][j]), wo=p['rw_wo'][j].astype(BF),
        seg=seg, seg_t=seg.T,
    )


def ab_mixer(x, mod, st, A, lg, lb):
    mC, mn, mm, rh, rconv = st
    z = mod_matmul(x, mod, A['w_in'])
    gates = z[:, :, 6 * MLSTM_WIDTH:6 * MLSTM_WIDTH + 2 * MLSTM_HEADS]
    hm, C1, n1, m1 = mlstm_mixer(z, gates, mC, mn, mm, A['b_gates'], A['norm_g'])
    hr, rh1, buf1 = rglru_mixer(z, rh, rconv, A['conv_w'], A['conv_b'], A['wa'], A['wx'],
                                A['b_a'], A['b_x'], A['lam'])
    x = proj2_post(x, mod, hm, hr, A['w_out_m'], A['w_out_r'], lg, lb)
    return x, (C1, n1, m1, rh1, buf1)


def rwkv_mixer(x, mod, st, R, lg, lb):
    wkv0, prev0 = st
    r, lw, k, v, kn, a, g, shift = rwkv_pre(x, mod, prev0, R)
    y, wkv1 = wkv_recurrence(r, lw, k, v, kn, a, wkv0)
    x = rwkv_post(x, mod, y, r, k, v, g, R, lg, lb)
    return x, (wkv1, shift.reshape(shift.shape[0], D_MODEL))


def run_trunk(x, mods, states, W):
    mC, mn, mm, rh, rconv, wkv, shift = states
    new_ab, new_c = [], []
    for layer in range(DEPTH):
        mod = mods[layer]
        lg = lambda i: W['ln_g'][layer, i].reshape(1, D_MODEL)
        lb = lambda i: W['ln_b'][layer, i].reshape(1, D_MODEL)
        f = W['ffn'][layer]
        x = ffn_block(x, mod, 0, f[0][0], f[0][1], f[0][2], lg(0), lb(0))
        j = layer // 2
        if layer % 2 == 0:
            x, st = ab_mixer(x, mod, (mC[j], mn[j], mm[j], rh[j], rconv[j]), W['ab'][j], lg(1), lb(1))
            new_ab.append(st)
        else:
            x, st = rwkv_mixer(x, mod, (wkv[j], shift[j]), W['rwkv'][j], lg(1), lb(1))
            new_c.append(st)
        x = ffn_block(x, mod, 2, f[1][0], f[1][1], f[1][2], lg(2), lb(2))
    stk = lambda sts, i: jnp.stack([s[i] for s in sts], axis=0)
    return x, (stk(new_ab, 0), stk(new_ab, 1), stk(new_ab, 2), stk(new_ab, 3), stk(new_ab, 4),
               stk(new_c, 0), stk(new_c, 1))


def _zero_states(B):
    n_ab, n_c = (DEPTH + 1) // 2, DEPTH // 2
    return (jnp.zeros((n_ab, B, MLSTM_HEADS, MLSTM_HD, MLSTM_HD), F32),
            jnp.zeros((n_ab, B, MLSTM_HEADS, MLSTM_HD), F32),
            jnp.zeros((n_ab, B, MLSTM_HEADS), F32),
            jnp.zeros((n_ab, B, RG_WIDTH), F32),
            jnp.zeros((n_ab, B, RG_CONV - 1, RG_WIDTH), F32),
            jnp.zeros((n_c, B, RWKV_HEADS, RWKV_HS, RWKV_HS), F32),
            jnp.zeros((n_c, B, D_MODEL), F32))


def kernel(x_prompt, x_sample, c_prompt, c_sample, state_mlstm_C, state_mlstm_n, state_mlstm_m, state_rglru_h, state_rglru_conv, state_rwkv_wkv, state_rwkv_shift, ada_w, ada_b, ln_g, ln_b, ffn_w1, ffn_w3, ffn_w2, ab_w_in, mlstm_b_gates, mlstm_norm_g, rg_conv_w, rg_conv_b, rg_w_a, rg_b_a, rg_w_x, rg_b_x, rg_lambda, ab_w_out, rw_mu, rw_wr, rw_wk, rw_wv, rw_w0, rw_w1, rw_w2, rw_a0, rw_a1, rw_a2, rw_g1, rw_g2, rw_k_k, rw_k_a, rw_r_k, rw_lnx_g, rw_lnx_b, rw_wo):
    p = dict(ab_w_in=ab_w_in, mlstm_b_gates=mlstm_b_gates, mlstm_norm_g=mlstm_norm_g,
             rg_conv_w=rg_conv_w, rg_conv_b=rg_conv_b, rg_w_a=rg_w_a, rg_b_a=rg_b_a, rg_w_x=rg_w_x,
             rg_b_x=rg_b_x, rg_lambda=rg_lambda, ab_w_out=ab_w_out, rw_mu=rw_mu, rw_wr=rw_wr,
             rw_wk=rw_wk, rw_wv=rw_wv, rw_w0=rw_w0, rw_w1=rw_w1, rw_w2=rw_w2, rw_a0=rw_a0,
             rw_a1=rw_a1, rw_a2=rw_a2, rw_g1=rw_g1, rw_g2=rw_g2, rw_k_k=rw_k_k, rw_k_a=rw_k_a,
             rw_r_k=rw_r_k, rw_lnx_g=rw_lnx_g, rw_lnx_b=rw_lnx_b, rw_wo=rw_wo)
    W = dict(
        ln_g=ln_g, ln_b=ln_b,
        ffn=[[(ffn_w1[l, i].astype(BF), ffn_w3[l, i].astype(BF), ffn_w2[l, i].astype(BF))
              for i in range(2)] for l in range(DEPTH)],
        ab=[_prep_ab(j, p) for j in range((DEPTH + 1) // 2)],
        rwkv=[_prep_rwkv(j, p) for j in range(DEPTH // 2)],
    )
    Bp, Bs = x_prompt.shape[0], x_sample.shape[0]
    mod_all = adaln(jnp.concatenate([c_prompt, c_sample], axis=0), ada_w, ada_b)
    mods_p = [mod_all[l, :Bp].reshape(Bp, 3 * N_SUB, 1, D_MODEL) for l in range(DEPTH)]
    mods_s = [mod_all[l, Bp:].reshape(Bs, 3 * N_SUB, 1, D_MODEL) for l in range(DEPTH)]
    y_prompt, sp = run_trunk(x_prompt, mods_p, _zero_states(Bp), W)
    y_sample, ss = run_trunk(x_sample, mods_s,
                             (state_mlstm_C, state_mlstm_n, state_mlstm_m, state_rglru_h,
                              state_rglru_conv, state_rwkv_wkv, state_rwkv_shift), W)
    return (y_prompt, y_sample) + tuple(sp) + tuple(ss)
```

```python
import functools

import jax
import jax.numpy as jnp
from jax import lax
from jax.experimental import pallas as pl
from jax.experimental.pallas import tpu as pltpu

D_MODEL = 1024
DEPTH = 2
N_SUB = 3
MLSTM_WIDTH = 512
MLSTM_HEADS = 4
MLSTM_HD = 128
MLSTM_CHUNK = 128
RG_WIDTH = 512
RG_BLOCKS = 8
RG_BD = 64
RG_CONV = 4
RG_C = 8.0
RWKV_HS = 64
RWKV_HEADS = 16
RWKV_LN_EPS = 64e-5
RWKV_CHUNK = 64
INV_BASE = 8
D_FF = 2816
ALPHA = (2.0 * DEPTH) ** 0.25
LN_EPS = 1e-5
HEAD_NORM_EPS = 1e-6

LANES = 128
SUBLANES = 8
VMEM_LIMIT_BYTES = 56 * 1024 * 1024
Z_COLS = 3200
LORA_PAD = 128

BF = jnp.bfloat16
F32 = jnp.float32

NT_DIMS = (((1,), (1,)), ((), ()))
TN_DIMS = (((0,), (0,)), ((), ()))


def _cparams(sem):
    return pltpu.CompilerParams(dimension_semantics=sem, vmem_limit_bytes=VMEM_LIMIT_BYTES)


def _dot(a, b):
    return jnp.dot(a, b, preferred_element_type=F32)


def _dot_nt(a, b):
    return lax.dot_general(a, b, NT_DIMS, preferred_element_type=F32)


def _dot_tn(a, b):
    return lax.dot_general(a, b, TN_DIMS, preferred_element_type=F32)


def _layer_norm(z, g, b):
    mu = jnp.mean(z, axis=-1, keepdims=True)
    zc = z - mu
    var = jnp.mean(zc * zc, axis=-1, keepdims=True)
    return zc * lax.rsqrt(var + LN_EPS) * g + b


def _row_tile(B, S, rows):
    if S >= rows:
        assert S % rows == 0
        return 1, rows
    bb = min(B, rows // S)
    assert B % bb == 0
    return bb, S


def _split2(x):
    hi = x.astype(BF)
    lo = (x - hi.astype(F32)).astype(BF)
    return hi, lo


def _split3(x):
    hi = x.astype(BF)
    r1 = x - hi.astype(F32)
    mid = r1.astype(BF)
    lo = (r1 - mid.astype(F32)).astype(BF)
    return hi, mid, lo


def _adaln_kernel(c_ref, w_ref, b_ref, o_ref):
    h = jax.nn.silu(c_ref[...]).astype(BF)
    o_ref[...] = _dot(h, w_ref[...].astype(BF)) + b_ref[...]


def adaln(c_all, ada_w, ada_b):
    Bc = c_all.shape[0]
    n_out = ada_w.shape[-1]
    tn = 1024
    return pl.pallas_call(
        _adaln_kernel,
        out_shape=jax.ShapeDtypeStruct((DEPTH, Bc, n_out), F32),
        grid=(DEPTH, n_out // tn),
        in_specs=[
            pl.BlockSpec((Bc, D_MODEL), lambda l, j: (0, 0)),
            pl.BlockSpec((None, D_MODEL, tn), lambda l, j: (l, 0, j)),
            pl.BlockSpec((None, 1, tn), lambda l, j: (l, 0, j)),
        ],
        out_specs=pl.BlockSpec((None, Bc, tn), lambda l, j: (l, 0, j)),
        compiler_params=_cparams(("parallel", "parallel")),
        name="adaln",
    )(c_all, ada_w, ada_b.reshape(DEPTH, 1, n_out))


def _mod_spec(bb, k):
    return pl.BlockSpec((bb, None, 1, D_MODEL), lambda i, *_: (i, k, 0, 0))


FFN_CHUNKS = 2


def _ffn_kernel(x_ref, sh_ref, sc_ref, gt_ref, w1_ref, w3_ref, w2_ref, lg_ref, lb_ref, o_ref, *, res_w):
    bb, ts, _ = x_ref.shape
    x = x_ref[...]
    h = (x * (1.0 + sc_ref[...]) + sh_ref[...]).reshape(bb * ts, D_MODEL).astype(BF)
    tf = D_FF // FFN_CHUNKS
    y = None
    for c in range(FFN_CHUNKS):
        cols = slice(c * tf, (c + 1) * tf)
        a = _dot(h, w1_ref[:, cols])
        b = _dot(h, w3_ref[:, cols])
        g = (jax.nn.silu(a) * b).astype(BF)
        yc = _dot(g, w2_ref[cols, :])
        y = yc if y is None else y + yc
    z = ALPHA * x + (res_w * (1.0 + gt_ref[...])) * y.reshape(bb, ts, D_MODEL)
    o_ref[...] = _layer_norm(z, lg_ref[...], lb_ref[...])


def _resident(shape):
    return pl.BlockSpec(shape, lambda *_: (0,) * len(shape), pipeline_mode=pl.Buffered(1))


def ffn_block(x, mod, sub, w1, w3, w2, lg, lb):
    B, S, _ = x.shape
    bb, ts = _row_tile(B, S, 512)
    assert (D_FF // FFN_CHUNKS) % LANES == 0
    xmap = lambda i, s: (i, s, 0)
    return pl.pallas_call(
        functools.partial(_ffn_kernel, res_w=0.5),
        out_shape=jax.ShapeDtypeStruct(x.shape, F32),
        grid=(B // bb, S // ts),
        in_specs=[
            pl.BlockSpec((bb, ts, D_MODEL), xmap),
            _mod_spec(bb, 3 * sub), _mod_spec(bb, 3 * sub + 1), _mod_spec(bb, 3 * sub + 2),
            _resident((D_MODEL, D_FF)), _resident((D_MODEL, D_FF)), _resident((D_FF, D_MODEL)),
            _resident((1, D_MODEL)), _resident((1, D_MODEL)),
        ],
        out_specs=pl.BlockSpec((bb, ts, D_MODEL), xmap),
        compiler_params=_cparams(("parallel", "parallel")),
        name="ffn_block",
    )(x, mod, mod, mod, w1, w3, w2, lg, lb)


def _modmm_kernel(x_ref, sh_ref, sc_ref, w_ref, o_ref):
    bb, ts, _ = x_ref.shape
    h = (x_ref[...] * (1.0 + sc_ref[...]) + sh_ref[...]).reshape(bb * ts, D_MODEL).astype(BF)
    o_ref[...] = _dot(h, w_ref[...]).reshape(o_ref.shape)


def mod_matmul(x, mod, w):
    B, S, _ = x.shape
    n_out = w.shape[1]
    bb, ts = _row_tile(B, S, 512)
    xmap = lambda i, s: (i, s, 0)
    return pl.pallas_call(
        _modmm_kernel,
        out_shape=jax.ShapeDtypeStruct((B, S, n_out), F32),
        grid=(B // bb, S // ts),
        in_specs=[
            pl.BlockSpec((bb, ts, D_MODEL), xmap),
            _mod_spec(bb, 3), _mod_spec(bb, 4),
            _resident((D_MODEL, n_out)),
        ],
        out_specs=pl.BlockSpec((bb, ts, n_out), xmap),
        compiler_params=_cparams(("parallel", "parallel")),
        name="mod_matmul",
    )(x, mod, mod, w)


def _log_sigmoid(x):
    return jnp.minimum(x, 0.0) - jnp.log1p(jnp.exp(-jnp.abs(x)))


def _lockstep(gens):
    results = [None] * len(gens)
    live = list(range(len(gens)))
    while live:
        still = []
        for idx in live:
            try:
                next(gens[idx])
                still.append(idx)
            except StopIteration as stop:
                results[idx] = stop.value
        live = still
    return results


def _mlstm_head_chunk(q, k, v, o, gc, gr, b_i, b_f, C0, n0, m0, g, L):
    ig_c = gc[:, 0:1] + b_i
    lf_c = _log_sigmoid(gc[:, 1:2] + b_f)
    ig_r = gr[0:1, :] + b_i
    lf_r = _log_sigmoid(gr[1:2, :] + b_f)

    row = lax.broadcasted_iota(jnp.int32, (L, L), 0)
    col = lax.broadcasted_iota(jnp.int32, (L, L), 1)
    causal = col <= row
    b_c = jnp.sum(jnp.where(causal, lf_r, 0.0), axis=1, keepdims=True)
    b_r = jnp.sum(jnp.where(row <= col, lf_c, 0.0), axis=0, keepdims=True)
    dmat = jnp.where(causal, b_c - b_r + ig_r, -jnp.inf)
    m_inter = b_c + m0
    m = jnp.maximum(m_inter, jnp.max(dmat, axis=1, keepdims=True))
    w_inter = jnp.exp(m_inter - m)
    p = jnp.exp(dmat - m)

    k = k * (MLSTM_HD ** -0.5)
    qb = q.astype(BF)
    kb = k.astype(BF)
    yield
    s_raw = _dot_nt(qb, kb)
    q_c0 = _dot_nt(qb, C0.astype(BF))
    yield
    scores = s_raw * p
    num = _dot(scores.astype(BF), v.astype(BF)) + w_inter * q_c0
    den = jnp.sum(scores, axis=1, keepdims=True) + w_inter * jnp.sum(q * n0, axis=1, keepdims=True)
    hh = num / jnp.maximum(jnp.abs(den), jnp.exp(-m))

    m_end = m[L - 1:L, :]
    b_end = b_c[L - 1:L, :]
    w_state = jnp.exp(b_end + m0 - m_end)
    w_rows = jnp.exp(b_end - b_c + ig_c - m_end)
    yield
    c_new = w_state * C0 + _dot_tn((w_rows * v).astype(BF), kb)
    n_new = w_state * n0 + jnp.sum(w_rows * k, axis=0, keepdims=True)

    mu = jnp.mean(hh, axis=1, keepdims=True)
    hc = hh - mu
    var = jnp.mean(hc * hc, axis=1, keepdims=True)
    hn = hc * lax.rsqrt(var + HEAD_NORM_EPS)
    return hn * g * jax.nn.sigmoid(o), c_new, n_new, m_end


def _mlstm_kernel(m0_ref, bg_ref, q_ref, k_ref, v_ref, o_ref, gc_ref, gr_ref, c0_ref, n0_ref, g_ref,
                  h_out, c_out, n_out, m_out, c_scr, n_scr, m_scr, *, L, nc, bb):
    ib = pl.program_id(0)
    c = pl.program_id(1)
    H = MLSTM_HEADS

    @pl.when(c == 0)
    def _():
        c_scr[...] = c0_ref[...]
        n_scr[...] = n0_ref[...]

        def init(i, carry):
            for h in range(H):
                m_scr[i, h] = jnp.full((1, LANES), m0_ref[(ib * bb + i) * H + h], F32)
            return carry

        lax.fori_loop(0, bb, init, 0)

    def seq(i, carry):
        sls = [slice(h * MLSTM_HD, (h + 1) * MLSTM_HD) for h in range(H)]
        heads = _lockstep([_mlstm_head_chunk(
            q_ref[i, :, sls[h]], k_ref[i, :, sls[h]], v_ref[i, :, sls[h]], o_ref[i, :, sls[h]], gc_ref[i, h],
            gr_ref[i, h], bg_ref[h], bg_ref[H + h], c_scr[i, h], n_scr[i, h], m_scr[i, h][:, 0:1],
            g_ref[:, sls[h]], L) for h in range(H)])
        for h, (out, c_new, n_new, m_end) in enumerate(heads):
            h_out[i, :, sls[h]] = out
            c_scr[i, h] = c_new
            n_scr[i, h] = n_new
            m_scr[i, h] = jnp.broadcast_to(m_end, (1, LANES))
        return carry

    lax.fori_loop(0, bb, seq, 0)

    @pl.when(c == nc - 1)
    def _():
        c_out[...] = c_scr[...]
        n_out[...] = n_scr[...]
        m_out[...] = m_scr[...]


def mlstm_mixer(z, gates, C0, n0, m0, b_gates, norm_g):
    B, S, _ = z.shape
    H = MLSTM_HEADS
    W = MLSTM_WIDTH
    L = MLSTM_CHUNK if S % MLSTM_CHUNK == 0 else S
    nc = S // L
    bb = 1 if nc > 1 else min(B, 8)
    gcol = gates.reshape(B, S, 2, H).transpose(0, 3, 1, 2)
    grow = gates.reshape(B, S, 2, H).transpose(0, 3, 2, 1)
    zspec = lambda blk: pl.BlockSpec((bb, L, W), lambda b, c: (b, c, blk))
    smem = pl.BlockSpec(memory_space=pltpu.SMEM)
    hd_spec = pl.BlockSpec((bb, H, 1, MLSTM_HD), lambda b, c: (b, 0, 0, 0))
    c_spec = pl.BlockSpec((bb, H, MLSTM_HD, MLSTM_HD), lambda b, c: (b, 0, 0, 0))
    hm, C1, n1, m1 = pl.pallas_call(
        functools.partial(_mlstm_kernel, L=L, nc=nc, bb=bb),
        out_shape=(
            jax.ShapeDtypeStruct((B, S, W), F32),
            jax.ShapeDtypeStruct((B, H, MLSTM_HD, MLSTM_HD), F32),
            jax.ShapeDtypeStruct((B, H, 1, MLSTM_HD), F32),
            jax.ShapeDtypeStruct((B, H, 1, LANES), F32),
        ),
        grid=(B // bb, nc),
        in_specs=[
            smem, smem,
            zspec(0), zspec(1), zspec(2), zspec(3),
            pl.BlockSpec((bb, H, L, 2), lambda b, c: (b, 0, c, 0)),
            pl.BlockSpec((bb, H, 2, L), lambda b, c: (b, 0, 0, c)),
            c_spec, hd_spec,
            pl.BlockSpec((1, W), lambda b, c: (0, 0)),
        ],
        out_specs=(pl.BlockSpec((bb, L, W), lambda b, c: (b, c, 0)), c_spec, hd_spec, hd_spec),
        scratch_shapes=[pltpu.VMEM((bb, H, MLSTM_HD, MLSTM_HD), F32), pltpu.VMEM((bb, H, 1, MLSTM_HD), F32),
                        pltpu.VMEM((bb, H, 1, LANES), F32)],
        compiler_params=_cparams(("parallel", "arbitrary")),
        name="mlstm",
    )(m0.reshape(B * H), b_gates.reshape(2 * H), z, z, z, z, gcol, grow,
      C0, n0.reshape(B, H, 1, MLSTM_HD), norm_g.reshape(1, W))
    return hm, C1, n1.reshape(B, H, MLSTM_HD), m1[:, :, 0, 0]


CONV_PAD = SUBLANES


def _expm1(y):
    u = jnp.exp(y)
    small = jnp.where(u == 1.0, y, (u - 1.0) * y / jnp.log(u))
    return jnp.where(jnp.abs(y) > 0.5, u - 1.0, small)


def _rglru_kernel(xr_ref, gr_ref, buf0_ref, h0_ref, cw_ref, cb_ref, wa_ref, wx_ref, ba_ref, bx_ref,
                  lam_ref, hr_out, h_out, buf_out, xp_scr, a_scr, u_scr, hs_scr, h_scr, *, ts, ns):
    s = pl.program_id(1)
    lo = CONV_PAD - (RG_CONV - 1)

    @pl.when(s == 0)
    def _():
        xp_scr[lo:CONV_PAD, :] = buf0_ref[...]
        h_scr[...] = h0_ref[...]

    xp_scr[CONV_PAD:CONV_PAD + ts, :] = xr_ref[...]
    cw = cw_ref[...]
    xc = cb_ref[...] + cw[0:1, :] * xp_scr[lo:lo + ts, :]
    for j in range(1, RG_CONV):
        xc = xc + cw[j:j + 1, :] * xp_scr[lo + j:lo + j + ts, :]
    hist = xp_scr[ts + lo:ts + CONV_PAD, :]
    xp_scr[lo:CONV_PAD, :] = hist

    xb = xc.astype(BF)
    r = jax.nn.sigmoid(_dot(xb, wa_ref[...]) + ba_ref[...])
    i = jax.nn.sigmoid(_dot(xb, wx_ref[...]) + bx_ref[...])
    lam = lam_ref[...]
    softplus_neg = jnp.maximum(-lam, 0.0) + jnp.log1p(jnp.exp(-jnp.abs(lam)))
    log_a = (-RG_C * softplus_neg) * r
    a_scr[...] = jnp.exp(log_a)
    u_scr[...] = jnp.sqrt(-_expm1(2.0 * log_a)) * i * xc

    def body(blk, h):
        r0 = pl.multiple_of(blk * SUBLANES, SUBLANES)
        a8 = a_scr[pl.ds(r0, SUBLANES), :]
        u8 = u_scr[pl.ds(r0, SUBLANES), :]
        rows = []
        for t in range(SUBLANES):
            h = a8[t:t + 1, :] * h + u8[t:t + 1, :]
            rows.append(h)
        hs_scr[pl.ds(r0, SUBLANES), :] = jnp.concatenate(rows, axis=0)
        return h

    h_last = lax.fori_loop(0, ts // SUBLANES, body, h_scr[...])
    h_scr[...] = h_last
    hr_out[...] = hs_scr[...] * jax.nn.gelu(gr_ref[...])

    @pl.when(s == ns - 1)
    def _():
        h_out[...] = h_last
        buf_out[...] = hist


def rglru_mixer(z, h0, buf0, conv_w, conv_b, wa_bd, wx_bd, b_a, b_x, lam):
    B, S, _ = z.shape
    W = RG_WIDTH
    ts = min(S, 512)
    ns = S // ts
    xr_blk = (4 * MLSTM_WIDTH) // W
    row = lambda a: a.reshape(1, W)
    full = lambda shp: pl.BlockSpec(shp, lambda b, s: (0,) * len(shp))
    hr, h1, buf1 = pl.pallas_call(
        functools.partial(_rglru_kernel, ts=ts, ns=ns),
        out_shape=(
            jax.ShapeDtypeStruct((B, S, W), F32),
            jax.ShapeDtypeStruct((B, 1, W), F32),
            jax.ShapeDtypeStruct((B, RG_CONV - 1, W), F32),
        ),
        grid=(B, ns),
        in_specs=[
            pl.BlockSpec((None, ts, W), lambda b, s: (b, s, xr_blk)),
            pl.BlockSpec((None, ts, W), lambda b, s: (b, s, xr_blk + 1)),
            pl.BlockSpec((None, RG_CONV - 1, W), lambda b, s: (b, 0, 0)),
            pl.BlockSpec((None, 1, W), lambda b, s: (b, 0, 0)),
            full((RG_CONV, W)), full((1, W)), full((W, W)), full((W, W)),
            full((1, W)), full((1, W)), full((1, W)),
        ],
        out_specs=(
            pl.BlockSpec((None, ts, W), lambda b, s: (b, s, 0)),
            pl.BlockSpec((None, 1, W), lambda b, s: (b, 0, 0)),
            pl.BlockSpec((None, RG_CONV - 1, W), lambda b, s: (b, 0, 0)),
        ),
        scratch_shapes=[pltpu.VMEM((CONV_PAD + ts, W), F32), pltpu.VMEM((ts, W), F32),
                        pltpu.VMEM((ts, W), F32), pltpu.VMEM((ts, W), F32), pltpu.VMEM((1, W), F32)],
        compiler_params=_cparams(("parallel", "arbitrary")),
        name="rglru",
    )(z, z, buf0, h0.reshape(B, 1, W), conv_w, row(conv_b), wa_bd, wx_bd, row(b_a), row(b_x), row(lam))
    return hr, h1.reshape(B, W), buf1


def _proj2_post_kernel(x_ref, gt_ref, a1_ref, a2_ref, w1_ref, w2_ref, lg_ref, lb_ref, o_ref):
    bb, ts, _ = x_ref.shape
    a1 = a1_ref[...].reshape(bb * ts, -1).astype(BF)
    a2 = a2_ref[...].reshape(bb * ts, -1).astype(BF)
    y = _dot(a1, w1_ref[...]) + _dot(a2, w2_ref[...])
    z = ALPHA * x_ref[...] + (1.0 + gt_ref[...]) * y.reshape(bb, ts, D_MODEL)
    o_ref[...] = _layer_norm(z, lg_ref[...], lb_ref[...])


def proj2_post(x, mod, a1, a2, w1, w2, lg, lb):
    B, S, _ = x.shape
    bb, ts = _row_tile(B, S, 512)
    K1, K2 = a1.shape[-1], a2.shape[-1]
    xmap = lambda i, s: (i, s, 0)
    full = lambda shp: pl.BlockSpec(shp, lambda i, s: (0,) * len(shp))
    return pl.pallas_call(
        _proj2_post_kernel,
        out_shape=jax.ShapeDtypeStruct(x.shape, F32),
        grid=(B // bb, S // ts),
        in_specs=[
            pl.BlockSpec((bb, ts, D_MODEL), xmap), _mod_spec(bb, 5),
            pl.BlockSpec((bb, ts, K1), xmap), pl.BlockSpec((bb, ts, K2), xmap),
            full((K1, D_MODEL)), full((K2, D_MODEL)), full((1, D_MODEL)), full((1, D_MODEL)),
        ],
        out_specs=pl.BlockSpec((bb, ts, D_MODEL), xmap),
        compiler_params=_cparams(("parallel", "parallel")),
        name="proj2_post",
    )(x, mod, a1, a2, w1, w2, lg, lb)


def _rwkv_pre_kernel(x_ref, sh_ref, sc_ref, prev0_ref, mu_ref, wr_ref, wk_ref, wv_ref, w0_ref, w1_ref,
                     w2_ref, a0_ref, a1_ref, a2_ref, g1_ref, g2_ref, kk_ref, ka_ref,
                     r_out, lw_out, k_out, v_out, kk_out, a_out, g_out, shift_out, carry_scr, *, ns):
    s = pl.program_id(1)
    bb, ts, _ = x_ref.shape
    tm = bb * ts

    @pl.when(s == 0)
    def _():
        carry_scr[...] = prev0_ref[...]

    h = x_ref[...] * (1.0 + sc_ref[...]) + sh_ref[...]
    t_idx = lax.broadcasted_iota(jnp.int32, h.shape, 1)
    h_prev = jnp.where(t_idx == 0, carry_scr[...], pltpu.roll(h, 1, 1))
    last = h[:, ts - 1:ts, :]
    carry_scr[...] = last
    dx = (h_prev - h).reshape(tm, D_MODEL)
    h2 = h.reshape(tm, D_MODEL)
    mu = mu_ref[...]
    mix = lambda j: (h2 + dx * mu[j:j + 1, :]).astype(BF)
    xr, xw, xk, xv, xa, xg = [mix(j) for j in range(6)]

    r = _dot(xr, wr_ref[...])
    k = _dot(xk, wk_ref[...])
    v = _dot(xv, wv_ref[...])
    wl = w0_ref[...] + _dot(jnp.tanh(_dot(xw, w1_ref[...])).astype(BF), w2_ref[...])
    w_log = -(jnp.maximum(-wl, 0.0) + jnp.log1p(jnp.exp(-jnp.abs(wl)))) - 0.5
    a = jax.nn.sigmoid(a0_ref[...] + _dot(_dot(xa, a1_ref[...]).astype(BF), a2_ref[...]))
    g = _dot(jax.nn.sigmoid(_dot(xg, g1_ref[...])).astype(BF), g2_ref[...])
    kk = k * kk_ref[...]
    k = k * (1.0 + (a - 1.0) * ka_ref[...])

    shp = x_ref.shape
    r_out[...] = r.reshape(shp)
    lw_out[...] = (-jnp.exp(w_log)).reshape(shp)
    k_out[...] = k.reshape(shp)
    v_out[...] = v.reshape(shp)
    kk_out[...] = kk.reshape(shp)
    a_out[...] = a.reshape(shp)
    g_out[...] = g.reshape(shp)

    @pl.when(s == ns - 1)
    def _():
        shift_out[...] = last


def rwkv_pre(x, mod, prev0, P):
    B, S, _ = x.shape
    bb, ts = _row_tile(B, S, 256)
    ns = S // ts
    xmap = lambda i, s: (i, s, 0)
    full = lambda shp: pl.BlockSpec(shp, lambda i, s: (0,) * len(shp))
    tok = pl.BlockSpec((bb, ts, D_MODEL), xmap)
    st = pl.BlockSpec((bb, 1, D_MODEL), lambda i, s: (i, 0, 0))
    sq, lo_in, lo_out, vec = (D_MODEL, D_MODEL), (D_MODEL, LORA_PAD), (LORA_PAD, D_MODEL), (1, D_MODEL)
    outs = pl.pallas_call(
        functools.partial(_rwkv_pre_kernel, ns=ns),
        out_shape=tuple([jax.ShapeDtypeStruct(x.shape, F32)] * 7
                        + [jax.ShapeDtypeStruct((B, 1, D_MODEL), F32)]),
        grid=(B // bb, ns),
        in_specs=[
            tok, _mod_spec(bb, 3), _mod_spec(bb, 4), st, full((6, D_MODEL)),
            full(sq), full(sq), full(sq), full(vec), full(lo_in), full(lo_out),
            full(vec), full(lo_in), full(lo_out), full(lo_in), full(lo_out),
            full(vec), full(vec),
        ],
        out_specs=tuple([tok] * 7 + [st]),
        scratch_shapes=[pltpu.VMEM((bb, 1, D_MODEL), F32)],
        compiler_params=_cparams(("parallel", "arbitrary")),
        name="rwkv_pre",
    )(x, mod, mod, prev0.reshape(B, 1, D_MODEL), P['mu'], P['wr'], P['wk'], P['wv'], P['w0'], P['w1'],
      P['w2'], P['a0'], P['a1'], P['a2'], P['g1'], P['g2'], P['k_k'], P['k_a'])
    return outs


def _stack_heads(x, first):
    return jnp.concatenate([jnp.where(first, x, 0.0), jnp.where(first, 0.0, x)], axis=0)


def _seg_sum(x, first):
    s0 = jnp.sum(jnp.where(first, x, 0.0), axis=1, keepdims=True)
    s1 = jnp.sum(jnp.where(first, 0.0, x), axis=1, keepdims=True)
    return jnp.where(first, s0, s1)


def _wkv_pair_chunks(toks, sps, vecs, consts, L):
    tri, first, strict, incl = consts
    L2 = 2 * L
    rng = range(len(toks))
    wide = L2 % LANES == 0
    splits = [_split3(t[1]) for t in toks]
    c_incl = [_dot(tri, s[0]) + _dot(tri, s[1]) + _dot(tri, s[2]) for s in splits]
    e_inv = [jnp.exp(-c) for c in c_incl]
    kn = [t[4] / jnp.maximum(jnp.sqrt(_seg_sum(t[4] * t[4], first)), 1e-12) for t in toks]
    la = [_stack_heads(-kn[p] * jnp.exp(c_incl[p] - toks[p][1]), first).astype(BF) for p in rng]
    lr = [_stack_heads(toks[p][0] * jnp.exp(c_incl[p]), first).astype(BF) for p in rng]
    rb = [_stack_heads(kn[p] * toks[p][5] * e_inv[p], first).astype(BF) for p in rng]
    rk = [_stack_heads(toks[p][2] * e_inv[p], first).astype(BF) for p in rng]
    vs = [_stack_heads(toks[p][3], first).astype(BF) for p in rng]
    rbk = [jnp.concatenate([rb[p], rk[p]], axis=0) for p in rng]

    if wide:
        nn = [_dot_nt(la[p], rbk[p]) for p in rng]
        mm = [_dot_nt(lr[p], rbk[p]) for p in rng]
        n_raw = [x[:, :L2] for x in nn]
        n_ak = [jnp.where(strict, x[:, L2:], 0.0).astype(BF) for x in nn]
        m_bk = [jnp.concatenate([jnp.where(incl, x[:, :L2], 0.0), jnp.where(incl, x[:, L2:], 0.0)],
                                axis=1).astype(BF) for x in mm]
    else:
        n_raw = [_dot_nt(la[p], rb[p]) for p in rng]
        n_ak = [jnp.where(strict, _dot_nt(la[p], rk[p]), 0.0).astype(BF) for p in rng]
        m_rb = [jnp.where(incl, _dot_nt(lr[p], rb[p]), 0.0).astype(BF) for p in rng]
        m_rk = [jnp.where(incl, _dot_nt(lr[p], rk[p]), 0.0).astype(BF) for p in rng]

    row = lax.broadcasted_iota(jnp.int32, (L2, L2), 0)
    col = lax.broadcasted_iota(jnp.int32, (L2, L2), 1)
    base = (row // INV_BASE == col // INV_BASE) & (col < row)
    xb = [jnp.where(base, x, 0.0).astype(BF) for x in n_raw]
    tinv = [jnp.where(row == col, 1.0, 0.0) + x.astype(F32) for x in xb]
    x2 = [_dot(x, x).astype(BF) for x in xb]
    if wide:
        prod = [_dot(x2[p], jnp.concatenate([x2[p], tinv[p].astype(BF)], axis=1)) for p in rng]
        tinv = [tinv[p] + prod[p][:, L2:] for p in rng]
        x4 = [x[:, :L2].astype(BF) for x in prod]
    else:
        tinv = [tinv[p] + _dot(x2[p], tinv[p].astype(BF)) for p in rng]
        x4 = [_dot(x, x).astype(BF) for x in x2]
    tinv = [tinv[p] + _dot(x4[p], tinv[p].astype(BF)) for p in rng]
    blk = INV_BASE
    while blk < L:
        off = (row // (2 * blk) == col // (2 * blk)) & (row // blk == col // blk + 1)
        tb = [t.astype(BF) for t in tinv]
        pr = [_dot(jnp.where(off, n_raw[p], 0.0).astype(BF), tb[p]).astype(BF) for p in rng]
        tinv = [tinv[p] + _dot(tb[p], pr[p]) for p in rng]
        blk *= 2

    spb = [s.astype(BF) for s in sps]
    rhs = [_dot_nt(la[p], spb[p]) + _dot(n_ak[p], vs[p]) for p in rng]
    ub = [_dot(tinv[p].astype(BF), rhs[p].astype(BF)).astype(BF) for p in rng]

    uv = [jnp.concatenate([ub[p], vs[p]], axis=0) for p in rng]
    if wide:
        ys = [_dot_nt(lr[p], spb[p]) + _dot(m_bk[p], uv[p]) for p in rng]
    else:
        ys = [_dot_nt(lr[p], spb[p]) + _dot(m_rb[p], ub[p]) + _dot(m_rk[p], vs[p]) for p in rng]
    sp_new = [(sps[p] + _dot_tn(uv[p], rbk[p])) * jnp.exp(c_incl[p][L - 1:L, :]) for p in rng]

    outs = []
    for p in rng:
        r, _, k, v, _, _ = toks[p]
        r_k, lnx_g, lnx_b = vecs[p]
        y = ys[p][0:L, :] + ys[p][L:L2, :]
        yc = y - _seg_sum(y, first) * (1.0 / RWKV_HS)
        yv = _seg_sum(yc * yc, first) * (1.0 / RWKV_HS)
        yn = yc * lax.rsqrt(yv + RWKV_LN_EPS) * lnx_g + lnx_b
        outs.append((yn + _seg_sum(r * k * r_k, first) * v, sp_new[p]))
    return outs


def _wkv_kernel(r_ref, lw_ref, k_ref, v_ref, kk_ref, a_ref, s0_ref, rk_ref, lxg_ref, lxb_ref,
                y_out, s_out, sp_scr, *, L, nc, bb, group):
    c = pl.program_id(1)
    hs = RWKV_HS
    npair = RWKV_HEADS // 2
    L2 = 2 * L

    @pl.when(c == 0)
    def _():
        zero = jnp.zeros((hs, hs), F32)

        def init(i, carry):
            for p in range(npair):
                top = jnp.concatenate([s0_ref[i, 2 * p], zero], axis=1)
                bot = jnp.concatenate([zero, s0_ref[i, 2 * p + 1]], axis=1)
                sp_scr[i, p] = jnp.concatenate([top, bot], axis=0)
            return carry

        lax.fori_loop(0, bb, init, 0)

    row = lax.broadcasted_iota(jnp.int32, (L, L), 0)
    col = lax.broadcasted_iota(jnp.int32, (L, L), 1)
    tri = jnp.where(col <= row, 1.0, 0.0).astype(BF)
    first = lax.broadcasted_iota(jnp.int32, (L, LANES), 1) < hs
    row2 = lax.broadcasted_iota(jnp.int32, (L2, L2), 0)
    col2 = lax.broadcasted_iota(jnp.int32, (L2, L2), 1)
    same = (row2 >= L) == (col2 >= L)
    consts = (tri, first, same & (col2 < row2), same & (col2 <= row2))
    sls = [slice(p * LANES, (p + 1) * LANES) for p in range(npair)]

    def seqs(j, carry):
        ids = [(j * group + g, p) for g in range(group) for p in range(npair)]
        toks = [tuple(ref[i, :, sls[p]] for ref in (r_ref, lw_ref, k_ref, v_ref, kk_ref, a_ref)) for i, p in ids]
        vecs = [(rk_ref[:, sls[p]], lxg_ref[:, sls[p]], lxb_ref[:, sls[p]]) for _, p in ids]
        outs = _wkv_pair_chunks(toks, [sp_scr[i, p] for i, p in ids], vecs, consts, L)
        for (i, p), (y, sp_new) in zip(ids, outs):
            y_out[i, :, sls[p]] = y
            sp_scr[i, p] = sp_new
        return carry

    lax.fori_loop(0, bb // group, seqs, 0)

    @pl.when(c == nc - 1)
    def _():
        def fin(i, carry):
            for p in range(npair):
                sp = sp_scr[i, p]
                s_out[i, 2 * p] = sp[0:hs, 0:hs]
                s_out[i, 2 * p + 1] = sp[hs:2 * hs, hs:2 * hs]
            return carry

        lax.fori_loop(0, bb, fin, 0)


def wkv_recurrence(r, lw, k, v, kk, a, s0, P):
    B, S, _ = r.shape
    L = RWKV_CHUNK if S % RWKV_CHUNK == 0 else S
    assert L & (L - 1) == 0 and L % SUBLANES == 0
    nc = S // L
    bb, group = (2, 2) if nc > 1 else (min(B, 8), 1)
    assert B % bb == 0
    npair = RWKV_HEADS // 2
    tok = pl.BlockSpec((bb, L, D_MODEL), lambda b, c: (b, c, 0))
    st = pl.BlockSpec((bb, RWKV_HEADS, RWKV_HS, RWKV_HS), lambda b, c: (b, 0, 0, 0))
    vec = pl.BlockSpec((1, D_MODEL), lambda b, c: (0, 0))
    return pl.pallas_call(
        functools.partial(_wkv_kernel, L=L, nc=nc, bb=bb, group=group),
        out_shape=(jax.ShapeDtypeStruct(r.shape, F32), jax.ShapeDtypeStruct(s0.shape, F32)),
        grid=(B // bb, nc),
        in_specs=[tok] * 6 + [st, vec, vec, vec],
        out_specs=(tok, st),
        scratch_shapes=[pltpu.VMEM((bb, npair, 2 * RWKV_HS, 2 * RWKV_HS), F32)],
        compiler_params=_cparams(("parallel", "arbitrary")),
        name="wkv",
    )(r, lw, k, v, kk, a, s0, P['r_k'], P['lnx_g'], P['lnx_b'])


def _rwkv_post_kernel(x_ref, gt_ref, y_ref, g_ref, wo_ref, lg_ref, lb_ref, o_ref):
    bb, ts, _ = x_ref.shape
    yg = (y_ref[...] * g_ref[...]).reshape(bb * ts, D_MODEL).astype(BF)
    out = _dot(yg, wo_ref[...])
    z = ALPHA * x_ref[...] + (1.0 + gt_ref[...]) * out.reshape(bb, ts, D_MODEL)
    o_ref[...] = _layer_norm(z, lg_ref[...], lb_ref[...])


def rwkv_post(x, mod, y, g, P, lg, lb):
    B, S, _ = x.shape
    bb, ts = _row_tile(B, S, 512)
    xmap = lambda i, s: (i, s, 0)
    tok = pl.BlockSpec((bb, ts, D_MODEL), xmap)
    vec = (1, D_MODEL)
    return pl.pallas_call(
        _rwkv_post_kernel,
        out_shape=jax.ShapeDtypeStruct(x.shape, F32),
        grid=(B // bb, S // ts),
        in_specs=[tok, _mod_spec(bb, 5), tok, tok, _resident((D_MODEL, D_MODEL)), _resident(vec), _resident(vec)],
        out_specs=tok,
        compiler_params=_cparams(("parallel", "parallel")),
        name="rwkv_post",
    )(x, mod, y, g, P['wo'], lg, lb)


def _block_diag(w):
    G, n, _ = w.shape
    eye = jnp.eye(G, dtype=w.dtype)
    return (eye[:, None, :, None] * w[:, :, None, :]).reshape(G * n, G * n)


def _pad_cols(w, n):
    return jnp.pad(w, ((0, 0), (0, n - w.shape[1])))


def _pad_rows(w, n):
    return jnp.pad(w, ((0, n - w.shape[0]), (0, 0)))


def _prep_ab(j, p):
    w_in = p['ab_w_in'][j]
    n_main = 4 * MLSTM_WIDTH
    n_gate = 2 * MLSTM_HEADS
    w_all = jnp.concatenate([w_in[:, :n_main], w_in[:, n_main + n_gate:],
                             w_in[:, n_main:n_main + n_gate]], axis=1)
    return dict(
        w_in=_pad_cols(w_all, Z_COLS).astype(BF),
        b_gates=p['mlstm_b_gates'][j], norm_g=p['mlstm_norm_g'][j],
        conv_w=p['rg_conv_w'][j], conv_b=p['rg_conv_b'][j],
        wa=_block_diag(p['rg_w_a'][j]).astype(BF), wx=_block_diag(p['rg_w_x'][j]).astype(BF),
        b_a=p['rg_b_a'][j], b_x=p['rg_b_x'][j], lam=p['rg_lambda'][j],
        w_out_m=p['ab_w_out'][j][:MLSTM_WIDTH].astype(BF), w_out_r=p['ab_w_out'][j][MLSTM_WIDTH:].astype(BF),
    )


def _prep_rwkv(j, p):
    vec = lambda a: a.reshape(1, D_MODEL)
    return dict(
        mu=p['rw_mu'][j],
        wr=p['rw_wr'][j].astype(BF), wk=p['rw_wk'][j].astype(BF), wv=p['rw_wv'][j].astype(BF),
        w0=vec(p['rw_w0'][j]), w1=_pad_cols(p['rw_w1'][j], LORA_PAD).astype(BF),
        w2=_pad_rows(p['rw_w2'][j], LORA_PAD).astype(BF),
        a0=vec(p['rw_a0'][j]), a1=_pad_cols(p['rw_a1'][j], LORA_PAD).astype(BF),
        a2=_pad_rows(p['rw_a2'][j], LORA_PAD).astype(BF),
        g1=_pad_cols(p['rw_g1'][j], LORA_PAD).astype(BF), g2=_pad_rows(p['rw_g2'][j], LORA_PAD).astype(BF),
        k_k=vec(p['rw_k_k'][j]), k_a=vec(p['rw_k_a'][j]), r_k=vec(p['rw_r_k'][j]),
        lnx_g=vec(p['rw_lnx_g'][j]), lnx_b=vec(p['rw_lnx_b'][j]), wo=p['rw_wo'][j].astype(BF),
    )


def ab_mixer(x, mod, st, A, lg, lb):
    mC, mn, mm, rh, rconv = st
    z = mod_matmul(x, mod, A['w_in'])
    gates = z[:, :, 6 * MLSTM_WIDTH:6 * MLSTM_WIDTH + 2 * MLSTM_HEADS]
    hm, C1, n1, m1 = mlstm_mixer(z, gates, mC, mn, mm, A['b_gates'], A['norm_g'])
    hr, rh1, buf1 = rglru_mixer(z, rh, rconv, A['conv_w'], A['conv_b'], A['wa'], A['wx'],
                                A['b_a'], A['b_x'], A['lam'])
    x = proj2_post(x, mod, hm, hr, A['w_out_m'], A['w_out_r'], lg, lb)
    return x, (C1, n1, m1, rh1, buf1)


def rwkv_mixer(x, mod, st, R, lg, lb):
    wkv0, prev0 = st
    r, lw, k, v, kk, a, g, shift = rwkv_pre(x, mod, prev0, R)
    y, wkv1 = wkv_recurrence(r, lw, k, v, kk, a, wkv0, R)
    x = rwkv_post(x, mod, y, g, R, lg, lb)
    return x, (wkv1, shift.reshape(shift.shape[0], D_MODEL))


def run_trunk(x, mods, states, W):
    mC, mn, mm, rh, rconv, wkv, shift = states
    new_ab, new_c = [], []
    for layer in range(DEPTH):
        mod = mods[layer]
        lg = lambda i: W['ln_g'][layer, i].reshape(1, D_MODEL)
        lb = lambda i: W['ln_b'][layer, i].reshape(1, D_MODEL)
        f = W['ffn'][layer]
        x = ffn_block(x, mod, 0, f[0][0], f[0][1], f[0][2], lg(0), lb(0))
        j = layer // 2
        if layer % 2 == 0:
            x, st = ab_mixer(x, mod, (mC[j], mn[j], mm[j], rh[j], rconv[j]), W['ab'][j], lg(1), lb(1))
            new_ab.append(st)
        else:
            x, st = rwkv_mixer(x, mod, (wkv[j], shift[j]), W['rwkv'][j], lg(1), lb(1))
            new_c.append(st)
        x = ffn_block(x, mod, 2, f[1][0], f[1][1], f[1][2], lg(2), lb(2))
    stk = lambda sts, i: jnp.stack([s[i] for s in sts], axis=0)
    return x, (stk(new_ab, 0), stk(new_ab, 1), stk(new_ab, 2), stk(new_ab, 3), stk(new_ab, 4),
               stk(new_c, 0), stk(new_c, 1))


def _zero_states(B):
    n_ab, n_c = (DEPTH + 1) // 2, DEPTH // 2
    return (jnp.zeros((n_ab, B, MLSTM_HEADS, MLSTM_HD, MLSTM_HD), F32),
            jnp.zeros((n_ab, B, MLSTM_HEADS, MLSTM_HD), F32),
            jnp.zeros((n_ab, B, MLSTM_HEADS), F32),
            jnp.zeros((n_ab, B, RG_WIDTH), F32),
            jnp.zeros((n_ab, B, RG_CONV - 1, RG_WIDTH), F32),
            jnp.zeros((n_c, B, RWKV_HEADS, RWKV_HS, RWKV_HS), F32),
            jnp.zeros((n_c, B, D_MODEL), F32))


def kernel(x_prompt, x_sample, c_prompt, c_sample, state_mlstm_C, state_mlstm_n, state_mlstm_m, state_rglru_h, state_rglru_conv, state_rwkv_wkv, state_rwkv_shift, ada_w, ada_b, ln_g, ln_b, ffn_w1, ffn_w3, ffn_w2, ab_w_in, mlstm_b_gates, mlstm_norm_g, rg_conv_w, rg_conv_b, rg_w_a, rg_b_a, rg_w_x, rg_b_x, rg_lambda, ab_w_out, rw_mu, rw_wr, rw_wk, rw_wv, rw_w0, rw_w1, rw_w2, rw_a0, rw_a1, rw_a2, rw_g1, rw_g2, rw_k_k, rw_k_a, rw_r_k, rw_lnx_g, rw_lnx_b, rw_wo):
    p = dict(ab_w_in=ab_w_in, mlstm_b_gates=mlstm_b_gates, mlstm_norm_g=mlstm_norm_g,
             rg_conv_w=rg_conv_w, rg_conv_b=rg_conv_b, rg_w_a=rg_w_a, rg_b_a=rg_b_a, rg_w_x=rg_w_x,
             rg_b_x=rg_b_x, rg_lambda=rg_lambda, ab_w_out=ab_w_out, rw_mu=rw_mu, rw_wr=rw_wr,
             rw_wk=rw_wk, rw_wv=rw_wv, rw_w0=rw_w0, rw_w1=rw_w1, rw_w2=rw_w2, rw_a0=rw_a0,
             rw_a1=rw_a1, rw_a2=rw_a2, rw_g1=rw_g1, rw_g2=rw_g2, rw_k_k=rw_k_k, rw_k_a=rw_k_a,
             rw_r_k=rw_r_k, rw_lnx_g=rw_lnx_g, rw_lnx_b=rw_lnx_b, rw_wo=rw_wo)
    W = dict(
        ln_g=ln_g, ln_b=ln_b,
        ffn=[[(ffn_w1[l, i].astype(BF), ffn_w3[l, i].astype(BF), ffn_w2[l, i].astype(BF))
              for i in range(2)] for l in range(DEPTH)],
        ab=[_prep_ab(j, p) for j in range((DEPTH + 1) // 2)],
        rwkv=[_prep_rwkv(j, p) for j in range(DEPTH // 2)],
    )
    Bp, Bs = x_prompt.shape[0], x_sample.shape[0]
    mod_all = adaln(jnp.concatenate([c_prompt, c_sample], axis=0), ada_w, ada_b)
    mods_p = [mod_all[l, :Bp].reshape(Bp, 3 * N_SUB, 1, D_MODEL) for l in range(DEPTH)]
    mods_s = [mod_all[l, Bp:].reshape(Bs, 3 * N_SUB, 1, D_MODEL) for l in range(DEPTH)]
    y_prompt, sp = run_trunk(x_prompt, mods_p, _zero_states(Bp), W)
    y_sample, ss = run_trunk(x_sample, mods_s,
                             (state_mlstm_C, state_mlstm_n, state_mlstm_m, state_rglru_h,
                              state_rglru_conv, state_rwkv_wkv, state_rwkv_shift), W)
    return (y_prompt, y_sample) + tuple(sp) + tuple(ss)
```

```python
import functools

import jax
import jax.numpy as jnp
from jax import lax
from jax.experimental import pallas as pl
from jax.experimental.pallas import tpu as pltpu

D_MODEL = 1024
DEPTH = 2
N_SUB = 3
MLSTM_WIDTH = 512
MLSTM_HEADS = 4
MLSTM_HD = 128
MLSTM_CHUNK = 128
RG_WIDTH = 512
RG_BLOCKS = 8
RG_BD = 64
RG_CONV = 4
RG_C = 8.0
RWKV_HS = 64
RWKV_HEADS = 16
RWKV_LN_EPS = 64e-5
RWKV_CHUNK = 64
INV_BASE = 8
D_FF = 2816
ALPHA = (2.0 * DEPTH) ** 0.25
LN_EPS = 1e-5
HEAD_NORM_EPS = 1e-6

LANES = 128
SUBLANES = 8
VMEM_LIMIT_BYTES = 56 * 1024 * 1024
Z_COLS = 3200
LORA_PAD = 128

BF = jnp.bfloat16
F32 = jnp.float32

NT_DIMS = (((1,), (1,)), ((), ()))
TN_DIMS = (((0,), (0,)), ((), ()))


def _cparams(sem):
    return pltpu.CompilerParams(dimension_semantics=sem, vmem_limit_bytes=VMEM_LIMIT_BYTES)


def _dot(a, b):
    return jnp.dot(a, b, preferred_element_type=F32)


def _dot_nt(a, b):
    return lax.dot_general(a, b, NT_DIMS, preferred_element_type=F32)


def _dot_tn(a, b):
    return lax.dot_general(a, b, TN_DIMS, preferred_element_type=F32)


def _layer_norm(z, g, b):
    mu = jnp.mean(z, axis=-1, keepdims=True)
    zc = z - mu
    var = jnp.mean(zc * zc, axis=-1, keepdims=True)
    return zc * lax.rsqrt(var + LN_EPS) * g + b


def _row_tile(B, S, rows):
    if S >= rows:
        assert S % rows == 0
        return 1, rows
    bb = min(B, rows // S)
    assert B % bb == 0
    return bb, S


def _split2(x):
    hi = x.astype(BF)
    lo = (x - hi.astype(F32)).astype(BF)
    return hi, lo


def _split3(x):
    hi = x.astype(BF)
    r1 = x - hi.astype(F32)
    mid = r1.astype(BF)
    lo = (r1 - mid.astype(F32)).astype(BF)
    return hi, mid, lo


def _adaln_kernel(c_ref, w_ref, b_ref, o_ref):
    h = jax.nn.silu(c_ref[...]).astype(BF)
    o_ref[...] = _dot(h, w_ref[...].astype(BF)) + b_ref[...]


def adaln(c_all, ada_w, ada_b):
    Bc = c_all.shape[0]
    n_out = ada_w.shape[-1]
    tn = 1024
    return pl.pallas_call(
        _adaln_kernel,
        out_shape=jax.ShapeDtypeStruct((DEPTH, Bc, n_out), F32),
        grid=(DEPTH, n_out // tn),
        in_specs=[
            pl.BlockSpec((Bc, D_MODEL), lambda l, j: (0, 0)),
            pl.BlockSpec((None, D_MODEL, tn), lambda l, j: (l, 0, j)),
            pl.BlockSpec((None, 1, tn), lambda l, j: (l, 0, j)),
        ],
        out_specs=pl.BlockSpec((None, Bc, tn), lambda l, j: (l, 0, j)),
        compiler_params=_cparams(("parallel", "parallel")),
        name="adaln",
    )(c_all, ada_w, ada_b.reshape(DEPTH, 1, n_out))


def _mod_spec(bb, k):
    return pl.BlockSpec((bb, None, 1, D_MODEL), lambda i, *_: (i, k, 0, 0))


FFN_CHUNKS = 2


def _ffn_kernel(x_ref, sh_ref, sc_ref, gt_ref, w1_ref, w3_ref, w2_ref, lg_ref, lb_ref, o_ref, *, res_w):
    bb, ts, _ = x_ref.shape
    x = x_ref[...]
    h = (x * (1.0 + sc_ref[...]) + sh_ref[...]).reshape(bb * ts, D_MODEL).astype(BF)
    tf = D_FF // FFN_CHUNKS
    y = None
    for c in range(FFN_CHUNKS):
        cols = slice(c * tf, (c + 1) * tf)
        a = _dot(h, w1_ref[:, cols])
        b = _dot(h, w3_ref[:, cols])
        g = (jax.nn.silu(a) * b).astype(BF)
        yc = _dot(g, w2_ref[cols, :])
        y = yc if y is None else y + yc
    z = ALPHA * x + (res_w * (1.0 + gt_ref[...])) * y.reshape(bb, ts, D_MODEL)
    o_ref[...] = _layer_norm(z, lg_ref[...], lb_ref[...])


def _resident(shape):
    return pl.BlockSpec(shape, lambda *_: (0,) * len(shape), pipeline_mode=pl.Buffered(1))


def ffn_block(x, mod, sub, layer, half, w1, w3, w2, lg, lb):
    B, S, _ = x.shape
    bb, ts = _row_tile(B, S, 512)
    assert (D_FF // FFN_CHUNKS) % LANES == 0
    xmap = lambda i, s: (i, s, 0)
    wspec = lambda r, c: pl.BlockSpec((None, None, r, c), lambda i, s: (layer, half, 0, 0),
                                      pipeline_mode=pl.Buffered(1))
    return pl.pallas_call(
        functools.partial(_ffn_kernel, res_w=0.5),
        out_shape=jax.ShapeDtypeStruct(x.shape, F32),
        grid=(B // bb, S // ts),
        in_specs=[
            pl.BlockSpec((bb, ts, D_MODEL), xmap),
            _mod_spec(bb, 3 * sub), _mod_spec(bb, 3 * sub + 1), _mod_spec(bb, 3 * sub + 2),
            wspec(D_MODEL, D_FF), wspec(D_MODEL, D_FF), wspec(D_FF, D_MODEL),
            _resident((1, D_MODEL)), _resident((1, D_MODEL)),
        ],
        out_specs=pl.BlockSpec((bb, ts, D_MODEL), xmap),
        compiler_params=_cparams(("parallel", "parallel")),
        name="ffn_block",
    )(x, mod, mod, mod, w1, w3, w2, lg, lb)


def _modmm_kernel(x_ref, sh_ref, sc_ref, w_ref, o_ref, gate_ref):
    bb, ts, _ = x_ref.shape
    n_main = o_ref.shape[-1]
    n_gate = gate_ref.shape[-1]
    h = (x_ref[...] * (1.0 + sc_ref[...]) + sh_ref[...]).reshape(bb * ts, D_MODEL).astype(BF)
    o_ref[...] = _dot(h, w_ref[:, :n_main]).reshape(o_ref.shape)
    gate_ref[...] = _dot(h, w_ref[:, n_main:])[:, :n_gate].reshape(gate_ref.shape)


def mod_matmul(x, mod, w, n_main, n_gate):
    B, S, _ = x.shape
    bb, ts = _row_tile(B, S, 512)
    xmap = lambda i, s: (i, s, 0)
    return pl.pallas_call(
        _modmm_kernel,
        out_shape=(jax.ShapeDtypeStruct((B, S, n_main), F32), jax.ShapeDtypeStruct((B, S, n_gate), F32)),
        grid=(B // bb, S // ts),
        in_specs=[
            pl.BlockSpec((bb, ts, D_MODEL), xmap),
            _mod_spec(bb, 3), _mod_spec(bb, 4),
            _resident(w.shape),
        ],
        out_specs=(pl.BlockSpec((bb, ts, n_main), xmap), pl.BlockSpec((bb, ts, n_gate), xmap)),
        compiler_params=_cparams(("parallel", "parallel")),
        name="mod_matmul",
    )(x, mod, mod, w)


def _log_sigmoid(x):
    return jnp.minimum(x, 0.0) - jnp.log1p(jnp.exp(-jnp.abs(x)))


def _lockstep(gens):
    results = [None] * len(gens)
    live = list(range(len(gens)))
    while live:
        still = []
        for idx in live:
            try:
                next(gens[idx])
                still.append(idx)
            except StopIteration as stop:
                results[idx] = stop.value
        live = still
    return results


def _mlstm_head_chunk(q, k, v, o, ig_raw, fg_raw, b_i, b_f, C0, n0, m0, g, L):
    ig_c = ig_raw + b_i
    lf_c = _log_sigmoid(fg_raw + b_f)

    row = lax.broadcasted_iota(jnp.int32, (L, L), 0)
    col = lax.broadcasted_iota(jnp.int32, (L, L), 1)
    causal = col <= row
    diag = row == col
    b_r = jnp.sum(jnp.where(row <= col, lf_c, 0.0), axis=0, keepdims=True)
    b_c = jnp.sum(jnp.where(diag, b_r, 0.0), axis=1, keepdims=True)
    ig_r = jnp.sum(jnp.where(diag, ig_c, 0.0), axis=0, keepdims=True)
    dmat = jnp.where(causal, b_c - b_r + ig_r, -jnp.inf)
    m_inter = b_c + m0
    m = jnp.maximum(m_inter, jnp.max(dmat, axis=1, keepdims=True))
    w_inter = jnp.exp(m_inter - m)
    p = jnp.exp(dmat - m)

    k = k * (MLSTM_HD ** -0.5)
    qb = q.astype(BF)
    kb = k.astype(BF)
    yield
    s_raw = _dot_nt(qb, kb)
    q_c0 = _dot_nt(qb, C0.astype(BF))
    yield
    scores = s_raw * p
    num = _dot(scores.astype(BF), v.astype(BF)) + w_inter * q_c0
    den = jnp.sum(scores, axis=1, keepdims=True) + w_inter * jnp.sum(q * n0, axis=1, keepdims=True)
    hh = num / jnp.maximum(jnp.abs(den), jnp.exp(-m))

    m_end = m[L - 1:L, :]
    b_end = b_c[L - 1:L, :]
    w_state = jnp.exp(b_end + m0 - m_end)
    w_rows = jnp.exp(b_end - b_c + ig_c - m_end)
    yield
    c_new = w_state * C0 + _dot_tn((w_rows * v).astype(BF), kb)
    n_new = w_state * n0 + jnp.sum(w_rows * k, axis=0, keepdims=True)

    mu = jnp.mean(hh, axis=1, keepdims=True)
    hc = hh - mu
    var = jnp.mean(hc * hc, axis=1, keepdims=True)
    hn = hc * lax.rsqrt(var + HEAD_NORM_EPS)
    return hn * g * jax.nn.sigmoid(o), c_new, n_new, m_end


def _mlstm_kernel(m0_ref, bg_ref, q_ref, k_ref, v_ref, o_ref, gc_ref, c0_ref, n0_ref, g_ref,
                  h_out, c_out, n_out, m_out, c_scr, n_scr, m_scr, *, L, nc, bb, group):
    ib = pl.program_id(0)
    c = pl.program_id(1)
    H = MLSTM_HEADS

    @pl.when(c == 0)
    def _():
        c_scr[...] = c0_ref[...]
        n_scr[...] = n0_ref[...]

        def init(i, carry):
            for h in range(H):
                m_scr[i, h] = jnp.full((1, LANES), m0_ref[(ib * bb + i) * H + h], F32)
            return carry

        lax.fori_loop(0, bb, init, 0)

    sls = [slice(h * MLSTM_HD, (h + 1) * MLSTM_HD) for h in range(H)]

    def seqs(j, carry):
        ids = [(j * group + g, h) for g in range(group) for h in range(H)]
        heads = _lockstep([_mlstm_head_chunk(
            q_ref[i, :, sls[h]], k_ref[i, :, sls[h]], v_ref[i, :, sls[h]], o_ref[i, :, sls[h]],
            gc_ref[i, :, h:h + 1], gc_ref[i, :, H + h:H + h + 1], bg_ref[h], bg_ref[H + h],
            c_scr[i, h], n_scr[i, h], m_scr[i, h][:, 0:1], g_ref[:, sls[h]], L) for i, h in ids])
        for (i, h), (out, c_new, n_new, m_end) in zip(ids, heads):
            h_out[i, :, sls[h]] = out
            c_scr[i, h] = c_new
            n_scr[i, h] = n_new
            m_scr[i, h] = jnp.broadcast_to(m_end, (1, LANES))
        return carry

    lax.fori_loop(0, bb // group, seqs, 0)

    @pl.when(c == nc - 1)
    def _():
        c_out[...] = c_scr[...]
        n_out[...] = n_scr[...]
        m_out[...] = m_scr[...]


def mlstm_mixer(z, gates, C0, n0, m0, b_gates, norm_g):
    B, S, _ = z.shape
    H = MLSTM_HEADS
    W = MLSTM_WIDTH
    L = MLSTM_CHUNK if S % MLSTM_CHUNK == 0 else S
    nc = S // L
    bb, group = (1, 1) if nc > 1 else (min(B, 8), 1)
    assert B % bb == 0 and bb % group == 0
    zspec = lambda blk: pl.BlockSpec((bb, L, W), lambda b, c: (b, c, blk))
    smem = pl.BlockSpec(memory_space=pltpu.SMEM)
    hd_spec = pl.BlockSpec((bb, H, 1, MLSTM_HD), lambda b, c: (b, 0, 0, 0))
    c_spec = pl.BlockSpec((bb, H, MLSTM_HD, MLSTM_HD), lambda b, c: (b, 0, 0, 0))
    hm, C1, n1, m1 = pl.pallas_call(
        functools.partial(_mlstm_kernel, L=L, nc=nc, bb=bb, group=group),
        out_shape=(
            jax.ShapeDtypeStruct((B, S, W), F32),
            jax.ShapeDtypeStruct((B, H, MLSTM_HD, MLSTM_HD), F32),
            jax.ShapeDtypeStruct((B, H, 1, MLSTM_HD), F32),
            jax.ShapeDtypeStruct((B, H, 1, LANES), F32),
        ),
        grid=(B // bb, nc),
        in_specs=[
            smem, smem,
            zspec(0), zspec(1), zspec(2), zspec(3),
            pl.BlockSpec((bb, L, 2 * H), lambda b, c: (b, c, 0)),
            c_spec, hd_spec,
            pl.BlockSpec((1, W), lambda b, c: (0, 0)),
        ],
        out_specs=(pl.BlockSpec((bb, L, W), lambda b, c: (b, c, 0)), c_spec, hd_spec, hd_spec),
        scratch_shapes=[pltpu.VMEM((bb, H, MLSTM_HD, MLSTM_HD), F32), pltpu.VMEM((bb, H, 1, MLSTM_HD), F32),
                        pltpu.VMEM((bb, H, 1, LANES), F32)],
        compiler_params=_cparams(("parallel", "arbitrary")),
        name="mlstm",
    )(m0.reshape(B * H), b_gates.reshape(2 * H), z, z, z, z, gates,
      C0, n0.reshape(B, H, 1, MLSTM_HD), norm_g.reshape(1, W))
    return hm, C1, n1.reshape(B, H, MLSTM_HD), m1[:, :, 0, 0]


CONV_PAD = SUBLANES


def _expm1(y):
    u = jnp.exp(y)
    small = jnp.where(u == 1.0, y, (u - 1.0) * y / jnp.log(u))
    return jnp.where(jnp.abs(y) > 0.5, u - 1.0, small)


def _rglru_kernel(xr_ref, gr_ref, buf0_ref, h0_ref, cw_ref, cb_ref, wa_ref, wx_ref, ba_ref, bx_ref,
                  lam_ref, hr_out, h_out, buf_out, xp_scr, a_scr, u_scr, hs_scr, h_scr, *, ts, ns):
    s = pl.program_id(1)
    lo = CONV_PAD - (RG_CONV - 1)

    @pl.when(s == 0)
    def _():
        xp_scr[lo:CONV_PAD, :] = buf0_ref[...]
        h_scr[...] = h0_ref[...]

    xp_scr[CONV_PAD:CONV_PAD + ts, :] = xr_ref[...]
    cw = cw_ref[...]
    xc = cb_ref[...] + cw[0:1, :] * xp_scr[lo:lo + ts, :]
    for j in range(1, RG_CONV):
        xc = xc + cw[j:j + 1, :] * xp_scr[lo + j:lo + j + ts, :]
    hist = xp_scr[ts + lo:ts + CONV_PAD, :]
    xp_scr[lo:CONV_PAD, :] = hist

    xb = xc.astype(BF)
    r = jax.nn.sigmoid(_dot(xb, wa_ref[...]) + ba_ref[...])
    i = jax.nn.sigmoid(_dot(xb, wx_ref[...]) + bx_ref[...])
    lam = lam_ref[...]
    softplus_neg = jnp.maximum(-lam, 0.0) + jnp.log1p(jnp.exp(-jnp.abs(lam)))
    log_a = (-RG_C * softplus_neg) * r
    a_scr[...] = jnp.exp(log_a)
    u_scr[...] = jnp.sqrt(-_expm1(2.0 * log_a)) * i * xc

    def body(blk, h):
        r0 = pl.multiple_of(blk * SUBLANES, SUBLANES)
        a8 = a_scr[pl.ds(r0, SUBLANES), :]
        u8 = u_scr[pl.ds(r0, SUBLANES), :]
        rows = []
        for t in range(SUBLANES):
            h = a8[t:t + 1, :] * h + u8[t:t + 1, :]
            rows.append(h)
        hs_scr[pl.ds(r0, SUBLANES), :] = jnp.concatenate(rows, axis=0)
        return h

    h_last = lax.fori_loop(0, ts // SUBLANES, body, h_scr[...])
    h_scr[...] = h_last
    hr_out[...] = hs_scr[...] * jax.nn.gelu(gr_ref[...])

    @pl.when(s == ns - 1)
    def _():
        h_out[...] = h_last
        buf_out[...] = hist


def rglru_mixer(z, h0, buf0, conv_w, conv_b, wa_bd, wx_bd, b_a, b_x, lam):
    B, S, _ = z.shape
    W = RG_WIDTH
    ts = min(S, 512)
    ns = S // ts
    xr_blk = (4 * MLSTM_WIDTH) // W
    row = lambda a: a.reshape(1, W)
    full = lambda shp: pl.BlockSpec(shp, lambda b, s: (0,) * len(shp))
    hr, h1, buf1 = pl.pallas_call(
        functools.partial(_rglru_kernel, ts=ts, ns=ns),
        out_shape=(
            jax.ShapeDtypeStruct((B, S, W), F32),
            jax.ShapeDtypeStruct((B, 1, W), F32),
            jax.ShapeDtypeStruct((B, RG_CONV - 1, W), F32),
        ),
        grid=(B, ns),
        in_specs=[
            pl.BlockSpec((None, ts, W), lambda b, s: (b, s, xr_blk)),
            pl.BlockSpec((None, ts, W), lambda b, s: (b, s, xr_blk + 1)),
            pl.BlockSpec((None, RG_CONV - 1, W), lambda b, s: (b, 0, 0)),
            pl.BlockSpec((None, 1, W), lambda b, s: (b, 0, 0)),
            full((RG_CONV, W)), full((1, W)), full((W, W)), full((W, W)),
            full((1, W)), full((1, W)), full((1, W)),
        ],
        out_specs=(
            pl.BlockSpec((None, ts, W), lambda b, s: (b, s, 0)),
            pl.BlockSpec((None, 1, W), lambda b, s: (b, 0, 0)),
            pl.BlockSpec((None, RG_CONV - 1, W), lambda b, s: (b, 0, 0)),
        ),
        scratch_shapes=[pltpu.VMEM((CONV_PAD + ts, W), F32), pltpu.VMEM((ts, W), F32),
                        pltpu.VMEM((ts, W), F32), pltpu.VMEM((ts, W), F32), pltpu.VMEM((1, W), F32)],
        compiler_params=_cparams(("parallel", "arbitrary")),
        name="rglru",
    )(z, z, buf0, h0.reshape(B, 1, W), conv_w, row(conv_b), wa_bd, wx_bd, row(b_a), row(b_x), row(lam))
    return hr, h1.reshape(B, W), buf1


def _proj2_post_kernel(x_ref, gt_ref, a1_ref, a2_ref, w1_ref, w2_ref, lg_ref, lb_ref, o_ref):
    bb, ts, _ = x_ref.shape
    a1 = a1_ref[...].reshape(bb * ts, -1).astype(BF)
    a2 = a2_ref[...].reshape(bb * ts, -1).astype(BF)
    y = _dot(a1, w1_ref[...]) + _dot(a2, w2_ref[...])
    z = ALPHA * x_ref[...] + (1.0 + gt_ref[...]) * y.reshape(bb, ts, D_MODEL)
    o_ref[...] = _layer_norm(z, lg_ref[...], lb_ref[...])


def proj2_post(x, mod, a1, a2, w1, w2, lg, lb):
    B, S, _ = x.shape
    bb, ts = _row_tile(B, S, 512)
    K1, K2 = a1.shape[-1], a2.shape[-1]
    xmap = lambda i, s: (i, s, 0)
    full = lambda shp: pl.BlockSpec(shp, lambda i, s: (0,) * len(shp))
    return pl.pallas_call(
        _proj2_post_kernel,
        out_shape=jax.ShapeDtypeStruct(x.shape, F32),
        grid=(B // bb, S // ts),
        in_specs=[
            pl.BlockSpec((bb, ts, D_MODEL), xmap), _mod_spec(bb, 5),
            pl.BlockSpec((bb, ts, K1), xmap), pl.BlockSpec((bb, ts, K2), xmap),
            full((K1, D_MODEL)), full((K2, D_MODEL)), full((1, D_MODEL)), full((1, D_MODEL)),
        ],
        out_specs=pl.BlockSpec((bb, ts, D_MODEL), xmap),
        compiler_params=_cparams(("parallel", "parallel")),
        name="proj2_post",
    )(x, mod, a1, a2, w1, w2, lg, lb)


def _rwkv_pre_kernel(x_ref, sh_ref, sc_ref, prev0_ref, mu_ref, wr_ref, wk_ref, wv_ref, w0_ref, w1_ref,
                     w2_ref, a0_ref, a1_ref, a2_ref, g1_ref, g2_ref, kk_ref, ka_ref,
                     r_out, lw_out, k_out, v_out, kk_out, a_out, g_out, shift_out, carry_scr, *, ns):
    s = pl.program_id(1)
    bb, ts, _ = x_ref.shape
    tm = bb * ts

    @pl.when(s == 0)
    def _():
        carry_scr[...] = prev0_ref[...]

    h = x_ref[...] * (1.0 + sc_ref[...]) + sh_ref[...]
    t_idx = lax.broadcasted_iota(jnp.int32, h.shape, 1)
    h_prev = jnp.where(t_idx == 0, carry_scr[...], pltpu.roll(h, 1, 1))
    last = h[:, ts - 1:ts, :]
    carry_scr[...] = last
    dx = (h_prev - h).reshape(tm, D_MODEL)
    h2 = h.reshape(tm, D_MODEL)
    mu = mu_ref[...]
    mix = lambda j: (h2 + dx * mu[j:j + 1, :]).astype(BF)
    xr, xw, xk, xv, xa, xg = [mix(j) for j in range(6)]

    r = _dot(xr, wr_ref[...])
    k = _dot(xk, wk_ref[...])
    v = _dot(xv, wv_ref[...])
    wl = w0_ref[...] + _dot(jnp.tanh(_dot(xw, w1_ref[...])).astype(BF), w2_ref[...])
    w_log = -(jnp.maximum(-wl, 0.0) + jnp.log1p(jnp.exp(-jnp.abs(wl)))) - 0.5
    a = jax.nn.sigmoid(a0_ref[...] + _dot(_dot(xa, a1_ref[...]).astype(BF), a2_ref[...]))
    g = _dot(jax.nn.sigmoid(_dot(xg, g1_ref[...])).astype(BF), g2_ref[...])
    kk = k * kk_ref[...]
    k = k * (1.0 + (a - 1.0) * ka_ref[...])

    shp = x_ref.shape
    r_out[...] = r.reshape(shp)
    lw_out[...] = (-jnp.exp(w_log)).reshape(shp)
    k_out[...] = k.reshape(shp)
    v_out[...] = v.reshape(shp)
    kk_out[...] = kk.reshape(shp)
    a_out[...] = a.reshape(shp)
    g_out[...] = g.reshape(shp)

    @pl.when(s == ns - 1)
    def _():
        shift_out[...] = last


def rwkv_pre(x, mod, prev0, P):
    B, S, _ = x.shape
    bb, ts = _row_tile(B, S, 256)
    ns = S // ts
    xmap = lambda i, s: (i, s, 0)
    full = lambda shp: pl.BlockSpec(shp, lambda i, s: (0,) * len(shp))
    tok = pl.BlockSpec((bb, ts, D_MODEL), xmap)
    st = pl.BlockSpec((bb, 1, D_MODEL), lambda i, s: (i, 0, 0))
    sq, lo_in, lo_out, vec = (D_MODEL, D_MODEL), (D_MODEL, LORA_PAD), (LORA_PAD, D_MODEL), (1, D_MODEL)
    outs = pl.pallas_call(
        functools.partial(_rwkv_pre_kernel, ns=ns),
        out_shape=tuple([jax.ShapeDtypeStruct(x.shape, F32)] * 7
                        + [jax.ShapeDtypeStruct((B, 1, D_MODEL), F32)]),
        grid=(B // bb, ns),
        in_specs=[
            tok, _mod_spec(bb, 3), _mod_spec(bb, 4), st, full((6, D_MODEL)),
            full(sq), full(sq), full(sq), full(vec), full(lo_in), full(lo_out),
            full(vec), full(lo_in), full(lo_out), full(lo_in), full(lo_out),
            full(vec), full(vec),
        ],
        out_specs=tuple([tok] * 7 + [st]),
        scratch_shapes=[pltpu.VMEM((bb, 1, D_MODEL), F32)],
        compiler_params=_cparams(("parallel", "arbitrary")),
        name="rwkv_pre",
    )(x, mod, mod, prev0.reshape(B, 1, D_MODEL), P['mu'], P['wr'], P['wk'], P['wv'], P['w0'], P['w1'],
      P['w2'], P['a0'], P['a1'], P['a2'], P['g1'], P['g2'], P['k_k'], P['k_a'])
    return outs


def _stack_heads(x, first):
    return jnp.concatenate([jnp.where(first, x, 0.0), jnp.where(first, 0.0, x)], axis=0)


def _seg_sum(x, first):
    s0 = jnp.sum(jnp.where(first, x, 0.0), axis=1, keepdims=True)
    s1 = jnp.sum(jnp.where(first, 0.0, x), axis=1, keepdims=True)
    return jnp.where(first, s0, s1)


def _wkv_pair_chunks(toks, sps, vecs, consts, L):
    tri, first, strict, incl = consts
    L2 = 2 * L
    rng = range(len(toks))
    wide = L2 % LANES == 0
    splits = [_split3(t[1]) for t in toks]
    c_incl = [_dot(tri, s[0]) + _dot(tri, s[1]) + _dot(tri, s[2]) for s in splits]
    e_inv = [jnp.exp(-c) for c in c_incl]
    kn = [t[4] / jnp.maximum(jnp.sqrt(_seg_sum(t[4] * t[4], first)), 1e-12) for t in toks]
    la = [_stack_heads(-kn[p] * jnp.exp(c_incl[p] - toks[p][1]), first).astype(BF) for p in rng]
    lr = [_stack_heads(toks[p][0] * jnp.exp(c_incl[p]), first).astype(BF) for p in rng]
    rb = [_stack_heads(kn[p] * toks[p][5] * e_inv[p], first).astype(BF) for p in rng]
    rk = [_stack_heads(toks[p][2] * e_inv[p], first).astype(BF) for p in rng]
    vs = [_stack_heads(toks[p][3], first).astype(BF) for p in rng]
    rbk = [jnp.concatenate([rb[p], rk[p]], axis=0) for p in rng]

    if wide:
        nn = [_dot_nt(la[p], rbk[p]) for p in rng]
        mm = [_dot_nt(lr[p], rbk[p]) for p in rng]
        n_raw = [x[:, :L2] for x in nn]
        n_ak = [jnp.where(strict, x[:, L2:], 0.0).astype(BF) for x in nn]
        m_bk = [jnp.concatenate([jnp.where(incl, x[:, :L2], 0.0), jnp.where(incl, x[:, L2:], 0.0)],
                                axis=1).astype(BF) for x in mm]
    else:
        n_raw = [_dot_nt(la[p], rb[p]) for p in rng]
        n_ak = [jnp.where(strict, _dot_nt(la[p], rk[p]), 0.0).astype(BF) for p in rng]
        m_rb = [jnp.where(incl, _dot_nt(lr[p], rb[p]), 0.0).astype(BF) for p in rng]
        m_rk = [jnp.where(incl, _dot_nt(lr[p], rk[p]), 0.0).astype(BF) for p in rng]

    row = lax.broadcasted_iota(jnp.int32, (L2, L2), 0)
    col = lax.broadcasted_iota(jnp.int32, (L2, L2), 1)
    base = (row // INV_BASE == col // INV_BASE) & (col < row)
    xb = [jnp.where(base, x, 0.0).astype(BF) for x in n_raw]
    tinv = [jnp.where(row == col, 1.0, 0.0) + x.astype(F32) for x in xb]
    x2 = [_dot(x, x).astype(BF) for x in xb]
    if wide:
        prod = [_dot(x2[p], jnp.concatenate([x2[p], tinv[p].astype(BF)], axis=1)) for p in rng]
        tinv = [tinv[p] + prod[p][:, L2:] for p in rng]
        x4 = [x[:, :L2].astype(BF) for x in prod]
    else:
        tinv = [tinv[p] + _dot(x2[p], tinv[p].astype(BF)) for p in rng]
        x4 = [_dot(x, x).astype(BF) for x in x2]
    tinv = [tinv[p] + _dot(x4[p], tinv[p].astype(BF)) for p in rng]
    blk = INV_BASE
    while blk < L:
        off = (row // (2 * blk) == col // (2 * blk)) & (row // blk == col // blk + 1)
        tb = [t.astype(BF) for t in tinv]
        pr = [_dot(jnp.where(off, n_raw[p], 0.0).astype(BF), tb[p]).astype(BF) for p in rng]
        tinv = [tinv[p] + _dot(tb[p], pr[p]) for p in rng]
        blk *= 2

    spb = [s.astype(BF) for s in sps]
    rhs = [_dot_nt(la[p], spb[p]) + _dot(n_ak[p], vs[p]) for p in rng]
    ub = [_dot(tinv[p].astype(BF), rhs[p].astype(BF)).astype(BF) for p in rng]

    uv = [jnp.concatenate([ub[p], vs[p]], axis=0) for p in rng]
    if wide:
        ys = [_dot_nt(lr[p], spb[p]) + _dot(m_bk[p], uv[p]) for p in rng]
    else:
        ys = [_dot_nt(lr[p], spb[p]) + _dot(m_rb[p], ub[p]) + _dot(m_rk[p], vs[p]) for p in rng]
    sp_new = [(sps[p] + _dot_tn(uv[p], rbk[p])) * jnp.exp(c_incl[p][L - 1:L, :]) for p in rng]

    outs = []
    for p in rng:
        r, _, k, v, _, _ = toks[p]
        r_k, lnx_g, lnx_b = vecs[p]
        y = ys[p][0:L, :] + ys[p][L:L2, :]
        yc = y - _seg_sum(y, first) * (1.0 / RWKV_HS)
        yv = _seg_sum(yc * yc, first) * (1.0 / RWKV_HS)
        yn = yc * lax.rsqrt(yv + RWKV_LN_EPS) * lnx_g + lnx_b
        outs.append((yn + _seg_sum(r * k * r_k, first) * v, sp_new[p]))
    return outs


def _wkv_kernel(r_ref, lw_ref, k_ref, v_ref, kk_ref, a_ref, s0_ref, rk_ref, lxg_ref, lxb_ref,
                y_out, s_out, sp_scr, *, L, nc, bb, group):
    c = pl.program_id(1)
    hs = RWKV_HS
    npair = RWKV_HEADS // 2
    L2 = 2 * L

    @pl.when(c == 0)
    def _():
        zero = jnp.zeros((hs, hs), F32)

        def init(i, carry):
            for p in range(npair):
                top = jnp.concatenate([s0_ref[i, 2 * p], zero], axis=1)
                bot = jnp.concatenate([zero, s0_ref[i, 2 * p + 1]], axis=1)
                sp_scr[i, p] = jnp.concatenate([top, bot], axis=0)
            return carry

        lax.fori_loop(0, bb, init, 0)

    row = lax.broadcasted_iota(jnp.int32, (L, L), 0)
    col = lax.broadcasted_iota(jnp.int32, (L, L), 1)
    tri = jnp.where(col <= row, 1.0, 0.0).astype(BF)
    first = lax.broadcasted_iota(jnp.int32, (L, LANES), 1) < hs
    row2 = lax.broadcasted_iota(jnp.int32, (L2, L2), 0)
    col2 = lax.broadcasted_iota(jnp.int32, (L2, L2), 1)
    same = (row2 >= L) == (col2 >= L)
    consts = (tri, first, same & (col2 < row2), same & (col2 <= row2))
    sls = [slice(p * LANES, (p + 1) * LANES) for p in range(npair)]

    def seqs(j, carry):
        ids = [(j * group + g, p) for g in range(group) for p in range(npair)]
        toks = [tuple(ref[i, :, sls[p]] for ref in (r_ref, lw_ref, k_ref, v_ref, kk_ref, a_ref)) for i, p in ids]
        vecs = [(rk_ref[:, sls[p]], lxg_ref[:, sls[p]], lxb_ref[:, sls[p]]) for _, p in ids]
        outs = _wkv_pair_chunks(toks, [sp_scr[i, p] for i, p in ids], vecs, consts, L)
        for (i, p), (y, sp_new) in zip(ids, outs):
            y_out[i, :, sls[p]] = y
            sp_scr[i, p] = sp_new
        return carry

    lax.fori_loop(0, bb // group, seqs, 0)

    @pl.when(c == nc - 1)
    def _():
        def fin(i, carry):
            for p in range(npair):
                sp = sp_scr[i, p]
                s_out[i, 2 * p] = sp[0:hs, 0:hs]
                s_out[i, 2 * p + 1] = sp[hs:2 * hs, hs:2 * hs]
            return carry

        lax.fori_loop(0, bb, fin, 0)


def wkv_recurrence(r, lw, k, v, kk, a, s0, P):
    B, S, _ = r.shape
    L = RWKV_CHUNK if S % RWKV_CHUNK == 0 else S
    assert L & (L - 1) == 0 and L % SUBLANES == 0
    nc = S // L
    bb, group = (2, 2) if nc > 1 else (min(B, 8), 4)
    assert B % bb == 0 and bb % group == 0
    npair = RWKV_HEADS // 2
    tok = pl.BlockSpec((bb, L, D_MODEL), lambda b, c: (b, c, 0))
    st = pl.BlockSpec((bb, RWKV_HEADS, RWKV_HS, RWKV_HS), lambda b, c: (b, 0, 0, 0))
    vec = pl.BlockSpec((1, D_MODEL), lambda b, c: (0, 0))
    return pl.pallas_call(
        functools.partial(_wkv_kernel, L=L, nc=nc, bb=bb, group=group),
        out_shape=(jax.ShapeDtypeStruct(r.shape, F32), jax.ShapeDtypeStruct(s0.shape, F32)),
        grid=(B // bb, nc),
        in_specs=[tok] * 6 + [st, vec, vec, vec],
        out_specs=(tok, st),
        scratch_shapes=[pltpu.VMEM((bb, npair, 2 * RWKV_HS, 2 * RWKV_HS), F32)],
        compiler_params=_cparams(("parallel", "arbitrary")),
        name="wkv",
    )(r, lw, k, v, kk, a, s0, P['r_k'], P['lnx_g'], P['lnx_b'])


def _rwkv_post_kernel(x_ref, gt_ref, y_ref, g_ref, wo_ref, lg_ref, lb_ref, o_ref):
    bb, ts, _ = x_ref.shape
    yg = (y_ref[...] * g_ref[...]).reshape(bb * ts, D_MODEL).astype(BF)
    out = _dot(yg, wo_ref[...])
    z = ALPHA * x_ref[...] + (1.0 + gt_ref[...]) * out.reshape(bb, ts, D_MODEL)
    o_ref[...] = _layer_norm(z, lg_ref[...], lb_ref[...])


def rwkv_post(x, mod, y, g, P, lg, lb):
    B, S, _ = x.shape
    bb, ts = _row_tile(B, S, 512)
    xmap = lambda i, s: (i, s, 0)
    tok = pl.BlockSpec((bb, ts, D_MODEL), xmap)
    vec = (1, D_MODEL)
    return pl.pallas_call(
        _rwkv_post_kernel,
        out_shape=jax.ShapeDtypeStruct(x.shape, F32),
        grid=(B // bb, S // ts),
        in_specs=[tok, _mod_spec(bb, 5), tok, tok, _resident((D_MODEL, D_MODEL)), _resident(vec), _resident(vec)],
        out_specs=tok,
        compiler_params=_cparams(("parallel", "parallel")),
        name="rwkv_post",
    )(x, mod, y, g, P['wo'], lg, lb)


def _block_diag(w):
    G, n, _ = w.shape
    eye = jnp.eye(G, dtype=w.dtype)
    return (eye[:, None, :, None] * w[:, :, None, :]).reshape(G * n, G * n)


def _pad_cols(w, n):
    return jnp.pad(w, ((0, 0), (0, n - w.shape[1])))


def _pad_rows(w, n):
    return jnp.pad(w, ((0, n - w.shape[0]), (0, 0)))


def _prep_ab(j, p):
    w_in = p['ab_w_in'][j]
    n_main = 4 * MLSTM_WIDTH
    n_gate = 2 * MLSTM_HEADS
    w_all = jnp.concatenate([w_in[:, :n_main], w_in[:, n_main + n_gate:],
                             w_in[:, n_main:n_main + n_gate]], axis=1)
    return dict(
        w_in=_pad_cols(w_all, Z_COLS).astype(BF),
        b_gates=p['mlstm_b_gates'][j], norm_g=p['mlstm_norm_g'][j],
        conv_w=p['rg_conv_w'][j], conv_b=p['rg_conv_b'][j],
        wa=_block_diag(p['rg_w_a'][j]).astype(BF), wx=_block_diag(p['rg_w_x'][j]).astype(BF),
        b_a=p['rg_b_a'][j], b_x=p['rg_b_x'][j], lam=p['rg_lambda'][j],
        w_out_m=p['ab_w_out'][j][:MLSTM_WIDTH].astype(BF), w_out_r=p['ab_w_out'][j][MLSTM_WIDTH:].astype(BF),
    )


def _prep_rwkv(j, p):
    vec = lambda a: a.reshape(1, D_MODEL)
    return dict(
        mu=p['rw_mu'][j],
        wr=p['rw_wr'][j].astype(BF), wk=p['rw_wk'][j].astype(BF), wv=p['rw_wv'][j].astype(BF),
        w0=vec(p['rw_w0'][j]), w1=_pad_cols(p['rw_w1'][j], LORA_PAD).astype(BF),
        w2=_pad_rows(p['rw_w2'][j], LORA_PAD).astype(BF),
        a0=vec(p['rw_a0'][j]), a1=_pad_cols(p['rw_a1'][j], LORA_PAD).astype(BF),
        a2=_pad_rows(p['rw_a2'][j], LORA_PAD).astype(BF),
        g1=_pad_cols(p['rw_g1'][j], LORA_PAD).astype(BF), g2=_pad_rows(p['rw_g2'][j], LORA_PAD).astype(BF),
        k_k=vec(p['rw_k_k'][j]), k_a=vec(p['rw_k_a'][j]), r_k=vec(p['rw_r_k'][j]),
        lnx_g=vec(p['rw_lnx_g'][j]), lnx_b=vec(p['rw_lnx_b'][j]), wo=p['rw_wo'][j].astype(BF),
    )


def ab_mixer(x, mod, st, A, lg, lb):
    mC, mn, mm, rh, rconv = st
    z, gates = mod_matmul(x, mod, A['w_in'], 4 * MLSTM_WIDTH + 2 * RG_WIDTH, 2 * MLSTM_HEADS)
    hm, C1, n1, m1 = mlstm_mixer(z, gates, mC, mn, mm, A['b_gates'], A['norm_g'])
    hr, rh1, buf1 = rglru_mixer(z, rh, rconv, A['conv_w'], A['conv_b'], A['wa'], A['wx'],
                                A['b_a'], A['b_x'], A['lam'])
    x = proj2_post(x, mod, hm, hr, A['w_out_m'], A['w_out_r'], lg, lb)
    return x, (C1, n1, m1, rh1, buf1)


def rwkv_mixer(x, mod, st, R, lg, lb):
    wkv0, prev0 = st
    r, lw, k, v, kk, a, g, shift = rwkv_pre(x, mod, prev0, R)
    y, wkv1 = wkv_recurrence(r, lw, k, v, kk, a, wkv0, R)
    x = rwkv_post(x, mod, y, g, R, lg, lb)
    return x, (wkv1, shift.reshape(shift.shape[0], D_MODEL))


def run_trunk(x, mods, states, W):
    mC, mn, mm, rh, rconv, wkv, shift = states
    new_ab, new_c = [], []
    for layer in range(DEPTH):
        mod = mods[layer]
        lg = lambda i: W['ln_g'][layer, i].reshape(1, D_MODEL)
        lb = lambda i: W['ln_b'][layer, i].reshape(1, D_MODEL)
        f = W['ffn']
        x = ffn_block(x, mod, 0, layer, 0, f[0], f[1], f[2], lg(0), lb(0))
        j = layer // 2
        if layer % 2 == 0:
            x, st = ab_mixer(x, mod, (mC[j], mn[j], mm[j], rh[j], rconv[j]), W['ab'][j], lg(1), lb(1))
            new_ab.append(st)
        else:
            x, st = rwkv_mixer(x, mod, (wkv[j], shift[j]), W['rwkv'][j], lg(1), lb(1))
            new_c.append(st)
        x = ffn_block(x, mod, 2, layer, 1, f[0], f[1], f[2], lg(2), lb(2))
    stk = lambda sts, i: sts[0][i][None] if len(sts) == 1 else jnp.stack([s[i] for s in sts], axis=0)
    return x, (stk(new_ab, 0), stk(new_ab, 1), stk(new_ab, 2), stk(new_ab, 3), stk(new_ab, 4),
               stk(new_c, 0), stk(new_c, 1))


def _zero_states(B):
    n_ab, n_c = (DEPTH + 1) // 2, DEPTH // 2
    return (jnp.zeros((n_ab, B, MLSTM_HEADS, MLSTM_HD, MLSTM_HD), F32),
            jnp.zeros((n_ab, B, MLSTM_HEADS, MLSTM_HD), F32),
            jnp.zeros((n_ab, B, MLSTM_HEADS), F32),
            jnp.zeros((n_ab, B, RG_WIDTH), F32),
            jnp.zeros((n_ab, B, RG_CONV - 1, RG_WIDTH), F32),
            jnp.zeros((n_c, B, RWKV_HEADS, RWKV_HS, RWKV_HS), F32),
            jnp.zeros((n_c, B, D_MODEL), F32))


def kernel(x_prompt, x_sample, c_prompt, c_sample, state_mlstm_C, state_mlstm_n, state_mlstm_m, state_rglru_h, state_rglru_conv, state_rwkv_wkv, state_rwkv_shift, ada_w, ada_b, ln_g, ln_b, ffn_w1, ffn_w3, ffn_w2, ab_w_in, mlstm_b_gates, mlstm_norm_g, rg_conv_w, rg_conv_b, rg_w_a, rg_b_a, rg_w_x, rg_b_x, rg_lambda, ab_w_out, rw_mu, rw_wr, rw_wk, rw_wv, rw_w0, rw_w1, rw_w2, rw_a0, rw_a1, rw_a2, rw_g1, rw_g2, rw_k_k, rw_k_a, rw_r_k, rw_lnx_g, rw_lnx_b, rw_wo):
    p = dict(ab_w_in=ab_w_in, mlstm_b_gates=mlstm_b_gates, mlstm_norm_g=mlstm_norm_g,
             rg_conv_w=rg_conv_w, rg_conv_b=rg_conv_b, rg_w_a=rg_w_a, rg_b_a=rg_b_a, rg_w_x=rg_w_x,
             rg_b_x=rg_b_x, rg_lambda=rg_lambda, ab_w_out=ab_w_out, rw_mu=rw_mu, rw_wr=rw_wr,
             rw_wk=rw_wk, rw_wv=rw_wv, rw_w0=rw_w0, rw_w1=rw_w1, rw_w2=rw_w2, rw_a0=rw_a0,
             rw_a1=rw_a1, rw_a2=rw_a2, rw_g1=rw_g1, rw_g2=rw_g2, rw_k_k=rw_k_k, rw_k_a=rw_k_a,
             rw_r_k=rw_r_k, rw_lnx_g=rw_lnx_g, rw_lnx_b=rw_lnx_b, rw_wo=rw_wo)
    W = dict(
        ln_g=ln_g, ln_b=ln_b,
        ffn=(ffn_w1.astype(BF), ffn_w3.astype(BF), ffn_w2.astype(BF)),
        ab=[_prep_ab(j, p) for j in range((DEPTH + 1) // 2)],
        rwkv=[_prep_rwkv(j, p) for j in range(DEPTH // 2)],
    )
    Bp, Bs = x_prompt.shape[0], x_sample.shape[0]
    mod_all = adaln(jnp.concatenate([c_prompt, c_sample], axis=0), ada_w, ada_b)
    mods_p = [mod_all[l, :Bp].reshape(Bp, 3 * N_SUB, 1, D_MODEL) for l in range(DEPTH)]
    mods_s = [mod_all[l, Bp:].reshape(Bs, 3 * N_SUB, 1, D_MODEL) for l in range(DEPTH)]
    y_prompt, sp = run_trunk(x_prompt, mods_p, _zero_states(Bp), W)
    y_sample, ss = run_trunk(x_sample, mods_s,
                             (state_mlstm_C, state_mlstm_n, state_mlstm_m, state_rglru_h,
                              state_rglru_conv, state_rwkv_wkv, state_rwkv_shift), W)
    return (y_prompt, y_sample) + tuple(sp) + tuple(ss)
```

```python
import functools

import jax
import jax.numpy as jnp
from jax import lax
from jax.experimental import pallas as pl
from jax.experimental.pallas import tpu as pltpu

D_MODEL = 1024
DEPTH = 2
N_SUB = 3
MLSTM_WIDTH = 512
MLSTM_HEADS = 4
MLSTM_HD = 128
MLSTM_CHUNK = 128
RG_WIDTH = 512
RG_BLOCKS = 8
RG_BD = 64
RG_CONV = 4
RG_C = 8.0
RWKV_HS = 64
RWKV_HEADS = 16
RWKV_LN_EPS = 64e-5
RWKV_CHUNK = 64
INV_BASE = 8
D_FF = 2816
ALPHA = (2.0 * DEPTH) ** 0.25
LN_EPS = 1e-5
HEAD_NORM_EPS = 1e-6

LANES = 128
SUBLANES = 8
VMEM_LIMIT_BYTES = 56 * 1024 * 1024
Z_COLS = 3200
LORA_PAD = 128

BF = jnp.bfloat16
F32 = jnp.float32

NT_DIMS = (((1,), (1,)), ((), ()))
TN_DIMS = (((0,), (0,)), ((), ()))


def _cparams(sem):
    return pltpu.CompilerParams(dimension_semantics=sem, vmem_limit_bytes=VMEM_LIMIT_BYTES)


def _dot(a, b):
    return jnp.dot(a, b, preferred_element_type=F32)


def _dot_nt(a, b):
    return lax.dot_general(a, b, NT_DIMS, preferred_element_type=F32)


def _dot_tn(a, b):
    return lax.dot_general(a, b, TN_DIMS, preferred_element_type=F32)


def _layer_norm(z, g, b):
    mu = jnp.mean(z, axis=-1, keepdims=True)
    zc = z - mu
    var = jnp.mean(zc * zc, axis=-1, keepdims=True)
    return zc * lax.rsqrt(var + LN_EPS) * g + b


def _row_tile(B, S, rows):
    if S >= rows:
        assert S % rows == 0
        return 1, rows
    bb = min(B, rows // S)
    assert B % bb == 0
    return bb, S


def _split2(x):
    hi = x.astype(BF)
    lo = (x - hi.astype(F32)).astype(BF)
    return hi, lo


def _split3(x):
    hi = x.astype(BF)
    r1 = x - hi.astype(F32)
    mid = r1.astype(BF)
    lo = (r1 - mid.astype(F32)).astype(BF)
    return hi, mid, lo


def _adaln_kernel(c_ref, w_ref, b_ref, o_ref):
    h = jax.nn.silu(c_ref[...]).astype(BF)
    o_ref[...] = _dot(h, w_ref[...].astype(BF)) + b_ref[...]


def adaln(c_all, ada_w, ada_b):
    Bc = c_all.shape[0]
    n_out = ada_w.shape[-1]
    tn = 1024
    return pl.pallas_call(
        _adaln_kernel,
        out_shape=jax.ShapeDtypeStruct((DEPTH, Bc, n_out), F32),
        grid=(DEPTH, n_out // tn),
        in_specs=[
            pl.BlockSpec((Bc, D_MODEL), lambda l, j: (0, 0)),
            pl.BlockSpec((None, D_MODEL, tn), lambda l, j: (l, 0, j)),
            pl.BlockSpec((None, 1, tn), lambda l, j: (l, 0, j)),
        ],
        out_specs=pl.BlockSpec((None, Bc, tn), lambda l, j: (l, 0, j)),
        compiler_params=_cparams(("parallel", "parallel")),
        name="adaln",
    )(c_all, ada_w, ada_b.reshape(DEPTH, 1, n_out))


def _mod_spec(bb, k):
    return pl.BlockSpec((bb, None, 1, D_MODEL), lambda i, *_: (i, k, 0, 0))


FFN_CHUNKS = 2


def _ffn_kernel(x_ref, sh_ref, sc_ref, gt_ref, w1_ref, w3_ref, w2_ref, lg_ref, lb_ref, o_ref, *, res_w):
    bb, ts, _ = x_ref.shape
    x = x_ref[...]
    h = (x * (1.0 + sc_ref[...]) + sh_ref[...]).reshape(bb * ts, D_MODEL).astype(BF)
    tf = D_FF // FFN_CHUNKS
    y = None
    for c in range(FFN_CHUNKS):
        cols = slice(c * tf, (c + 1) * tf)
        a = _dot(h, w1_ref[:, cols])
        b = _dot(h, w3_ref[:, cols])
        g = (jax.nn.silu(a) * b).astype(BF)
        yc = _dot(g, w2_ref[cols, :])
        y = yc if y is None else y + yc
    z = ALPHA * x + (res_w * (1.0 + gt_ref[...])) * y.reshape(bb, ts, D_MODEL)
    o_ref[...] = _layer_norm(z, lg_ref[...], lb_ref[...])


def _resident(shape):
    return pl.BlockSpec(shape, lambda *_: (0,) * len(shape), pipeline_mode=pl.Buffered(1))


def ffn_block(x, mod, sub, layer, half, w1, w3, w2, lg, lb):
    B, S, _ = x.shape
    bb, ts = _row_tile(B, S, 512)
    assert (D_FF // FFN_CHUNKS) % LANES == 0
    xmap = lambda i, s: (i, s, 0)
    wspec = lambda r, c: pl.BlockSpec((None, None, r, c), lambda i, s: (layer, half, 0, 0),
                                      pipeline_mode=pl.Buffered(1))
    return pl.pallas_call(
        functools.partial(_ffn_kernel, res_w=0.5),
        out_shape=jax.ShapeDtypeStruct(x.shape, F32),
        grid=(B // bb, S // ts),
        in_specs=[
            pl.BlockSpec((bb, ts, D_MODEL), xmap),
            _mod_spec(bb, 3 * sub), _mod_spec(bb, 3 * sub + 1), _mod_spec(bb, 3 * sub + 2),
            wspec(D_MODEL, D_FF), wspec(D_MODEL, D_FF), wspec(D_FF, D_MODEL),
            _resident((1, D_MODEL)), _resident((1, D_MODEL)),
        ],
        out_specs=pl.BlockSpec((bb, ts, D_MODEL), xmap),
        compiler_params=_cparams(("parallel", "parallel")),
        name="ffn_block",
    )(x, mod, mod, mod, w1, w3, w2, lg, lb)


def _modmm_kernel(x_ref, sh_ref, sc_ref, w_ref, o_ref, gate_ref):
    bb, ts, _ = x_ref.shape
    n_main = o_ref.shape[-1]
    n_gate = gate_ref.shape[-1]
    h = (x_ref[...] * (1.0 + sc_ref[...]) + sh_ref[...]).reshape(bb * ts, D_MODEL).astype(BF)
    o_ref[...] = _dot(h, w_ref[:, :n_main]).reshape(o_ref.shape)
    gate_ref[...] = _dot(h, w_ref[:, n_main:])[:, :n_gate].reshape(gate_ref.shape)


def mod_matmul(x, mod, w, n_main, n_gate):
    B, S, _ = x.shape
    bb, ts = _row_tile(B, S, 512)
    xmap = lambda i, s: (i, s, 0)
    return pl.pallas_call(
        _modmm_kernel,
        out_shape=(jax.ShapeDtypeStruct((B, S, n_main), F32), jax.ShapeDtypeStruct((B, S, n_gate), F32)),
        grid=(B // bb, S // ts),
        in_specs=[
            pl.BlockSpec((bb, ts, D_MODEL), xmap),
            _mod_spec(bb, 3), _mod_spec(bb, 4),
            _resident(w.shape),
        ],
        out_specs=(pl.BlockSpec((bb, ts, n_main), xmap), pl.BlockSpec((bb, ts, n_gate), xmap)),
        compiler_params=_cparams(("parallel", "parallel")),
        name="mod_matmul",
    )(x, mod, mod, w)


def _log_sigmoid(x):
    return jnp.minimum(x, 0.0) - jnp.log1p(jnp.exp(-jnp.abs(x)))


def _lockstep(gens):
    results = [None] * len(gens)
    live = list(range(len(gens)))
    while live:
        still = []
        for idx in live:
            try:
                next(gens[idx])
                still.append(idx)
            except StopIteration as stop:
                results[idx] = stop.value
        live = still
    return results


def _dot_exact(a, b, dims):
    return lax.dot_general(a, b, dims, precision=lax.Precision.HIGHEST, preferred_element_type=F32)


def _mlstm_gate_views(gl, L):
    row = lax.broadcasted_iota(jnp.int32, (L, L), 0)
    col = lax.broadcasted_iota(jnp.int32, (L, L), 1)
    eye = jnp.where(row == col, 1.0, 0.0)
    lower = jnp.where(col <= row, 1.0, 0.0)
    upper = jnp.where(row <= col, 1.0, 0.0)
    gl_t = _dot_exact(gl, eye, TN_DIMS)
    cum_c = _dot_exact(lower, gl, (((1,), (0,)), ((), ())))
    cum_r = _dot_exact(gl, upper, TN_DIMS)
    return gl_t, cum_c, cum_r


def _mlstm_head_chunk(q, k, v, o, ig_c, b_c, ig_r, b_r, C0, n0, m0, g, L):
    row = lax.broadcasted_iota(jnp.int32, (L, L), 0)
    col = lax.broadcasted_iota(jnp.int32, (L, L), 1)
    causal = col <= row
    dmat = jnp.where(causal, b_c - b_r + ig_r, -jnp.inf)
    m_inter = b_c + m0
    m = jnp.maximum(m_inter, jnp.max(dmat, axis=1, keepdims=True))
    w_inter = jnp.exp(m_inter - m)
    p = jnp.exp(dmat - m)

    k = k * (MLSTM_HD ** -0.5)
    qb = q.astype(BF)
    kb = k.astype(BF)
    yield
    s_raw = _dot_nt(qb, kb)
    q_c0 = _dot_nt(qb, C0.astype(BF))
    yield
    scores = s_raw * p
    num = _dot(scores.astype(BF), v.astype(BF)) + w_inter * q_c0
    den = jnp.sum(scores, axis=1, keepdims=True) + w_inter * jnp.sum(q * n0, axis=1, keepdims=True)
    hh = num / jnp.maximum(jnp.abs(den), jnp.exp(-m))

    m_end = m[L - 1:L, :]
    b_end = b_c[L - 1:L, :]
    w_state = jnp.exp(b_end + m0 - m_end)
    w_rows = jnp.exp(b_end - b_c + ig_c - m_end)
    yield
    c_new = w_state * C0 + _dot_tn((w_rows * v).astype(BF), kb)
    n_new = w_state * n0 + jnp.sum(w_rows * k, axis=0, keepdims=True)

    mu = jnp.mean(hh, axis=1, keepdims=True)
    hc = hh - mu
    var = jnp.mean(hc * hc, axis=1, keepdims=True)
    hn = hc * lax.rsqrt(var + HEAD_NORM_EPS)
    return hn * g * jax.nn.sigmoid(o), c_new, n_new, m_end


def _mlstm_kernel(m0_ref, bg_ref, q_ref, k_ref, v_ref, o_ref, gc_ref, c0_ref, n0_ref, g_ref,
                  h_out, c_out, n_out, m_out, c_scr, n_scr, m_scr, *, L, nc, bb, group):
    ib = pl.program_id(0)
    c = pl.program_id(1)
    H = MLSTM_HEADS

    @pl.when(c == 0)
    def _():
        c_scr[...] = c0_ref[...]
        n_scr[...] = n0_ref[...]

        def init(i, carry):
            for h in range(H):
                m_scr[i, h] = jnp.full((1, LANES), m0_ref[(ib * bb + i) * H + h], F32)
            return carry

        lax.fori_loop(0, bb, init, 0)

    sls = [slice(h * MLSTM_HD, (h + 1) * MLSTM_HD) for h in range(H)]

    def seqs(j, carry):
        views = []
        for g in range(group):
            gates = gc_ref[j * group + g] + bg_ref[...]
            is_input = lax.broadcasted_iota(jnp.int32, gates.shape, 1) < H
            gl = jnp.where(is_input, gates, _log_sigmoid(gates))
            views.append((gl,) + _mlstm_gate_views(gl, L))
        ids = [(j * group + g, h) for g in range(group) for h in range(H)]
        gate_args = [(views[g][0][:, h:h + 1], views[g][2][:, H + h:H + h + 1],
                      views[g][1][h:h + 1, :], views[g][3][H + h:H + h + 1, :])
                     for g in range(group) for h in range(H)]
        heads = _lockstep([_mlstm_head_chunk(
            q_ref[i, :, sls[h]], k_ref[i, :, sls[h]], v_ref[i, :, sls[h]], o_ref[i, :, sls[h]], *gate_args[n],
            c_scr[i, h], n_scr[i, h], m_scr[i, h][:, 0:1], g_ref[:, sls[h]], L) for n, (i, h) in enumerate(ids)])
        for (i, h), (out, c_new, n_new, m_end) in zip(ids, heads):
            h_out[i, :, sls[h]] = out
            c_scr[i, h] = c_new
            n_scr[i, h] = n_new
            m_scr[i, h] = jnp.broadcast_to(m_end, (1, LANES))
        return carry

    lax.fori_loop(0, bb // group, seqs, 0)

    @pl.when(c == nc - 1)
    def _():
        c_out[...] = c_scr[...]
        n_out[...] = n_scr[...]
        m_out[...] = m_scr[...]


def mlstm_mixer(z, gates, C0, n0, m0, b_gates, norm_g):
    B, S, _ = z.shape
    H = MLSTM_HEADS
    W = MLSTM_WIDTH
    L = MLSTM_CHUNK if S % MLSTM_CHUNK == 0 else S
    nc = S // L
    bb, group = (1, 1) if nc > 1 else (min(B, 8), 1)
    assert B % bb == 0 and bb % group == 0
    zspec = lambda blk: pl.BlockSpec((bb, L, W), lambda b, c: (b, c, blk))
    smem = pl.BlockSpec(memory_space=pltpu.SMEM)
    hd_spec = pl.BlockSpec((bb, H, 1, MLSTM_HD), lambda b, c: (b, 0, 0, 0))
    c_spec = pl.BlockSpec((bb, H, MLSTM_HD, MLSTM_HD), lambda b, c: (b, 0, 0, 0))
    hm, C1, n1, m1 = pl.pallas_call(
        functools.partial(_mlstm_kernel, L=L, nc=nc, bb=bb, group=group),
        out_shape=(
            jax.ShapeDtypeStruct((B, S, W), F32),
            jax.ShapeDtypeStruct((B, H, MLSTM_HD, MLSTM_HD), F32),
            jax.ShapeDtypeStruct((B, H, 1, MLSTM_HD), F32),
            jax.ShapeDtypeStruct((B, H, 1, LANES), F32),
        ),
        grid=(B // bb, nc),
        in_specs=[
            smem, pl.BlockSpec((1, 2 * H), lambda b, c: (0, 0)),
            zspec(0), zspec(1), zspec(2), zspec(3),
            pl.BlockSpec((bb, L, 2 * H), lambda b, c: (b, c, 0)),
            c_spec, hd_spec,
            pl.BlockSpec((1, W), lambda b, c: (0, 0)),
        ],
        out_specs=(pl.BlockSpec((bb, L, W), lambda b, c: (b, c, 0)), c_spec, hd_spec, hd_spec),
        scratch_shapes=[pltpu.VMEM((bb, H, MLSTM_HD, MLSTM_HD), F32), pltpu.VMEM((bb, H, 1, MLSTM_HD), F32),
                        pltpu.VMEM((bb, H, 1, LANES), F32)],
        compiler_params=_cparams(("parallel", "arbitrary")),
        name="mlstm",
    )(m0.reshape(B * H), b_gates.reshape(1, 2 * H), z, z, z, z, gates,
      C0, n0.reshape(B, H, 1, MLSTM_HD), norm_g.reshape(1, W))
    return hm, C1, n1.reshape(B, H, MLSTM_HD), m1[:, :, 0, 0]


def _expm1(y):
    u = jnp.exp(y)
    small = jnp.where(u == 1.0, y, (u - 1.0) * y / jnp.log(u))
    return jnp.where(jnp.abs(y) > 0.5, u - 1.0, small)


def _rglru_kernel(xr_ref, gr_ref, buf_ref, h0_ref, cw_ref, cb_ref, wa_ref, wx_ref, ba_ref, bx_ref, lam_ref,
                  hr_out, h_out, buf_out, prev_scr, h_scr, a_scr, u_scr, c_scr, *, ns, chained):
    s = pl.program_id(1)
    bb, ts, W = xr_ref.shape
    nb = bb * ts // SUBLANES
    x = xr_ref[...].reshape(nb, SUBLANES, W)

    if chained:
        @pl.when(s == 0)
        def _():
            prev_scr[...] = buf_ref[0]
            h_scr[...] = h0_ref[0]

        prev = jnp.concatenate([prev_scr[...][None], x[:nb - 1]], axis=0)
        prev_scr[...] = x[nb - 1]
    else:
        prev = buf_ref[...]

    t_idx = lax.broadcasted_iota(jnp.int32, x.shape, 1)
    cw = cw_ref[...]
    xc = cb_ref[...] + cw[RG_CONV - 1:RG_CONV, :] * x
    for d in range(1, RG_CONV):
        back = jnp.where(t_idx >= d, pltpu.roll(x, d, 1), pltpu.roll(prev, d, 1))
        xc = xc + cw[RG_CONV - 1 - d:RG_CONV - d, :] * back

    xc2 = xc.reshape(nb * SUBLANES, W)
    xb = xc2.astype(BF)
    r = jax.nn.sigmoid(_dot(xb, wa_ref[...]) + ba_ref[...])
    i = jax.nn.sigmoid(_dot(xb, wx_ref[...]) + bx_ref[...])
    lam = lam_ref[...]
    softplus_neg = jnp.maximum(-lam, 0.0) + jnp.log1p(jnp.exp(-jnp.abs(lam)))
    log_a = (-RG_C * softplus_neg) * r
    a = jnp.exp(log_a).reshape(nb, SUBLANES, W)
    u = (jnp.sqrt(-_expm1(2.0 * log_a)) * i * xc2).reshape(nb, SUBLANES, W)

    d = 1
    while d < SUBLANES:
        inside = t_idx >= d
        u = jnp.where(inside, a * pltpu.roll(u, d, 1) + u, u)
        a = jnp.where(inside, a * pltpu.roll(a, d, 1), a)
        d *= 2

    if chained:
        a_scr[...] = a
        u_scr[...] = u

        def carry_in(k, h):
            c_scr[k] = jnp.broadcast_to(h, (SUBLANES, W))
            return a_scr[k, SUBLANES - 1:SUBLANES, :] * h + u_scr[k, SUBLANES - 1:SUBLANES, :]

        h_last = lax.fori_loop(0, nb, carry_in, h_scr[...], unroll=8)
        h_scr[...] = h_last
        hs = a * c_scr[...] + u
    else:
        hs = a * h0_ref[...] + u
        h_last = hs[:, SUBLANES - 1:SUBLANES, :]
    hr_out[...] = (hs * jax.nn.gelu(gr_ref[...].reshape(nb, SUBLANES, W))).reshape(bb, ts, W)

    tail = SUBLANES - (RG_CONV - 1)
    if chained:
        @pl.when(s == ns - 1)
        def _():
            h_out[0] = h_last
            buf_out[0] = x[nb - 1, tail:, :]
    else:
        h_out[...] = h_last
        buf_out[...] = x[:, tail:, :]


def rglru_mixer(z, h0, buf0, conv_w, conv_b, wa_bd, wx_bd, b_a, b_x, lam):
    B, S, _ = z.shape
    W = RG_WIDTH
    rows = 512
    chained = S > SUBLANES
    if chained:
        bb, ts = 1, min(S, rows)
        assert S % ts == 0 and ts % SUBLANES == 0
    else:
        bb, ts = min(B, rows // SUBLANES), S
        assert S == SUBLANES and B % bb == 0
    ns = S // ts
    nb = bb * ts // SUBLANES
    xr_blk = (4 * MLSTM_WIDTH) // W
    row = lambda a: a.reshape(1, W)
    buf8 = jnp.pad(buf0, ((0, 0), (SUBLANES - (RG_CONV - 1), 0), (0, 0)))
    st = lambda n: pl.BlockSpec((bb, n, W), lambda b, s: (b, 0, 0))
    hr, h1, buf1 = pl.pallas_call(
        functools.partial(_rglru_kernel, ns=ns, chained=chained),
        out_shape=(
            jax.ShapeDtypeStruct((B, S, W), F32),
            jax.ShapeDtypeStruct((B, 1, W), F32),
            jax.ShapeDtypeStruct((B, RG_CONV - 1, W), F32),
        ),
        grid=(B // bb, ns),
        in_specs=[
            pl.BlockSpec((bb, ts, W), lambda b, s: (b, s, xr_blk)),
            pl.BlockSpec((bb, ts, W), lambda b, s: (b, s, xr_blk + 1)),
            st(SUBLANES), st(1),
            _resident((RG_CONV, W)), _resident((1, W)), _resident((W, W)), _resident((W, W)),
            _resident((1, W)), _resident((1, W)), _resident((1, W)),
        ],
        out_specs=(pl.BlockSpec((bb, ts, W), lambda b, s: (b, s, 0)), st(1), st(RG_CONV - 1)),
        scratch_shapes=[pltpu.VMEM((SUBLANES, W), F32), pltpu.VMEM((1, W), F32),
                        pltpu.VMEM((nb, SUBLANES, W), F32), pltpu.VMEM((nb, SUBLANES, W), F32),
                        pltpu.VMEM((nb, SUBLANES, W), F32)],
        compiler_params=_cparams(("parallel", "arbitrary")),
        name="rglru",
    )(z, z, buf8, h0.reshape(B, 1, W), conv_w, row(conv_b), wa_bd, wx_bd, row(b_a), row(b_x), row(lam))
    return hr, h1.reshape(B, W), buf1


def _proj2_post_kernel(x_ref, gt_ref, a1_ref, a2_ref, w1_ref, w2_ref, lg_ref, lb_ref, o_ref):
    bb, ts, _ = x_ref.shape
    a1 = a1_ref[...].reshape(bb * ts, -1).astype(BF)
    a2 = a2_ref[...].reshape(bb * ts, -1).astype(BF)
    y = _dot(a1, w1_ref[...]) + _dot(a2, w2_ref[...])
    z = ALPHA * x_ref[...] + (1.0 + gt_ref[...]) * y.reshape(bb, ts, D_MODEL)
    o_ref[...] = _layer_norm(z, lg_ref[...], lb_ref[...])


def proj2_post(x, mod, a1, a2, w1, w2, lg, lb):
    B, S, _ = x.shape
    bb, ts = _row_tile(B, S, 512)
    K1, K2 = a1.shape[-1], a2.shape[-1]
    xmap = lambda i, s: (i, s, 0)
    full = lambda shp: pl.BlockSpec(shp, lambda i, s: (0,) * len(shp))
    return pl.pallas_call(
        _proj2_post_kernel,
        out_shape=jax.ShapeDtypeStruct(x.shape, F32),
        grid=(B // bb, S // ts),
        in_specs=[
            pl.BlockSpec((bb, ts, D_MODEL), xmap), _mod_spec(bb, 5),
            pl.BlockSpec((bb, ts, K1), xmap), pl.BlockSpec((bb, ts, K2), xmap),
            full((K1, D_MODEL)), full((K2, D_MODEL)), full((1, D_MODEL)), full((1, D_MODEL)),
        ],
        out_specs=pl.BlockSpec((bb, ts, D_MODEL), xmap),
        compiler_params=_cparams(("parallel", "parallel")),
        name="proj2_post",
    )(x, mod, a1, a2, w1, w2, lg, lb)


def _rwkv_pre_kernel(x_ref, sh_ref, sc_ref, prev0_ref, mu_ref, wr_ref, wk_ref, wv_ref, w0_ref, w1_ref,
                     w2_ref, a0_ref, a1_ref, a2_ref, g1_ref, g2_ref, kk_ref, ka_ref,
                     r_out, lw_out, k_out, v_out, kk_out, a_out, g_out, shift_out, carry_scr, *, ns):
    s = pl.program_id(1)
    bb, ts, _ = x_ref.shape
    tm = bb * ts

    @pl.when(s == 0)
    def _():
        carry_scr[...] = prev0_ref[...]

    h = x_ref[...] * (1.0 + sc_ref[...]) + sh_ref[...]
    t_idx = lax.broadcasted_iota(jnp.int32, h.shape, 1)
    h_prev = jnp.where(t_idx == 0, carry_scr[...], pltpu.roll(h, 1, 1))
    last = h[:, ts - 1:ts, :]
    carry_scr[...] = last
    dx = (h_prev - h).reshape(tm, D_MODEL)
    h2 = h.reshape(tm, D_MODEL)
    mu = mu_ref[...]
    mix = lambda j: (h2 + dx * mu[j:j + 1, :]).astype(BF)
    xr, xw, xk, xv, xa, xg = [mix(j) for j in range(6)]

    r = _dot(xr, wr_ref[...])
    k = _dot(xk, wk_ref[...])
    v = _dot(xv, wv_ref[...])
    wl = w0_ref[...] + _dot(jnp.tanh(_dot(xw, w1_ref[...])).astype(BF), w2_ref[...])
    w_log = -(jnp.maximum(-wl, 0.0) + jnp.log1p(jnp.exp(-jnp.abs(wl)))) - 0.5
    a = jax.nn.sigmoid(a0_ref[...] + _dot(_dot(xa, a1_ref[...]).astype(BF), a2_ref[...]))
    g = _dot(jax.nn.sigmoid(_dot(xg, g1_ref[...])).astype(BF), g2_ref[...])
    kk = k * kk_ref[...]
    k = k * (1.0 + (a - 1.0) * ka_ref[...])

    shp = x_ref.shape
    r_out[...] = r.reshape(shp)
    lw_out[...] = (-jnp.exp(w_log)).reshape(shp)
    k_out[...] = k.reshape(shp)
    v_out[...] = v.reshape(shp)
    kk_out[...] = kk.reshape(shp)
    a_out[...] = a.reshape(shp)
    g_out[...] = g.reshape(shp)

    @pl.when(s == ns - 1)
    def _():
        shift_out[...] = last


def rwkv_pre(x, mod, prev0, P):
    B, S, _ = x.shape
    bb, ts = _row_tile(B, S, 256)
    ns = S // ts
    xmap = lambda i, s: (i, s, 0)
    full = lambda shp: pl.BlockSpec(shp, lambda i, s: (0,) * len(shp))
    tok = pl.BlockSpec((bb, ts, D_MODEL), xmap)
    st = pl.BlockSpec((bb, 1, D_MODEL), lambda i, s: (i, 0, 0))
    sq, lo_in, lo_out, vec = (D_MODEL, D_MODEL), (D_MODEL, LORA_PAD), (LORA_PAD, D_MODEL), (1, D_MODEL)
    outs = pl.pallas_call(
        functools.partial(_rwkv_pre_kernel, ns=ns),
        out_shape=tuple([jax.ShapeDtypeStruct(x.shape, F32)] * 7
                        + [jax.ShapeDtypeStruct((B, 1, D_MODEL), F32)]),
        grid=(B // bb, ns),
        in_specs=[
            tok, _mod_spec(bb, 3), _mod_spec(bb, 4), st, full((6, D_MODEL)),
            full(sq), full(sq), full(sq), full(vec), full(lo_in), full(lo_out),
            full(vec), full(lo_in), full(lo_out), full(lo_in), full(lo_out),
            full(vec), full(vec),
        ],
        out_specs=tuple([tok] * 7 + [st]),
        scratch_shapes=[pltpu.VMEM((bb, 1, D_MODEL), F32)],
        compiler_params=_cparams(("parallel", "arbitrary")),
        name="rwkv_pre",
    )(x, mod, mod, prev0.reshape(B, 1, D_MODEL), P['mu'], P['wr'], P['wk'], P['wv'], P['w0'], P['w1'],
      P['w2'], P['a0'], P['a1'], P['a2'], P['g1'], P['g2'], P['k_k'], P['k_a'])
    return outs


def _stack_heads(x, first):
    return jnp.concatenate([jnp.where(first, x, 0.0), jnp.where(first, 0.0, x)], axis=0)


def _seg_sum(x, first):
    s0 = jnp.sum(jnp.where(first, x, 0.0), axis=1, keepdims=True)
    s1 = jnp.sum(jnp.where(first, 0.0, x), axis=1, keepdims=True)
    return jnp.where(first, s0, s1)


def _wkv_pair_chunks(toks, sps, vecs, consts, L):
    tri, first, strict, incl = consts
    L2 = 2 * L
    rng = range(len(toks))
    wide = L2 % LANES == 0
    splits = [_split3(t[1]) for t in toks]
    c_incl = [_dot(tri, s[0]) + _dot(tri, s[1]) + _dot(tri, s[2]) for s in splits]
    e_inv = [jnp.exp(-c) for c in c_incl]
    kn = [t[4] / jnp.maximum(jnp.sqrt(_seg_sum(t[4] * t[4], first)), 1e-12) for t in toks]
    la = [_stack_heads(-kn[p] * jnp.exp(c_incl[p] - toks[p][1]), first).astype(BF) for p in rng]
    lr = [_stack_heads(toks[p][0] * jnp.exp(c_incl[p]), first).astype(BF) for p in rng]
    rb = [_stack_heads(kn[p] * toks[p][5] * e_inv[p], first).astype(BF) for p in rng]
    rk = [_stack_heads(toks[p][2] * e_inv[p], first).astype(BF) for p in rng]
    vs = [_stack_heads(toks[p][3], first).astype(BF) for p in rng]
    rbk = [jnp.concatenate([rb[p], rk[p]], axis=0) for p in rng]

    if wide:
        nn = [_dot_nt(la[p], rbk[p]) for p in rng]
        mm = [_dot_nt(lr[p], rbk[p]) for p in rng]
        n_raw = [x[:, :L2] for x in nn]
        n_ak = [jnp.where(strict, x[:, L2:], 0.0).astype(BF) for x in nn]
        m_bk = [jnp.concatenate([jnp.where(incl, x[:, :L2], 0.0), jnp.where(incl, x[:, L2:], 0.0)],
                                axis=1).astype(BF) for x in mm]
    else:
        n_raw = [_dot_nt(la[p], rb[p]) for p in rng]
        n_ak = [jnp.where(strict, _dot_nt(la[p], rk[p]), 0.0).astype(BF) for p in rng]
        m_rb = [jnp.where(incl, _dot_nt(lr[p], rb[p]), 0.0).astype(BF) for p in rng]
        m_rk = [jnp.where(incl, _dot_nt(lr[p], rk[p]), 0.0).astype(BF) for p in rng]

    row = lax.broadcasted_iota(jnp.int32, (L2, L2), 0)
    col = lax.broadcasted_iota(jnp.int32, (L2, L2), 1)
    base = (row // INV_BASE == col // INV_BASE) & (col < row)
    xb = [jnp.where(base, x, 0.0).astype(BF) for x in n_raw]
    tinv = [jnp.where(row == col, 1.0, 0.0) + x.astype(F32) for x in xb]
    x2 = [_dot(x, x).astype(BF) for x in xb]
    if wide:
        prod = [_dot(x2[p], jnp.concatenate([x2[p], tinv[p].astype(BF)], axis=1)) for p in rng]
        tinv = [tinv[p] + prod[p][:, L2:] for p in rng]
        x4 = [x[:, :L2].astype(BF) for x in prod]
    else:
        tinv = [tinv[p] + _dot(x2[p], tinv[p].astype(BF)) for p in rng]
        x4 = [_dot(x, x).astype(BF) for x in x2]
    tinv = [tinv[p] + _dot(x4[p], tinv[p].astype(BF)) for p in rng]
    blk = INV_BASE
    while blk < L:
        off = (row // (2 * blk) == col // (2 * blk)) & (row // blk == col // blk + 1)
        tb = [t.astype(BF) for t in tinv]
        pr = [_dot(jnp.where(off, n_raw[p], 0.0).astype(BF), tb[p]).astype(BF) for p in rng]
        tinv = [tinv[p] + _dot(tb[p], pr[p]) for p in rng]
        blk *= 2

    spb = [s.astype(BF) for s in sps]
    rhs = [_dot_nt(la[p], spb[p]) + _dot(n_ak[p], vs[p]) for p in rng]
    ub = [_dot(tinv[p].astype(BF), rhs[p].astype(BF)).astype(BF) for p in rng]

    uv = [jnp.concatenate([ub[p], vs[p]], axis=0) for p in rng]
    if wide:
        ys = [_dot_nt(lr[p], spb[p]) + _dot(m_bk[p], uv[p]) for p in rng]
    else:
        ys = [_dot_nt(lr[p], spb[p]) + _dot(m_rb[p], ub[p]) + _dot(m_rk[p], vs[p]) for p in rng]
    sp_new = [(sps[p] + _dot_tn(uv[p], rbk[p])) * jnp.exp(c_incl[p][L - 1:L, :]) for p in rng]

    outs = []
    for p in rng:
        r, _, k, v, _, _ = toks[p]
        r_k, lnx_g, lnx_b = vecs[p]
        y = ys[p][0:L, :] + ys[p][L:L2, :]
        yc = y - _seg_sum(y, first) * (1.0 / RWKV_HS)
        yv = _seg_sum(yc * yc, first) * (1.0 / RWKV_HS)
        yn = yc * lax.rsqrt(yv + RWKV_LN_EPS) * lnx_g + lnx_b
        outs.append((yn + _seg_sum(r * k * r_k, first) * v, sp_new[p]))
    return outs


def _wkv_kernel(r_ref, lw_ref, k_ref, v_ref, kk_ref, a_ref, s0_ref, rk_ref, lxg_ref, lxb_ref,
                y_out, s_out, sp_scr, *, L, nc, bb, group):
    c = pl.program_id(1)
    hs = RWKV_HS
    npair = RWKV_HEADS // 2
    L2 = 2 * L

    @pl.when(c == 0)
    def _():
        zero = jnp.zeros((hs, hs), F32)

        def init(i, carry):
            for p in range(npair):
                top = jnp.concatenate([s0_ref[i, 2 * p], zero], axis=1)
                bot = jnp.concatenate([zero, s0_ref[i, 2 * p + 1]], axis=1)
                sp_scr[i, p] = jnp.concatenate([top, bot], axis=0)
            return carry

        lax.fori_loop(0, bb, init, 0)

    row = lax.broadcasted_iota(jnp.int32, (L, L), 0)
    col = lax.broadcasted_iota(jnp.int32, (L, L), 1)
    tri = jnp.where(col <= row, 1.0, 0.0).astype(BF)
    first = lax.broadcasted_iota(jnp.int32, (L, LANES), 1) < hs
    row2 = lax.broadcasted_iota(jnp.int32, (L2, L2), 0)
    col2 = lax.broadcasted_iota(jnp.int32, (L2, L2), 1)
    same = (row2 >= L) == (col2 >= L)
    consts = (tri, first, same & (col2 < row2), same & (col2 <= row2))
    sls = [slice(p * LANES, (p + 1) * LANES) for p in range(npair)]

    def seqs(j, carry):
        ids = [(j * group + g, p) for g in range(group) for p in range(npair)]
        toks = [tuple(ref[i, :, sls[p]] for ref in (r_ref, lw_ref, k_ref, v_ref, kk_ref, a_ref)) for i, p in ids]
        vecs = [(rk_ref[:, sls[p]], lxg_ref[:, sls[p]], lxb_ref[:, sls[p]]) for _, p in ids]
        outs = _wkv_pair_chunks(toks, [sp_scr[i, p] for i, p in ids], vecs, consts, L)
        for (i, p), (y, sp_new) in zip(ids, outs):
            y_out[i, :, sls[p]] = y
            sp_scr[i, p] = sp_new
        return carry

    lax.fori_loop(0, bb // group, seqs, 0)

    @pl.when(c == nc - 1)
    def _():
        def fin(i, carry):
            for p in range(npair):
                sp = sp_scr[i, p]
                s_out[i, 2 * p] = sp[0:hs, 0:hs]
                s_out[i, 2 * p + 1] = sp[hs:2 * hs, hs:2 * hs]
            return carry

        lax.fori_loop(0, bb, fin, 0)


def wkv_recurrence(r, lw, k, v, kk, a, s0, P):
    B, S, _ = r.shape
    L = RWKV_CHUNK if S % RWKV_CHUNK == 0 else S
    assert L & (L - 1) == 0 and L % SUBLANES == 0
    nc = S // L
    bb, group = (2, 2) if nc > 1 else (min(B, 8), 4)
    assert B % bb == 0 and bb % group == 0
    npair = RWKV_HEADS // 2
    tok = pl.BlockSpec((bb, L, D_MODEL), lambda b, c: (b, c, 0))
    st = pl.BlockSpec((bb, RWKV_HEADS, RWKV_HS, RWKV_HS), lambda b, c: (b, 0, 0, 0))
    vec = pl.BlockSpec((1, D_MODEL), lambda b, c: (0, 0))
    return pl.pallas_call(
        functools.partial(_wkv_kernel, L=L, nc=nc, bb=bb, group=group),
        out_shape=(jax.ShapeDtypeStruct(r.shape, F32), jax.ShapeDtypeStruct(s0.shape, F32)),
        grid=(B // bb, nc),
        in_specs=[tok] * 6 + [st, vec, vec, vec],
        out_specs=(tok, st),
        scratch_shapes=[pltpu.VMEM((bb, npair, 2 * RWKV_HS, 2 * RWKV_HS), F32)],
        compiler_params=_cparams(("parallel", "arbitrary")),
        name="wkv",
    )(r, lw, k, v, kk, a, s0, P['r_k'], P['lnx_g'], P['lnx_b'])


def _rwkv_post_kernel(x_ref, gt_ref, y_ref, g_ref, wo_ref, lg_ref, lb_ref, o_ref):
    bb, ts, _ = x_ref.shape
    yg = (y_ref[...] * g_ref[...]).reshape(bb * ts, D_MODEL).astype(BF)
    out = _dot(yg, wo_ref[...])
    z = ALPHA * x_ref[...] + (1.0 + gt_ref[...]) * out.reshape(bb, ts, D_MODEL)
    o_ref[...] = _layer_norm(z, lg_ref[...], lb_ref[...])


def rwkv_post(x, mod, y, g, P, lg, lb):
    B, S, _ = x.shape
    bb, ts = _row_tile(B, S, 512)
    xmap = lambda i, s: (i, s, 0)
    tok = pl.BlockSpec((bb, ts, D_MODEL), xmap)
    vec = (1, D_MODEL)
    return pl.pallas_call(
        _rwkv_post_kernel,
        out_shape=jax.ShapeDtypeStruct(x.shape, F32),
        grid=(B // bb, S // ts),
        in_specs=[tok, _mod_spec(bb, 5), tok, tok, _resident((D_MODEL, D_MODEL)), _resident(vec), _resident(vec)],
        out_specs=tok,
        compiler_params=_cparams(("parallel", "parallel")),
        name="rwkv_post",
    )(x, mod, y, g, P['wo'], lg, lb)


def _block_diag(w):
    G, n, _ = w.shape
    eye = jnp.eye(G, dtype=w.dtype)
    return (eye[:, None, :, None] * w[:, :, None, :]).reshape(G * n, G * n)


def _pad_cols(w, n):
    return jnp.pad(w, ((0, 0), (0, n - w.shape[1])))


def _pad_rows(w, n):
    return jnp.pad(w, ((0, n - w.shape[0]), (0, 0)))


def _prep_ab(j, p):
    w_in = p['ab_w_in'][j]
    n_main = 4 * MLSTM_WIDTH
    n_gate = 2 * MLSTM_HEADS
    w_all = jnp.concatenate([w_in[:, :n_main], w_in[:, n_main + n_gate:],
                             w_in[:, n_main:n_main + n_gate]], axis=1)
    return dict(
        w_in=_pad_cols(w_all, Z_COLS).astype(BF),
        b_gates=p['mlstm_b_gates'][j], norm_g=p['mlstm_norm_g'][j],
        conv_w=p['rg_conv_w'][j], conv_b=p['rg_conv_b'][j],
        wa=_block_diag(p['rg_w_a'][j]).astype(BF), wx=_block_diag(p['rg_w_x'][j]).astype(BF),
        b_a=p['rg_b_a'][j], b_x=p['rg_b_x'][j], lam=p['rg_lambda'][j],
        w_out_m=p['ab_w_out'][j][:MLSTM_WIDTH].astype(BF), w_out_r=p['ab_w_out'][j][MLSTM_WIDTH:].astype(BF),
    )


def _prep_rwkv(j, p):
    vec = lambda a: a.reshape(1, D_MODEL)
    return dict(
        mu=p['rw_mu'][j],
        wr=p['rw_wr'][j].astype(BF), wk=p['rw_wk'][j].astype(BF), wv=p['rw_wv'][j].astype(BF),
        w0=vec(p['rw_w0'][j]), w1=_pad_cols(p['rw_w1'][j], LORA_PAD).astype(BF),
        w2=_pad_rows(p['rw_w2'][j], LORA_PAD).astype(BF),
        a0=vec(p['rw_a0'][j]), a1=_pad_cols(p['rw_a1'][j], LORA_PAD).astype(BF),
        a2=_pad_rows(p['rw_a2'][j], LORA_PAD).astype(BF),
        g1=_pad_cols(p['rw_g1'][j], LORA_PAD).astype(BF), g2=_pad_rows(p['rw_g2'][j], LORA_PAD).astype(BF),
        k_k=vec(p['rw_k_k'][j]), k_a=vec(p['rw_k_a'][j]), r_k=vec(p['rw_r_k'][j]),
        lnx_g=vec(p['rw_lnx_g'][j]), lnx_b=vec(p['rw_lnx_b'][j]), wo=p['rw_wo'][j].astype(BF),
    )


def ab_mixer(x, mod, st, A, lg, lb):
    mC, mn, mm, rh, rconv = st
    z, gates = mod_matmul(x, mod, A['w_in'], 4 * MLSTM_WIDTH + 2 * RG_WIDTH, 2 * MLSTM_HEADS)
    hm, C1, n1, m1 = mlstm_mixer(z, gates, mC, mn, mm, A['b_gates'], A['norm_g'])
    hr, rh1, buf1 = rglru_mixer(z, rh, rconv, A['conv_w'], A['conv_b'], A['wa'], A['wx'],
                                A['b_a'], A['b_x'], A['lam'])
    x = proj2_post(x, mod, hm, hr, A['w_out_m'], A['w_out_r'], lg, lb)
    return x, (C1, n1, m1, rh1, buf1)


def rwkv_mixer(x, mod, st, R, lg, lb):
    wkv0, prev0 = st
    r, lw, k, v, kk, a, g, shift = rwkv_pre(x, mod, prev0, R)
    y, wkv1 = wkv_recurrence(r, lw, k, v, kk, a, wkv0, R)
    x = rwkv_post(x, mod, y, g, R, lg, lb)
    return x, (wkv1, shift.reshape(shift.shape[0], D_MODEL))


def run_trunk(x, mods, states, W):
    mC, mn, mm, rh, rconv, wkv, shift = states
    new_ab, new_c = [], []
    for layer in range(DEPTH):
        mod = mods[layer]
        lg = lambda i: W['ln_g'][layer, i].reshape(1, D_MODEL)
        lb = lambda i: W['ln_b'][layer, i].reshape(1, D_MODEL)
        f = W['ffn']
        x = ffn_block(x, mod, 0, layer, 0, f[0], f[1], f[2], lg(0), lb(0))
        j = layer // 2
        if layer % 2 == 0:
            x, st = ab_mixer(x, mod, (mC[j], mn[j], mm[j], rh[j], rconv[j]), W['ab'][j], lg(1), lb(1))
            new_ab.append(st)
        else:
            x, st = rwkv_mixer(x, mod, (wkv[j], shift[j]), W['rwkv'][j], lg(1), lb(1))
            new_c.append(st)
        x = ffn_block(x, mod, 2, layer, 1, f[0], f[1], f[2], lg(2), lb(2))
    stk = lambda sts, i: sts[0][i][None] if len(sts) == 1 else jnp.stack([s[i] for s in sts], axis=0)
    return x, (stk(new_ab, 0), stk(new_ab, 1), stk(new_ab, 2), stk(new_ab, 3), stk(new_ab, 4),
               stk(new_c, 0), stk(new_c, 1))


def _zero_states(B):
    n_ab, n_c = (DEPTH + 1) // 2, DEPTH // 2
    return (jnp.zeros((n_ab, B, MLSTM_HEADS, MLSTM_HD, MLSTM_HD), F32),
            jnp.zeros((n_ab, B, MLSTM_HEADS, MLSTM_HD), F32),
            jnp.zeros((n_ab, B, MLSTM_HEADS), F32),
            jnp.zeros((n_ab, B, RG_WIDTH), F32),
            jnp.zeros((n_ab, B, RG_CONV - 1, RG_WIDTH), F32),
            jnp.zeros((n_c, B, RWKV_HEADS, RWKV_HS, RWKV_HS), F32),
            jnp.zeros((n_c, B, D_MODEL), F32))


def kernel(x_prompt, x_sample, c_prompt, c_sample, state_mlstm_C, state_mlstm_n, state_mlstm_m, state_rglru_h, state_rglru_conv, state_rwkv_wkv, state_rwkv_shift, ada_w, ada_b, ln_g, ln_b, ffn_w1, ffn_w3, ffn_w2, ab_w_in, mlstm_b_gates, mlstm_norm_g, rg_conv_w, rg_conv_b, rg_w_a, rg_b_a, rg_w_x, rg_b_x, rg_lambda, ab_w_out, rw_mu, rw_wr, rw_wk, rw_wv, rw_w0, rw_w1, rw_w2, rw_a0, rw_a1, rw_a2, rw_g1, rw_g2, rw_k_k, rw_k_a, rw_r_k, rw_lnx_g, rw_lnx_b, rw_wo):
    p = dict(ab_w_in=ab_w_in, mlstm_b_gates=mlstm_b_gates, mlstm_norm_g=mlstm_norm_g,
             rg_conv_w=rg_conv_w, rg_conv_b=rg_conv_b, rg_w_a=rg_w_a, rg_b_a=rg_b_a, rg_w_x=rg_w_x,
             rg_b_x=rg_b_x, rg_lambda=rg_lambda, ab_w_out=ab_w_out, rw_mu=rw_mu, rw_wr=rw_wr,
             rw_wk=rw_wk, rw_wv=rw_wv, rw_w0=rw_w0, rw_w1=rw_w1, rw_w2=rw_w2, rw_a0=rw_a0,
             rw_a1=rw_a1, rw_a2=rw_a2, rw_g1=rw_g1, rw_g2=rw_g2, rw_k_k=rw_k_k, rw_k_a=rw_k_a,
             rw_r_k=rw_r_k, rw_lnx_g=rw_lnx_g, rw_lnx_b=rw_lnx_b, rw_wo=rw_wo)
    W = dict(
        ln_g=ln_g, ln_b=ln_b,
        ffn=(ffn_w1.astype(BF), ffn_w3.astype(BF), ffn_w2.astype(BF)),
        ab=[_prep_ab(j, p) for j in range((DEPTH + 1) // 2)],
        rwkv=[_prep_rwkv(j, p) for j in range(DEPTH // 2)],
    )
    Bp, Bs = x_prompt.shape[0], x_sample.shape[0]
    mod_all = adaln(jnp.concatenate([c_prompt, c_sample], axis=0), ada_w, ada_b)
    mods_p = [mod_all[l, :Bp].reshape(Bp, 3 * N_SUB, 1, D_MODEL) for l in range(DEPTH)]
    mods_s = [mod_all[l, Bp:].reshape(Bs, 3 * N_SUB, 1, D_MODEL) for l in range(DEPTH)]
    y_prompt, sp = run_trunk(x_prompt, mods_p, _zero_states(Bp), W)
    y_sample, ss = run_trunk(x_sample, mods_s,
                             (state_mlstm_C, state_mlstm_n, state_mlstm_m, state_rglru_h,
                              state_rglru_conv, state_rwkv_wkv, state_rwkv_shift), W)
    return (y_prompt, y_sample) + tuple(sp) + tuple(ss)
```

```python
import functools

import jax
import jax.numpy as jnp
from jax import lax
from jax.experimental import pallas as pl
from jax.experimental.pallas import tpu as pltpu

D_MODEL = 1024
DEPTH = 2
N_SUB = 3
MLSTM_WIDTH = 512
MLSTM_HEADS = 4
MLSTM_HD = 128
MLSTM_CHUNK = 128
RG_WIDTH = 512
RG_BLOCKS = 8
RG_BD = 64
RG_CONV = 4
RG_C = 8.0
RWKV_HS = 64
RWKV_HEADS = 16
RWKV_LN_EPS = 64e-5
RWKV_CHUNK = 64
INV_BASE = 8
D_FF = 2816
ALPHA = (2.0 * DEPTH) ** 0.25
LN_EPS = 1e-5
HEAD_NORM_EPS = 1e-6

LANES = 128
SUBLANES = 8
VMEM_LIMIT_BYTES = 56 * 1024 * 1024
Z_COLS = 3200
LORA_PAD = 128

BF = jnp.bfloat16
F32 = jnp.float32

NT_DIMS = (((1,), (1,)), ((), ()))
TN_DIMS = (((0,), (0,)), ((), ()))


def _cparams(sem):
    return pltpu.CompilerParams(dimension_semantics=sem, vmem_limit_bytes=VMEM_LIMIT_BYTES)


def _dot(a, b):
    return jnp.dot(a, b, preferred_element_type=F32)


def _dot_nt(a, b):
    return lax.dot_general(a, b, NT_DIMS, preferred_element_type=F32)


def _dot_tn(a, b):
    return lax.dot_general(a, b, TN_DIMS, preferred_element_type=F32)


def _layer_norm(z, g, b):
    mu = jnp.mean(z, axis=-1, keepdims=True)
    zc = z - mu
    var = jnp.mean(zc * zc, axis=-1, keepdims=True)
    return zc * lax.rsqrt(var + LN_EPS) * g + b


def _row_tile(B, S, rows):
    if S >= rows:
        assert S % rows == 0
        return 1, rows
    bb = min(B, rows // S)
    assert B % bb == 0
    return bb, S


def _split2(x):
    hi = x.astype(BF)
    lo = (x - hi.astype(F32)).astype(BF)
    return hi, lo


def _split3(x):
    hi = x.astype(BF)
    r1 = x - hi.astype(F32)
    mid = r1.astype(BF)
    lo = (r1 - mid.astype(F32)).astype(BF)
    return hi, mid, lo


def _adaln_kernel(c_ref, w_ref, b_ref, o_ref):
    h = jax.nn.silu(c_ref[...]).astype(BF)
    o_ref[...] = _dot(h, w_ref[...].astype(BF)) + b_ref[...]


def adaln(c_all, ada_w, ada_b):
    Bc = c_all.shape[0]
    n_out = ada_w.shape[-1]
    tn = 1024
    return pl.pallas_call(
        _adaln_kernel,
        out_shape=jax.ShapeDtypeStruct((DEPTH, Bc, n_out), F32),
        grid=(DEPTH, n_out // tn),
        in_specs=[
            pl.BlockSpec((Bc, D_MODEL), lambda l, j: (0, 0)),
            pl.BlockSpec((None, D_MODEL, tn), lambda l, j: (l, 0, j)),
            pl.BlockSpec((None, 1, tn), lambda l, j: (l, 0, j)),
        ],
        out_specs=pl.BlockSpec((None, Bc, tn), lambda l, j: (l, 0, j)),
        compiler_params=_cparams(("parallel", "parallel")),
        name="adaln",
    )(c_all, ada_w, ada_b.reshape(DEPTH, 1, n_out))


def _mod_spec(bb, k):
    return pl.BlockSpec((bb, None, 1, D_MODEL), lambda i, *_: (i, k, 0, 0))


def _ffn_kernel(x_ref, sh_ref, sc_ref, gt_ref, w1_ref, w3_ref, w2_ref, lg_ref, lb_ref, o_ref, *, res_w):
    bb, ts, _ = x_ref.shape
    x = x_ref[...]
    h = (x * (1.0 + sc_ref[...]) + sh_ref[...]).reshape(bb * ts, D_MODEL).astype(BF)
    a = _dot(h, w1_ref[...])
    b = _dot(h, w3_ref[...])
    g = (jax.nn.silu(a) * b).astype(BF)
    y = _dot(g, w2_ref[...])
    z = ALPHA * x + (res_w * (1.0 + gt_ref[...])) * y.reshape(bb, ts, D_MODEL)
    o_ref[...] = _layer_norm(z, lg_ref[...], lb_ref[...])


def _resident(shape):
    return pl.BlockSpec(shape, lambda *_: (0,) * len(shape), pipeline_mode=pl.Buffered(1))


def ffn_block(x, mod, sub, layer, half, w1, w3, w2, lg, lb):
    B, S, _ = x.shape
    bb, ts = _row_tile(B, S, 512)
    xmap = lambda i, s: (i, s, 0)
    wspec = lambda r, c: pl.BlockSpec((None, None, r, c), lambda i, s: (layer, half, 0, 0),
                                      pipeline_mode=pl.Buffered(1))
    return pl.pallas_call(
        functools.partial(_ffn_kernel, res_w=0.5),
        out_shape=jax.ShapeDtypeStruct(x.shape, F32),
        grid=(B // bb, S // ts),
        in_specs=[
            pl.BlockSpec((bb, ts, D_MODEL), xmap),
            _mod_spec(bb, 3 * sub), _mod_spec(bb, 3 * sub + 1), _mod_spec(bb, 3 * sub + 2),
            wspec(D_MODEL, D_FF), wspec(D_MODEL, D_FF), wspec(D_FF, D_MODEL),
            _resident((1, D_MODEL)), _resident((1, D_MODEL)),
        ],
        out_specs=pl.BlockSpec((bb, ts, D_MODEL), xmap),
        compiler_params=_cparams(("parallel", "parallel")),
        name="ffn_block",
    )(x, mod, mod, mod, w1, w3, w2, lg, lb)


def _modmm_kernel(x_ref, sh_ref, sc_ref, w_ref, o_ref, gate_ref):
    bb, ts, _ = x_ref.shape
    n_main = o_ref.shape[-1]
    n_gate = gate_ref.shape[-1]
    h = (x_ref[...] * (1.0 + sc_ref[...]) + sh_ref[...]).reshape(bb * ts, D_MODEL).astype(BF)
    o_ref[...] = _dot(h, w_ref[:, :n_main]).reshape(o_ref.shape)
    gate_ref[...] = _dot(h, w_ref[:, n_main:])[:, :n_gate].reshape(gate_ref.shape)


def mod_matmul(x, mod, w, n_main, n_gate):
    B, S, _ = x.shape
    bb, ts = _row_tile(B, S, 512)
    xmap = lambda i, s: (i, s, 0)
    return pl.pallas_call(
        _modmm_kernel,
        out_shape=(jax.ShapeDtypeStruct((B, S, n_main), F32), jax.ShapeDtypeStruct((B, S, n_gate), F32)),
        grid=(B // bb, S // ts),
        in_specs=[
            pl.BlockSpec((bb, ts, D_MODEL), xmap),
            _mod_spec(bb, 3), _mod_spec(bb, 4),
            _resident(w.shape),
        ],
        out_specs=(pl.BlockSpec((bb, ts, n_main), xmap), pl.BlockSpec((bb, ts, n_gate), xmap)),
        compiler_params=_cparams(("parallel", "parallel")),
        name="mod_matmul",
    )(x, mod, mod, w)


def _log_sigmoid(x):
    return jnp.minimum(x, 0.0) - jnp.log1p(jnp.exp(-jnp.abs(x)))


def _lockstep(gens):
    results = [None] * len(gens)
    live = list(range(len(gens)))
    while live:
        still = []
        for idx in live:
            try:
                next(gens[idx])
                still.append(idx)
            except StopIteration as stop:
                results[idx] = stop.value
        live = still
    return results


def _dot_exact(a, b, dims):
    return lax.dot_general(a, b, dims, precision=lax.Precision.HIGHEST, preferred_element_type=F32)


def _mlstm_gate_views(gl, L):
    row = lax.broadcasted_iota(jnp.int32, (L, L), 0)
    col = lax.broadcasted_iota(jnp.int32, (L, L), 1)
    eye = jnp.where(row == col, 1.0, 0.0)
    lower = jnp.where(col <= row, 1.0, 0.0)
    upper = jnp.where(row <= col, 1.0, 0.0)
    gl_t = _dot_exact(gl, eye, TN_DIMS)
    cum_c = _dot_exact(lower, gl, (((1,), (0,)), ((), ())))
    cum_r = _dot_exact(gl, upper, TN_DIMS)
    return gl_t, cum_c, cum_r


def _mlstm_head_chunk(q, k, v, o, ig_c, b_c, ig_r, b_r, C0, n0, m0, g, L):
    row = lax.broadcasted_iota(jnp.int32, (L, L), 0)
    col = lax.broadcasted_iota(jnp.int32, (L, L), 1)
    causal = col <= row
    dmat = jnp.where(causal, b_c - b_r + ig_r, -jnp.inf)
    m_inter = b_c + m0
    m = jnp.maximum(m_inter, jnp.max(dmat, axis=1, keepdims=True))
    w_inter = jnp.exp(m_inter - m)
    p = jnp.exp(dmat - m)

    k = k * (MLSTM_HD ** -0.5)
    qb = q.astype(BF)
    kb = k.astype(BF)
    yield
    s_raw = _dot_nt(qb, kb)
    q_c0 = _dot_nt(qb, C0.astype(BF))
    yield
    scores = s_raw * p
    num = _dot(scores.astype(BF), v.astype(BF)) + w_inter * q_c0
    den = jnp.sum(scores, axis=1, keepdims=True) + w_inter * jnp.sum(q * n0, axis=1, keepdims=True)
    hh = num / jnp.maximum(jnp.abs(den), jnp.exp(-m))

    m_end = m[L - 1:L, :]
    b_end = b_c[L - 1:L, :]
    w_state = jnp.exp(b_end + m0 - m_end)
    w_rows = jnp.exp(b_end - b_c + ig_c - m_end)
    yield
    c_new = w_state * C0 + _dot_tn((w_rows * v).astype(BF), kb)
    n_new = w_state * n0 + jnp.sum(w_rows * k, axis=0, keepdims=True)

    mu = jnp.mean(hh, axis=1, keepdims=True)
    hc = hh - mu
    var = jnp.mean(hc * hc, axis=1, keepdims=True)
    hn = hc * lax.rsqrt(var + HEAD_NORM_EPS)
    return hn * g * jax.nn.sigmoid(o), c_new, n_new, m_end


def _mlstm_kernel(m0_ref, bg_ref, q_ref, k_ref, v_ref, o_ref, gc_ref, c0_ref, n0_ref, g_ref,
                  h_out, c_out, n_out, m_out, c_scr, n_scr, m_scr, *, L, nc, bb, group):
    ib = pl.program_id(0)
    c = pl.program_id(1)
    H = MLSTM_HEADS

    @pl.when(c == 0)
    def _():
        c_scr[...] = c0_ref[...]
        n_scr[...] = n0_ref[...]

        def init(i, carry):
            for h in range(H):
                m_scr[i, h] = jnp.full((1, LANES), m0_ref[(ib * bb + i) * H + h], F32)
            return carry

        lax.fori_loop(0, bb, init, 0)

    sls = [slice(h * MLSTM_HD, (h + 1) * MLSTM_HD) for h in range(H)]

    def seqs(j, carry):
        views = []
        for g in range(group):
            gates = gc_ref[j * group + g] + bg_ref[...]
            is_input = lax.broadcasted_iota(jnp.int32, gates.shape, 1) < H
            gl = jnp.where(is_input, gates, _log_sigmoid(gates))
            views.append((gl,) + _mlstm_gate_views(gl, L))
        ids = [(j * group + g, h) for g in range(group) for h in range(H)]
        gate_args = [(views[g][0][:, h:h + 1], views[g][2][:, H + h:H + h + 1],
                      views[g][1][h:h + 1, :], views[g][3][H + h:H + h + 1, :])
                     for g in range(group) for h in range(H)]
        heads = _lockstep([_mlstm_head_chunk(
            q_ref[i, :, sls[h]], k_ref[i, :, sls[h]], v_ref[i, :, sls[h]], o_ref[i, :, sls[h]], *gate_args[n],
            c_scr[i, h], n_scr[i, h], m_scr[i, h][:, 0:1], g_ref[:, sls[h]], L) for n, (i, h) in enumerate(ids)])
        for (i, h), (out, c_new, n_new, m_end) in zip(ids, heads):
            h_out[i, :, sls[h]] = out
            c_scr[i, h] = c_new
            n_scr[i, h] = n_new
            m_scr[i, h] = jnp.broadcast_to(m_end, (1, LANES))
        return carry

    lax.fori_loop(0, bb // group, seqs, 0)

    @pl.when(c == nc - 1)
    def _():
        c_out[...] = c_scr[...]
        n_out[...] = n_scr[...]
        m_out[...] = m_scr[...]


def mlstm_mixer(z, gates, C0, n0, m0, b_gates, norm_g):
    B, S, _ = z.shape
    H = MLSTM_HEADS
    W = MLSTM_WIDTH
    L = MLSTM_CHUNK if S % MLSTM_CHUNK == 0 else S
    nc = S // L
    bb, group = (1, 1) if nc > 1 else (min(B, 8), min(B, 4))
    assert B % bb == 0 and bb % group == 0
    zspec = lambda blk: pl.BlockSpec((bb, L, W), lambda b, c: (b, c, blk))
    smem = pl.BlockSpec(memory_space=pltpu.SMEM)
    hd_spec = pl.BlockSpec((bb, H, 1, MLSTM_HD), lambda b, c: (b, 0, 0, 0))
    c_spec = pl.BlockSpec((bb, H, MLSTM_HD, MLSTM_HD), lambda b, c: (b, 0, 0, 0))
    hm, C1, n1, m1 = pl.pallas_call(
        functools.partial(_mlstm_kernel, L=L, nc=nc, bb=bb, group=group),
        out_shape=(
            jax.ShapeDtypeStruct((B, S, W), F32),
            jax.ShapeDtypeStruct((B, H, MLSTM_HD, MLSTM_HD), F32),
            jax.ShapeDtypeStruct((B, H, 1, MLSTM_HD), F32),
            jax.ShapeDtypeStruct((B, H, 1, LANES), F32),
        ),
        grid=(B // bb, nc),
        in_specs=[
            smem, pl.BlockSpec((1, 2 * H), lambda b, c: (0, 0)),
            zspec(0), zspec(1), zspec(2), zspec(3),
            pl.BlockSpec((bb, L, 2 * H), lambda b, c: (b, c, 0)),
            c_spec, hd_spec,
            pl.BlockSpec((1, W), lambda b, c: (0, 0)),
        ],
        out_specs=(pl.BlockSpec((bb, L, W), lambda b, c: (b, c, 0)), c_spec, hd_spec, hd_spec),
        scratch_shapes=[pltpu.VMEM((bb, H, MLSTM_HD, MLSTM_HD), F32), pltpu.VMEM((bb, H, 1, MLSTM_HD), F32),
                        pltpu.VMEM((bb, H, 1, LANES), F32)],
        compiler_params=_cparams(("parallel", "arbitrary")),
        name="mlstm",
    )(m0.reshape(B * H), b_gates.reshape(1, 2 * H), z, z, z, z, gates,
      C0, n0.reshape(B, H, 1, MLSTM_HD), norm_g.reshape(1, W))
    return hm, C1, n1.reshape(B, H, MLSTM_HD), m1[:, :, 0, 0]


def _expm1(y):
    u = jnp.exp(y)
    small = jnp.where(u == 1.0, y, (u - 1.0) * y / jnp.log(u))
    return jnp.where(jnp.abs(y) > 0.5, u - 1.0, small)


def _rglru_kernel(xr_ref, gr_ref, buf_ref, h0_ref, cw_ref, cb_ref, wa_ref, wx_ref, ba_ref, bx_ref, lam_ref,
                  hr_out, h_out, buf_out, prev_scr, h_scr, a_scr, u_scr, c_scr, *, ns, chained):
    s = pl.program_id(1)
    bb, ts, W = xr_ref.shape
    nb = bb * ts // SUBLANES
    x = xr_ref[...].reshape(nb, SUBLANES, W)

    if chained:
        @pl.when(s == 0)
        def _():
            prev_scr[...] = buf_ref[0]
            h_scr[...] = h0_ref[0]

        prev = jnp.concatenate([prev_scr[...][None], x[:nb - 1]], axis=0)
        prev_scr[...] = x[nb - 1]
    else:
        prev = buf_ref[...]

    t_idx = lax.broadcasted_iota(jnp.int32, x.shape, 1)
    cw = cw_ref[...]
    xc = cb_ref[...] + cw[RG_CONV - 1:RG_CONV, :] * x
    for d in range(1, RG_CONV):
        back = jnp.where(t_idx >= d, pltpu.roll(x, d, 1), pltpu.roll(prev, d, 1))
        xc = xc + cw[RG_CONV - 1 - d:RG_CONV - d, :] * back

    xc2 = xc.reshape(nb * SUBLANES, W)
    xb = xc2.astype(BF)
    r = jax.nn.sigmoid(_dot(xb, wa_ref[...]) + ba_ref[...])
    i = jax.nn.sigmoid(_dot(xb, wx_ref[...]) + bx_ref[...])
    lam = lam_ref[...]
    softplus_neg = jnp.maximum(-lam, 0.0) + jnp.log1p(jnp.exp(-jnp.abs(lam)))
    log_a = (-RG_C * softplus_neg) * r
    a = jnp.exp(log_a).reshape(nb, SUBLANES, W)
    u = (jnp.sqrt(-_expm1(2.0 * log_a)) * i * xc2).reshape(nb, SUBLANES, W)

    d = 1
    while d < SUBLANES:
        inside = t_idx >= d
        u = jnp.where(inside, a * pltpu.roll(u, d, 1) + u, u)
        a = jnp.where(inside, a * pltpu.roll(a, d, 1), a)
        d *= 2

    if chained:
        a_scr[...] = a
        u_scr[...] = u

        def carry_in(k, h):
            c_scr[k] = jnp.broadcast_to(h, (SUBLANES, W))
            return a_scr[k, SUBLANES - 1:SUBLANES, :] * h + u_scr[k, SUBLANES - 1:SUBLANES, :]

        h_last = lax.fori_loop(0, nb, carry_in, h_scr[...], unroll=8)
        h_scr[...] = h_last
        hs = a * c_scr[...] + u
    else:
        hs = a * h0_ref[...] + u
        h_last = hs[:, SUBLANES - 1:SUBLANES, :]
    hr_out[...] = (hs * jax.nn.gelu(gr_ref[...].reshape(nb, SUBLANES, W))).reshape(bb, ts, W)

    tail = SUBLANES - (RG_CONV - 1)
    if chained:
        @pl.when(s == ns - 1)
        def _():
            h_out[0] = h_last
            buf_out[0] = x[nb - 1, tail:, :]
    else:
        h_out[...] = h_last
        buf_out[...] = x[:, tail:, :]


def rglru_mixer(z, h0, buf0, conv_w, conv_b, wa_bd, wx_bd, b_a, b_x, lam):
    B, S, _ = z.shape
    W = RG_WIDTH
    rows = 512
    chained = S > SUBLANES
    if chained:
        bb, ts = 1, min(S, rows)
        assert S % ts == 0 and ts % SUBLANES == 0
    else:
        bb, ts = min(B, rows // SUBLANES), S
        assert S == SUBLANES and B % bb == 0
    ns = S // ts
    nb = bb * ts // SUBLANES
    xr_blk = (4 * MLSTM_WIDTH) // W
    row = lambda a: a.reshape(1, W)
    buf8 = jnp.pad(buf0, ((0, 0), (SUBLANES - (RG_CONV - 1), 0), (0, 0)))
    st = lambda n: pl.BlockSpec((bb, n, W), lambda b, s: (b, 0, 0))
    hr, h1, buf1 = pl.pallas_call(
        functools.partial(_rglru_kernel, ns=ns, chained=chained),
        out_shape=(
            jax.ShapeDtypeStruct((B, S, W), F32),
            jax.ShapeDtypeStruct((B, 1, W), F32),
            jax.ShapeDtypeStruct((B, RG_CONV - 1, W), F32),
        ),
        grid=(B // bb, ns),
        in_specs=[
            pl.BlockSpec((bb, ts, W), lambda b, s: (b, s, xr_blk)),
            pl.BlockSpec((bb, ts, W), lambda b, s: (b, s, xr_blk + 1)),
            st(SUBLANES), st(1),
            _resident((RG_CONV, W)), _resident((1, W)), _resident((W, W)), _resident((W, W)),
            _resident((1, W)), _resident((1, W)), _resident((1, W)),
        ],
        out_specs=(pl.BlockSpec((bb, ts, W), lambda b, s: (b, s, 0)), st(1), st(RG_CONV - 1)),
        scratch_shapes=[pltpu.VMEM((SUBLANES, W), F32), pltpu.VMEM((1, W), F32),
                        pltpu.VMEM((nb, SUBLANES, W), F32), pltpu.VMEM((nb, SUBLANES, W), F32),
                        pltpu.VMEM((nb, SUBLANES, W), F32)],
        compiler_params=_cparams(("parallel", "arbitrary")),
        name="rglru",
    )(z, z, buf8, h0.reshape(B, 1, W), conv_w, row(conv_b), wa_bd, wx_bd, row(b_a), row(b_x), row(lam))
    return hr, h1.reshape(B, W), buf1


def _proj2_post_kernel(x_ref, gt_ref, a1_ref, a2_ref, w1_ref, w2_ref, lg_ref, lb_ref, o_ref):
    bb, ts, _ = x_ref.shape
    a1 = a1_ref[...].reshape(bb * ts, -1).astype(BF)
    a2 = a2_ref[...].reshape(bb * ts, -1).astype(BF)
    y = _dot(a1, w1_ref[...]) + _dot(a2, w2_ref[...])
    z = ALPHA * x_ref[...] + (1.0 + gt_ref[...]) * y.reshape(bb, ts, D_MODEL)
    o_ref[...] = _layer_norm(z, lg_ref[...], lb_ref[...])


def proj2_post(x, mod, a1, a2, w1, w2, lg, lb):
    B, S, _ = x.shape
    bb, ts = _row_tile(B, S, 512)
    K1, K2 = a1.shape[-1], a2.shape[-1]
    xmap = lambda i, s: (i, s, 0)
    full = lambda shp: pl.BlockSpec(shp, lambda i, s: (0,) * len(shp))
    return pl.pallas_call(
        _proj2_post_kernel,
        out_shape=jax.ShapeDtypeStruct(x.shape, F32),
        grid=(B // bb, S // ts),
        in_specs=[
            pl.BlockSpec((bb, ts, D_MODEL), xmap), _mod_spec(bb, 5),
            pl.BlockSpec((bb, ts, K1), xmap), pl.BlockSpec((bb, ts, K2), xmap),
            full((K1, D_MODEL)), full((K2, D_MODEL)), full((1, D_MODEL)), full((1, D_MODEL)),
        ],
        out_specs=pl.BlockSpec((bb, ts, D_MODEL), xmap),
        compiler_params=_cparams(("parallel", "parallel")),
        name="proj2_post",
    )(x, mod, a1, a2, w1, w2, lg, lb)


def _rwkv_pre_kernel(x_ref, sh_ref, sc_ref, prev0_ref, mu_ref, wr_ref, wk_ref, wv_ref, w0_ref, w1_ref,
                     w2_ref, a0_ref, a1_ref, a2_ref, g1_ref, g2_ref, kk_ref, ka_ref,
                     r_out, lw_out, k_out, v_out, kk_out, a_out, g_out, shift_out, carry_scr, *, ns):
    s = pl.program_id(1)
    bb, ts, _ = x_ref.shape
    tm = bb * ts

    @pl.when(s == 0)
    def _():
        carry_scr[...] = prev0_ref[...]

    h = x_ref[...] * (1.0 + sc_ref[...]) + sh_ref[...]
    t_idx = lax.broadcasted_iota(jnp.int32, h.shape, 1)
    h_prev = jnp.where(t_idx == 0, carry_scr[...], pltpu.roll(h, 1, 1))
    last = h[:, ts - 1:ts, :]
    carry_scr[...] = last
    dx = (h_prev - h).reshape(tm, D_MODEL)
    h2 = h.reshape(tm, D_MODEL)
    mu = mu_ref[...]
    mix = lambda j: (h2 + dx * mu[j:j + 1, :]).astype(BF)
    xr, xw, xk, xv, xa, xg = [mix(j) for j in range(6)]

    r = _dot(xr, wr_ref[...])
    k = _dot(xk, wk_ref[...])
    v = _dot(xv, wv_ref[...])
    wl = w0_ref[...] + _dot(jnp.tanh(_dot(xw, w1_ref[...])).astype(BF), w2_ref[...])
    w_log = -(jnp.maximum(-wl, 0.0) + jnp.log1p(jnp.exp(-jnp.abs(wl)))) - 0.5
    a = jax.nn.sigmoid(a0_ref[...] + _dot(_dot(xa, a1_ref[...]).astype(BF), a2_ref[...]))
    g = _dot(jax.nn.sigmoid(_dot(xg, g1_ref[...])).astype(BF), g2_ref[...])
    kk = k * kk_ref[...]
    k = k * (1.0 + (a - 1.0) * ka_ref[...])

    shp = x_ref.shape
    r_out[...] = r.reshape(shp)
    lw_out[...] = (-jnp.exp(w_log)).reshape(shp)
    k_out[...] = k.reshape(shp)
    v_out[...] = v.reshape(shp)
    kk_out[...] = kk.reshape(shp)
    a_out[...] = a.reshape(shp)
    g_out[...] = g.reshape(shp)

    @pl.when(s == ns - 1)
    def _():
        shift_out[...] = last


def rwkv_pre(x, mod, prev0, P):
    B, S, _ = x.shape
    bb, ts = _row_tile(B, S, 256)
    ns = S // ts
    xmap = lambda i, s: (i, s, 0)
    full = lambda shp: pl.BlockSpec(shp, lambda i, s: (0,) * len(shp))
    tok = pl.BlockSpec((bb, ts, D_MODEL), xmap)
    st = pl.BlockSpec((bb, 1, D_MODEL), lambda i, s: (i, 0, 0))
    sq, lo_in, lo_out, vec = (D_MODEL, D_MODEL), (D_MODEL, LORA_PAD), (LORA_PAD, D_MODEL), (1, D_MODEL)
    outs = pl.pallas_call(
        functools.partial(_rwkv_pre_kernel, ns=ns),
        out_shape=tuple([jax.ShapeDtypeStruct(x.shape, F32)] * 7
                        + [jax.ShapeDtypeStruct((B, 1, D_MODEL), F32)]),
        grid=(B // bb, ns),
        in_specs=[
            tok, _mod_spec(bb, 3), _mod_spec(bb, 4), st, full((6, D_MODEL)),
            full(sq), full(sq), full(sq), full(vec), full(lo_in), full(lo_out),
            full(vec), full(lo_in), full(lo_out), full(lo_in), full(lo_out),
            full(vec), full(vec),
        ],
        out_specs=tuple([tok] * 7 + [st]),
        scratch_shapes=[pltpu.VMEM((bb, 1, D_MODEL), F32)],
        compiler_params=_cparams(("parallel", "arbitrary")),
        name="rwkv_pre",
    )(x, mod, mod, prev0.reshape(B, 1, D_MODEL), P['mu'], P['wr'], P['wk'], P['wv'], P['w0'], P['w1'],
      P['w2'], P['a0'], P['a1'], P['a2'], P['g1'], P['g2'], P['k_k'], P['k_a'])
    return outs


def _stack_heads(x, first):
    return jnp.concatenate([jnp.where(first, x, 0.0), jnp.where(first, 0.0, x)], axis=0)


def _seg_sum(x, first):
    s0 = jnp.sum(jnp.where(first, x, 0.0), axis=1, keepdims=True)
    s1 = jnp.sum(jnp.where(first, 0.0, x), axis=1, keepdims=True)
    return jnp.where(first, s0, s1)


def _wkv_pair_chunks(toks, sps, vecs, consts, L):
    tri, first, strict, incl = consts
    L2 = 2 * L
    rng = range(len(toks))
    wide = L2 % LANES == 0
    splits = [_split3(t[1]) for t in toks]
    c_incl = [_dot(tri, s[0]) + _dot(tri, s[1]) + _dot(tri, s[2]) for s in splits]
    e_inv = [jnp.exp(-c) for c in c_incl]
    kn = [t[4] / jnp.maximum(jnp.sqrt(_seg_sum(t[4] * t[4], first)), 1e-12) for t in toks]
    la = [_stack_heads(-kn[p] * jnp.exp(c_incl[p] - toks[p][1]), first).astype(BF) for p in rng]
    lr = [_stack_heads(toks[p][0] * jnp.exp(c_incl[p]), first).astype(BF) for p in rng]
    rb = [_stack_heads(kn[p] * toks[p][5] * e_inv[p], first).astype(BF) for p in rng]
    rk = [_stack_heads(toks[p][2] * e_inv[p], first).astype(BF) for p in rng]
    vs = [_stack_heads(toks[p][3], first).astype(BF) for p in rng]
    rbk = [jnp.concatenate([rb[p], rk[p]], axis=0) for p in rng]

    if wide:
        nn = [_dot_nt(la[p], rbk[p]) for p in rng]
        mm = [_dot_nt(lr[p], rbk[p]) for p in rng]
        n_raw = [x[:, :L2] for x in nn]
        n_ak = [jnp.where(strict, x[:, L2:], 0.0).astype(BF) for x in nn]
        m_bk = [jnp.concatenate([jnp.where(incl, x[:, :L2], 0.0), jnp.where(incl, x[:, L2:], 0.0)],
                                axis=1).astype(BF) for x in mm]
    else:
        n_raw = [_dot_nt(la[p], rb[p]) for p in rng]
        n_ak = [jnp.where(strict, _dot_nt(la[p], rk[p]), 0.0).astype(BF) for p in rng]
        m_rb = [jnp.where(incl, _dot_nt(lr[p], rb[p]), 0.0).astype(BF) for p in rng]
        m_rk = [jnp.where(incl, _dot_nt(lr[p], rk[p]), 0.0).astype(BF) for p in rng]

    row = lax.broadcasted_iota(jnp.int32, (L2, L2), 0)
    col = lax.broadcasted_iota(jnp.int32, (L2, L2), 1)
    base = (row // INV_BASE == col // INV_BASE) & (col < row)
    xb = [jnp.where(base, x, 0.0).astype(BF) for x in n_raw]
    tinv = [jnp.where(row == col, 1.0, 0.0) + x.astype(F32) for x in xb]
    x2 = [_dot(x, x).astype(BF) for x in xb]
    if wide:
        prod = [_dot(x2[p], jnp.concatenate([x2[p], tinv[p].astype(BF)], axis=1)) for p in rng]
        tinv = [tinv[p] + prod[p][:, L2:] for p in rng]
        x4 = [x[:, :L2].astype(BF) for x in prod]
    else:
        tinv = [tinv[p] + _dot(x2[p], tinv[p].astype(BF)) for p in rng]
        x4 = [_dot(x, x).astype(BF) for x in x2]
    tinv = [tinv[p] + _dot(x4[p], tinv[p].astype(BF)) for p in rng]
    blk = INV_BASE
    while blk < L:
        off = (row // (2 * blk) == col // (2 * blk)) & (row // blk == col // blk + 1)
        tb = [t.astype(BF) for t in tinv]
        pr = [_dot(jnp.where(off, n_raw[p], 0.0).astype(BF), tb[p]).astype(BF) for p in rng]
        tinv = [tinv[p] + _dot(tb[p], pr[p]) for p in rng]
        blk *= 2

    spb = [s.astype(BF) for s in sps]
    rhs = [_dot_nt(la[p], spb[p]) + _dot(n_ak[p], vs[p]) for p in rng]
    ub = [_dot(tinv[p].astype(BF), rhs[p].astype(BF)).astype(BF) for p in rng]

    uv = [jnp.concatenate([ub[p], vs[p]], axis=0) for p in rng]
    if wide:
        ys = [_dot_nt(lr[p], spb[p]) + _dot(m_bk[p], uv[p]) for p in rng]
    else:
        ys = [_dot_nt(lr[p], spb[p]) + _dot(m_rb[p], ub[p]) + _dot(m_rk[p], vs[p]) for p in rng]
    sp_new = [(sps[p] + _dot_tn(uv[p], rbk[p])) * jnp.exp(c_incl[p][L - 1:L, :]) for p in rng]

    outs = []
    for p in rng:
        r, _, k, v, _, _ = toks[p]
        r_k, lnx_g, lnx_b = vecs[p]
        y = ys[p][0:L, :] + ys[p][L:L2, :]
        yc = y - _seg_sum(y, first) * (1.0 / RWKV_HS)
        yv = _seg_sum(yc * yc, first) * (1.0 / RWKV_HS)
        yn = yc * lax.rsqrt(yv + RWKV_LN_EPS) * lnx_g + lnx_b
        outs.append((yn + _seg_sum(r * k * r_k, first) * v, sp_new[p]))
    return outs


def _wkv_kernel(r_ref, lw_ref, k_ref, v_ref, kk_ref, a_ref, s0_ref, rk_ref, lxg_ref, lxb_ref,
                y_out, s_out, sp_scr, *, L, nc, bb, group):
    c = pl.program_id(1)
    hs = RWKV_HS
    npair = RWKV_HEADS // 2
    L2 = 2 * L

    @pl.when(c == 0)
    def _():
        zero = jnp.zeros((hs, hs), F32)

        def init(i, carry):
            for p in range(npair):
                top = jnp.concatenate([s0_ref[i, 2 * p], zero], axis=1)
                bot = jnp.concatenate([zero, s0_ref[i, 2 * p + 1]], axis=1)
                sp_scr[i, p] = jnp.concatenate([top, bot], axis=0)
            return carry

        lax.fori_loop(0, bb, init, 0)

    row = lax.broadcasted_iota(jnp.int32, (L, L), 0)
    col = lax.broadcasted_iota(jnp.int32, (L, L), 1)
    tri = jnp.where(col <= row, 1.0, 0.0).astype(BF)
    first = lax.broadcasted_iota(jnp.int32, (L, LANES), 1) < hs
    row2 = lax.broadcasted_iota(jnp.int32, (L2, L2), 0)
    col2 = lax.broadcasted_iota(jnp.int32, (L2, L2), 1)
    same = (row2 >= L) == (col2 >= L)
    consts = (tri, first, same & (col2 < row2), same & (col2 <= row2))
    sls = [slice(p * LANES, (p + 1) * LANES) for p in range(npair)]

    def seqs(j, carry):
        ids = [(j * group + g, p) for g in range(group) for p in range(npair)]
        toks = [tuple(ref[i, :, sls[p]] for ref in (r_ref, lw_ref, k_ref, v_ref, kk_ref, a_ref)) for i, p in ids]
        vecs = [(rk_ref[:, sls[p]], lxg_ref[:, sls[p]], lxb_ref[:, sls[p]]) for _, p in ids]
        outs = _wkv_pair_chunks(toks, [sp_scr[i, p] for i, p in ids], vecs, consts, L)
        for (i, p), (y, sp_new) in zip(ids, outs):
            y_out[i, :, sls[p]] = y
            sp_scr[i, p] = sp_new
        return carry

    lax.fori_loop(0, bb // group, seqs, 0)

    @pl.when(c == nc - 1)
    def _():
        def fin(i, carry):
            for p in range(npair):
                sp = sp_scr[i, p]
                s_out[i, 2 * p] = sp[0:hs, 0:hs]
                s_out[i, 2 * p + 1] = sp[hs:2 * hs, hs:2 * hs]
            return carry

        lax.fori_loop(0, bb, fin, 0)


def wkv_recurrence(r, lw, k, v, kk, a, s0, P):
    B, S, _ = r.shape
    L = RWKV_CHUNK if S % RWKV_CHUNK == 0 else S
    assert L & (L - 1) == 0 and L % SUBLANES == 0
    nc = S // L
    bb, group = (2, 2) if nc > 1 else (min(B, 8), min(B, 8))
    assert B % bb == 0 and bb % group == 0
    npair = RWKV_HEADS // 2
    tok = pl.BlockSpec((bb, L, D_MODEL), lambda b, c: (b, c, 0))
    st = pl.BlockSpec((bb, RWKV_HEADS, RWKV_HS, RWKV_HS), lambda b, c: (b, 0, 0, 0))
    vec = pl.BlockSpec((1, D_MODEL), lambda b, c: (0, 0))
    return pl.pallas_call(
        functools.partial(_wkv_kernel, L=L, nc=nc, bb=bb, group=group),
        out_shape=(jax.ShapeDtypeStruct(r.shape, F32), jax.ShapeDtypeStruct(s0.shape, F32)),
        grid=(B // bb, nc),
        in_specs=[tok] * 6 + [st, vec, vec, vec],
        out_specs=(tok, st),
        scratch_shapes=[pltpu.VMEM((bb, npair, 2 * RWKV_HS, 2 * RWKV_HS), F32)],
        compiler_params=_cparams(("parallel", "arbitrary")),
        name="wkv",
    )(r, lw, k, v, kk, a, s0, P['r_k'], P['lnx_g'], P['lnx_b'])


def _rwkv_post_kernel(x_ref, gt_ref, y_ref, g_ref, wo_ref, lg_ref, lb_ref, o_ref):
    bb, ts, _ = x_ref.shape
    yg = (y_ref[...] * g_ref[...]).reshape(bb * ts, D_MODEL).astype(BF)
    out = _dot(yg, wo_ref[...])
    z = ALPHA * x_ref[...] + (1.0 + gt_ref[...]) * out.reshape(bb, ts, D_MODEL)
    o_ref[...] = _layer_norm(z, lg_ref[...], lb_ref[...])


def rwkv_post(x, mod, y, g, P, lg, lb):
    B, S, _ = x.shape
    bb, ts = _row_tile(B, S, 512)
    xmap = lambda i, s: (i, s, 0)
    tok = pl.BlockSpec((bb, ts, D_MODEL), xmap)
    vec = (1, D_MODEL)
    return pl.pallas_call(
        _rwkv_post_kernel,
        out_shape=jax.ShapeDtypeStruct(x.shape, F32),
        grid=(B // bb, S // ts),
        in_specs=[tok, _mod_spec(bb, 5), tok, tok, _resident((D_MODEL, D_MODEL)), _resident(vec), _resident(vec)],
        out_specs=tok,
        compiler_params=_cparams(("parallel", "parallel")),
        name="rwkv_post",
    )(x, mod, y, g, P['wo'], lg, lb)


def _block_diag(w):
    G, n, _ = w.shape
    eye = jnp.eye(G, dtype=w.dtype)
    return (eye[:, None, :, None] * w[:, :, None, :]).reshape(G * n, G * n)


def _pad_cols(w, n):
    return jnp.pad(w, ((0, 0), (0, n - w.shape[1])))


def _pad_rows(w, n):
    return jnp.pad(w, ((0, n - w.shape[0]), (0, 0)))


def _prep_ab(j, p):
    w_in = p['ab_w_in'][j]
    n_main = 4 * MLSTM_WIDTH
    n_gate = 2 * MLSTM_HEADS
    w_all = jnp.concatenate([w_in[:, :n_main], w_in[:, n_main + n_gate:],
                             w_in[:, n_main:n_main + n_gate]], axis=1)
    return dict(
        w_in=_pad_cols(w_all, Z_COLS).astype(BF),
        b_gates=p['mlstm_b_gates'][j], norm_g=p['mlstm_norm_g'][j],
        conv_w=p['rg_conv_w'][j], conv_b=p['rg_conv_b'][j],
        wa=_block_diag(p['rg_w_a'][j]).astype(BF), wx=_block_diag(p['rg_w_x'][j]).astype(BF),
        b_a=p['rg_b_a'][j], b_x=p['rg_b_x'][j], lam=p['rg_lambda'][j],
        w_out_m=p['ab_w_out'][j][:MLSTM_WIDTH].astype(BF), w_out_r=p['ab_w_out'][j][MLSTM_WIDTH:].astype(BF),
    )


def _prep_rwkv(j, p):
    vec = lambda a: a.reshape(1, D_MODEL)
    return dict(
        mu=p['rw_mu'][j],
        wr=p['rw_wr'][j].astype(BF), wk=p['rw_wk'][j].astype(BF), wv=p['rw_wv'][j].astype(BF),
        w0=vec(p['rw_w0'][j]), w1=_pad_cols(p['rw_w1'][j], LORA_PAD).astype(BF),
        w2=_pad_rows(p['rw_w2'][j], LORA_PAD).astype(BF),
        a0=vec(p['rw_a0'][j]), a1=_pad_cols(p['rw_a1'][j], LORA_PAD).astype(BF),
        a2=_pad_rows(p['rw_a2'][j], LORA_PAD).astype(BF),
        g1=_pad_cols(p['rw_g1'][j], LORA_PAD).astype(BF), g2=_pad_rows(p['rw_g2'][j], LORA_PAD).astype(BF),
        k_k=vec(p['rw_k_k'][j]), k_a=vec(p['rw_k_a'][j]), r_k=vec(p['rw_r_k'][j]),
        lnx_g=vec(p['rw_lnx_g'][j]), lnx_b=vec(p['rw_lnx_b'][j]), wo=p['rw_wo'][j].astype(BF),
    )


def ab_mixer(x, mod, st, A, lg, lb):
    mC, mn, mm, rh, rconv = st
    z, gates = mod_matmul(x, mod, A['w_in'], 4 * MLSTM_WIDTH + 2 * RG_WIDTH, 2 * MLSTM_HEADS)
    hm, C1, n1, m1 = mlstm_mixer(z, gates, mC, mn, mm, A['b_gates'], A['norm_g'])
    hr, rh1, buf1 = rglru_mixer(z, rh, rconv, A['conv_w'], A['conv_b'], A['wa'], A['wx'],
                                A['b_a'], A['b_x'], A['lam'])
    x = proj2_post(x, mod, hm, hr, A['w_out_m'], A['w_out_r'], lg, lb)
    return x, (C1, n1, m1, rh1, buf1)


def rwkv_mixer(x, mod, st, R, lg, lb):
    wkv0, prev0 = st
    r, lw, k, v, kk, a, g, shift = rwkv_pre(x, mod, prev0, R)
    y, wkv1 = wkv_recurrence(r, lw, k, v, kk, a, wkv0, R)
    x = rwkv_post(x, mod, y, g, R, lg, lb)
    return x, (wkv1, shift.reshape(shift.shape[0], D_MODEL))


def run_trunk(x, mods, states, W):
    mC, mn, mm, rh, rconv, wkv, shift = states
    new_ab, new_c = [], []
    for layer in range(DEPTH):
        mod = mods[layer]
        lg = lambda i: W['ln_g'][layer, i].reshape(1, D_MODEL)
        lb = lambda i: W['ln_b'][layer, i].reshape(1, D_MODEL)
        f = W['ffn']
        x = ffn_block(x, mod, 0, layer, 0, f[0], f[1], f[2], lg(0), lb(0))
        j = layer // 2
        if layer % 2 == 0:
            x, st = ab_mixer(x, mod, (mC[j], mn[j], mm[j], rh[j], rconv[j]), W['ab'][j], lg(1), lb(1))
            new_ab.append(st)
        else:
            x, st = rwkv_mixer(x, mod, (wkv[j], shift[j]), W['rwkv'][j], lg(1), lb(1))
            new_c.append(st)
        x = ffn_block(x, mod, 2, layer, 1, f[0], f[1], f[2], lg(2), lb(2))
    stk = lambda sts, i: sts[0][i][None] if len(sts) == 1 else jnp.stack([s[i] for s in sts], axis=0)
    return x, (stk(new_ab, 0), stk(new_ab, 1), stk(new_ab, 2), stk(new_ab, 3), stk(new_ab, 4),
               stk(new_c, 0), stk(new_c, 1))


def _zero_states(B):
    n_ab, n_c = (DEPTH + 1) // 2, DEPTH // 2
    return (jnp.zeros((n_ab, B, MLSTM_HEADS, MLSTM_HD, MLSTM_HD), F32),
            jnp.zeros((n_ab, B, MLSTM_HEADS, MLSTM_HD), F32),
            jnp.zeros((n_ab, B, MLSTM_HEADS), F32),
            jnp.zeros((n_ab, B, RG_WIDTH), F32),
            jnp.zeros((n_ab, B, RG_CONV - 1, RG_WIDTH), F32),
            jnp.zeros((n_c, B, RWKV_HEADS, RWKV_HS, RWKV_HS), F32),
            jnp.zeros((n_c, B, D_MODEL), F32))


def kernel(x_prompt, x_sample, c_prompt, c_sample, state_mlstm_C, state_mlstm_n, state_mlstm_m, state_rglru_h, state_rglru_conv, state_rwkv_wkv, state_rwkv_shift, ada_w, ada_b, ln_g, ln_b, ffn_w1, ffn_w3, ffn_w2, ab_w_in, mlstm_b_gates, mlstm_norm_g, rg_conv_w, rg_conv_b, rg_w_a, rg_b_a, rg_w_x, rg_b_x, rg_lambda, ab_w_out, rw_mu, rw_wr, rw_wk, rw_wv, rw_w0, rw_w1, rw_w2, rw_a0, rw_a1, rw_a2, rw_g1, rw_g2, rw_k_k, rw_k_a, rw_r_k, rw_lnx_g, rw_lnx_b, rw_wo):
    p = dict(ab_w_in=ab_w_in, mlstm_b_gates=mlstm_b_gates, mlstm_norm_g=mlstm_norm_g,
             rg_conv_w=rg_conv_w, rg_conv_b=rg_conv_b, rg_w_a=rg_w_a, rg_b_a=rg_b_a, rg_w_x=rg_w_x,
             rg_b_x=rg_b_x, rg_lambda=rg_lambda, ab_w_out=ab_w_out, rw_mu=rw_mu, rw_wr=rw_wr,
             rw_wk=rw_wk, rw_wv=rw_wv, rw_w0=rw_w0, rw_w1=rw_w1, rw_w2=rw_w2, rw_a0=rw_a0,
             rw_a1=rw_a1, rw_a2=rw_a2, rw_g1=rw_g1, rw_g2=rw_g2, rw_k_k=rw_k_k, rw_k_a=rw_k_a,
             rw_r_k=rw_r_k, rw_lnx_g=rw_lnx_g, rw_lnx_b=rw_lnx_b, rw_wo=rw_wo)
    W = dict(
        ln_g=ln_g, ln_b=ln_b,
        ffn=(ffn_w1.astype(BF), ffn_w3.astype(BF), ffn_w2.astype(BF)),
        ab=[_prep_ab(j, p) for j in range((DEPTH + 1) // 2)],
        rwkv=[_prep_rwkv(j, p) for j in range(DEPTH // 2)],
    )
    Bp, Bs = x_prompt.shape[0], x_sample.shape[0]
    mod_all = adaln(jnp.concatenate([c_prompt, c_sample], axis=0), ada_w, ada_b)
    mods_p = [mod_all[l, :Bp].reshape(Bp, 3 * N_SUB, 1, D_MODEL) for l in range(DEPTH)]
    mods_s = [mod_all[l, Bp:].reshape(Bs, 3 * N_SUB, 1, D_MODEL) for l in range(DEPTH)]
    y_prompt, sp = run_trunk(x_prompt, mods_p, _zero_states(Bp), W)
    y_sample, ss = run_trunk(x_sample, mods_s,
                             (state_mlstm_C, state_mlstm_n, state_mlstm_m, state_rglru_h,
                              state_rglru_conv, state_rwkv_wkv, state_rwkv_shift), W)
    return (y_prompt, y_sample) + tuple(sp) + tuple(ss)
```

```python
import functools
import math

import jax
import jax.numpy as jnp
from jax import lax
from jax.experimental import pallas as pl
from jax.experimental.pallas import tpu as pltpu

D_MODEL = 1024
DEPTH = 2
N_SUB = 3
MLSTM_WIDTH = 512
MLSTM_HEADS = 4
MLSTM_HD = 128
MLSTM_CHUNK = 128
RG_WIDTH = 512
RG_BLOCKS = 8
RG_BD = 64
RG_CONV = 4
RG_C = 8.0
RWKV_HS = 64
RWKV_HEADS = 16
RWKV_LN_EPS = 64e-5
RWKV_CHUNK = 64
DECAY_SCALE = math.exp(-0.5)
INV_BASE = 8
D_FF = 2816
ALPHA = (2.0 * DEPTH) ** 0.25
LN_EPS = 1e-5
HEAD_NORM_EPS = 1e-6

LANES = 128
SUBLANES = 8
VMEM_LIMIT_BYTES = 56 * 1024 * 1024
Z_COLS = 3200
LORA_PAD = 128

BF = jnp.bfloat16
F32 = jnp.float32

NT_DIMS = (((1,), (1,)), ((), ()))
TN_DIMS = (((0,), (0,)), ((), ()))


def _cparams(sem):
    return pltpu.CompilerParams(dimension_semantics=sem, vmem_limit_bytes=VMEM_LIMIT_BYTES)


def _dot(a, b):
    return jnp.dot(a, b, preferred_element_type=F32)


def _dot_nt(a, b):
    return lax.dot_general(a, b, NT_DIMS, preferred_element_type=F32)


def _dot_tn(a, b):
    return lax.dot_general(a, b, TN_DIMS, preferred_element_type=F32)


def _layer_norm(z, g, b):
    mu = jnp.mean(z, axis=-1, keepdims=True)
    zc = z - mu
    var = jnp.mean(zc * zc, axis=-1, keepdims=True)
    return zc * lax.rsqrt(var + LN_EPS) * g + b


def _row_tile(B, S, rows):
    if S >= rows:
        assert S % rows == 0
        return 1, rows
    bb = min(B, rows // S)
    assert B % bb == 0
    return bb, S


def _split2(x):
    hi = x.astype(BF)
    lo = (x - hi.astype(F32)).astype(BF)
    return hi, lo


def _split3(x):
    hi = x.astype(BF)
    r1 = x - hi.astype(F32)
    mid = r1.astype(BF)
    lo = (r1 - mid.astype(F32)).astype(BF)
    return hi, mid, lo


def _adaln_kernel(c_ref, w_ref, b_ref, o_ref):
    h = jax.nn.silu(c_ref[...]).astype(BF)
    o_ref[...] = _dot(h, w_ref[...].astype(BF)) + b_ref[...]


def adaln(c_all, ada_w, ada_b):
    Bc = c_all.shape[0]
    n_out = ada_w.shape[-1]
    tn = 1024
    return pl.pallas_call(
        _adaln_kernel,
        out_shape=jax.ShapeDtypeStruct((DEPTH, Bc, n_out), F32),
        grid=(DEPTH, n_out // tn),
        in_specs=[
            pl.BlockSpec((Bc, D_MODEL), lambda l, j: (0, 0)),
            pl.BlockSpec((None, D_MODEL, tn), lambda l, j: (l, 0, j)),
            pl.BlockSpec((None, 1, tn), lambda l, j: (l, 0, j)),
        ],
        out_specs=pl.BlockSpec((None, Bc, tn), lambda l, j: (l, 0, j)),
        compiler_params=_cparams(("parallel", "parallel")),
        name="adaln",
    )(c_all, ada_w, ada_b.reshape(DEPTH, 1, n_out))


def _mod_spec(bb, k):
    return pl.BlockSpec((bb, None, 1, D_MODEL), lambda i, *_: (i, k, 0, 0))


def _ffn_kernel(*refs, res_w, after_mixer):
    if after_mixer:
        (x_ref, mgt_ref, m1_ref, m2_ref, mw1_ref, mw2_ref, mlg_ref, mlb_ref,
         sh_ref, sc_ref, gt_ref, w1_ref, w3_ref, w2_ref, lg_ref, lb_ref, o_ref) = refs
        bb, ts, _ = x_ref.shape
        flat = lambda ref: ref[...].reshape(bb * ts, -1).astype(BF)
        ym = _dot(flat(m1_ref), mw1_ref[...]) + _dot(flat(m2_ref), mw2_ref[...])
        zm = ALPHA * x_ref[...] + (1.0 + mgt_ref[...]) * ym.reshape(bb, ts, D_MODEL)
        x = _layer_norm(zm, mlg_ref[...], mlb_ref[...])
    else:
        x_ref, sh_ref, sc_ref, gt_ref, w1_ref, w3_ref, w2_ref, lg_ref, lb_ref, o_ref = refs
        bb, ts, _ = x_ref.shape
        x = x_ref[...]
    h = (x * (1.0 + sc_ref[...]) + sh_ref[...]).reshape(bb * ts, D_MODEL).astype(BF)
    a = _dot(h, w1_ref[...])
    b = _dot(h, w3_ref[...])
    g = (jax.nn.silu(a) * b).astype(BF)
    y = _dot(g, w2_ref[...])
    z = ALPHA * x + (res_w * (1.0 + gt_ref[...])) * y.reshape(bb, ts, D_MODEL)
    o_ref[...] = _layer_norm(z, lg_ref[...], lb_ref[...])


def _resident(shape):
    return pl.BlockSpec(shape, lambda *_: (0,) * len(shape), pipeline_mode=pl.Buffered(1))


def ffn_block(x, mod, sub, layer, half, w1, w3, w2, lg, lb, mixer=None):
    B, S, _ = x.shape
    bb, ts = _row_tile(B, S, 512)
    xmap = lambda i, s: (i, s, 0)
    wspec = lambda r, c: pl.BlockSpec((None, None, r, c), lambda i, s: (layer, half, 0, 0),
                                      pipeline_mode=pl.Buffered(1))
    vec = _resident((1, D_MODEL))
    specs = [pl.BlockSpec((bb, ts, D_MODEL), xmap)]
    args = [x]
    if mixer is not None:
        (m1, blk1), (m2, blk2), mw1, mw2, mlg, mlb = mixer
        mspec = lambda w, blk: pl.BlockSpec((bb, ts, w.shape[0]), lambda i, s: (i, s, blk))
        specs += [_mod_spec(bb, 5), mspec(mw1, blk1), mspec(mw2, blk2),
                  _resident(mw1.shape), _resident(mw2.shape), vec, vec]
        args += [mod, m1, m2, mw1, mw2, mlg, mlb]
    specs += [_mod_spec(bb, 3 * sub), _mod_spec(bb, 3 * sub + 1), _mod_spec(bb, 3 * sub + 2),
              wspec(D_MODEL, D_FF), wspec(D_MODEL, D_FF), wspec(D_FF, D_MODEL), vec, vec]
    args += [mod, mod, mod, w1, w3, w2, lg, lb]
    return pl.pallas_call(
        functools.partial(_ffn_kernel, res_w=0.5, after_mixer=mixer is not None),
        out_shape=jax.ShapeDtypeStruct(x.shape, F32),
        grid=(B // bb, S // ts),
        in_specs=specs,
        out_specs=pl.BlockSpec((bb, ts, D_MODEL), xmap),
        compiler_params=_cparams(("parallel", "parallel")),
        name="ffn_block",
    )(*args)


def _modmm_kernel(x_ref, sh_ref, sc_ref, w_ref, o_ref, gate_ref):
    bb, ts, _ = x_ref.shape
    n_main = o_ref.shape[-1]
    n_gate = gate_ref.shape[-1]
    h = (x_ref[...] * (1.0 + sc_ref[...]) + sh_ref[...]).reshape(bb * ts, D_MODEL).astype(BF)
    o_ref[...] = _dot(h, w_ref[:, :n_main]).reshape(o_ref.shape)
    gate_ref[...] = _dot(h, w_ref[:, n_main:])[:, :n_gate].reshape(gate_ref.shape)


def mod_matmul(x, mod, w, n_main, n_gate):
    B, S, _ = x.shape
    bb, ts = _row_tile(B, S, 512)
    xmap = lambda i, s: (i, s, 0)
    return pl.pallas_call(
        _modmm_kernel,
        out_shape=(jax.ShapeDtypeStruct((B, S, n_main), F32), jax.ShapeDtypeStruct((B, S, n_gate), F32)),
        grid=(B // bb, S // ts),
        in_specs=[
            pl.BlockSpec((bb, ts, D_MODEL), xmap),
            _mod_spec(bb, 3), _mod_spec(bb, 4),
            _resident(w.shape),
        ],
        out_specs=(pl.BlockSpec((bb, ts, n_main), xmap), pl.BlockSpec((bb, ts, n_gate), xmap)),
        compiler_params=_cparams(("parallel", "parallel")),
        name="mod_matmul",
    )(x, mod, mod, w)


def _log_sigmoid(x):
    return jnp.minimum(x, 0.0) - jnp.log1p(jnp.exp(-jnp.abs(x)))


def _lockstep(gens):
    results = [None] * len(gens)
    live = list(range(len(gens)))
    while live:
        still = []
        for idx in live:
            try:
                next(gens[idx])
                still.append(idx)
            except StopIteration as stop:
                results[idx] = stop.value
        live = still
    return results


def _dot_exact(a, b, dims):
    return lax.dot_general(a, b, dims, precision=lax.Precision.HIGHEST, preferred_element_type=F32)


def _mlstm_gate_views(gl, L):
    row = lax.broadcasted_iota(jnp.int32, (L, L), 0)
    col = lax.broadcasted_iota(jnp.int32, (L, L), 1)
    eye = jnp.where(row == col, 1.0, 0.0)
    lower = jnp.where(col <= row, 1.0, 0.0)
    upper = jnp.where(row <= col, 1.0, 0.0)
    gl_t = _dot_exact(gl, eye, TN_DIMS)
    cum_c = _dot_exact(lower, gl, (((1,), (0,)), ((), ())))
    cum_r = _dot_exact(gl, upper, TN_DIMS)
    return gl_t, cum_c, cum_r


def _mlstm_head_chunk(q, k, v, o, ig_c, b_c, ig_r, b_r, C0, n0, m0, g, L):
    row = lax.broadcasted_iota(jnp.int32, (L, L), 0)
    col = lax.broadcasted_iota(jnp.int32, (L, L), 1)
    causal = col <= row
    dmat = jnp.where(causal, b_c - b_r + ig_r, -jnp.inf)
    m_inter = b_c + m0
    m = jnp.maximum(m_inter, jnp.max(dmat, axis=1, keepdims=True))
    w_inter = jnp.exp(m_inter - m)
    p = jnp.exp(dmat - m)

    k = k * (MLSTM_HD ** -0.5)
    qb = q.astype(BF)
    kb = k.astype(BF)
    yield
    s_raw = _dot_nt(qb, kb)
    q_c0 = _dot_nt(qb, C0.astype(BF))
    yield
    scores = s_raw * p
    num = _dot(scores.astype(BF), v.astype(BF)) + w_inter * q_c0
    den = jnp.sum(scores, axis=1, keepdims=True) + w_inter * jnp.sum(q * n0, axis=1, keepdims=True)
    hh = num / jnp.maximum(jnp.abs(den), jnp.exp(-m))

    m_end = m[L - 1:L, :]
    b_end = b_c[L - 1:L, :]
    w_state = jnp.exp(b_end + m0 - m_end)
    w_rows = jnp.exp(b_end - b_c + ig_c - m_end)
    yield
    c_new = w_state * C0 + _dot_tn((w_rows * v).astype(BF), kb)
    n_new = w_state * n0 + jnp.sum(w_rows * k, axis=0, keepdims=True)

    mu = jnp.mean(hh, axis=1, keepdims=True)
    hc = hh - mu
    var = jnp.mean(hc * hc, axis=1, keepdims=True)
    hn = hc * lax.rsqrt(var + HEAD_NORM_EPS)
    return hn * g * jax.nn.sigmoid(o), c_new, n_new, m_end


def _mlstm_kernel(m0_ref, bg_ref, q_ref, k_ref, v_ref, o_ref, gc_ref, c0_ref, n0_ref, g_ref,
                  h_out, c_out, n_out, m_out, c_scr, n_scr, m_scr, *, L, nc, bb, group):
    ib = pl.program_id(0)
    c = pl.program_id(1)
    H = MLSTM_HEADS

    @pl.when(c == 0)
    def _():
        c_scr[...] = c0_ref[...]
        n_scr[...] = n0_ref[...]

        def init(i, carry):
            for h in range(H):
                m_scr[i, h] = jnp.full((1, LANES), m0_ref[(ib * bb + i) * H + h], F32)
            return carry

        lax.fori_loop(0, bb, init, 0)

    sls = [slice(h * MLSTM_HD, (h + 1) * MLSTM_HD) for h in range(H)]

    def seqs(j, carry):
        views = []
        for g in range(group):
            gates = gc_ref[j * group + g] + bg_ref[...]
            is_input = lax.broadcasted_iota(jnp.int32, gates.shape, 1) < H
            gl = jnp.where(is_input, gates, _log_sigmoid(gates))
            views.append((gl,) + _mlstm_gate_views(gl, L))
        ids = [(j * group + g, h) for g in range(group) for h in range(H)]
        gate_args = [(views[g][0][:, h:h + 1], views[g][2][:, H + h:H + h + 1],
                      views[g][1][h:h + 1, :], views[g][3][H + h:H + h + 1, :])
                     for g in range(group) for h in range(H)]
        heads = _lockstep([_mlstm_head_chunk(
            q_ref[i, :, sls[h]], k_ref[i, :, sls[h]], v_ref[i, :, sls[h]], o_ref[i, :, sls[h]], *gate_args[n],
            c_scr[i, h], n_scr[i, h], m_scr[i, h][:, 0:1], g_ref[:, sls[h]], L) for n, (i, h) in enumerate(ids)])
        for (i, h), (out, c_new, n_new, m_end) in zip(ids, heads):
            h_out[i, :, sls[h]] = out
            c_scr[i, h] = c_new
            n_scr[i, h] = n_new
            m_scr[i, h] = jnp.broadcast_to(m_end, (1, LANES))
        return carry

    lax.fori_loop(0, bb // group, seqs, 0)

    @pl.when(c == nc - 1)
    def _():
        c_out[...] = c_scr[...]
        n_out[...] = n_scr[...]
        m_out[...] = m_scr[...]


def mlstm_mixer(z, gates, C0, n0, m0, b_gates, norm_g):
    B, S, _ = z.shape
    H = MLSTM_HEADS
    W = MLSTM_WIDTH
    L = MLSTM_CHUNK if S % MLSTM_CHUNK == 0 else S
    nc = S // L
    bb, group = (1, 1) if nc > 1 else (min(B, 8), min(B, 4))
    assert B % bb == 0 and bb % group == 0
    zspec = lambda blk: pl.BlockSpec((bb, L, W), lambda b, c: (b, c, blk))
    smem = pl.BlockSpec(memory_space=pltpu.SMEM)
    hd_spec = pl.BlockSpec((bb, H, 1, MLSTM_HD), lambda b, c: (b, 0, 0, 0))
    c_spec = pl.BlockSpec((bb, H, MLSTM_HD, MLSTM_HD), lambda b, c: (b, 0, 0, 0))
    hm, C1, n1, m1 = pl.pallas_call(
        functools.partial(_mlstm_kernel, L=L, nc=nc, bb=bb, group=group),
        out_shape=(
            jax.ShapeDtypeStruct((B, S, W), F32),
            jax.ShapeDtypeStruct((B, H, MLSTM_HD, MLSTM_HD), F32),
            jax.ShapeDtypeStruct((B, H, 1, MLSTM_HD), F32),
            jax.ShapeDtypeStruct((B, H, 1, LANES), F32),
        ),
        grid=(B // bb, nc),
        in_specs=[
            smem, pl.BlockSpec((1, 2 * H), lambda b, c: (0, 0)),
            zspec(0), zspec(1), zspec(2), zspec(3),
            pl.BlockSpec((bb, L, 2 * H), lambda b, c: (b, c, 0)),
            c_spec, hd_spec,
            pl.BlockSpec((1, W), lambda b, c: (0, 0)),
        ],
        out_specs=(pl.BlockSpec((bb, L, W), lambda b, c: (b, c, 0)), c_spec, hd_spec, hd_spec),
        scratch_shapes=[pltpu.VMEM((bb, H, MLSTM_HD, MLSTM_HD), F32), pltpu.VMEM((bb, H, 1, MLSTM_HD), F32),
                        pltpu.VMEM((bb, H, 1, LANES), F32)],
        compiler_params=_cparams(("parallel", "arbitrary")),
        name="mlstm",
    )(m0.reshape(B * H), b_gates.reshape(1, 2 * H), z, z, z, z, gates,
      C0, n0.reshape(B, H, 1, MLSTM_HD), norm_g.reshape(1, W))
    return hm, C1, n1.reshape(B, H, MLSTM_HD), m1[:, :, 0, 0]


def _expm1(y):
    u = jnp.exp(y)
    small = jnp.where(u == 1.0, y, (u - 1.0) * y / jnp.log(u))
    return jnp.where(jnp.abs(y) > 0.5, u - 1.0, small)


def _rglru_kernel(xr_ref, gr_ref, buf_ref, h0_ref, cw_ref, cb_ref, wa_ref, wx_ref, ba_ref, bx_ref, lam_ref,
                  hr_out, h_out, buf_out, prev_scr, h_scr, a_scr, u_scr, c_scr, *, ns, chained):
    s = pl.program_id(1)
    bb, ts, W = xr_ref.shape
    nb = bb * ts // SUBLANES
    x = xr_ref[...].reshape(nb, SUBLANES, W)

    if chained:
        @pl.when(s == 0)
        def _():
            prev_scr[...] = buf_ref[0]
            h_scr[...] = h0_ref[0]

        prev = jnp.concatenate([prev_scr[...][None], x[:nb - 1]], axis=0)
        prev_scr[...] = x[nb - 1]
    else:
        prev = buf_ref[...]

    t_idx = lax.broadcasted_iota(jnp.int32, x.shape, 1)
    cw = cw_ref[...]
    xc = cb_ref[...] + cw[RG_CONV - 1:RG_CONV, :] * x
    for d in range(1, RG_CONV):
        back = jnp.where(t_idx >= d, pltpu.roll(x, d, 1), pltpu.roll(prev, d, 1))
        xc = xc + cw[RG_CONV - 1 - d:RG_CONV - d, :] * back

    xc2 = xc.reshape(nb * SUBLANES, W)
    xb = xc2.astype(BF)
    r = jax.nn.sigmoid(_dot(xb, wa_ref[...]) + ba_ref[...])
    i = jax.nn.sigmoid(_dot(xb, wx_ref[...]) + bx_ref[...])
    lam = lam_ref[...]
    softplus_neg = jnp.maximum(-lam, 0.0) + jnp.log1p(jnp.exp(-jnp.abs(lam)))
    log_a = (-RG_C * softplus_neg) * r
    a = jnp.exp(log_a).reshape(nb, SUBLANES, W)
    u = (jnp.sqrt(-_expm1(2.0 * log_a)) * i * xc2).reshape(nb, SUBLANES, W)

    d = 1
    while d < SUBLANES:
        inside = t_idx >= d
        u = jnp.where(inside, a * pltpu.roll(u, d, 1) + u, u)
        a = jnp.where(inside, a * pltpu.roll(a, d, 1), a)
        d *= 2

    if chained:
        a_scr[...] = a
        u_scr[...] = u

        def carry_in(k, h):
            c_scr[k] = jnp.broadcast_to(h, (SUBLANES, W))
            return a_scr[k, SUBLANES - 1:SUBLANES, :] * h + u_scr[k, SUBLANES - 1:SUBLANES, :]

        h_last = lax.fori_loop(0, nb, carry_in, h_scr[...], unroll=8)
        h_scr[...] = h_last
        hs = a * c_scr[...] + u
    else:
        hs = a * h0_ref[...] + u
        h_last = hs[:, SUBLANES - 1:SUBLANES, :]
    hr_out[...] = (hs * jax.nn.gelu(gr_ref[...].reshape(nb, SUBLANES, W))).reshape(bb, ts, W)

    tail = SUBLANES - (RG_CONV - 1)
    if chained:
        @pl.when(s == ns - 1)
        def _():
            h_out[0] = h_last
            buf_out[0] = x[nb - 1, tail:, :]
    else:
        h_out[...] = h_last
        buf_out[...] = x[:, tail:, :]


def rglru_mixer(z, h0, buf0, conv_w, conv_b, wa_bd, wx_bd, b_a, b_x, lam):
    B, S, _ = z.shape
    W = RG_WIDTH
    rows = 512
    chained = S > SUBLANES
    if chained:
        bb, ts = 1, min(S, rows)
        assert S % ts == 0 and ts % SUBLANES == 0
    else:
        bb, ts = min(B, rows // SUBLANES), S
        assert S == SUBLANES and B % bb == 0
    ns = S // ts
    nb = bb * ts // SUBLANES
    xr_blk = (4 * MLSTM_WIDTH) // W
    row = lambda a: a.reshape(1, W)
    buf8 = jnp.pad(buf0, ((0, 0), (SUBLANES - (RG_CONV - 1), 0), (0, 0)))
    st = lambda n: pl.BlockSpec((bb, n, W), lambda b, s: (b, 0, 0))
    hr, h1, buf1 = pl.pallas_call(
        functools.partial(_rglru_kernel, ns=ns, chained=chained),
        out_shape=(
            jax.ShapeDtypeStruct((B, S, W), F32),
            jax.ShapeDtypeStruct((B, 1, W), F32),
            jax.ShapeDtypeStruct((B, RG_CONV - 1, W), F32),
        ),
        grid=(B // bb, ns),
        in_specs=[
            pl.BlockSpec((bb, ts, W), lambda b, s: (b, s, xr_blk)),
            pl.BlockSpec((bb, ts, W), lambda b, s: (b, s, xr_blk + 1)),
            st(SUBLANES), st(1),
            _resident((RG_CONV, W)), _resident((1, W)), _resident((W, W)), _resident((W, W)),
            _resident((1, W)), _resident((1, W)), _resident((1, W)),
        ],
        out_specs=(pl.BlockSpec((bb, ts, W), lambda b, s: (b, s, 0)), st(1), st(RG_CONV - 1)),
        scratch_shapes=[pltpu.VMEM((SUBLANES, W), F32), pltpu.VMEM((1, W), F32),
                        pltpu.VMEM((nb, SUBLANES, W), F32), pltpu.VMEM((nb, SUBLANES, W), F32),
                        pltpu.VMEM((nb, SUBLANES, W), F32)],
        compiler_params=_cparams(("parallel", "arbitrary")),
        name="rglru",
    )(z, z, buf8, h0.reshape(B, 1, W), conv_w, row(conv_b), wa_bd, wx_bd, row(b_a), row(b_x), row(lam))
    return hr, h1.reshape(B, W), buf1


def _rwkv_pre_kernel(x_ref, sh_ref, sc_ref, prev0_ref, mu_ref, wr_ref, wk_ref, wv_ref, w0_ref, w1_ref,
                     w2_ref, a0_ref, a1_ref, a2_ref, g1_ref, g2_ref, kk_ref, ka_ref,
                     r_out, lw_out, k_out, v_out, kk_out, a_out, g_out, shift_out, carry_scr, *, ns):
    s = pl.program_id(1)
    bb, ts, _ = x_ref.shape
    tm = bb * ts

    @pl.when(s == 0)
    def _():
        carry_scr[...] = prev0_ref[...]

    h = x_ref[...] * (1.0 + sc_ref[...]) + sh_ref[...]
    t_idx = lax.broadcasted_iota(jnp.int32, h.shape, 1)
    h_prev = jnp.where(t_idx == 0, carry_scr[...], pltpu.roll(h, 1, 1))
    last = h[:, ts - 1:ts, :]
    carry_scr[...] = last
    dx = (h_prev - h).reshape(tm, D_MODEL)
    h2 = h.reshape(tm, D_MODEL)
    mu = mu_ref[...]
    mix = lambda j: (h2 + dx * mu[j:j + 1, :]).astype(BF)
    xr, xw, xk, xv, xa, xg = [mix(j) for j in range(6)]

    r = _dot(xr, wr_ref[...])
    k = _dot(xk, wk_ref[...])
    v = _dot(xv, wv_ref[...])
    wl = w0_ref[...] + _dot(jnp.tanh(_dot(xw, w1_ref[...])).astype(BF), w2_ref[...])
    log_decay = -DECAY_SCALE * jax.nn.sigmoid(wl)
    a = jax.nn.sigmoid(a0_ref[...] + _dot(_dot(xa, a1_ref[...]).astype(BF), a2_ref[...]))
    g = _dot(jax.nn.sigmoid(_dot(xg, g1_ref[...])).astype(BF), g2_ref[...])
    kk = k * kk_ref[...]
    k = k * (1.0 + (a - 1.0) * ka_ref[...])

    shp = x_ref.shape
    r_out[...] = r.reshape(shp)
    lw_out[...] = log_decay.reshape(shp)
    k_out[...] = k.reshape(shp)
    v_out[...] = v.reshape(shp)
    kk_out[...] = kk.reshape(shp)
    a_out[...] = a.reshape(shp)
    g_out[...] = g.reshape(shp)

    @pl.when(s == ns - 1)
    def _():
        shift_out[...] = last


def rwkv_pre(x, mod, prev0, P):
    B, S, _ = x.shape
    bb, ts = _row_tile(B, S, 256)
    ns = S // ts
    xmap = lambda i, s: (i, s, 0)
    full = lambda shp: pl.BlockSpec(shp, lambda i, s: (0,) * len(shp))
    tok = pl.BlockSpec((bb, ts, D_MODEL), xmap)
    st = pl.BlockSpec((bb, 1, D_MODEL), lambda i, s: (i, 0, 0))
    sq, lo_in, lo_out, vec = (D_MODEL, D_MODEL), (D_MODEL, LORA_PAD), (LORA_PAD, D_MODEL), (1, D_MODEL)
    outs = pl.pallas_call(
        functools.partial(_rwkv_pre_kernel, ns=ns),
        out_shape=tuple([jax.ShapeDtypeStruct(x.shape, F32)] * 7
                        + [jax.ShapeDtypeStruct((B, 1, D_MODEL), F32)]),
        grid=(B // bb, ns),
        in_specs=[
            tok, _mod_spec(bb, 3), _mod_spec(bb, 4), st, full((6, D_MODEL)),
            full(sq), full(sq), full(sq), full(vec), full(lo_in), full(lo_out),
            full(vec), full(lo_in), full(lo_out), full(lo_in), full(lo_out),
            full(vec), full(vec),
        ],
        out_specs=tuple([tok] * 7 + [st]),
        scratch_shapes=[pltpu.VMEM((bb, 1, D_MODEL), F32)],
        compiler_params=_cparams(("parallel", "arbitrary")),
        name="rwkv_pre",
    )(x, mod, mod, prev0.reshape(B, 1, D_MODEL), P['mu'], P['wr'], P['wk'], P['wv'], P['w0'], P['w1'],
      P['w2'], P['a0'], P['a1'], P['a2'], P['g1'], P['g2'], P['k_k'], P['k_a'])
    return outs


def _stack_heads(x, first):
    return jnp.concatenate([jnp.where(first, x, 0.0), jnp.where(first, 0.0, x)], axis=0)


def _seg_sum(x, first):
    s0 = jnp.sum(jnp.where(first, x, 0.0), axis=1, keepdims=True)
    s1 = jnp.sum(jnp.where(first, 0.0, x), axis=1, keepdims=True)
    return jnp.where(first, s0, s1)


def _wkv_pair_chunks(toks, sps, vecs, consts, L):
    tri, first, strict, incl = consts
    L2 = 2 * L
    rng = range(len(toks))
    wide = L2 % LANES == 0
    splits = [_split3(t[1]) for t in toks]
    c_incl = [_dot(tri, s[0]) + _dot(tri, s[1]) + _dot(tri, s[2]) for s in splits]
    e_inv = [jnp.exp(-c) for c in c_incl]
    kn = [t[4] / jnp.maximum(jnp.sqrt(_seg_sum(t[4] * t[4], first)), 1e-12) for t in toks]
    la = [_stack_heads(-kn[p] * jnp.exp(c_incl[p] - toks[p][1]), first).astype(BF) for p in rng]
    lr = [_stack_heads(toks[p][0] * jnp.exp(c_incl[p]), first).astype(BF) for p in rng]
    rb = [_stack_heads(kn[p] * toks[p][5] * e_inv[p], first).astype(BF) for p in rng]
    rk = [_stack_heads(toks[p][2] * e_inv[p], first).astype(BF) for p in rng]
    vs = [_stack_heads(toks[p][3], first).astype(BF) for p in rng]
    rbk = [jnp.concatenate([rb[p], rk[p]], axis=0) for p in rng]

    if wide:
        nn = [_dot_nt(la[p], rbk[p]) for p in rng]
        mm = [_dot_nt(lr[p], rbk[p]) for p in rng]
        n_raw = [x[:, :L2] for x in nn]
        n_ak = [jnp.where(strict, x[:, L2:], 0.0).astype(BF) for x in nn]
        m_bk = [jnp.concatenate([jnp.where(incl, x[:, :L2], 0.0), jnp.where(incl, x[:, L2:], 0.0)],
                                axis=1).astype(BF) for x in mm]
    else:
        n_raw = [_dot_nt(la[p], rb[p]) for p in rng]
        n_ak = [jnp.where(strict, _dot_nt(la[p], rk[p]), 0.0).astype(BF) for p in rng]
        m_rb = [jnp.where(incl, _dot_nt(lr[p], rb[p]), 0.0).astype(BF) for p in rng]
        m_rk = [jnp.where(incl, _dot_nt(lr[p], rk[p]), 0.0).astype(BF) for p in rng]

    row = lax.broadcasted_iota(jnp.int32, (L2, L2), 0)
    col = lax.broadcasted_iota(jnp.int32, (L2, L2), 1)
    base = (row // INV_BASE == col // INV_BASE) & (col < row)
    xb = [jnp.where(base, x, 0.0).astype(BF) for x in n_raw]
    tinv = [jnp.where(row == col, 1.0, 0.0) + x.astype(F32) for x in xb]
    x2 = [_dot(x, x).astype(BF) for x in xb]
    if wide:
        prod = [_dot(x2[p], jnp.concatenate([x2[p], tinv[p].astype(BF)], axis=1)) for p in rng]
        tinv = [tinv[p] + prod[p][:, L2:] for p in rng]
        x4 = [x[:, :L2].astype(BF) for x in prod]
    else:
        tinv = [tinv[p] + _dot(x2[p], tinv[p].astype(BF)) for p in rng]
        x4 = [_dot(x, x).astype(BF) for x in x2]
    tinv = [tinv[p] + _dot(x4[p], tinv[p].astype(BF)) for p in rng]
    blk = INV_BASE
    while blk < L:
        off = (row // (2 * blk) == col // (2 * blk)) & (row // blk == col // blk + 1)
        tb = [t.astype(BF) for t in tinv]
        pr = [_dot(jnp.where(off, n_raw[p], 0.0).astype(BF), tb[p]).astype(BF) for p in rng]
        tinv = [tinv[p] + _dot(tb[p], pr[p]) for p in rng]
        blk *= 2

    spb = [s.astype(BF) for s in sps]
    rhs = [_dot_nt(la[p], spb[p]) + _dot(n_ak[p], vs[p]) for p in rng]
    ub = [_dot(tinv[p].astype(BF), rhs[p].astype(BF)).astype(BF) for p in rng]

    uv = [jnp.concatenate([ub[p], vs[p]], axis=0) for p in rng]
    if wide:
        ys = [_dot_nt(lr[p], spb[p]) + _dot(m_bk[p], uv[p]) for p in rng]
    else:
        ys = [_dot_nt(lr[p], spb[p]) + _dot(m_rb[p], ub[p]) + _dot(m_rk[p], vs[p]) for p in rng]
    sp_new = [(sps[p] + _dot_tn(uv[p], rbk[p])) * jnp.exp(c_incl[p][L - 1:L, :]) for p in rng]

    outs = []
    for p in rng:
        r, _, k, v, _, _, gate = toks[p]
        r_k, lnx_g, lnx_b = vecs[p]
        y = ys[p][0:L, :] + ys[p][L:L2, :]
        yc = y - _seg_sum(y, first) * (1.0 / RWKV_HS)
        yv = _seg_sum(yc * yc, first) * (1.0 / RWKV_HS)
        yn = yc * lax.rsqrt(yv + RWKV_LN_EPS) * lnx_g + lnx_b
        outs.append(((yn + _seg_sum(r * k * r_k, first) * v) * gate, sp_new[p]))
    return outs


def _wkv_kernel(r_ref, lw_ref, k_ref, v_ref, kk_ref, a_ref, g_ref, s0_ref, rk_ref, lxg_ref, lxb_ref,
                y_out, s_out, sp_scr, *, L, nc, bb, group):
    c = pl.program_id(1)
    hs = RWKV_HS
    npair = RWKV_HEADS // 2
    L2 = 2 * L

    @pl.when(c == 0)
    def _():
        zero = jnp.zeros((hs, hs), F32)

        def init(i, carry):
            for p in range(npair):
                top = jnp.concatenate([s0_ref[i, 2 * p], zero], axis=1)
                bot = jnp.concatenate([zero, s0_ref[i, 2 * p + 1]], axis=1)
                sp_scr[i, p] = jnp.concatenate([top, bot], axis=0)
            return carry

        lax.fori_loop(0, bb, init, 0)

    row = lax.broadcasted_iota(jnp.int32, (L, L), 0)
    col = lax.broadcasted_iota(jnp.int32, (L, L), 1)
    tri = jnp.where(col <= row, 1.0, 0.0).astype(BF)
    first = lax.broadcasted_iota(jnp.int32, (L, LANES), 1) < hs
    row2 = lax.broadcasted_iota(jnp.int32, (L2, L2), 0)
    col2 = lax.broadcasted_iota(jnp.int32, (L2, L2), 1)
    same = (row2 >= L) == (col2 >= L)
    consts = (tri, first, same & (col2 < row2), same & (col2 <= row2))
    sls = [slice(p * LANES, (p + 1) * LANES) for p in range(npair)]

    def seqs(j, carry):
        ids = [(j * group + g, p) for g in range(group) for p in range(npair)]
        toks = [tuple(ref[i, :, sls[p]] for ref in (r_ref, lw_ref, k_ref, v_ref, kk_ref, a_ref, g_ref))
                for i, p in ids]
        vecs = [(rk_ref[:, sls[p]], lxg_ref[:, sls[p]], lxb_ref[:, sls[p]]) for _, p in ids]
        outs = _wkv_pair_chunks(toks, [sp_scr[i, p] for i, p in ids], vecs, consts, L)
        for (i, p), (y, sp_new) in zip(ids, outs):
            y_out[i, :, sls[p]] = y
            sp_scr[i, p] = sp_new
        return carry

    lax.fori_loop(0, bb // group, seqs, 0)

    @pl.when(c == nc - 1)
    def _():
        def fin(i, carry):
            for p in range(npair):
                sp = sp_scr[i, p]
                s_out[i, 2 * p] = sp[0:hs, 0:hs]
                s_out[i, 2 * p + 1] = sp[hs:2 * hs, hs:2 * hs]
            return carry

        lax.fori_loop(0, bb, fin, 0)


def wkv_recurrence(r, lw, k, v, kk, a, g, s0, P):
    B, S, _ = r.shape
    L = RWKV_CHUNK if S % RWKV_CHUNK == 0 else S
    assert L & (L - 1) == 0 and L % SUBLANES == 0
    nc = S // L
    bb, group = (2, 2) if nc > 1 else (min(B, 8), min(B, 8))
    assert B % bb == 0 and bb % group == 0
    npair = RWKV_HEADS // 2
    tok = pl.BlockSpec((bb, L, D_MODEL), lambda b, c: (b, c, 0))
    st = pl.BlockSpec((bb, RWKV_HEADS, RWKV_HS, RWKV_HS), lambda b, c: (b, 0, 0, 0))
    vec = pl.BlockSpec((1, D_MODEL), lambda b, c: (0, 0))
    return pl.pallas_call(
        functools.partial(_wkv_kernel, L=L, nc=nc, bb=bb, group=group),
        out_shape=(jax.ShapeDtypeStruct(r.shape, F32), jax.ShapeDtypeStruct(s0.shape, F32)),
        grid=(B // bb, nc),
        in_specs=[tok] * 7 + [st, vec, vec, vec],
        out_specs=(tok, st),
        scratch_shapes=[pltpu.VMEM((bb, npair, 2 * RWKV_HS, 2 * RWKV_HS), F32)],
        compiler_params=_cparams(("parallel", "arbitrary")),
        name="wkv",
    )(r, lw, k, v, kk, a, g, s0, P['r_k'], P['lnx_g'], P['lnx_b'])


def _block_diag(w):
    G, n, _ = w.shape
    eye = jnp.eye(G, dtype=w.dtype)
    return (eye[:, None, :, None] * w[:, :, None, :]).reshape(G * n, G * n)


def _pad_cols(w, n):
    return jnp.pad(w, ((0, 0), (0, n - w.shape[1])))


def _pad_rows(w, n):
    return jnp.pad(w, ((0, n - w.shape[0]), (0, 0)))


def _prep_ab(j, p):
    w_in = p['ab_w_in'][j]
    n_main = 4 * MLSTM_WIDTH
    n_gate = 2 * MLSTM_HEADS
    w_all = jnp.concatenate([w_in[:, :n_main], w_in[:, n_main + n_gate:],
                             w_in[:, n_main:n_main + n_gate]], axis=1)
    return dict(
        w_in=_pad_cols(w_all, Z_COLS).astype(BF),
        b_gates=p['mlstm_b_gates'][j], norm_g=p['mlstm_norm_g'][j],
        conv_w=p['rg_conv_w'][j], conv_b=p['rg_conv_b'][j],
        wa=_block_diag(p['rg_w_a'][j]).astype(BF), wx=_block_diag(p['rg_w_x'][j]).astype(BF),
        b_a=p['rg_b_a'][j], b_x=p['rg_b_x'][j], lam=p['rg_lambda'][j],
        w_out_m=p['ab_w_out'][j][:MLSTM_WIDTH].astype(BF), w_out_r=p['ab_w_out'][j][MLSTM_WIDTH:].astype(BF),
    )


def _prep_rwkv(j, p):
    vec = lambda a: a.reshape(1, D_MODEL)
    return dict(
        mu=p['rw_mu'][j],
        wr=p['rw_wr'][j].astype(BF), wk=p['rw_wk'][j].astype(BF), wv=p['rw_wv'][j].astype(BF),
        w0=vec(p['rw_w0'][j]), w1=_pad_cols(p['rw_w1'][j], LORA_PAD).astype(BF),
        w2=_pad_rows(p['rw_w2'][j], LORA_PAD).astype(BF),
        a0=vec(p['rw_a0'][j]), a1=_pad_cols(p['rw_a1'][j], LORA_PAD).astype(BF),
        a2=_pad_rows(p['rw_a2'][j], LORA_PAD).astype(BF),
        g1=_pad_cols(p['rw_g1'][j], LORA_PAD).astype(BF), g2=_pad_rows(p['rw_g2'][j], LORA_PAD).astype(BF),
        k_k=vec(p['rw_k_k'][j]), k_a=vec(p['rw_k_a'][j]), r_k=vec(p['rw_r_k'][j]),
        lnx_g=vec(p['rw_lnx_g'][j]), lnx_b=vec(p['rw_lnx_b'][j]), wo=p['rw_wo'][j].astype(BF),
    )


def ab_mixer(x, mod, st, A):
    mC, mn, mm, rh, rconv = st
    z, gates = mod_matmul(x, mod, A['w_in'], 4 * MLSTM_WIDTH + 2 * RG_WIDTH, 2 * MLSTM_HEADS)
    hm, C1, n1, m1 = mlstm_mixer(z, gates, mC, mn, mm, A['b_gates'], A['norm_g'])
    hr, rh1, buf1 = rglru_mixer(z, rh, rconv, A['conv_w'], A['conv_b'], A['wa'], A['wx'],
                                A['b_a'], A['b_x'], A['lam'])
    return ((hm, 0), (hr, 0), A['w_out_m'], A['w_out_r']), (C1, n1, m1, rh1, buf1)


def rwkv_mixer(x, mod, st, R):
    wkv0, prev0 = st
    r, lw, k, v, kk, a, g, shift = rwkv_pre(x, mod, prev0, R)
    y, wkv1 = wkv_recurrence(r, lw, k, v, kk, a, g, wkv0, R)
    half = D_MODEL // 2
    return ((y, 0), (y, 1), R['wo'][:half], R['wo'][half:]), (wkv1, shift.reshape(shift.shape[0], D_MODEL))


def run_trunk(x, mods, states, W):
    mC, mn, mm, rh, rconv, wkv, shift = states
    new_ab, new_c = [], []
    for layer in range(DEPTH):
        mod = mods[layer]
        lg = lambda i: W['ln_g'][layer, i].reshape(1, D_MODEL)
        lb = lambda i: W['ln_b'][layer, i].reshape(1, D_MODEL)
        f = W['ffn']
        x = ffn_block(x, mod, 0, layer, 0, f[0], f[1], f[2], lg(0), lb(0))
        j = layer // 2
        if layer % 2 == 0:
            mixed, st = ab_mixer(x, mod, (mC[j], mn[j], mm[j], rh[j], rconv[j]), W['ab'][j])
            new_ab.append(st)
        else:
            mixed, st = rwkv_mixer(x, mod, (wkv[j], shift[j]), W['rwkv'][j])
            new_c.append(st)
        x = ffn_block(x, mod, 2, layer, 1, f[0], f[1], f[2], lg(2), lb(2), mixer=mixed + (lg(1), lb(1)))
    stk = lambda sts, i: sts[0][i][None] if len(sts) == 1 else jnp.stack([s[i] for s in sts], axis=0)
    return x, (stk(new_ab, 0), stk(new_ab, 1), stk(new_ab, 2), stk(new_ab, 3), stk(new_ab, 4),
               stk(new_c, 0), stk(new_c, 1))


def _zero_states(B):
    n_ab, n_c = (DEPTH + 1) // 2, DEPTH // 2
    return (jnp.zeros((n_ab, B, MLSTM_HEADS, MLSTM_HD, MLSTM_HD), F32),
            jnp.zeros((n_ab, B, MLSTM_HEADS, MLSTM_HD), F32),
            jnp.zeros((n_ab, B, MLSTM_HEADS), F32),
            jnp.zeros((n_ab, B, RG_WIDTH), F32),
            jnp.zeros((n_ab, B, RG_CONV - 1, RG_WIDTH), F32),
            jnp.zeros((n_c, B, RWKV_HEADS, RWKV_HS, RWKV_HS), F32),
            jnp.zeros((n_c, B, D_MODEL), F32))


def kernel(x_prompt, x_sample, c_prompt, c_sample, state_mlstm_C, state_mlstm_n, state_mlstm_m, state_rglru_h, state_rglru_conv, state_rwkv_wkv, state_rwkv_shift, ada_w, ada_b, ln_g, ln_b, ffn_w1, ffn_w3, ffn_w2, ab_w_in, mlstm_b_gates, mlstm_norm_g, rg_conv_w, rg_conv_b, rg_w_a, rg_b_a, rg_w_x, rg_b_x, rg_lambda, ab_w_out, rw_mu, rw_wr, rw_wk, rw_wv, rw_w0, rw_w1, rw_w2, rw_a0, rw_a1, rw_a2, rw_g1, rw_g2, rw_k_k, rw_k_a, rw_r_k, rw_lnx_g, rw_lnx_b, rw_wo):
    p = dict(ab_w_in=ab_w_in, mlstm_b_gates=mlstm_b_gates, mlstm_norm_g=mlstm_norm_g,
             rg_conv_w=rg_conv_w, rg_conv_b=rg_conv_b, rg_w_a=rg_w_a, rg_b_a=rg_b_a, rg_w_x=rg_w_x,
             rg_b_x=rg_b_x, rg_lambda=rg_lambda, ab_w_out=ab_w_out, rw_mu=rw_mu, rw_wr=rw_wr,
             rw_wk=rw_wk, rw_wv=rw_wv, rw_w0=rw_w0, rw_w1=rw_w1, rw_w2=rw_w2, rw_a0=rw_a0,
             rw_a1=rw_a1, rw_a2=rw_a2, rw_g1=rw_g1, rw_g2=rw_g2, rw_k_k=rw_k_k, rw_k_a=rw_k_a,
             rw_r_k=rw_r_k, rw_lnx_g=rw_lnx_g, rw_lnx_b=rw_lnx_b, rw_wo=rw_wo)
    W = dict(
        ln_g=ln_g, ln_b=ln_b,
        ffn=(ffn_w1.astype(BF), ffn_w3.astype(BF), ffn_w2.astype(BF)),
        ab=[_prep_ab(j, p) for j in range((DEPTH + 1) // 2)],
        rwkv=[_prep_rwkv(j, p) for j in range(DEPTH // 2)],
    )
    Bp, Bs = x_prompt.shape[0], x_sample.shape[0]
    mod_all = adaln(jnp.concatenate([c_prompt, c_sample], axis=0), ada_w, ada_b)
    mods_p = [mod_all[l, :Bp].reshape(Bp, 3 * N_SUB, 1, D_MODEL) for l in range(DEPTH)]
    mods_s = [mod_all[l, Bp:].reshape(Bs, 3 * N_SUB, 1, D_MODEL) for l in range(DEPTH)]
    y_prompt, sp = run_trunk(x_prompt, mods_p, _zero_states(Bp), W)
    y_sample, ss = run_trunk(x_sample, mods_s,
                             (state_mlstm_C, state_mlstm_n, state_mlstm_m, state_rglru_h,
                              state_rglru_conv, state_rwkv_wkv, state_rwkv_shift), W)
    return (y_prompt, y_sample) + tuple(sp) + tuple(ss)
```

```python
import functools
import math

import jax
import jax.numpy as jnp
from jax import lax
from jax.experimental import pallas as pl
from jax.experimental.pallas import tpu as pltpu

D_MODEL = 1024
DEPTH = 2
N_SUB = 3
MLSTM_WIDTH = 512
MLSTM_HEADS = 4
MLSTM_HD = 128
MLSTM_CHUNK = 128
RG_WIDTH = 512
RG_BLOCKS = 8
RG_BD = 64
RG_CONV = 4
RG_C = 8.0
RWKV_HS = 64
RWKV_HEADS = 16
RWKV_LN_EPS = 64e-5
RWKV_CHUNK = 64
DECAY_SCALE = math.exp(-0.5)
INV_BASE = 8
D_FF = 2816
ALPHA = (2.0 * DEPTH) ** 0.25
LN_EPS = 1e-5
HEAD_NORM_EPS = 1e-6

LANES = 128
SUBLANES = 8
VMEM_LIMIT_BYTES = 56 * 1024 * 1024
Z_COLS = 3200
LORA_PAD = 128

BF = jnp.bfloat16
F32 = jnp.float32

NT_DIMS = (((1,), (1,)), ((), ()))
TN_DIMS = (((0,), (0,)), ((), ()))


def _cparams(sem):
    return pltpu.CompilerParams(dimension_semantics=sem, vmem_limit_bytes=VMEM_LIMIT_BYTES)


def _dot(a, b):
    return jnp.dot(a, b, preferred_element_type=F32)


def _dot_nt(a, b):
    return lax.dot_general(a, b, NT_DIMS, preferred_element_type=F32)


def _dot_tn(a, b):
    return lax.dot_general(a, b, TN_DIMS, preferred_element_type=F32)


def _layer_norm(z, g, b):
    mu = jnp.mean(z, axis=-1, keepdims=True)
    zc = z - mu
    var = jnp.mean(zc * zc, axis=-1, keepdims=True)
    return zc * lax.rsqrt(var + LN_EPS) * g + b


def _row_tile(B, S, rows):
    if S >= rows:
        assert S % rows == 0
        return 1, rows
    bb = min(B, rows // S)
    assert B % bb == 0
    return bb, S


def _split2(x):
    hi = x.astype(BF)
    lo = (x - hi.astype(F32)).astype(BF)
    return hi, lo


def _split3(x):
    hi = x.astype(BF)
    r1 = x - hi.astype(F32)
    mid = r1.astype(BF)
    lo = (r1 - mid.astype(F32)).astype(BF)
    return hi, mid, lo


def _adaln_kernel(c_ref, w_ref, b_ref, o_ref):
    h = jax.nn.silu(c_ref[...]).astype(BF)
    res = _dot(h, w_ref[...].astype(BF)) + b_ref[...]
    o_ref[...] = res.reshape(o_ref.shape)


def adaln(c_all, ada_w, ada_b):
    Bc = c_all.shape[0]
    n_chunk = ada_w.shape[-1] // D_MODEL
    return pl.pallas_call(
        _adaln_kernel,
        out_shape=jax.ShapeDtypeStruct((DEPTH, Bc, n_chunk, 1, D_MODEL), F32),
        grid=(DEPTH, n_chunk),
        in_specs=[
            pl.BlockSpec((Bc, D_MODEL), lambda l, j: (0, 0)),
            pl.BlockSpec((None, D_MODEL, D_MODEL), lambda l, j: (l, 0, j)),
            pl.BlockSpec((None, 1, D_MODEL), lambda l, j: (l, 0, j)),
        ],
        out_specs=pl.BlockSpec((None, Bc, None, 1, D_MODEL), lambda l, j: (l, 0, j, 0, 0)),
        compiler_params=_cparams(("parallel", "parallel")),
        name="adaln",
    )(c_all, ada_w, ada_b.reshape(DEPTH, 1, n_chunk * D_MODEL))


def _mod_spec(bb, k):
    return pl.BlockSpec((bb, None, 1, D_MODEL), lambda i, *_: (i, k, 0, 0))


FFN_ROW_SPLITS = 2


def _ffn_rows(x, mixer, sh, sc, gt, w1_ref, w3_ref, w2_ref, lg, lb, res_w):
    bb, ts, _ = x.shape
    if mixer is not None:
        mgt, m1, m2, mw1_ref, mw2_ref, mlg, mlb = mixer
        flat = lambda m: m.reshape(bb * ts, -1).astype(BF)
        ym = _dot(flat(m1), mw1_ref[...]) + _dot(flat(m2), mw2_ref[...])
        yield
        x = _layer_norm(ALPHA * x + (1.0 + mgt) * ym.reshape(bb, ts, D_MODEL), mlg, mlb)
    h = (x * (1.0 + sc) + sh).reshape(bb * ts, D_MODEL).astype(BF)
    a = _dot(h, w1_ref[...])
    b = _dot(h, w3_ref[...])
    yield
    g = (jax.nn.silu(a) * b).astype(BF)
    y = _dot(g, w2_ref[...])
    yield
    return _layer_norm(ALPHA * x + (res_w * (1.0 + gt)) * y.reshape(bb, ts, D_MODEL), lg, lb)


def _ffn_kernel(*refs, res_w, after_mixer):
    if after_mixer:
        (x_ref, mgt_ref, m1_ref, m2_ref, mw1_ref, mw2_ref, mlg_ref, mlb_ref,
         sh_ref, sc_ref, gt_ref, w1_ref, w3_ref, w2_ref, lg_ref, lb_ref, o_ref) = refs
    else:
        x_ref, sh_ref, sc_ref, gt_ref, w1_ref, w3_ref, w2_ref, lg_ref, lb_ref, o_ref = refs
    bb, ts, _ = x_ref.shape
    if bb > 1:
        n = bb // FFN_ROW_SPLITS
        tok = lambda ref, i: ref[i * n:(i + 1) * n]
        per_seq = tok
    else:
        n = ts // FFN_ROW_SPLITS
        tok = lambda ref, i: ref[:, i * n:(i + 1) * n, :]
        per_seq = lambda ref, i: ref[...]
    gens = []
    for i in range(FFN_ROW_SPLITS):
        mixer = None
        if after_mixer:
            mixer = (per_seq(mgt_ref, i), tok(m1_ref, i), tok(m2_ref, i), mw1_ref, mw2_ref, mlg_ref[...], mlb_ref[...])
        gens.append(_ffn_rows(tok(x_ref, i), mixer, per_seq(sh_ref, i), per_seq(sc_ref, i), per_seq(gt_ref, i),
                              w1_ref, w3_ref, w2_ref, lg_ref[...], lb_ref[...], res_w))
    for i, out in enumerate(_lockstep(gens)):
        if bb > 1:
            o_ref[i * n:(i + 1) * n] = out
        else:
            o_ref[:, i * n:(i + 1) * n, :] = out


def _resident(shape):
    return pl.BlockSpec(shape, lambda *_: (0,) * len(shape), pipeline_mode=pl.Buffered(1))


def ffn_block(x, mod, sub, layer, half, w1, w3, w2, lg, lb, mixer=None):
    B, S, _ = x.shape
    bb, ts = _row_tile(B, S, 512)
    xmap = lambda i, s: (i, s, 0)
    wspec = lambda r, c: pl.BlockSpec((None, None, r, c), lambda i, s: (layer, half, 0, 0),
                                      pipeline_mode=pl.Buffered(1))
    vec = _resident((1, D_MODEL))
    specs = [pl.BlockSpec((bb, ts, D_MODEL), xmap)]
    args = [x]
    if mixer is not None:
        (m1, blk1), (m2, blk2), mw1, mw2, mlg, mlb = mixer
        mspec = lambda w, blk: pl.BlockSpec((bb, ts, w.shape[0]), lambda i, s: (i, s, blk))
        specs += [_mod_spec(bb, 5), mspec(mw1, blk1), mspec(mw2, blk2),
                  _resident(mw1.shape), _resident(mw2.shape), vec, vec]
        args += [mod, m1, m2, mw1, mw2, mlg, mlb]
    specs += [_mod_spec(bb, 3 * sub), _mod_spec(bb, 3 * sub + 1), _mod_spec(bb, 3 * sub + 2),
              wspec(D_MODEL, D_FF), wspec(D_MODEL, D_FF), wspec(D_FF, D_MODEL), vec, vec]
    args += [mod, mod, mod, w1, w3, w2, lg, lb]
    return pl.pallas_call(
        functools.partial(_ffn_kernel, res_w=0.5, after_mixer=mixer is not None),
        out_shape=jax.ShapeDtypeStruct(x.shape, F32),
        grid=(B // bb, S // ts),
        in_specs=specs,
        out_specs=pl.BlockSpec((bb, ts, D_MODEL), xmap),
        compiler_params=_cparams(("parallel", "parallel")),
        name="ffn_block",
    )(*args)


def _modmm_kernel(x_ref, sh_ref, sc_ref, w_ref, o_ref, gate_ref):
    bb, ts, _ = x_ref.shape
    n_main = o_ref.shape[-1]
    n_gate = gate_ref.shape[-1]
    h = (x_ref[...] * (1.0 + sc_ref[...]) + sh_ref[...]).reshape(bb * ts, D_MODEL).astype(BF)
    o_ref[...] = _dot(h, w_ref[:, :n_main]).reshape(o_ref.shape)
    gate_ref[...] = _dot(h, w_ref[:, n_main:])[:, :n_gate].reshape(gate_ref.shape)


def mod_matmul(x, mod, w, n_main, n_gate):
    B, S, _ = x.shape
    bb, ts = _row_tile(B, S, 512)
    xmap = lambda i, s: (i, s, 0)
    return pl.pallas_call(
        _modmm_kernel,
        out_shape=(jax.ShapeDtypeStruct((B, S, n_main), F32), jax.ShapeDtypeStruct((B, S, n_gate), F32)),
        grid=(B // bb, S // ts),
        in_specs=[
            pl.BlockSpec((bb, ts, D_MODEL), xmap),
            _mod_spec(bb, 3), _mod_spec(bb, 4),
            _resident(w.shape),
        ],
        out_specs=(pl.BlockSpec((bb, ts, n_main), xmap), pl.BlockSpec((bb, ts, n_gate), xmap)),
        compiler_params=_cparams(("parallel", "parallel")),
        name="mod_matmul",
    )(x, mod, mod, w)


def _log_sigmoid(x):
    return jnp.minimum(x, 0.0) - jnp.log1p(jnp.exp(-jnp.abs(x)))


def _lockstep(gens):
    results = [None] * len(gens)
    live = list(range(len(gens)))
    while live:
        still = []
        for idx in live:
            try:
                next(gens[idx])
                still.append(idx)
            except StopIteration as stop:
                results[idx] = stop.value
        live = still
    return results


def _dot_exact(a, b, dims):
    return lax.dot_general(a, b, dims, precision=lax.Precision.HIGHEST, preferred_element_type=F32)


def _mlstm_gate_views(gl, L):
    row = lax.broadcasted_iota(jnp.int32, (L, L), 0)
    col = lax.broadcasted_iota(jnp.int32, (L, L), 1)
    eye = jnp.where(row == col, 1.0, 0.0)
    lower = jnp.where(col <= row, 1.0, 0.0)
    upper = jnp.where(row <= col, 1.0, 0.0)
    gl_t = _dot_exact(gl, eye, TN_DIMS)
    cum_c = _dot_exact(lower, gl, (((1,), (0,)), ((), ())))
    cum_r = _dot_exact(gl, upper, TN_DIMS)
    return gl_t, cum_c, cum_r


def _mlstm_head_chunk(q, k, v, o, ig_c, b_c, ig_r, b_r, C0, n0, m0, g, L):
    row = lax.broadcasted_iota(jnp.int32, (L, L), 0)
    col = lax.broadcasted_iota(jnp.int32, (L, L), 1)
    causal = col <= row
    dmat = jnp.where(causal, b_c - b_r + ig_r, -jnp.inf)
    m_inter = b_c + m0
    m = jnp.maximum(m_inter, jnp.max(dmat, axis=1, keepdims=True))
    w_inter = jnp.exp(m_inter - m)
    p = jnp.exp(dmat - m)

    k = k * (MLSTM_HD ** -0.5)
    qb = q.astype(BF)
    kb = k.astype(BF)
    yield
    s_raw = _dot_nt(qb, kb)
    q_c0 = _dot_nt(qb, C0.astype(BF))
    yield
    scores = s_raw * p
    num = _dot(scores.astype(BF), v.astype(BF)) + w_inter * q_c0
    den = jnp.sum(scores, axis=1, keepdims=True) + w_inter * jnp.sum(q * n0, axis=1, keepdims=True)
    hh = num / jnp.maximum(jnp.abs(den), jnp.exp(-m))

    m_end = m[L - 1:L, :]
    b_end = b_c[L - 1:L, :]
    w_state = jnp.exp(b_end + m0 - m_end)
    w_rows = jnp.exp(b_end - b_c + ig_c - m_end)
    yield
    c_new = w_state * C0 + _dot_tn((w_rows * v).astype(BF), kb)
    n_new = w_state * n0 + jnp.sum(w_rows * k, axis=0, keepdims=True)

    mu = jnp.mean(hh, axis=1, keepdims=True)
    hc = hh - mu
    var = jnp.mean(hc * hc, axis=1, keepdims=True)
    hn = hc * lax.rsqrt(var + HEAD_NORM_EPS)
    return hn * g * jax.nn.sigmoid(o), c_new, n_new, m_end


def _mlstm_kernel(m0_ref, bg_ref, q_ref, k_ref, v_ref, o_ref, gc_ref, c0_ref, n0_ref, g_ref,
                  h_out, c_out, n_out, m_out, c_scr, n_scr, m_scr, *, L, nc, bb, group):
    ib = pl.program_id(0)
    c = pl.program_id(1)
    H = MLSTM_HEADS

    @pl.when(c == 0)
    def _():
        c_scr[...] = c0_ref[...]
        n_scr[...] = n0_ref[...]

        def init(i, carry):
            for h in range(H):
                m_scr[i, h] = jnp.full((1, LANES), m0_ref[(ib * bb + i) * H + h], F32)
            return carry

        lax.fori_loop(0, bb, init, 0)

    sls = [slice(h * MLSTM_HD, (h + 1) * MLSTM_HD) for h in range(H)]

    def seqs(j, carry):
        views = []
        for g in range(group):
            gates = gc_ref[j * group + g] + bg_ref[...]
            is_input = lax.broadcasted_iota(jnp.int32, gates.shape, 1) < H
            gl = jnp.where(is_input, gates, _log_sigmoid(gates))
            views.append((gl,) + _mlstm_gate_views(gl, L))
        ids = [(j * group + g, h) for g in range(group) for h in range(H)]
        gate_args = [(views[g][0][:, h:h + 1], views[g][2][:, H + h:H + h + 1],
                      views[g][1][h:h + 1, :], views[g][3][H + h:H + h + 1, :])
                     for g in range(group) for h in range(H)]
        heads = _lockstep([_mlstm_head_chunk(
            q_ref[i, :, sls[h]], k_ref[i, :, sls[h]], v_ref[i, :, sls[h]], o_ref[i, :, sls[h]], *gate_args[n],
            c_scr[i, h], n_scr[i, h], m_scr[i, h][:, 0:1], g_ref[:, sls[h]], L) for n, (i, h) in enumerate(ids)])
        for (i, h), (out, c_new, n_new, m_end) in zip(ids, heads):
            h_out[i, :, sls[h]] = out
            c_scr[i, h] = c_new
            n_scr[i, h] = n_new
            m_scr[i, h] = jnp.broadcast_to(m_end, (1, LANES))
        return carry

    lax.fori_loop(0, bb // group, seqs, 0)

    @pl.when(c == nc - 1)
    def _():
        c_out[...] = c_scr[...]
        n_out[...] = n_scr[...]
        m_out[...] = m_scr[...]


def mlstm_mixer(z, gates, C0, n0, m0, b_gates, norm_g):
    B, S, _ = z.shape
    H = MLSTM_HEADS
    W = MLSTM_WIDTH
    L = MLSTM_CHUNK if S % MLSTM_CHUNK == 0 else S
    nc = S // L
    bb, group = (1, 1) if nc > 1 else (min(B, 8), min(B, 4))
    assert B % bb == 0 and bb % group == 0
    zspec = lambda blk: pl.BlockSpec((bb, L, W), lambda b, c: (b, c, blk))
    smem = pl.BlockSpec(memory_space=pltpu.SMEM)
    hd_spec = pl.BlockSpec((bb, H, 1, MLSTM_HD), lambda b, c: (b, 0, 0, 0))
    c_spec = pl.BlockSpec((bb, H, MLSTM_HD, MLSTM_HD), lambda b, c: (b, 0, 0, 0))
    hm, C1, n1, m1 = pl.pallas_call(
        functools.partial(_mlstm_kernel, L=L, nc=nc, bb=bb, group=group),
        out_shape=(
            jax.ShapeDtypeStruct((B, S, W), F32),
            jax.ShapeDtypeStruct((B, H, MLSTM_HD, MLSTM_HD), F32),
            jax.ShapeDtypeStruct((B, H, 1, MLSTM_HD), F32),
            jax.ShapeDtypeStruct((B, H, 1, LANES), F32),
        ),
        grid=(B // bb, nc),
        in_specs=[
            smem, pl.BlockSpec((1, 2 * H), lambda b, c: (0, 0)),
            zspec(0), zspec(1), zspec(2), zspec(3),
            pl.BlockSpec((bb, L, 2 * H), lambda b, c: (b, c, 0)),
            c_spec, hd_spec,
            pl.BlockSpec((1, W), lambda b, c: (0, 0)),
        ],
        out_specs=(pl.BlockSpec((bb, L, W), lambda b, c: (b, c, 0)), c_spec, hd_spec, hd_spec),
        scratch_shapes=[pltpu.VMEM((bb, H, MLSTM_HD, MLSTM_HD), F32), pltpu.VMEM((bb, H, 1, MLSTM_HD), F32),
                        pltpu.VMEM((bb, H, 1, LANES), F32)],
        compiler_params=_cparams(("parallel", "arbitrary")),
        name="mlstm",
    )(m0.reshape(B * H), b_gates.reshape(1, 2 * H), z, z, z, z, gates,
      C0, n0.reshape(B, H, 1, MLSTM_HD), norm_g.reshape(1, W))
    return hm, C1, n1.reshape(B, H, MLSTM_HD), m1[:, :, 0, 0]


def _expm1(y):
    u = jnp.exp(y)
    small = jnp.where(u == 1.0, y, (u - 1.0) * y / jnp.log(u))
    return jnp.where(jnp.abs(y) > 0.5, u - 1.0, small)


def _rglru_kernel(xr_ref, gr_ref, buf_ref, h0_ref, cw_ref, cb_ref, wa_ref, wx_ref, ba_ref, bx_ref, lam_ref,
                  hr_out, h_out, buf_out, prev_scr, h_scr, a_scr, u_scr, c_scr, *, ns, chained):
    s = pl.program_id(1)
    bb, ts, W = xr_ref.shape
    nb = bb * ts // SUBLANES
    x = xr_ref[...].reshape(nb, SUBLANES, W)

    if chained:
        @pl.when(s == 0)
        def _():
            prev_scr[...] = buf_ref[0]
            h_scr[...] = h0_ref[0]

        prev = jnp.concatenate([prev_scr[...][None], x[:nb - 1]], axis=0)
        prev_scr[...] = x[nb - 1]
    else:
        prev = buf_ref[...]

    t_idx = lax.broadcasted_iota(jnp.int32, x.shape, 1)
    cw = cw_ref[...]
    xc = cb_ref[...] + cw[RG_CONV - 1:RG_CONV, :] * x
    for d in range(1, RG_CONV):
        back = jnp.where(t_idx >= d, pltpu.roll(x, d, 1), pltpu.roll(prev, d, 1))
        xc = xc + cw[RG_CONV - 1 - d:RG_CONV - d, :] * back

    xc2 = xc.reshape(nb * SUBLANES, W)
    xb = xc2.astype(BF)
    r = jax.nn.sigmoid(_dot(xb, wa_ref[...]) + ba_ref[...])
    i = jax.nn.sigmoid(_dot(xb, wx_ref[...]) + bx_ref[...])
    lam = lam_ref[...]
    softplus_neg = jnp.maximum(-lam, 0.0) + jnp.log1p(jnp.exp(-jnp.abs(lam)))
    log_a = (-RG_C * softplus_neg) * r
    a = jnp.exp(log_a).reshape(nb, SUBLANES, W)
    u = (jnp.sqrt(-_expm1(2.0 * log_a)) * i * xc2).reshape(nb, SUBLANES, W)

    d = 1
    while d < SUBLANES:
        inside = t_idx >= d
        u = jnp.where(inside, a * pltpu.roll(u, d, 1) + u, u)
        a = jnp.where(inside, a * pltpu.roll(a, d, 1), a)
        d *= 2

    if chained:
        a_scr[...] = a
        u_scr[...] = u

        def carry_in(k, h):
            c_scr[k] = jnp.broadcast_to(h, (SUBLANES, W))
            return a_scr[k, SUBLANES - 1:SUBLANES, :] * h + u_scr[k, SUBLANES - 1:SUBLANES, :]

        h_last = lax.fori_loop(0, nb, carry_in, h_scr[...], unroll=8)
        h_scr[...] = h_last
        hs = a * c_scr[...] + u
    else:
        hs = a * h0_ref[...] + u
        h_last = hs[:, SUBLANES - 1:SUBLANES, :]
    hr_out[...] = (hs * jax.nn.gelu(gr_ref[...].reshape(nb, SUBLANES, W))).reshape(bb, ts, W)

    tail = SUBLANES - (RG_CONV - 1)
    if chained:
        @pl.when(s == ns - 1)
        def _():
            h_out[0] = h_last
            buf_out[0] = x[nb - 1, tail:, :]
    else:
        h_out[...] = h_last
        buf_out[...] = x[:, tail:, :]


def rglru_mixer(z, h0, buf0, conv_w, conv_b, wa_bd, wx_bd, b_a, b_x, lam):
    B, S, _ = z.shape
    W = RG_WIDTH
    rows = 512
    chained = S > SUBLANES
    if chained:
        bb, ts = 1, min(S, rows)
        assert S % ts == 0 and ts % SUBLANES == 0
    else:
        bb, ts = min(B, rows // SUBLANES), S
        assert S == SUBLANES and B % bb == 0
    ns = S // ts
    nb = bb * ts // SUBLANES
    xr_blk = (4 * MLSTM_WIDTH) // W
    row = lambda a: a.reshape(1, W)
    buf8 = jnp.pad(buf0, ((0, 0), (SUBLANES - (RG_CONV - 1), 0), (0, 0)))
    st = lambda n: pl.BlockSpec((bb, n, W), lambda b, s: (b, 0, 0))
    hr, h1, buf1 = pl.pallas_call(
        functools.partial(_rglru_kernel, ns=ns, chained=chained),
        out_shape=(
            jax.ShapeDtypeStruct((B, S, W), F32),
            jax.ShapeDtypeStruct((B, 1, W), F32),
            jax.ShapeDtypeStruct((B, RG_CONV - 1, W), F32),
        ),
        grid=(B // bb, ns),
        in_specs=[
            pl.BlockSpec((bb, ts, W), lambda b, s: (b, s, xr_blk)),
            pl.BlockSpec((bb, ts, W), lambda b, s: (b, s, xr_blk + 1)),
            st(SUBLANES), st(1),
            _resident((RG_CONV, W)), _resident((1, W)), _resident((W, W)), _resident((W, W)),
            _resident((1, W)), _resident((1, W)), _resident((1, W)),
        ],
        out_specs=(pl.BlockSpec((bb, ts, W), lambda b, s: (b, s, 0)), st(1), st(RG_CONV - 1)),
        scratch_shapes=[pltpu.VMEM((SUBLANES, W), F32), pltpu.VMEM((1, W), F32),
                        pltpu.VMEM((nb, SUBLANES, W), F32), pltpu.VMEM((nb, SUBLANES, W), F32),
                        pltpu.VMEM((nb, SUBLANES, W), F32)],
        compiler_params=_cparams(("parallel", "arbitrary")),
        name="rglru",
    )(z, z, buf8, h0.reshape(B, 1, W), conv_w, row(conv_b), wa_bd, wx_bd, row(b_a), row(b_x), row(lam))
    return hr, h1.reshape(B, W), buf1


def _rwkv_pre_kernel(x_ref, sh_ref, sc_ref, prev0_ref, mu_ref, wr_ref, wk_ref, wv_ref, w0_ref, w1_ref,
                     w2_ref, a0_ref, a1_ref, a2_ref, g1_ref, g2_ref, kk_ref, ka_ref,
                     r_out, lw_out, k_out, v_out, kk_out, a_out, g_out, shift_out, carry_scr, *, ns):
    s = pl.program_id(1)
    bb, ts, _ = x_ref.shape
    tm = bb * ts

    @pl.when(s == 0)
    def _():
        carry_scr[...] = prev0_ref[...]

    h = x_ref[...] * (1.0 + sc_ref[...]) + sh_ref[...]
    t_idx = lax.broadcasted_iota(jnp.int32, h.shape, 1)
    h_prev = jnp.where(t_idx == 0, carry_scr[...], pltpu.roll(h, 1, 1))
    last = h[:, ts - 1:ts, :]
    carry_scr[...] = last
    dx = (h_prev - h).reshape(tm, D_MODEL)
    h2 = h.reshape(tm, D_MODEL)
    mu = mu_ref[...]
    mix = lambda j: (h2 + dx * mu[j:j + 1, :]).astype(BF)
    xr, xw, xk, xv, xa, xg = [mix(j) for j in range(6)]

    r = _dot(xr, wr_ref[...])
    k = _dot(xk, wk_ref[...])
    v = _dot(xv, wv_ref[...])
    wl = w0_ref[...] + _dot(jnp.tanh(_dot(xw, w1_ref[...])).astype(BF), w2_ref[...])
    log_decay = -DECAY_SCALE * jax.nn.sigmoid(wl)
    a = jax.nn.sigmoid(a0_ref[...] + _dot(_dot(xa, a1_ref[...]).astype(BF), a2_ref[...]))
    g = _dot(jax.nn.sigmoid(_dot(xg, g1_ref[...])).astype(BF), g2_ref[...])
    kk = k * kk_ref[...]
    k = k * (1.0 + (a - 1.0) * ka_ref[...])

    shp = x_ref.shape
    r_out[...] = r.reshape(shp)
    lw_out[...] = log_decay.reshape(shp)
    k_out[...] = k.reshape(shp)
    v_out[...] = v.reshape(shp)
    kk_out[...] = kk.reshape(shp)
    a_out[...] = a.reshape(shp)
    g_out[...] = g.reshape(shp)

    @pl.when(s == ns - 1)
    def _():
        shift_out[...] = last


def rwkv_pre(x, mod, prev0, P):
    B, S, _ = x.shape
    bb, ts = _row_tile(B, S, 512)
    ns = S // ts
    xmap = lambda i, s: (i, s, 0)
    full = _resident
    tok = pl.BlockSpec((bb, ts, D_MODEL), xmap)
    st = pl.BlockSpec((bb, 1, D_MODEL), lambda i, s: (i, 0, 0))
    sq, lo_in, lo_out, vec = (D_MODEL, D_MODEL), (D_MODEL, LORA_PAD), (LORA_PAD, D_MODEL), (1, D_MODEL)
    outs = pl.pallas_call(
        functools.partial(_rwkv_pre_kernel, ns=ns),
        out_shape=tuple([jax.ShapeDtypeStruct(x.shape, F32)] * 7
                        + [jax.ShapeDtypeStruct((B, 1, D_MODEL), F32)]),
        grid=(B // bb, ns),
        in_specs=[
            tok, _mod_spec(bb, 3), _mod_spec(bb, 4), st, full((6, D_MODEL)),
            full(sq), full(sq), full(sq), full(vec), full(lo_in), full(lo_out),
            full(vec), full(lo_in), full(lo_out), full(lo_in), full(lo_out),
            full(vec), full(vec),
        ],
        out_specs=tuple([tok] * 7 + [st]),
        scratch_shapes=[pltpu.VMEM((bb, 1, D_MODEL), F32)],
        compiler_params=_cparams(("parallel", "arbitrary")),
        name="rwkv_pre",
    )(x, mod, mod, prev0.reshape(B, 1, D_MODEL), P['mu'], P['wr'], P['wk'], P['wv'], P['w0'], P['w1'],
      P['w2'], P['a0'], P['a1'], P['a2'], P['g1'], P['g2'], P['k_k'], P['k_a'])
    return outs


def _stack_heads(x, first):
    return jnp.concatenate([jnp.where(first, x, 0.0), jnp.where(first, 0.0, x)], axis=0)


def _seg_sum(x, first):
    s0 = jnp.sum(jnp.where(first, x, 0.0), axis=1, keepdims=True)
    s1 = jnp.sum(jnp.where(first, 0.0, x), axis=1, keepdims=True)
    return jnp.where(first, s0, s1)


def _wkv_pair_chunks(toks, sps, vecs, consts, L):
    tri, first, strict, incl = consts
    L2 = 2 * L
    rng = range(len(toks))
    wide = L2 % LANES == 0
    splits = [_split3(t[1]) for t in toks]
    c_incl = [_dot(tri, s[0]) + _dot(tri, s[1]) + _dot(tri, s[2]) for s in splits]
    e_inv = [jnp.exp(-c) for c in c_incl]
    kn = [t[4] / jnp.maximum(jnp.sqrt(_seg_sum(t[4] * t[4], first)), 1e-12) for t in toks]
    la = [_stack_heads(-kn[p] * jnp.exp(c_incl[p] - toks[p][1]), first).astype(BF) for p in rng]
    lr = [_stack_heads(toks[p][0] * jnp.exp(c_incl[p]), first).astype(BF) for p in rng]
    rb = [_stack_heads(kn[p] * toks[p][5] * e_inv[p], first).astype(BF) for p in rng]
    rk = [_stack_heads(toks[p][2] * e_inv[p], first).astype(BF) for p in rng]
    vs = [_stack_heads(toks[p][3], first).astype(BF) for p in rng]
    rbk = [jnp.concatenate([rb[p], rk[p]], axis=0) for p in rng]

    if wide:
        nn = [_dot_nt(la[p], rbk[p]) for p in rng]
        mm = [_dot_nt(lr[p], rbk[p]) for p in rng]
        n_raw = [x[:, :L2] for x in nn]
        n_ak = [jnp.where(strict, x[:, L2:], 0.0).astype(BF) for x in nn]
        m_bk = [jnp.concatenate([jnp.where(incl, x[:, :L2], 0.0), jnp.where(incl, x[:, L2:], 0.0)],
                                axis=1).astype(BF) for x in mm]
    else:
        n_raw = [_dot_nt(la[p], rb[p]) for p in rng]
        n_ak = [jnp.where(strict, _dot_nt(la[p], rk[p]), 0.0).astype(BF) for p in rng]
        m_rb = [jnp.where(incl, _dot_nt(lr[p], rb[p]), 0.0).astype(BF) for p in rng]
        m_rk = [jnp.where(incl, _dot_nt(lr[p], rk[p]), 0.0).astype(BF) for p in rng]

    row = lax.broadcasted_iota(jnp.int32, (L2, L2), 0)
    col = lax.broadcasted_iota(jnp.int32, (L2, L2), 1)
    base = (row // INV_BASE == col // INV_BASE) & (col < row)
    xb = [jnp.where(base, x, 0.0).astype(BF) for x in n_raw]
    tinv = [jnp.where(row == col, 1.0, 0.0) + x.astype(F32) for x in xb]
    x2 = [_dot(x, x).astype(BF) for x in xb]
    if wide:
        prod = [_dot(x2[p], jnp.concatenate([x2[p], tinv[p].astype(BF)], axis=1)) for p in rng]
        tinv = [tinv[p] + prod[p][:, L2:] for p in rng]
        x4 = [x[:, :L2].astype(BF) for x in prod]
    else:
        tinv = [tinv[p] + _dot(x2[p], tinv[p].astype(BF)) for p in rng]
        x4 = [_dot(x, x).astype(BF) for x in x2]
    tinv = [tinv[p] + _dot(x4[p], tinv[p].astype(BF)) for p in rng]
    blk = INV_BASE
    while blk < L:
        off = (row // (2 * blk) == col // (2 * blk)) & (row // blk == col // blk + 1)
        tb = [t.astype(BF) for t in tinv]
        pr = [_dot(jnp.where(off, n_raw[p], 0.0).astype(BF), tb[p]).astype(BF) for p in rng]
        tinv = [tinv[p] + _dot(tb[p], pr[p]) for p in rng]
        blk *= 2

    spb = [s.astype(BF) for s in sps]
    rhs = [_dot_nt(la[p], spb[p]) + _dot(n_ak[p], vs[p]) for p in rng]
    ub = [_dot(tinv[p].astype(BF), rhs[p].astype(BF)).astype(BF) for p in rng]

    uv = [jnp.concatenate([ub[p], vs[p]], axis=0) for p in rng]
    if wide:
        ys = [_dot_nt(lr[p], spb[p]) + _dot(m_bk[p], uv[p]) for p in rng]
    else:
        ys = [_dot_nt(lr[p], spb[p]) + _dot(m_rb[p], ub[p]) + _dot(m_rk[p], vs[p]) for p in rng]
    sp_new = [(sps[p] + _dot_tn(uv[p], rbk[p])) * jnp.exp(c_incl[p][L - 1:L, :]) for p in rng]

    outs = []
    for p in rng:
        r, _, k, v, _, _, gate = toks[p]
        r_k, lnx_g, lnx_b = vecs[p]
        y = ys[p][0:L, :] + ys[p][L:L2, :]
        yc = y - _seg_sum(y, first) * (1.0 / RWKV_HS)
        yv = _seg_sum(yc * yc, first) * (1.0 / RWKV_HS)
        yn = yc * lax.rsqrt(yv + RWKV_LN_EPS) * lnx_g + lnx_b
        outs.append(((yn + _seg_sum(r * k * r_k, first) * v) * gate, sp_new[p]))
    return outs


def _wkv_kernel(r_ref, lw_ref, k_ref, v_ref, kk_ref, a_ref, g_ref, s0_ref, rk_ref, lxg_ref, lxb_ref,
                y_out, s_out, sp_scr, *, L, nc, bb, group):
    c = pl.program_id(1)
    hs = RWKV_HS
    npair = RWKV_HEADS // 2
    L2 = 2 * L

    @pl.when(c == 0)
    def _():
        zero = jnp.zeros((hs, hs), F32)

        def init(i, carry):
            for p in range(npair):
                top = jnp.concatenate([s0_ref[i, 2 * p], zero], axis=1)
                bot = jnp.concatenate([zero, s0_ref[i, 2 * p + 1]], axis=1)
                sp_scr[i, p] = jnp.concatenate([top, bot], axis=0)
            return carry

        lax.fori_loop(0, bb, init, 0)

    row = lax.broadcasted_iota(jnp.int32, (L, L), 0)
    col = lax.broadcasted_iota(jnp.int32, (L, L), 1)
    tri = jnp.where(col <= row, 1.0, 0.0).astype(BF)
    first = lax.broadcasted_iota(jnp.int32, (L, LANES), 1) < hs
    row2 = lax.broadcasted_iota(jnp.int32, (L2, L2), 0)
    col2 = lax.broadcasted_iota(jnp.int32, (L2, L2), 1)
    same = (row2 >= L) == (col2 >= L)
    consts = (tri, first, same & (col2 < row2), same & (col2 <= row2))
    sls = [slice(p * LANES, (p + 1) * LANES) for p in range(npair)]

    def seqs(j, carry):
        ids = [(j * group + g, p) for g in range(group) for p in range(npair)]
        toks = [tuple(ref[i, :, sls[p]] for ref in (r_ref, lw_ref, k_ref, v_ref, kk_ref, a_ref, g_ref))
                for i, p in ids]
        vecs = [(rk_ref[:, sls[p]], lxg_ref[:, sls[p]], lxb_ref[:, sls[p]]) for _, p in ids]
        outs = _wkv_pair_chunks(toks, [sp_scr[i, p] for i, p in ids], vecs, consts, L)
        for (i, p), (y, sp_new) in zip(ids, outs):
            y_out[i, :, sls[p]] = y
            sp_scr[i, p] = sp_new
        return carry

    lax.fori_loop(0, bb // group, seqs, 0)

    @pl.when(c == nc - 1)
    def _():
        def fin(i, carry):
            for p in range(npair):
                sp = sp_scr[i, p]
                s_out[i, 2 * p] = sp[0:hs, 0:hs]
                s_out[i, 2 * p + 1] = sp[hs:2 * hs, hs:2 * hs]
            return carry

        lax.fori_loop(0, bb, fin, 0)


def wkv_recurrence(r, lw, k, v, kk, a, g, s0, P):
    B, S, _ = r.shape
    L = RWKV_CHUNK if S % RWKV_CHUNK == 0 else S
    assert L & (L - 1) == 0 and L % SUBLANES == 0
    nc = S // L
    bb, group = (2, 2) if nc > 1 else (min(B, 8), min(B, 8))
    assert B % bb == 0 and bb % group == 0
    npair = RWKV_HEADS // 2
    tok = pl.BlockSpec((bb, L, D_MODEL), lambda b, c: (b, c, 0))
    st = pl.BlockSpec((bb, RWKV_HEADS, RWKV_HS, RWKV_HS), lambda b, c: (b, 0, 0, 0))
    vec = pl.BlockSpec((1, D_MODEL), lambda b, c: (0, 0))
    return pl.pallas_call(
        functools.partial(_wkv_kernel, L=L, nc=nc, bb=bb, group=group),
        out_shape=(jax.ShapeDtypeStruct(r.shape, F32), jax.ShapeDtypeStruct(s0.shape, F32)),
        grid=(B // bb, nc),
        in_specs=[tok] * 7 + [st, vec, vec, vec],
        out_specs=(tok, st),
        scratch_shapes=[pltpu.VMEM((bb, npair, 2 * RWKV_HS, 2 * RWKV_HS), F32)],
        compiler_params=_cparams(("parallel", "arbitrary")),
        name="wkv",
    )(r, lw, k, v, kk, a, g, s0, P['r_k'], P['lnx_g'], P['lnx_b'])


def _block_diag(w):
    G, n, _ = w.shape
    eye = jnp.eye(G, dtype=w.dtype)
    return (eye[:, None, :, None] * w[:, :, None, :]).reshape(G * n, G * n)


def _pad_cols(w, n):
    return jnp.pad(w, ((0, 0), (0, n - w.shape[1])))


def _pad_rows(w, n):
    return jnp.pad(w, ((0, n - w.shape[0]), (0, 0)))


def _prep_ab(j, p):
    w_in = p['ab_w_in'][j]
    n_main = 4 * MLSTM_WIDTH
    n_gate = 2 * MLSTM_HEADS
    w_all = jnp.concatenate([w_in[:, :n_main], w_in[:, n_main + n_gate:],
                             w_in[:, n_main:n_main + n_gate]], axis=1)
    return dict(
        w_in=_pad_cols(w_all, Z_COLS).astype(BF),
        b_gates=p['mlstm_b_gates'][j], norm_g=p['mlstm_norm_g'][j],
        conv_w=p['rg_conv_w'][j], conv_b=p['rg_conv_b'][j],
        wa=_block_diag(p['rg_w_a'][j]).astype(BF), wx=_block_diag(p['rg_w_x'][j]).astype(BF),
        b_a=p['rg_b_a'][j], b_x=p['rg_b_x'][j], lam=p['rg_lambda'][j],
        w_out_m=p['ab_w_out'][j][:MLSTM_WIDTH].astype(BF), w_out_r=p['ab_w_out'][j][MLSTM_WIDTH:].astype(BF),
    )


def _prep_rwkv(j, p):
    vec = lambda a: a.reshape(1, D_MODEL)
    return dict(
        mu=p['rw_mu'][j],
        wr=p['rw_wr'][j].astype(BF), wk=p['rw_wk'][j].astype(BF), wv=p['rw_wv'][j].astype(BF),
        w0=vec(p['rw_w0'][j]), w1=_pad_cols(p['rw_w1'][j], LORA_PAD).astype(BF),
        w2=_pad_rows(p['rw_w2'][j], LORA_PAD).astype(BF),
        a0=vec(p['rw_a0'][j]), a1=_pad_cols(p['rw_a1'][j], LORA_PAD).astype(BF),
        a2=_pad_rows(p['rw_a2'][j], LORA_PAD).astype(BF),
        g1=_pad_cols(p['rw_g1'][j], LORA_PAD).astype(BF), g2=_pad_rows(p['rw_g2'][j], LORA_PAD).astype(BF),
        k_k=vec(p['rw_k_k'][j]), k_a=vec(p['rw_k_a'][j]), r_k=vec(p['rw_r_k'][j]),
        lnx_g=vec(p['rw_lnx_g'][j]), lnx_b=vec(p['rw_lnx_b'][j]), wo=p['rw_wo'][j].astype(BF),
    )


def ab_mixer(x, mod, st, A):
    mC, mn, mm, rh, rconv = st
    z, gates = mod_matmul(x, mod, A['w_in'], 4 * MLSTM_WIDTH + 2 * RG_WIDTH, 2 * MLSTM_HEADS)
    hm, C1, n1, m1 = mlstm_mixer(z, gates, mC, mn, mm, A['b_gates'], A['norm_g'])
    hr, rh1, buf1 = rglru_mixer(z, rh, rconv, A['conv_w'], A['conv_b'], A['wa'], A['wx'],
                                A['b_a'], A['b_x'], A['lam'])
    return ((hm, 0), (hr, 0), A['w_out_m'], A['w_out_r']), (C1, n1, m1, rh1, buf1)


def rwkv_mixer(x, mod, st, R):
    wkv0, prev0 = st
    r, lw, k, v, kk, a, g, shift = rwkv_pre(x, mod, prev0, R)
    y, wkv1 = wkv_recurrence(r, lw, k, v, kk, a, g, wkv0, R)
    half = D_MODEL // 2
    return ((y, 0), (y, 1), R['wo'][:half], R['wo'][half:]), (wkv1, shift.reshape(shift.shape[0], D_MODEL))


def run_trunk(x, mods, states, W):
    mC, mn, mm, rh, rconv, wkv, shift = states
    new_ab, new_c = [], []
    for layer in range(DEPTH):
        mod = mods[layer]
        lg = lambda i: W['ln_g'][layer, i].reshape(1, D_MODEL)
        lb = lambda i: W['ln_b'][layer, i].reshape(1, D_MODEL)
        f = W['ffn']
        x = ffn_block(x, mod, 0, layer, 0, f[0], f[1], f[2], lg(0), lb(0))
        j = layer // 2
        if layer % 2 == 0:
            mixed, st = ab_mixer(x, mod, (mC[j], mn[j], mm[j], rh[j], rconv[j]), W['ab'][j])
            new_ab.append(st)
        else:
            mixed, st = rwkv_mixer(x, mod, (wkv[j], shift[j]), W['rwkv'][j])
            new_c.append(st)
        x = ffn_block(x, mod, 2, layer, 1, f[0], f[1], f[2], lg(2), lb(2), mixer=mixed + (lg(1), lb(1)))
    stk = lambda sts, i: sts[0][i][None] if len(sts) == 1 else jnp.stack([s[i] for s in sts], axis=0)
    return x, (stk(new_ab, 0), stk(new_ab, 1), stk(new_ab, 2), stk(new_ab, 3), stk(new_ab, 4),
               stk(new_c, 0), stk(new_c, 1))


def _zero_states(B):
    n_ab, n_c = (DEPTH + 1) // 2, DEPTH // 2
    return (jnp.zeros((n_ab, B, MLSTM_HEADS, MLSTM_HD, MLSTM_HD), F32),
            jnp.zeros((n_ab, B, MLSTM_HEADS, MLSTM_HD), F32),
            jnp.zeros((n_ab, B, MLSTM_HEADS), F32),
            jnp.zeros((n_ab, B, RG_WIDTH), F32),
            jnp.zeros((n_ab, B, RG_CONV - 1, RG_WIDTH), F32),
            jnp.zeros((n_c, B, RWKV_HEADS, RWKV_HS, RWKV_HS), F32),
            jnp.zeros((n_c, B, D_MODEL), F32))


def kernel(x_prompt, x_sample, c_prompt, c_sample, state_mlstm_C, state_mlstm_n, state_mlstm_m, state_rglru_h, state_rglru_conv, state_rwkv_wkv, state_rwkv_shift, ada_w, ada_b, ln_g, ln_b, ffn_w1, ffn_w3, ffn_w2, ab_w_in, mlstm_b_gates, mlstm_norm_g, rg_conv_w, rg_conv_b, rg_w_a, rg_b_a, rg_w_x, rg_b_x, rg_lambda, ab_w_out, rw_mu, rw_wr, rw_wk, rw_wv, rw_w0, rw_w1, rw_w2, rw_a0, rw_a1, rw_a2, rw_g1, rw_g2, rw_k_k, rw_k_a, rw_r_k, rw_lnx_g, rw_lnx_b, rw_wo):
    p = dict(ab_w_in=ab_w_in, mlstm_b_gates=mlstm_b_gates, mlstm_norm_g=mlstm_norm_g,
             rg_conv_w=rg_conv_w, rg_conv_b=rg_conv_b, rg_w_a=rg_w_a, rg_b_a=rg_b_a, rg_w_x=rg_w_x,
             rg_b_x=rg_b_x, rg_lambda=rg_lambda, ab_w_out=ab_w_out, rw_mu=rw_mu, rw_wr=rw_wr,
             rw_wk=rw_wk, rw_wv=rw_wv, rw_w0=rw_w0, rw_w1=rw_w1, rw_w2=rw_w2, rw_a0=rw_a0,
             rw_a1=rw_a1, rw_a2=rw_a2, rw_g1=rw_g1, rw_g2=rw_g2, rw_k_k=rw_k_k, rw_k_a=rw_k_a,
             rw_r_k=rw_r_k, rw_lnx_g=rw_lnx_g, rw_lnx_b=rw_lnx_b, rw_wo=rw_wo)
    W = dict(
        ln_g=ln_g, ln_b=ln_b,
        ffn=(ffn_w1.astype(BF), ffn_w3.astype(BF), ffn_w2.astype(BF)),
        ab=[_prep_ab(j, p) for j in range((DEPTH + 1) // 2)],
        rwkv=[_prep_rwkv(j, p) for j in range(DEPTH // 2)],
    )
    Bp, Bs = x_prompt.shape[0], x_sample.shape[0]
    mod_all = adaln(jnp.concatenate([c_prompt, c_sample], axis=0), ada_w, ada_b)
    mods_p = [mod_all[l, :Bp] for l in range(DEPTH)]
    mods_s = [mod_all[l, Bp:] for l in range(DEPTH)]
    y_prompt, sp = run_trunk(x_prompt, mods_p, _zero_states(Bp), W)
    y_sample, ss = run_trunk(x_sample, mods_s,
                             (state_mlstm_C, state_mlstm_n, state_mlstm_m, state_rglru_h,
                              state_rglru_conv, state_rwkv_wkv, state_rwkv_shift), W)
    return (y_prompt, y_sample) + tuple(sp) + tuple(ss)
```

```python
import functools
import math

import jax
import jax.numpy as jnp
from jax import lax
from jax.experimental import pallas as pl
from jax.experimental.pallas import tpu as pltpu

D_MODEL = 1024
DEPTH = 2
N_SUB = 3
MLSTM_WIDTH = 512
MLSTM_HEADS = 4
MLSTM_HD = 128
MLSTM_CHUNK = 128
RG_WIDTH = 512
RG_BLOCKS = 8
RG_BD = 64
RG_CONV = 4
RG_C = 8.0
RWKV_HS = 64
RWKV_HEADS = 16
RWKV_LN_EPS = 64e-5
RWKV_CHUNK = 64
DECAY_SCALE = math.exp(-0.5)
INV_BASE = 8
D_FF = 2816
ALPHA = (2.0 * DEPTH) ** 0.25
LN_EPS = 1e-5
HEAD_NORM_EPS = 1e-6

LANES = 128
SUBLANES = 8
VMEM_LIMIT_BYTES = 56 * 1024 * 1024
Z_COLS = 3200
LORA_PAD = 128

BF = jnp.bfloat16
F32 = jnp.float32

NT_DIMS = (((1,), (1,)), ((), ()))
TN_DIMS = (((0,), (0,)), ((), ()))


def _cparams(sem):
    return pltpu.CompilerParams(dimension_semantics=sem, vmem_limit_bytes=VMEM_LIMIT_BYTES)


def _dot(a, b):
    return jnp.dot(a, b, preferred_element_type=F32)


def _dot_nt(a, b):
    return lax.dot_general(a, b, NT_DIMS, preferred_element_type=F32)


def _dot_tn(a, b):
    return lax.dot_general(a, b, TN_DIMS, preferred_element_type=F32)


def _layer_norm(z, g, b):
    mu = jnp.mean(z, axis=-1, keepdims=True)
    zc = z - mu
    var = jnp.mean(zc * zc, axis=-1, keepdims=True)
    return zc * lax.rsqrt(var + LN_EPS) * g + b


def _row_tile(B, S, rows):
    if S >= rows:
        assert S % rows == 0
        return 1, rows
    bb = min(B, rows // S)
    assert B % bb == 0
    return bb, S


def _split2(x):
    hi = x.astype(BF)
    lo = (x - hi.astype(F32)).astype(BF)
    return hi, lo


def _split3(x):
    hi = x.astype(BF)
    r1 = x - hi.astype(F32)
    mid = r1.astype(BF)
    lo = (r1 - mid.astype(F32)).astype(BF)
    return hi, mid, lo


def _adaln_kernel(c_ref, w_ref, b_ref, o_ref):
    h = jax.nn.silu(c_ref[...]).astype(BF)
    res = _dot(h, w_ref[...].astype(BF)) + b_ref[...]
    o_ref[...] = res.reshape(o_ref.shape)


def adaln(c_all, ada_w, ada_b):
    Bc = c_all.shape[0]
    n_chunk = ada_w.shape[-1] // D_MODEL
    return pl.pallas_call(
        _adaln_kernel,
        out_shape=jax.ShapeDtypeStruct((DEPTH, Bc, n_chunk, 1, D_MODEL), F32),
        grid=(DEPTH, n_chunk),
        in_specs=[
            pl.BlockSpec((Bc, D_MODEL), lambda l, j: (0, 0)),
            pl.BlockSpec((None, D_MODEL, D_MODEL), lambda l, j: (l, 0, j)),
            pl.BlockSpec((None, 1, D_MODEL), lambda l, j: (l, 0, j)),
        ],
        out_specs=pl.BlockSpec((None, Bc, None, 1, D_MODEL), lambda l, j: (l, 0, j, 0, 0)),
        compiler_params=_cparams(("parallel", "parallel")),
        name="adaln",
    )(c_all, ada_w, ada_b.reshape(DEPTH, 1, n_chunk * D_MODEL))


def _mod_spec(bb, k):
    return pl.BlockSpec((bb, None, 1, D_MODEL), lambda i, *_: (i, k, 0, 0))


FFN_ROW_SPLITS = 2


def _ffn_rows(x, mixer, sh, sc, gt, w1_ref, w3_ref, w2_ref, lg, lb, res_w):
    bb, ts, _ = x.shape
    if mixer is not None:
        mgt, m1, m2, mw1_ref, mw2_ref, mlg, mlb = mixer
        flat = lambda m: m.reshape(bb * ts, -1).astype(BF)
        ym = _dot(flat(m1), mw1_ref[...]) + _dot(flat(m2), mw2_ref[...])
        yield
        x = _layer_norm(ALPHA * x + (1.0 + mgt) * ym.reshape(bb, ts, D_MODEL), mlg, mlb)
    h = (x * (1.0 + sc) + sh).reshape(bb * ts, D_MODEL).astype(BF)
    a = _dot(h, w1_ref[...])
    b = _dot(h, w3_ref[...])
    yield
    g = (jax.nn.silu(a) * b).astype(BF)
    y = _dot(g, w2_ref[...])
    yield
    return _layer_norm(ALPHA * x + (res_w * (1.0 + gt)) * y.reshape(bb, ts, D_MODEL), lg, lb)


def _ffn_kernel(*refs, res_w, after_mixer):
    if after_mixer:
        (x_ref, mgt_ref, m1_ref, m2_ref, mw1_ref, mw2_ref, mlg_ref, mlb_ref,
         sh_ref, sc_ref, gt_ref, w1_ref, w3_ref, w2_ref, lg_ref, lb_ref, o_ref) = refs
    else:
        x_ref, sh_ref, sc_ref, gt_ref, w1_ref, w3_ref, w2_ref, lg_ref, lb_ref, o_ref = refs
    bb, ts, _ = x_ref.shape
    if bb > 1:
        n = bb // FFN_ROW_SPLITS
        tok = lambda ref, i: ref[i * n:(i + 1) * n]
        per_seq = tok
    else:
        n = ts // FFN_ROW_SPLITS
        tok = lambda ref, i: ref[:, i * n:(i + 1) * n, :]
        per_seq = lambda ref, i: ref[...]
    gens = []
    for i in range(FFN_ROW_SPLITS):
        mixer = None
        if after_mixer:
            mixer = (per_seq(mgt_ref, i), tok(m1_ref, i), tok(m2_ref, i), mw1_ref, mw2_ref, mlg_ref[...], mlb_ref[...])
        gens.append(_ffn_rows(tok(x_ref, i), mixer, per_seq(sh_ref, i), per_seq(sc_ref, i), per_seq(gt_ref, i),
                              w1_ref, w3_ref, w2_ref, lg_ref[...], lb_ref[...], res_w))
    for i, out in enumerate(_lockstep(gens)):
        if bb > 1:
            o_ref[i * n:(i + 1) * n] = out
        else:
            o_ref[:, i * n:(i + 1) * n, :] = out


def _resident(shape):
    return pl.BlockSpec(shape, lambda *_: (0,) * len(shape), pipeline_mode=pl.Buffered(1))


def ffn_block(x, mod, sub, layer, half, w1, w3, w2, lg, lb, mixer=None):
    B, S, _ = x.shape
    bb, ts = _row_tile(B, S, 512)
    xmap = lambda i, s: (i, s, 0)
    wspec = lambda r, c: pl.BlockSpec((None, None, r, c), lambda i, s: (layer, half, 0, 0),
                                      pipeline_mode=pl.Buffered(1))
    vec = _resident((1, D_MODEL))
    specs = [pl.BlockSpec((bb, ts, D_MODEL), xmap)]
    args = [x]
    if mixer is not None:
        (m1, blk1), (m2, blk2), mw1, mw2, mlg, mlb = mixer
        mspec = lambda w, blk: pl.BlockSpec((bb, ts, w.shape[0]), lambda i, s: (i, s, blk))
        specs += [_mod_spec(bb, 5), mspec(mw1, blk1), mspec(mw2, blk2),
                  _resident(mw1.shape), _resident(mw2.shape), vec, vec]
        args += [mod, m1, m2, mw1, mw2, mlg, mlb]
    specs += [_mod_spec(bb, 3 * sub), _mod_spec(bb, 3 * sub + 1), _mod_spec(bb, 3 * sub + 2),
              wspec(D_MODEL, D_FF), wspec(D_MODEL, D_FF), wspec(D_FF, D_MODEL), vec, vec]
    args += [mod, mod, mod, w1, w3, w2, lg, lb]
    return pl.pallas_call(
        functools.partial(_ffn_kernel, res_w=0.5, after_mixer=mixer is not None),
        out_shape=jax.ShapeDtypeStruct(x.shape, F32),
        grid=(B // bb, S // ts),
        in_specs=specs,
        out_specs=pl.BlockSpec((bb, ts, D_MODEL), xmap),
        compiler_params=_cparams(("parallel", "parallel")),
        name="ffn_block",
    )(*args)


def _modmm_kernel(x_ref, sh_ref, sc_ref, w_ref, o_ref, gate_ref):
    bb, ts, _ = x_ref.shape
    n_main = o_ref.shape[-1]
    n_gate = gate_ref.shape[-1]
    h = (x_ref[...] * (1.0 + sc_ref[...]) + sh_ref[...]).reshape(bb * ts, D_MODEL).astype(BF)
    o_ref[...] = _dot(h, w_ref[:, :n_main]).reshape(o_ref.shape)
    gate_ref[...] = _dot(h, w_ref[:, n_main:])[:, :n_gate].reshape(gate_ref.shape)


def mod_matmul(x, mod, w, n_main, n_gate):
    B, S, _ = x.shape
    bb, ts = _row_tile(B, S, 512)
    xmap = lambda i, s: (i, s, 0)
    return pl.pallas_call(
        _modmm_kernel,
        out_shape=(jax.ShapeDtypeStruct((B, S, n_main), F32), jax.ShapeDtypeStruct((B, S, n_gate), F32)),
        grid=(B // bb, S // ts),
        in_specs=[
            pl.BlockSpec((bb, ts, D_MODEL), xmap),
            _mod_spec(bb, 3), _mod_spec(bb, 4),
            _resident(w.shape),
        ],
        out_specs=(pl.BlockSpec((bb, ts, n_main), xmap), pl.BlockSpec((bb, ts, n_gate), xmap)),
        compiler_params=_cparams(("parallel", "parallel")),
        name="mod_matmul",
    )(x, mod, mod, w)


def _log_sigmoid(x):
    return jnp.minimum(x, 0.0) - jnp.log1p(jnp.exp(-jnp.abs(x)))


def _lockstep(gens):
    results = [None] * len(gens)
    live = list(range(len(gens)))
    while live:
        still = []
        for idx in live:
            try:
                next(gens[idx])
                still.append(idx)
            except StopIteration as stop:
                results[idx] = stop.value
        live = still
    return results


def _dot_exact(a, b, dims):
    return lax.dot_general(a, b, dims, precision=lax.Precision.HIGHEST, preferred_element_type=F32)


def _mlstm_gate_views(gl, L):
    row = lax.broadcasted_iota(jnp.int32, (L, L), 0)
    col = lax.broadcasted_iota(jnp.int32, (L, L), 1)
    eye = jnp.where(row == col, 1.0, 0.0)
    lower = jnp.where(col <= row, 1.0, 0.0)
    upper = jnp.where(row <= col, 1.0, 0.0)
    gl_t = _dot_exact(gl, eye, TN_DIMS)
    cum_c = _dot_exact(lower, gl, (((1,), (0,)), ((), ())))
    cum_r = _dot_exact(gl, upper, TN_DIMS)
    return gl_t, cum_c, cum_r


def _mlstm_head_chunk(q, k, v, o, ig_c, b_c, ig_r, b_r, C0, n0, m0, g, L):
    d = MLSTM_HD
    spread = lambda colv, n: jnp.broadcast_to(colv, (L, n))

    def both(colv):
        over_l = spread(colv, L)
        return over_l, (over_l if L == d else spread(colv, d))

    ones_l = jnp.ones((L, d), BF)
    ones_d = jnp.ones((d, d), BF)

    on_mxu = L >= d

    def row_sum(x, ones):
        if not on_mxu:
            return spread(jnp.sum(x, axis=1, keepdims=True), d), x.astype(BF)
        hi, lo = _split2(x)
        return _dot(hi, ones) + _dot(lo, ones), hi

    row = lax.broadcasted_iota(jnp.int32, (L, L), 0)
    col = lax.broadcasted_iota(jnp.int32, (L, L), 1)
    causal = col <= row
    bc_ll = spread(b_c, L)
    dmat = jnp.where(causal, bc_ll - b_r + ig_r, -jnp.inf)
    m_inter = b_c + m0
    m = jnp.maximum(m_inter, jnp.max(dmat, axis=1, keepdims=True))
    m_ll, m_ld = both(m)
    p = jnp.exp(dmat - m_ll)
    w_inter = jnp.exp(spread(m_inter, d) - m_ld)

    k = k * (d ** -0.5)
    qb = q.astype(BF)
    kb = k.astype(BF)
    if on_mxu:
        q_hi, q_lo = _split2(q)
        n_hi, n_lo = _split2(jnp.broadcast_to(n0, (d, d)))
    yield
    s_raw = _dot_nt(qb, kb)
    q_c0 = _dot_nt(qb, C0.astype(BF))
    if on_mxu:
        q_n0 = _dot_nt(q_hi, n_hi) + _dot_nt(q_lo, n_hi) + _dot_nt(q_hi, n_lo)
    else:
        q_n0 = spread(jnp.sum(q * n0, axis=1, keepdims=True), d)
    yield
    scores = s_raw * p
    score_sum, scores_b = row_sum(scores, ones_l)
    num = _dot(scores_b, v.astype(BF)) + w_inter * q_c0
    den = score_sum + w_inter * q_n0
    hh = num / jnp.maximum(jnp.abs(den), jnp.exp(-m_ld))

    m_end = m[L - 1:L, :]
    b_end = b_c[L - 1:L, :]
    w_state = jnp.exp(b_end + m0 - m_end)
    w_rows = spread(jnp.exp(b_end - b_c + ig_c - m_end), d)
    yield
    c_new = w_state * C0 + _dot_tn((w_rows * v).astype(BF), kb)
    n_new = w_state * n0 + jnp.sum(w_rows * k, axis=0, keepdims=True)
    mu = row_sum(hh, ones_d)[0] * (1.0 / d)
    yield
    hc = hh - mu
    var = row_sum(hc * hc, ones_d)[0] * (1.0 / d)
    hn = hc * lax.rsqrt(var + HEAD_NORM_EPS)
    return hn * g * jax.nn.sigmoid(o), c_new, n_new, m_end


def _mlstm_kernel(m0_ref, bg_ref, q_ref, k_ref, v_ref, o_ref, gc_ref, c0_ref, n0_ref, g_ref,
                  h_out, c_out, n_out, m_out, c_scr, n_scr, m_scr, *, L, nc, bb, group):
    ib = pl.program_id(0)
    c = pl.program_id(1)
    H = MLSTM_HEADS

    @pl.when(c == 0)
    def _():
        c_scr[...] = c0_ref[...]
        n_scr[...] = n0_ref[...]

        def init(i, carry):
            for h in range(H):
                m_scr[i, h] = jnp.full((1, LANES), m0_ref[(ib * bb + i) * H + h], F32)
            return carry

        lax.fori_loop(0, bb, init, 0)

    sls = [slice(h * MLSTM_HD, (h + 1) * MLSTM_HD) for h in range(H)]

    def seqs(j, carry):
        views = []
        for g in range(group):
            gates = gc_ref[j * group + g] + bg_ref[...]
            is_input = lax.broadcasted_iota(jnp.int32, gates.shape, 1) < H
            gl = jnp.where(is_input, gates, _log_sigmoid(gates))
            views.append((gl,) + _mlstm_gate_views(gl, L))
        ids = [(j * group + g, h) for g in range(group) for h in range(H)]
        gate_args = [(views[g][0][:, h:h + 1], views[g][2][:, H + h:H + h + 1],
                      views[g][1][h:h + 1, :], views[g][3][H + h:H + h + 1, :])
                     for g in range(group) for h in range(H)]
        heads = _lockstep([_mlstm_head_chunk(
            q_ref[i, :, sls[h]], k_ref[i, :, sls[h]], v_ref[i, :, sls[h]], o_ref[i, :, sls[h]], *gate_args[n],
            c_scr[i, h], n_scr[i, h], m_scr[i, h][:, 0:1], g_ref[:, sls[h]], L) for n, (i, h) in enumerate(ids)])
        for (i, h), (out, c_new, n_new, m_end) in zip(ids, heads):
            h_out[i, :, sls[h]] = out
            c_scr[i, h] = c_new
            n_scr[i, h] = n_new
            m_scr[i, h] = jnp.broadcast_to(m_end, (1, LANES))
        return carry

    lax.fori_loop(0, bb // group, seqs, 0)

    @pl.when(c == nc - 1)
    def _():
        c_out[...] = c_scr[...]
        n_out[...] = n_scr[...]
        m_out[...] = m_scr[...]


def mlstm_mixer(z, gates, C0, n0, m0, b_gates, norm_g):
    B, S, _ = z.shape
    H = MLSTM_HEADS
    W = MLSTM_WIDTH
    L = MLSTM_CHUNK if S % MLSTM_CHUNK == 0 else S
    nc = S // L
    bb, group = (1, 1) if nc > 1 else (min(B, 8), min(B, 4))
    assert B % bb == 0 and bb % group == 0
    zspec = lambda blk: pl.BlockSpec((bb, L, W), lambda b, c: (b, c, blk))
    smem = pl.BlockSpec(memory_space=pltpu.SMEM)
    hd_spec = pl.BlockSpec((bb, H, 1, MLSTM_HD), lambda b, c: (b, 0, 0, 0))
    c_spec = pl.BlockSpec((bb, H, MLSTM_HD, MLSTM_HD), lambda b, c: (b, 0, 0, 0))
    hm, C1, n1, m1 = pl.pallas_call(
        functools.partial(_mlstm_kernel, L=L, nc=nc, bb=bb, group=group),
        out_shape=(
            jax.ShapeDtypeStruct((B, S, W), F32),
            jax.ShapeDtypeStruct((B, H, MLSTM_HD, MLSTM_HD), F32),
            jax.ShapeDtypeStruct((B, H, 1, MLSTM_HD), F32),
            jax.ShapeDtypeStruct((B, H, 1, LANES), F32),
        ),
        grid=(B // bb, nc),
        in_specs=[
            smem, pl.BlockSpec((1, 2 * H), lambda b, c: (0, 0)),
            zspec(0), zspec(1), zspec(2), zspec(3),
            pl.BlockSpec((bb, L, 2 * H), lambda b, c: (b, c, 0)),
            c_spec, hd_spec,
            pl.BlockSpec((1, W), lambda b, c: (0, 0)),
        ],
        out_specs=(pl.BlockSpec((bb, L, W), lambda b, c: (b, c, 0)), c_spec, hd_spec, hd_spec),
        scratch_shapes=[pltpu.VMEM((bb, H, MLSTM_HD, MLSTM_HD), F32), pltpu.VMEM((bb, H, 1, MLSTM_HD), F32),
                        pltpu.VMEM((bb, H, 1, LANES), F32)],
        compiler_params=_cparams(("parallel", "arbitrary")),
        name="mlstm",
    )(m0.reshape(B * H), b_gates.reshape(1, 2 * H), z, z, z, z, gates,
      C0, n0.reshape(B, H, 1, MLSTM_HD), norm_g.reshape(1, W))
    return hm, C1, n1.reshape(B, H, MLSTM_HD), m1[:, :, 0, 0]


def _expm1(y):
    u = jnp.exp(y)
    small = jnp.where(u == 1.0, y, (u - 1.0) * y / jnp.log(u))
    return jnp.where(jnp.abs(y) > 0.5, u - 1.0, small)


def _rglru_kernel(xr_ref, gr_ref, buf_ref, h0_ref, cw_ref, cb_ref, wa_ref, wx_ref, ba_ref, bx_ref, lam_ref,
                  hr_out, h_out, buf_out, prev_scr, h_scr, a_scr, u_scr, c_scr, *, ns, chained):
    s = pl.program_id(1)
    bb, ts, W = xr_ref.shape
    nb = bb * ts // SUBLANES
    x = xr_ref[...].reshape(nb, SUBLANES, W)

    if chained:
        @pl.when(s == 0)
        def _():
            prev_scr[...] = buf_ref[0]
            h_scr[...] = h0_ref[0]

        prev = jnp.concatenate([prev_scr[...][None], x[:nb - 1]], axis=0)
        prev_scr[...] = x[nb - 1]
    else:
        prev = buf_ref[...]

    t_idx = lax.broadcasted_iota(jnp.int32, x.shape, 1)
    cw = cw_ref[...]
    xc = cb_ref[...] + cw[RG_CONV - 1:RG_CONV, :] * x
    for d in range(1, RG_CONV):
        back = jnp.where(t_idx >= d, pltpu.roll(x, d, 1), pltpu.roll(prev, d, 1))
        xc = xc + cw[RG_CONV - 1 - d:RG_CONV - d, :] * back

    xc2 = xc.reshape(nb * SUBLANES, W)
    xb = xc2.astype(BF)
    r = jax.nn.sigmoid(_dot(xb, wa_ref[...]) + ba_ref[...])
    i = jax.nn.sigmoid(_dot(xb, wx_ref[...]) + bx_ref[...])
    lam = lam_ref[...]
    softplus_neg = jnp.maximum(-lam, 0.0) + jnp.log1p(jnp.exp(-jnp.abs(lam)))
    log_a = (-RG_C * softplus_neg) * r
    a = jnp.exp(log_a).reshape(nb, SUBLANES, W)
    u = (jnp.sqrt(-_expm1(2.0 * log_a)) * i * xc2).reshape(nb, SUBLANES, W)

    d = 1
    while d < SUBLANES:
        inside = t_idx >= d
        u = jnp.where(inside, a * pltpu.roll(u, d, 1) + u, u)
        a = jnp.where(inside, a * pltpu.roll(a, d, 1), a)
        d *= 2

    if chained:
        a_scr[...] = a
        u_scr[...] = u

        def carry_in(k, h):
            c_scr[k] = jnp.broadcast_to(h, (SUBLANES, W))
            return a_scr[k, SUBLANES - 1:SUBLANES, :] * h + u_scr[k, SUBLANES - 1:SUBLANES, :]

        h_last = lax.fori_loop(0, nb, carry_in, h_scr[...], unroll=8)
        h_scr[...] = h_last
        hs = a * c_scr[...] + u
    else:
        hs = a * h0_ref[...] + u
        h_last = hs[:, SUBLANES - 1:SUBLANES, :]
    hr_out[...] = (hs * jax.nn.gelu(gr_ref[...].reshape(nb, SUBLANES, W))).reshape(bb, ts, W)

    tail = SUBLANES - (RG_CONV - 1)
    if chained:
        @pl.when(s == ns - 1)
        def _():
            h_out[0] = h_last
            buf_out[0] = x[nb - 1, tail:, :]
    else:
        h_out[...] = h_last
        buf_out[...] = x[:, tail:, :]


def rglru_mixer(z, h0, buf0, conv_w, conv_b, wa_bd, wx_bd, b_a, b_x, lam):
    B, S, _ = z.shape
    W = RG_WIDTH
    rows = 512
    chained = S > SUBLANES
    if chained:
        bb, ts = 1, min(S, rows)
        assert S % ts == 0 and ts % SUBLANES == 0
    else:
        bb, ts = min(B, rows // SUBLANES), S
        assert S == SUBLANES and B % bb == 0
    ns = S // ts
    nb = bb * ts // SUBLANES
    xr_blk = (4 * MLSTM_WIDTH) // W
    row = lambda a: a.reshape(1, W)
    buf8 = jnp.pad(buf0, ((0, 0), (SUBLANES - (RG_CONV - 1), 0), (0, 0)))
    st = lambda n: pl.BlockSpec((bb, n, W), lambda b, s: (b, 0, 0))
    hr, h1, buf1 = pl.pallas_call(
        functools.partial(_rglru_kernel, ns=ns, chained=chained),
        out_shape=(
            jax.ShapeDtypeStruct((B, S, W), F32),
            jax.ShapeDtypeStruct((B, 1, W), F32),
            jax.ShapeDtypeStruct((B, RG_CONV - 1, W), F32),
        ),
        grid=(B // bb, ns),
        in_specs=[
            pl.BlockSpec((bb, ts, W), lambda b, s: (b, s, xr_blk)),
            pl.BlockSpec((bb, ts, W), lambda b, s: (b, s, xr_blk + 1)),
            st(SUBLANES), st(1),
            _resident((RG_CONV, W)), _resident((1, W)), _resident((W, W)), _resident((W, W)),
            _resident((1, W)), _resident((1, W)), _resident((1, W)),
        ],
        out_specs=(pl.BlockSpec((bb, ts, W), lambda b, s: (b, s, 0)), st(1), st(RG_CONV - 1)),
        scratch_shapes=[pltpu.VMEM((SUBLANES, W), F32), pltpu.VMEM((1, W), F32),
                        pltpu.VMEM((nb, SUBLANES, W), F32), pltpu.VMEM((nb, SUBLANES, W), F32),
                        pltpu.VMEM((nb, SUBLANES, W), F32)],
        compiler_params=_cparams(("parallel", "arbitrary")),
        name="rglru",
    )(z, z, buf8, h0.reshape(B, 1, W), conv_w, row(conv_b), wa_bd, wx_bd, row(b_a), row(b_x), row(lam))
    return hr, h1.reshape(B, W), buf1


def _rwkv_pre_kernel(x_ref, sh_ref, sc_ref, prev0_ref, mu_ref, wr_ref, wk_ref, wv_ref, w0_ref, w1_ref,
                     w2_ref, a0_ref, a1_ref, a2_ref, g1_ref, g2_ref, kk_ref, ka_ref,
                     r_out, lw_out, k_out, v_out, kk_out, a_out, g_out, shift_out, carry_scr, *, ns):
    s = pl.program_id(1)
    bb, ts, _ = x_ref.shape
    tm = bb * ts

    @pl.when(s == 0)
    def _():
        carry_scr[...] = prev0_ref[...]

    h = x_ref[...] * (1.0 + sc_ref[...]) + sh_ref[...]
    t_idx = lax.broadcasted_iota(jnp.int32, h.shape, 1)
    h_prev = jnp.where(t_idx == 0, carry_scr[...], pltpu.roll(h, 1, 1))
    last = h[:, ts - 1:ts, :]
    carry_scr[...] = last
    dx = (h_prev - h).reshape(tm, D_MODEL)
    h2 = h.reshape(tm, D_MODEL)
    mu = mu_ref[...]
    mix = lambda j: (h2 + dx * mu[j:j + 1, :]).astype(BF)
    xr, xw, xk, xv, xa, xg = [mix(j) for j in range(6)]

    r = _dot(xr, wr_ref[...])
    k = _dot(xk, wk_ref[...])
    v = _dot(xv, wv_ref[...])
    wl = w0_ref[...] + _dot(jnp.tanh(_dot(xw, w1_ref[...])).astype(BF), w2_ref[...])
    log_decay = -DECAY_SCALE * jax.nn.sigmoid(wl)
    a = jax.nn.sigmoid(a0_ref[...] + _dot(_dot(xa, a1_ref[...]).astype(BF), a2_ref[...]))
    g = _dot(jax.nn.sigmoid(_dot(xg, g1_ref[...])).astype(BF), g2_ref[...])
    kk = k * kk_ref[...]
    k = k * (1.0 + (a - 1.0) * ka_ref[...])

    shp = x_ref.shape
    r_out[...] = r.reshape(shp)
    lw_out[...] = log_decay.reshape(shp)
    k_out[...] = k.reshape(shp)
    v_out[...] = v.reshape(shp)
    kk_out[...] = kk.reshape(shp)
    a_out[...] = a.reshape(shp)
    g_out[...] = g.reshape(shp)

    @pl.when(s == ns - 1)
    def _():
        shift_out[...] = last


def rwkv_pre(x, mod, prev0, P):
    B, S, _ = x.shape
    bb, ts = _row_tile(B, S, 512)
    ns = S // ts
    xmap = lambda i, s: (i, s, 0)
    full = _resident
    tok = pl.BlockSpec((bb, ts, D_MODEL), xmap)
    st = pl.BlockSpec((bb, 1, D_MODEL), lambda i, s: (i, 0, 0))
    sq, lo_in, lo_out, vec = (D_MODEL, D_MODEL), (D_MODEL, LORA_PAD), (LORA_PAD, D_MODEL), (1, D_MODEL)
    outs = pl.pallas_call(
        functools.partial(_rwkv_pre_kernel, ns=ns),
        out_shape=tuple([jax.ShapeDtypeStruct(x.shape, F32)] * 7
                        + [jax.ShapeDtypeStruct((B, 1, D_MODEL), F32)]),
        grid=(B // bb, ns),
        in_specs=[
            tok, _mod_spec(bb, 3), _mod_spec(bb, 4), st, full((6, D_MODEL)),
            full(sq), full(sq), full(sq), full(vec), full(lo_in), full(lo_out),
            full(vec), full(lo_in), full(lo_out), full(lo_in), full(lo_out),
            full(vec), full(vec),
        ],
        out_specs=tuple([tok] * 7 + [st]),
        scratch_shapes=[pltpu.VMEM((bb, 1, D_MODEL), F32)],
        compiler_params=_cparams(("parallel", "arbitrary")),
        name="rwkv_pre",
    )(x, mod, mod, prev0.reshape(B, 1, D_MODEL), P['mu'], P['wr'], P['wk'], P['wv'], P['w0'], P['w1'],
      P['w2'], P['a0'], P['a1'], P['a2'], P['g1'], P['g2'], P['k_k'], P['k_a'])
    return outs


def _stack_heads(x, first):
    return jnp.concatenate([jnp.where(first, x, 0.0), jnp.where(first, 0.0, x)], axis=0)


def _seg_sum(x, first):
    s0 = jnp.sum(jnp.where(first, x, 0.0), axis=1, keepdims=True)
    s1 = jnp.sum(jnp.where(first, 0.0, x), axis=1, keepdims=True)
    return jnp.where(first, s0, s1)


def _wkv_pair_chunks(toks, sps, vecs, consts, L):
    tri, first, strict, incl = consts
    L2 = 2 * L
    rng = range(len(toks))
    wide = L2 % LANES == 0
    splits = [_split3(t[1]) for t in toks]
    c_incl = [_dot(tri, s[0]) + _dot(tri, s[1]) + _dot(tri, s[2]) for s in splits]
    e_inv = [jnp.exp(-c) for c in c_incl]
    kn = [t[4] / jnp.maximum(jnp.sqrt(_seg_sum(t[4] * t[4], first)), 1e-12) for t in toks]
    la = [_stack_heads(-kn[p] * jnp.exp(c_incl[p] - toks[p][1]), first).astype(BF) for p in rng]
    lr = [_stack_heads(toks[p][0] * jnp.exp(c_incl[p]), first).astype(BF) for p in rng]
    rb = [_stack_heads(kn[p] * toks[p][5] * e_inv[p], first).astype(BF) for p in rng]
    rk = [_stack_heads(toks[p][2] * e_inv[p], first).astype(BF) for p in rng]
    vs = [_stack_heads(toks[p][3], first).astype(BF) for p in rng]
    rbk = [jnp.concatenate([rb[p], rk[p]], axis=0) for p in rng]

    if wide:
        nn = [_dot_nt(la[p], rbk[p]) for p in rng]
        mm = [_dot_nt(lr[p], rbk[p]) for p in rng]
        n_raw = [x[:, :L2] for x in nn]
        n_ak = [jnp.where(strict, x[:, L2:], 0.0).astype(BF) for x in nn]
        m_bk = [jnp.concatenate([jnp.where(incl, x[:, :L2], 0.0), jnp.where(incl, x[:, L2:], 0.0)],
                                axis=1).astype(BF) for x in mm]
    else:
        n_raw = [_dot_nt(la[p], rb[p]) for p in rng]
        n_ak = [jnp.where(strict, _dot_nt(la[p], rk[p]), 0.0).astype(BF) for p in rng]
        m_rb = [jnp.where(incl, _dot_nt(lr[p], rb[p]), 0.0).astype(BF) for p in rng]
        m_rk = [jnp.where(incl, _dot_nt(lr[p], rk[p]), 0.0).astype(BF) for p in rng]

    row = lax.broadcasted_iota(jnp.int32, (L2, L2), 0)
    col = lax.broadcasted_iota(jnp.int32, (L2, L2), 1)
    base = (row // INV_BASE == col // INV_BASE) & (col < row)
    xb = [jnp.where(base, x, 0.0).astype(BF) for x in n_raw]
    tinv = [jnp.where(row == col, 1.0, 0.0) + x.astype(F32) for x in xb]
    x2 = [_dot(x, x).astype(BF) for x in xb]
    if wide:
        prod = [_dot(x2[p], jnp.concatenate([x2[p], tinv[p].astype(BF)], axis=1)) for p in rng]
        tinv = [tinv[p] + prod[p][:, L2:] for p in rng]
        x4 = [x[:, :L2].astype(BF) for x in prod]
    else:
        tinv = [tinv[p] + _dot(x2[p], tinv[p].astype(BF)) for p in rng]
        x4 = [_dot(x, x).astype(BF) for x in x2]
    tinv = [tinv[p] + _dot(x4[p], tinv[p].astype(BF)) for p in rng]
    blk = INV_BASE
    while blk < L:
        off = (row // (2 * blk) == col // (2 * blk)) & (row // blk == col // blk + 1)
        tb = [t.astype(BF) for t in tinv]
        pr = [_dot(jnp.where(off, n_raw[p], 0.0).astype(BF), tb[p]).astype(BF) for p in rng]
        tinv = [tinv[p] + _dot(tb[p], pr[p]) for p in rng]
        blk *= 2

    spb = [s.astype(BF) for s in sps]
    rhs = [_dot_nt(la[p], spb[p]) + _dot(n_ak[p], vs[p]) for p in rng]
    ub = [_dot(tinv[p].astype(BF), rhs[p].astype(BF)).astype(BF) for p in rng]

    uv = [jnp.concatenate([ub[p], vs[p]], axis=0) for p in rng]
    if wide:
        ys = [_dot_nt(lr[p], spb[p]) + _dot(m_bk[p], uv[p]) for p in rng]
    else:
        ys = [_dot_nt(lr[p], spb[p]) + _dot(m_rb[p], ub[p]) + _dot(m_rk[p], vs[p]) for p in rng]
    sp_new = [(sps[p] + _dot_tn(uv[p], rbk[p])) * jnp.exp(c_incl[p][L - 1:L, :]) for p in rng]

    outs = []
    for p in rng:
        r, _, k, v, _, _, gate = toks[p]
        r_k, lnx_g, lnx_b = vecs[p]
        y = ys[p][0:L, :] + ys[p][L:L2, :]
        yc = y - _seg_sum(y, first) * (1.0 / RWKV_HS)
        yv = _seg_sum(yc * yc, first) * (1.0 / RWKV_HS)
        yn = yc * lax.rsqrt(yv + RWKV_LN_EPS) * lnx_g + lnx_b
        outs.append(((yn + _seg_sum(r * k * r_k, first) * v) * gate, sp_new[p]))
    return outs


def _wkv_kernel(r_ref, lw_ref, k_ref, v_ref, kk_ref, a_ref, g_ref, s0_ref, rk_ref, lxg_ref, lxb_ref,
                y_out, s_out, sp_scr, *, L, nc, bb, group):
    c = pl.program_id(1)
    hs = RWKV_HS
    npair = RWKV_HEADS // 2
    L2 = 2 * L

    @pl.when(c == 0)
    def _():
        zero = jnp.zeros((hs, hs), F32)

        def init(i, carry):
            for p in range(npair):
                top = jnp.concatenate([s0_ref[i, 2 * p], zero], axis=1)
                bot = jnp.concatenate([zero, s0_ref[i, 2 * p + 1]], axis=1)
                sp_scr[i, p] = jnp.concatenate([top, bot], axis=0)
            return carry

        lax.fori_loop(0, bb, init, 0)

    row = lax.broadcasted_iota(jnp.int32, (L, L), 0)
    col = lax.broadcasted_iota(jnp.int32, (L, L), 1)
    tri = jnp.where(col <= row, 1.0, 0.0).astype(BF)
    first = lax.broadcasted_iota(jnp.int32, (L, LANES), 1) < hs
    row2 = lax.broadcasted_iota(jnp.int32, (L2, L2), 0)
    col2 = lax.broadcasted_iota(jnp.int32, (L2, L2), 1)
    same = (row2 >= L) == (col2 >= L)
    consts = (tri, first, same & (col2 < row2), same & (col2 <= row2))
    sls = [slice(p * LANES, (p + 1) * LANES) for p in range(npair)]

    def seqs(j, carry):
        ids = [(j * group + g, p) for g in range(group) for p in range(npair)]
        toks = [tuple(ref[i, :, sls[p]] for ref in (r_ref, lw_ref, k_ref, v_ref, kk_ref, a_ref, g_ref))
                for i, p in ids]
        vecs = [(rk_ref[:, sls[p]], lxg_ref[:, sls[p]], lxb_ref[:, sls[p]]) for _, p in ids]
        outs = _wkv_pair_chunks(toks, [sp_scr[i, p] for i, p in ids], vecs, consts, L)
        for (i, p), (y, sp_new) in zip(ids, outs):
            y_out[i, :, sls[p]] = y
            sp_scr[i, p] = sp_new
        return carry

    lax.fori_loop(0, bb // group, seqs, 0)

    @pl.when(c == nc - 1)
    def _():
        def fin(i, carry):
            for p in range(npair):
                sp = sp_scr[i, p]
                s_out[i, 2 * p] = sp[0:hs, 0:hs]
                s_out[i, 2 * p + 1] = sp[hs:2 * hs, hs:2 * hs]
            return carry

        lax.fori_loop(0, bb, fin, 0)


def wkv_recurrence(r, lw, k, v, kk, a, g, s0, P):
    B, S, _ = r.shape
    L = RWKV_CHUNK if S % RWKV_CHUNK == 0 else S
    assert L & (L - 1) == 0 and L % SUBLANES == 0
    nc = S // L
    bb, group = (2, 2) if nc > 1 else (min(B, 8), min(B, 8))
    assert B % bb == 0 and bb % group == 0
    npair = RWKV_HEADS // 2
    tok = pl.BlockSpec((bb, L, D_MODEL), lambda b, c: (b, c, 0))
    st = pl.BlockSpec((bb, RWKV_HEADS, RWKV_HS, RWKV_HS), lambda b, c: (b, 0, 0, 0))
    vec = pl.BlockSpec((1, D_MODEL), lambda b, c: (0, 0))
    return pl.pallas_call(
        functools.partial(_wkv_kernel, L=L, nc=nc, bb=bb, group=group),
        out_shape=(jax.ShapeDtypeStruct(r.shape, F32), jax.ShapeDtypeStruct(s0.shape, F32)),
        grid=(B // bb, nc),
        in_specs=[tok] * 7 + [st, vec, vec, vec],
        out_specs=(tok, st),
        scratch_shapes=[pltpu.VMEM((bb, npair, 2 * RWKV_HS, 2 * RWKV_HS), F32)],
        compiler_params=_cparams(("parallel", "arbitrary")),
        name="wkv",
    )(r, lw, k, v, kk, a, g, s0, P['r_k'], P['lnx_g'], P['lnx_b'])


def _block_diag(w):
    G, n, _ = w.shape
    eye = jnp.eye(G, dtype=w.dtype)
    return (eye[:, None, :, None] * w[:, :, None, :]).reshape(G * n, G * n)


def _pad_cols(w, n):
    return jnp.pad(w, ((0, 0), (0, n - w.shape[1])))


def _pad_rows(w, n):
    return jnp.pad(w, ((0, n - w.shape[0]), (0, 0)))


def _prep_ab(j, p):
    w_in = p['ab_w_in'][j]
    n_main = 4 * MLSTM_WIDTH
    n_gate = 2 * MLSTM_HEADS
    w_all = jnp.concatenate([w_in[:, :n_main], w_in[:, n_main + n_gate:],
                             w_in[:, n_main:n_main + n_gate]], axis=1)
    return dict(
        w_in=_pad_cols(w_all, Z_COLS).astype(BF),
        b_gates=p['mlstm_b_gates'][j], norm_g=p['mlstm_norm_g'][j],
        conv_w=p['rg_conv_w'][j], conv_b=p['rg_conv_b'][j],
        wa=_block_diag(p['rg_w_a'][j]).astype(BF), wx=_block_diag(p['rg_w_x'][j]).astype(BF),
        b_a=p['rg_b_a'][j], b_x=p['rg_b_x'][j], lam=p['rg_lambda'][j],
        w_out_m=p['ab_w_out'][j][:MLSTM_WIDTH].astype(BF), w_out_r=p['ab_w_out'][j][MLSTM_WIDTH:].astype(BF),
    )


def _prep_rwkv(j, p):
    vec = lambda a: a.reshape(1, D_MODEL)
    return dict(
        mu=p['rw_mu'][j],
        wr=p['rw_wr'][j].astype(BF), wk=p['rw_wk'][j].astype(BF), wv=p['rw_wv'][j].astype(BF),
        w0=vec(p['rw_w0'][j]), w1=_pad_cols(p['rw_w1'][j], LORA_PAD).astype(BF),
        w2=_pad_rows(p['rw_w2'][j], LORA_PAD).astype(BF),
        a0=vec(p['rw_a0'][j]), a1=_pad_cols(p['rw_a1'][j], LORA_PAD).astype(BF),
        a2=_pad_rows(p['rw_a2'][j], LORA_PAD).astype(BF),
        g1=_pad_cols(p['rw_g1'][j], LORA_PAD).astype(BF), g2=_pad_rows(p['rw_g2'][j], LORA_PAD).astype(BF),
        k_k=vec(p['rw_k_k'][j]), k_a=vec(p['rw_k_a'][j]), r_k=vec(p['rw_r_k'][j]),
        lnx_g=vec(p['rw_lnx_g'][j]), lnx_b=vec(p['rw_lnx_b'][j]), wo=p['rw_wo'][j].astype(BF),
    )


def ab_mixer(x, mod, st, A):
    mC, mn, mm, rh, rconv = st
    z, gates = mod_matmul(x, mod, A['w_in'], 4 * MLSTM_WIDTH + 2 * RG_WIDTH, 2 * MLSTM_HEADS)
    hm, C1, n1, m1 = mlstm_mixer(z, gates, mC, mn, mm, A['b_gates'], A['norm_g'])
    hr, rh1, buf1 = rglru_mixer(z, rh, rconv, A['conv_w'], A['conv_b'], A['wa'], A['wx'],
                                A['b_a'], A['b_x'], A['lam'])
    return ((hm, 0), (hr, 0), A['w_out_m'], A['w_out_r']), (C1, n1, m1, rh1, buf1)


def rwkv_mixer(x, mod, st, R):
    wkv0, prev0 = st
    r, lw, k, v, kk, a, g, shift = rwkv_pre(x, mod, prev0, R)
    y, wkv1 = wkv_recurrence(r, lw, k, v, kk, a, g, wkv0, R)
    half = D_MODEL // 2
    return ((y, 0), (y, 1), R['wo'][:half], R['wo'][half:]), (wkv1, shift.reshape(shift.shape[0], D_MODEL))


def run_trunk(x, mods, states, W):
    mC, mn, mm, rh, rconv, wkv, shift = states
    new_ab, new_c = [], []
    for layer in range(DEPTH):
        mod = mods[layer]
        lg = lambda i: W['ln_g'][layer, i].reshape(1, D_MODEL)
        lb = lambda i: W['ln_b'][layer, i].reshape(1, D_MODEL)
        f = W['ffn']
        x = ffn_block(x, mod, 0, layer, 0, f[0], f[1], f[2], lg(0), lb(0))
        j = layer // 2
        if layer % 2 == 0:
            mixed, st = ab_mixer(x, mod, (mC[j], mn[j], mm[j], rh[j], rconv[j]), W['ab'][j])
            new_ab.append(st)
        else:
            mixed, st = rwkv_mixer(x, mod, (wkv[j], shift[j]), W['rwkv'][j])
            new_c.append(st)
        x = ffn_block(x, mod, 2, layer, 1, f[0], f[1], f[2], lg(2), lb(2), mixer=mixed + (lg(1), lb(1)))
    stk = lambda sts, i: sts[0][i][None] if len(sts) == 1 else jnp.stack([s[i] for s in sts], axis=0)
    return x, (stk(new_ab, 0), stk(new_ab, 1), stk(new_ab, 2), stk(new_ab, 3), stk(new_ab, 4),
               stk(new_c, 0), stk(new_c, 1))


def _zero_states(B):
    n_ab, n_c = (DEPTH + 1) // 2, DEPTH // 2
    return (jnp.zeros((n_ab, B, MLSTM_HEADS, MLSTM_HD, MLSTM_HD), F32),
            jnp.zeros((n_ab, B, MLSTM_HEADS, MLSTM_HD), F32),
            jnp.zeros((n_ab, B, MLSTM_HEADS), F32),
            jnp.zeros((n_ab, B, RG_WIDTH), F32),
            jnp.zeros((n_ab, B, RG_CONV - 1, RG_WIDTH), F32),
            jnp.zeros((n_c, B, RWKV_HEADS, RWKV_HS, RWKV_HS), F32),
            jnp.zeros((n_c, B, D_MODEL), F32))


def kernel(x_prompt, x_sample, c_prompt, c_sample, state_mlstm_C, state_mlstm_n, state_mlstm_m, state_rglru_h, state_rglru_conv, state_rwkv_wkv, state_rwkv_shift, ada_w, ada_b, ln_g, ln_b, ffn_w1, ffn_w3, ffn_w2, ab_w_in, mlstm_b_gates, mlstm_norm_g, rg_conv_w, rg_conv_b, rg_w_a, rg_b_a, rg_w_x, rg_b_x, rg_lambda, ab_w_out, rw_mu, rw_wr, rw_wk, rw_wv, rw_w0, rw_w1, rw_w2, rw_a0, rw_a1, rw_a2, rw_g1, rw_g2, rw_k_k, rw_k_a, rw_r_k, rw_lnx_g, rw_lnx_b, rw_wo):
    p = dict(ab_w_in=ab_w_in, mlstm_b_gates=mlstm_b_gates, mlstm_norm_g=mlstm_norm_g,
             rg_conv_w=rg_conv_w, rg_conv_b=rg_conv_b, rg_w_a=rg_w_a, rg_b_a=rg_b_a, rg_w_x=rg_w_x,
             rg_b_x=rg_b_x, rg_lambda=rg_lambda, ab_w_out=ab_w_out, rw_mu=rw_mu, rw_wr=rw_wr,
             rw_wk=rw_wk, rw_wv=rw_wv, rw_w0=rw_w0, rw_w1=rw_w1, rw_w2=rw_w2, rw_a0=rw_a0,
             rw_a1=rw_a1, rw_a2=rw_a2, rw_g1=rw_g1, rw_g2=rw_g2, rw_k_k=rw_k_k, rw_k_a=rw_k_a,
             rw_r_k=rw_r_k, rw_lnx_g=rw_lnx_g, rw_lnx_b=rw_lnx_b, rw_wo=rw_wo)
    W = dict(
        ln_g=ln_g, ln_b=ln_b,
        ffn=(ffn_w1.astype(BF), ffn_w3.astype(BF), ffn_w2.astype(BF)),
        ab=[_prep_ab(j, p) for j in range((DEPTH + 1) // 2)],
        rwkv=[_prep_rwkv(j, p) for j in range(DEPTH // 2)],
    )
    Bp, Bs = x_prompt.shape[0], x_sample.shape[0]
    mod_all = adaln(jnp.concatenate([c_prompt, c_sample], axis=0), ada_w, ada_b)
    mods_p = [mod_all[l, :Bp] for l in range(DEPTH)]
    mods_s = [mod_all[l, Bp:] for l in range(DEPTH)]
    y_prompt, sp = run_trunk(x_prompt, mods_p, _zero_states(Bp), W)
    y_sample, ss = run_trunk(x_sample, mods_s,
                             (state_mlstm_C, state_mlstm_n, state_mlstm_m, state_rglru_h,
                              state_rglru_conv, state_rwkv_wkv, state_rwkv_shift), W)
    return (y_prompt, y_sample) + tuple(sp) + tuple(ss)
```

```python
import functools
import math

import jax
import jax.numpy as jnp
from jax import lax
from jax.experimental import pallas as pl
from jax.experimental.pallas import tpu as pltpu

D_MODEL = 1024
DEPTH = 2
N_SUB = 3
MLSTM_WIDTH = 512
MLSTM_HEADS = 4
MLSTM_HD = 128
MLSTM_CHUNK = 128
RG_WIDTH = 512
RG_BLOCKS = 8
RG_BD = 64
RG_CONV = 4
RG_C = 8.0
RWKV_HS = 64
RWKV_HEADS = 16
RWKV_LN_EPS = 64e-5
RWKV_CHUNK = 64
DECAY_SCALE = math.exp(-0.5)
INV_BASE = 8
D_FF = 2816
ALPHA = (2.0 * DEPTH) ** 0.25
LN_EPS = 1e-5
HEAD_NORM_EPS = 1e-6

LANES = 128
SUBLANES = 8
VMEM_LIMIT_BYTES = 56 * 1024 * 1024
Z_COLS = 3200
LORA_PAD = 128

BF = jnp.bfloat16
F32 = jnp.float32

NT_DIMS = (((1,), (1,)), ((), ()))
TN_DIMS = (((0,), (0,)), ((), ()))


def _cparams(sem):
    return pltpu.CompilerParams(dimension_semantics=sem, vmem_limit_bytes=VMEM_LIMIT_BYTES)


def _dot(a, b):
    return jnp.dot(a, b, preferred_element_type=F32)


def _dot_nt(a, b):
    return lax.dot_general(a, b, NT_DIMS, preferred_element_type=F32)


def _dot_tn(a, b):
    return lax.dot_general(a, b, TN_DIMS, preferred_element_type=F32)


def _layer_norm(z, g, b):
    mu = jnp.mean(z, axis=-1, keepdims=True)
    zc = z - mu
    var = jnp.mean(zc * zc, axis=-1, keepdims=True)
    return zc * lax.rsqrt(var + LN_EPS) * g + b


def _row_tile(B, S, rows):
    if S >= rows:
        assert S % rows == 0
        return 1, rows
    bb = min(B, rows // S)
    assert B % bb == 0
    return bb, S


def _split2(x):
    hi = x.astype(BF)
    lo = (x - hi.astype(F32)).astype(BF)
    return hi, lo


def _split3(x):
    hi = x.astype(BF)
    r1 = x - hi.astype(F32)
    mid = r1.astype(BF)
    lo = (r1 - mid.astype(F32)).astype(BF)
    return hi, mid, lo


def _adaln_kernel(c_ref, w_ref, b_ref, o_ref):
    h = jax.nn.silu(c_ref[...]).astype(BF)
    res = _dot(h, w_ref[...].astype(BF)) + b_ref[...]
    o_ref[...] = res.reshape(o_ref.shape)


def adaln(c_all, ada_w, ada_b):
    Bc = c_all.shape[0]
    n_chunk = ada_w.shape[-1] // D_MODEL
    return pl.pallas_call(
        _adaln_kernel,
        out_shape=jax.ShapeDtypeStruct((DEPTH, Bc, n_chunk, 1, D_MODEL), F32),
        grid=(DEPTH, n_chunk),
        in_specs=[
            pl.BlockSpec((Bc, D_MODEL), lambda l, j: (0, 0)),
            pl.BlockSpec((None, D_MODEL, D_MODEL), lambda l, j: (l, 0, j)),
            pl.BlockSpec((None, 1, D_MODEL), lambda l, j: (l, 0, j)),
        ],
        out_specs=pl.BlockSpec((None, Bc, None, 1, D_MODEL), lambda l, j: (l, 0, j, 0, 0)),
        compiler_params=_cparams(("parallel", "parallel")),
        name="adaln",
    )(c_all, ada_w, ada_b.reshape(DEPTH, 1, n_chunk * D_MODEL))


def _mod_spec(bb, k):
    return pl.BlockSpec((bb, None, 1, D_MODEL), lambda i, *_: (i, k, 0, 0))


FFN_ROW_SPLITS = 2


def _ffn_rows(x, mixer, sh, sc, gt, w1_ref, w3_ref, w2_ref, lg, lb, res_w):
    bb, ts, _ = x.shape
    if mixer is not None:
        mgt, m1, m2, mw1_ref, mw2_ref, mlg, mlb = mixer
        flat = lambda m: m.reshape(bb * ts, -1).astype(BF)
        ym = _dot(flat(m1), mw1_ref[...]) + _dot(flat(m2), mw2_ref[...])
        yield
        x = _layer_norm(ALPHA * x + (1.0 + mgt) * ym.reshape(bb, ts, D_MODEL), mlg, mlb)
    h = (x * (1.0 + sc) + sh).reshape(bb * ts, D_MODEL).astype(BF)
    a = _dot(h, w1_ref[...])
    b = _dot(h, w3_ref[...])
    yield
    g = (jax.nn.silu(a) * b).astype(BF)
    y = _dot(g, w2_ref[...])
    yield
    return _layer_norm(ALPHA * x + (res_w * (1.0 + gt)) * y.reshape(bb, ts, D_MODEL), lg, lb)


def _ffn_kernel(*refs, res_w, after_mixer):
    if after_mixer:
        (x_ref, mgt_ref, m1_ref, m2_ref, mw1_ref, mw2_ref, mlg_ref, mlb_ref,
         sh_ref, sc_ref, gt_ref, w1_ref, w3_ref, w2_ref, lg_ref, lb_ref, o_ref) = refs
    else:
        x_ref, sh_ref, sc_ref, gt_ref, w1_ref, w3_ref, w2_ref, lg_ref, lb_ref, o_ref = refs
    bb, ts, _ = x_ref.shape
    if bb > 1:
        n = bb // FFN_ROW_SPLITS
        tok = lambda ref, i: ref[i * n:(i + 1) * n]
        per_seq = tok
    else:
        n = ts // FFN_ROW_SPLITS
        tok = lambda ref, i: ref[:, i * n:(i + 1) * n, :]
        per_seq = lambda ref, i: ref[...]
    gens = []
    for i in range(FFN_ROW_SPLITS):
        mixer = None
        if after_mixer:
            mixer = (per_seq(mgt_ref, i), tok(m1_ref, i), tok(m2_ref, i), mw1_ref, mw2_ref, mlg_ref[...], mlb_ref[...])
        gens.append(_ffn_rows(tok(x_ref, i), mixer, per_seq(sh_ref, i), per_seq(sc_ref, i), per_seq(gt_ref, i),
                              w1_ref, w3_ref, w2_ref, lg_ref[...], lb_ref[...], res_w))
    for i, out in enumerate(_lockstep(gens)):
        if bb > 1:
            o_ref[i * n:(i + 1) * n] = out
        else:
            o_ref[:, i * n:(i + 1) * n, :] = out


def _resident(shape):
    return pl.BlockSpec(shape, lambda *_: (0,) * len(shape), pipeline_mode=pl.Buffered(1))


def ffn_block(x, mod, sub, layer, half, w1, w3, w2, lg, lb, mixer=None):
    B, S, _ = x.shape
    bb, ts = _row_tile(B, S, 512)
    xmap = lambda i, s: (i, s, 0)
    wspec = lambda r, c: pl.BlockSpec((None, None, r, c), lambda i, s: (layer, half, 0, 0),
                                      pipeline_mode=pl.Buffered(1))
    vec = _resident((1, D_MODEL))
    specs = [pl.BlockSpec((bb, ts, D_MODEL), xmap)]
    args = [x]
    if mixer is not None:
        (m1, blk1), (m2, blk2), mw1, mw2, mlg, mlb = mixer
        mspec = lambda w, blk: pl.BlockSpec((bb, ts, w.shape[0]), lambda i, s: (i, s, blk))
        specs += [_mod_spec(bb, 5), mspec(mw1, blk1), mspec(mw2, blk2),
                  _resident(mw1.shape), _resident(mw2.shape), vec, vec]
        args += [mod, m1, m2, mw1, mw2, mlg, mlb]
    specs += [_mod_spec(bb, 3 * sub), _mod_spec(bb, 3 * sub + 1), _mod_spec(bb, 3 * sub + 2),
              wspec(D_MODEL, D_FF), wspec(D_MODEL, D_FF), wspec(D_FF, D_MODEL), vec, vec]
    args += [mod, mod, mod, w1, w3, w2, lg, lb]
    return pl.pallas_call(
        functools.partial(_ffn_kernel, res_w=0.5, after_mixer=mixer is not None),
        out_shape=jax.ShapeDtypeStruct(x.shape, F32),
        grid=(B // bb, S // ts),
        in_specs=specs,
        out_specs=pl.BlockSpec((bb, ts, D_MODEL), xmap),
        compiler_params=_cparams(("parallel", "parallel")),
        name="ffn_block",
    )(*args)


def _modmm_kernel(x_ref, sh_ref, sc_ref, w_ref, o_ref, gate_ref):
    bb, ts, _ = x_ref.shape
    n_main = o_ref.shape[-1]
    n_gate = gate_ref.shape[-1]
    h = (x_ref[...] * (1.0 + sc_ref[...]) + sh_ref[...]).reshape(bb * ts, D_MODEL).astype(BF)
    o_ref[...] = _dot(h, w_ref[:, :n_main]).reshape(o_ref.shape)
    gate_ref[...] = _dot(h, w_ref[:, n_main:])[:, :n_gate].reshape(gate_ref.shape)


def mod_matmul(x, mod, w, n_main, n_gate):
    B, S, _ = x.shape
    bb, ts = _row_tile(B, S, 512)
    xmap = lambda i, s: (i, s, 0)
    return pl.pallas_call(
        _modmm_kernel,
        out_shape=(jax.ShapeDtypeStruct((B, S, n_main), F32), jax.ShapeDtypeStruct((B, S, n_gate), F32)),
        grid=(B // bb, S // ts),
        in_specs=[
            pl.BlockSpec((bb, ts, D_MODEL), xmap),
            _mod_spec(bb, 3), _mod_spec(bb, 4),
            _resident(w.shape),
        ],
        out_specs=(pl.BlockSpec((bb, ts, n_main), xmap), pl.BlockSpec((bb, ts, n_gate), xmap)),
        compiler_params=_cparams(("parallel", "parallel")),
        name="mod_matmul",
    )(x, mod, mod, w)


def _log_sigmoid(x):
    return jnp.minimum(x, 0.0) - jnp.log1p(jnp.exp(-jnp.abs(x)))


def _lockstep(gens):
    results = [None] * len(gens)
    live = list(range(len(gens)))
    while live:
        still = []
        for idx in live:
            try:
                next(gens[idx])
                still.append(idx)
            except StopIteration as stop:
                results[idx] = stop.value
        live = still
    return results


def _dot_exact(a, b, dims):
    return lax.dot_general(a, b, dims, precision=lax.Precision.HIGHEST, preferred_element_type=F32)


def _mlstm_gate_views(gl, L):
    row = lax.broadcasted_iota(jnp.int32, (L, L), 0)
    col = lax.broadcasted_iota(jnp.int32, (L, L), 1)
    eye = jnp.where(row == col, 1.0, 0.0)
    lower = jnp.where(col <= row, 1.0, 0.0)
    upper = jnp.where(row <= col, 1.0, 0.0)
    gl_t = _dot_exact(gl, eye, TN_DIMS)
    cum_c = _dot_exact(lower, gl, (((1,), (0,)), ((), ())))
    cum_r = _dot_exact(gl, upper, TN_DIMS)
    return gl_t, cum_c, cum_r


def _mlstm_head_chunk(q, k, v, o, ig_c, b_c, ig_r, b_r, C0, n0, m0, g, L):
    d = MLSTM_HD
    spread = lambda colv, n: jnp.broadcast_to(colv, (L, n))

    def both(colv):
        over_l = spread(colv, L)
        return over_l, (over_l if L == d else spread(colv, d))

    ones_l = jnp.ones((L, d), BF)
    ones_d = jnp.ones((d, d), BF)

    on_mxu = L >= d

    def row_sum(x, ones):
        if not on_mxu:
            return spread(jnp.sum(x, axis=1, keepdims=True), d), x.astype(BF)
        hi, lo = _split2(x)
        return _dot(hi, ones) + _dot(lo, ones), hi

    row = lax.broadcasted_iota(jnp.int32, (L, L), 0)
    col = lax.broadcasted_iota(jnp.int32, (L, L), 1)
    causal = col <= row
    bc_ll = spread(b_c, L)
    dmat = jnp.where(causal, bc_ll - b_r + ig_r, -jnp.inf)
    m_inter = b_c + m0
    m = jnp.maximum(m_inter, jnp.max(dmat, axis=1, keepdims=True))
    m_ll, m_ld = both(m)
    p = jnp.exp(dmat - m_ll)
    w_inter = jnp.exp(spread(m_inter, d) - m_ld)

    k = k * (d ** -0.5)
    qb = q.astype(BF)
    kb = k.astype(BF)
    if on_mxu:
        q_hi, q_lo = _split2(q)
        n_hi, n_lo = _split2(jnp.broadcast_to(n0, (d, d)))
    yield
    s_raw = _dot_nt(qb, kb)
    q_c0 = _dot_nt(qb, C0.astype(BF))
    if on_mxu:
        q_n0 = _dot_nt(q_hi, n_hi) + _dot_nt(q_lo, n_hi) + _dot_nt(q_hi, n_lo)
    else:
        q_n0 = spread(jnp.sum(q * n0, axis=1, keepdims=True), d)
    yield
    scores = s_raw * p
    score_sum, scores_b = row_sum(scores, ones_l)
    num = _dot(scores_b, v.astype(BF)) + w_inter * q_c0
    den = score_sum + w_inter * q_n0
    hh = num / jnp.maximum(jnp.abs(den), jnp.exp(-m_ld))

    m_end = m[L - 1:L, :]
    b_end = b_c[L - 1:L, :]
    w_state = jnp.exp(b_end + m0 - m_end)
    w_rows = spread(jnp.exp(b_end - b_c + ig_c - m_end), d)
    yield
    c_new = w_state * C0 + _dot_tn((w_rows * v).astype(BF), kb)
    n_new = w_state * n0 + jnp.sum(w_rows * k, axis=0, keepdims=True)
    mu = row_sum(hh, ones_d)[0] * (1.0 / d)
    yield
    hc = hh - mu
    var = row_sum(hc * hc, ones_d)[0] * (1.0 / d)
    hn = hc * lax.rsqrt(var + HEAD_NORM_EPS)
    return hn * g * jax.nn.sigmoid(o), c_new, n_new, m_end


def _mlstm_kernel(m0_ref, bg_ref, q_ref, k_ref, v_ref, o_ref, gc_ref, c0_ref, n0_ref, g_ref,
                  h_out, c_out, n_out, m_out, c_scr, n_scr, m_scr, *, L, nc, bb, group):
    ib = pl.program_id(0)
    c = pl.program_id(1)
    H = MLSTM_HEADS

    @pl.when(c == 0)
    def _():
        c_scr[...] = c0_ref[...]
        n_scr[...] = n0_ref[...]

        def init(i, carry):
            for h in range(H):
                m_scr[i, h] = jnp.full((1, LANES), m0_ref[(ib * bb + i) * H + h], F32)
            return carry

        lax.fori_loop(0, bb, init, 0)

    sls = [slice(h * MLSTM_HD, (h + 1) * MLSTM_HD) for h in range(H)]

    def seqs(j, carry):
        views = []
        for g in range(group):
            gates = gc_ref[j * group + g] + bg_ref[...]
            is_input = lax.broadcasted_iota(jnp.int32, gates.shape, 1) < H
            gl = jnp.where(is_input, gates, _log_sigmoid(gates))
            views.append((gl,) + _mlstm_gate_views(gl, L))
        ids = [(j * group + g, h) for g in range(group) for h in range(H)]
        gate_args = [(views[g][0][:, h:h + 1], views[g][2][:, H + h:H + h + 1],
                      views[g][1][h:h + 1, :], views[g][3][H + h:H + h + 1, :])
                     for g in range(group) for h in range(H)]
        heads = _lockstep([_mlstm_head_chunk(
            q_ref[i, :, sls[h]], k_ref[i, :, sls[h]], v_ref[i, :, sls[h]], o_ref[i, :, sls[h]], *gate_args[n],
            c_scr[i, h], n_scr[i, h], m_scr[i, h][:, 0:1], g_ref[:, sls[h]], L) for n, (i, h) in enumerate(ids)])
        for (i, h), (out, c_new, n_new, m_end) in zip(ids, heads):
            h_out[i, :, sls[h]] = out
            c_scr[i, h] = c_new
            n_scr[i, h] = n_new
            m_scr[i, h] = jnp.broadcast_to(m_end, (1, LANES))
        return carry

    lax.fori_loop(0, bb // group, seqs, 0)

    @pl.when(c == nc - 1)
    def _():
        c_out[...] = c_scr[...]
        n_out[...] = n_scr[...]
        m_out[...] = m_scr[...]


def mlstm_mixer(z, gates, C0, n0, m0, b_gates, norm_g):
    B, S, _ = z.shape
    H = MLSTM_HEADS
    W = MLSTM_WIDTH
    L = MLSTM_CHUNK if S % MLSTM_CHUNK == 0 else S
    nc = S // L
    bb, group = (1, 1) if nc > 1 else (min(B, 8), min(B, 4))
    assert B % bb == 0 and bb % group == 0
    zspec = lambda blk: pl.BlockSpec((bb, L, W), lambda b, c: (b, c, blk))
    smem = pl.BlockSpec(memory_space=pltpu.SMEM)
    hd_spec = pl.BlockSpec((bb, H, 1, MLSTM_HD), lambda b, c: (b, 0, 0, 0))
    c_spec = pl.BlockSpec((bb, H, MLSTM_HD, MLSTM_HD), lambda b, c: (b, 0, 0, 0))
    hm, C1, n1, m1 = pl.pallas_call(
        functools.partial(_mlstm_kernel, L=L, nc=nc, bb=bb, group=group),
        out_shape=(
            jax.ShapeDtypeStruct((B, S, W), F32),
            jax.ShapeDtypeStruct((B, H, MLSTM_HD, MLSTM_HD), F32),
            jax.ShapeDtypeStruct((B, H, 1, MLSTM_HD), F32),
            jax.ShapeDtypeStruct((B, H, 1, LANES), F32),
        ),
        grid=(B // bb, nc),
        in_specs=[
            smem, pl.BlockSpec((1, 2 * H), lambda b, c: (0, 0)),
            zspec(0), zspec(1), zspec(2), zspec(3),
            pl.BlockSpec((bb, L, 2 * H), lambda b, c: (b, c, 0)),
            c_spec, hd_spec,
            pl.BlockSpec((1, W), lambda b, c: (0, 0)),
        ],
        out_specs=(pl.BlockSpec((bb, L, W), lambda b, c: (b, c, 0)), c_spec, hd_spec, hd_spec),
        scratch_shapes=[pltpu.VMEM((bb, H, MLSTM_HD, MLSTM_HD), F32), pltpu.VMEM((bb, H, 1, MLSTM_HD), F32),
                        pltpu.VMEM((bb, H, 1, LANES), F32)],
        compiler_params=_cparams(("parallel", "arbitrary")),
        name="mlstm",
    )(m0.reshape(B * H), b_gates.reshape(1, 2 * H), z, z, z, z, gates,
      C0, n0.reshape(B, H, 1, MLSTM_HD), norm_g.reshape(1, W))
    return hm, C1, n1.reshape(B, H, MLSTM_HD), m1[:, :, 0, 0]


def _expm1(y):
    u = jnp.exp(y)
    small = jnp.where(u == 1.0, y, (u - 1.0) * y / jnp.log(u))
    return jnp.where(jnp.abs(y) > 0.5, u - 1.0, small)


def _rglru_kernel(xr_ref, gr_ref, buf_ref, h0_ref, cw_ref, cb_ref, wa_ref, wx_ref, ba_ref, bx_ref, lam_ref,
                  hr_out, h_out, buf_out, prev_scr, h_scr, a_scr, u_scr, c_scr, *, ns, chained):
    s = pl.program_id(1)
    bb, ts, W = xr_ref.shape
    nb = bb * ts // SUBLANES
    x = xr_ref[...].reshape(nb, SUBLANES, W)

    if chained:
        @pl.when(s == 0)
        def _():
            prev_scr[...] = buf_ref[0]
            h_scr[...] = h0_ref[0]

        prev = jnp.concatenate([prev_scr[...][None], x[:nb - 1]], axis=0)
        prev_scr[...] = x[nb - 1]
    else:
        prev = buf_ref[...]

    t_idx = lax.broadcasted_iota(jnp.int32, x.shape, 1)
    cw = cw_ref[...]
    xc = cb_ref[...] + cw[RG_CONV - 1:RG_CONV, :] * x
    for d in range(1, RG_CONV):
        back = jnp.where(t_idx >= d, pltpu.roll(x, d, 1), pltpu.roll(prev, d, 1))
        xc = xc + cw[RG_CONV - 1 - d:RG_CONV - d, :] * back

    xc2 = xc.reshape(nb * SUBLANES, W)
    xb = xc2.astype(BF)
    r = jax.nn.sigmoid(_dot(xb, wa_ref[...]) + ba_ref[...])
    i = jax.nn.sigmoid(_dot(xb, wx_ref[...]) + bx_ref[...])
    lam = lam_ref[...]
    softplus_neg = jnp.maximum(-lam, 0.0) + jnp.log1p(jnp.exp(-jnp.abs(lam)))
    log_a = (-RG_C * softplus_neg) * r
    a = jnp.exp(log_a).reshape(nb, SUBLANES, W)
    u = (jnp.sqrt(-_expm1(2.0 * log_a)) * i * xc2).reshape(nb, SUBLANES, W)

    d = 1
    while d < SUBLANES:
        inside = t_idx >= d
        u = jnp.where(inside, a * pltpu.roll(u, d, 1) + u, u)
        a = jnp.where(inside, a * pltpu.roll(a, d, 1), a)
        d *= 2

    if chained:
        a_scr[...] = a
        u_scr[...] = u

        def carry_in(k, h):
            c_scr[k] = jnp.broadcast_to(h, (SUBLANES, W))
            return a_scr[k, SUBLANES - 1:SUBLANES, :] * h + u_scr[k, SUBLANES - 1:SUBLANES, :]

        h_last = lax.fori_loop(0, nb, carry_in, h_scr[...], unroll=8)
        h_scr[...] = h_last
        hs = a * c_scr[...] + u
    else:
        hs = a * h0_ref[...] + u
        h_last = hs[:, SUBLANES - 1:SUBLANES, :]
    hr_out[...] = (hs * jax.nn.gelu(gr_ref[...].reshape(nb, SUBLANES, W))).reshape(bb, ts, W)

    tail = SUBLANES - (RG_CONV - 1)
    if chained:
        @pl.when(s == ns - 1)
        def _():
            h_out[0] = h_last
            buf_out[0] = x[nb - 1, tail:, :]
    else:
        h_out[...] = h_last
        buf_out[...] = x[:, tail:, :]


def rglru_mixer(z, h0, buf0, conv_w, conv_b, wa_bd, wx_bd, b_a, b_x, lam):
    B, S, _ = z.shape
    W = RG_WIDTH
    rows = 512
    chained = S > SUBLANES
    if chained:
        bb, ts = 1, min(S, rows)
        assert S % ts == 0 and ts % SUBLANES == 0
    else:
        bb, ts = min(B, rows // SUBLANES), S
        assert S == SUBLANES and B % bb == 0
    ns = S // ts
    nb = bb * ts // SUBLANES
    xr_blk = (4 * MLSTM_WIDTH) // W
    row = lambda a: a.reshape(1, W)
    buf8 = jnp.pad(buf0, ((0, 0), (SUBLANES - (RG_CONV - 1), 0), (0, 0)))
    st = lambda n: pl.BlockSpec((bb, n, W), lambda b, s: (b, 0, 0))
    hr, h1, buf1 = pl.pallas_call(
        functools.partial(_rglru_kernel, ns=ns, chained=chained),
        out_shape=(
            jax.ShapeDtypeStruct((B, S, W), F32),
            jax.ShapeDtypeStruct((B, 1, W), F32),
            jax.ShapeDtypeStruct((B, RG_CONV - 1, W), F32),
        ),
        grid=(B // bb, ns),
        in_specs=[
            pl.BlockSpec((bb, ts, W), lambda b, s: (b, s, xr_blk)),
            pl.BlockSpec((bb, ts, W), lambda b, s: (b, s, xr_blk + 1)),
            st(SUBLANES), st(1),
            _resident((RG_CONV, W)), _resident((1, W)), _resident((W, W)), _resident((W, W)),
            _resident((1, W)), _resident((1, W)), _resident((1, W)),
        ],
        out_specs=(pl.BlockSpec((bb, ts, W), lambda b, s: (b, s, 0)), st(1), st(RG_CONV - 1)),
        scratch_shapes=[pltpu.VMEM((SUBLANES, W), F32), pltpu.VMEM((1, W), F32),
                        pltpu.VMEM((nb, SUBLANES, W), F32), pltpu.VMEM((nb, SUBLANES, W), F32),
                        pltpu.VMEM((nb, SUBLANES, W), F32)],
        compiler_params=_cparams(("parallel", "arbitrary")),
        name="rglru",
    )(z, z, buf8, h0.reshape(B, 1, W), conv_w, row(conv_b), wa_bd, wx_bd, row(b_a), row(b_x), row(lam))
    return hr, h1.reshape(B, W), buf1


def _rwkv_pre_kernel(x_ref, sh_ref, sc_ref, prev0_ref, mu_ref, wr_ref, wk_ref, wv_ref, w0_ref, w1_ref,
                     w2_ref, a0_ref, a1_ref, a2_ref, g1_ref, g2_ref, kk_ref, ka_ref,
                     r_out, lw_out, k_out, v_out, kk_out, a_out, g_out, shift_out, carry_scr, *, ns):
    s = pl.program_id(1)
    bb, ts, _ = x_ref.shape
    tm = bb * ts

    @pl.when(s == 0)
    def _():
        carry_scr[...] = prev0_ref[...]

    h = x_ref[...] * (1.0 + sc_ref[...]) + sh_ref[...]
    t_idx = lax.broadcasted_iota(jnp.int32, h.shape, 1)
    h_prev = jnp.where(t_idx == 0, carry_scr[...], pltpu.roll(h, 1, 1))
    last = h[:, ts - 1:ts, :]
    carry_scr[...] = last
    dx = (h_prev - h).reshape(tm, D_MODEL)
    h2 = h.reshape(tm, D_MODEL)
    mu = mu_ref[...]
    mix = lambda j: (h2 + dx * mu[j:j + 1, :]).astype(BF)
    xr, xw, xk, xv, xa, xg = [mix(j) for j in range(6)]

    r = _dot(xr, wr_ref[...])
    k = _dot(xk, wk_ref[...])
    v = _dot(xv, wv_ref[...])
    wl = w0_ref[...] + _dot(jnp.tanh(_dot(xw, w1_ref[...])).astype(BF), w2_ref[...])
    log_decay = -DECAY_SCALE * jax.nn.sigmoid(wl)
    a = jax.nn.sigmoid(a0_ref[...] + _dot(_dot(xa, a1_ref[...]).astype(BF), a2_ref[...]))
    g = _dot(jax.nn.sigmoid(_dot(xg, g1_ref[...])).astype(BF), g2_ref[...])
    kk = k * kk_ref[...]
    k = k * (1.0 + (a - 1.0) * ka_ref[...])

    shp = x_ref.shape
    r_out[...] = r.reshape(shp)
    lw_out[...] = log_decay.reshape(shp)
    k_out[...] = k.reshape(shp)
    v_out[...] = v.reshape(shp)
    kk_out[...] = kk.reshape(shp)
    a_out[...] = a.reshape(shp)
    g_out[...] = g.reshape(shp)

    @pl.when(s == ns - 1)
    def _():
        shift_out[...] = last


def rwkv_pre(x, mod, prev0, P):
    B, S, _ = x.shape
    bb, ts = _row_tile(B, S, 512)
    ns = S // ts
    xmap = lambda i, s: (i, s, 0)
    full = _resident
    tok = pl.BlockSpec((bb, ts, D_MODEL), xmap)
    st = pl.BlockSpec((bb, 1, D_MODEL), lambda i, s: (i, 0, 0))
    sq, lo_in, lo_out, vec = (D_MODEL, D_MODEL), (D_MODEL, LORA_PAD), (LORA_PAD, D_MODEL), (1, D_MODEL)
    outs = pl.pallas_call(
        functools.partial(_rwkv_pre_kernel, ns=ns),
        out_shape=tuple([jax.ShapeDtypeStruct(x.shape, F32)] * 7
                        + [jax.ShapeDtypeStruct((B, 1, D_MODEL), F32)]),
        grid=(B // bb, ns),
        in_specs=[
            tok, _mod_spec(bb, 3), _mod_spec(bb, 4), st, full((6, D_MODEL)),
            full(sq), full(sq), full(sq), full(vec), full(lo_in), full(lo_out),
            full(vec), full(lo_in), full(lo_out), full(lo_in), full(lo_out),
            full(vec), full(vec),
        ],
        out_specs=tuple([tok] * 7 + [st]),
        scratch_shapes=[pltpu.VMEM((bb, 1, D_MODEL), F32)],
        compiler_params=_cparams(("parallel", "arbitrary")),
        name="rwkv_pre",
    )(x, mod, mod, prev0.reshape(B, 1, D_MODEL), P['mu'], P['wr'], P['wk'], P['wv'], P['w0'], P['w1'],
      P['w2'], P['a0'], P['a1'], P['a2'], P['g1'], P['g2'], P['k_k'], P['k_a'])
    return outs


def _stack_heads(x, first):
    return jnp.concatenate([jnp.where(first, x, 0.0), jnp.where(first, 0.0, x)], axis=0)


def _seg_sum(x, first):
    s0 = jnp.sum(jnp.where(first, x, 0.0), axis=1, keepdims=True)
    s1 = jnp.sum(jnp.where(first, 0.0, x), axis=1, keepdims=True)
    return jnp.where(first, s0, s1)


def _wkv_pair_chunks(toks, sps, vecs, consts, L, tall_state):
    tri, first, strict, incl, own = consts
    L2 = 2 * L
    hs = RWKV_HS
    rng = range(len(toks))
    wide = L2 % LANES == 0
    if tall_state:
        tall = lambda x: jnp.concatenate([x[:, :hs], x[:, hs:]], axis=0)
        keep_own = lambda x: jnp.where(own, x, 0.0)
    else:
        tall = lambda x: _stack_heads(x, first)
        keep_own = lambda x: x
    splits = [_split3(t[1]) for t in toks]
    c_incl = [_dot(tri, s[0]) + _dot(tri, s[1]) + _dot(tri, s[2]) for s in splits]
    e_inv = [jnp.exp(-c) for c in c_incl]
    kn = [t[4] / jnp.maximum(jnp.sqrt(_seg_sum(t[4] * t[4], first)), 1e-12) for t in toks]
    la = [tall(-kn[p] * jnp.exp(c_incl[p] - toks[p][1])).astype(BF) for p in rng]
    lr = [tall(toks[p][0] * jnp.exp(c_incl[p])).astype(BF) for p in rng]
    rb = [tall(kn[p] * toks[p][5] * e_inv[p]).astype(BF) for p in rng]
    rk = [tall(toks[p][2] * e_inv[p]).astype(BF) for p in rng]
    vs = [_stack_heads(toks[p][3], first).astype(BF) for p in rng]
    rbk = [jnp.concatenate([rb[p], rk[p]], axis=0) for p in rng]

    if wide:
        nn = [_dot_nt(la[p], rbk[p]) for p in rng]
        mm = [_dot_nt(lr[p], rbk[p]) for p in rng]
        n_raw = [x[:, :L2] for x in nn]
        n_ak = [jnp.where(strict, x[:, L2:], 0.0).astype(BF) for x in nn]
        m_bk = [jnp.concatenate([jnp.where(incl, x[:, :L2], 0.0), jnp.where(incl, x[:, L2:], 0.0)],
                                axis=1).astype(BF) for x in mm]
    else:
        n_raw = [_dot_nt(la[p], rb[p]) for p in rng]
        n_ak = [jnp.where(strict, _dot_nt(la[p], rk[p]), 0.0).astype(BF) for p in rng]
        m_rb = [jnp.where(incl, _dot_nt(lr[p], rb[p]), 0.0).astype(BF) for p in rng]
        m_rk = [jnp.where(incl, _dot_nt(lr[p], rk[p]), 0.0).astype(BF) for p in rng]

    row = lax.broadcasted_iota(jnp.int32, (L2, L2), 0)
    col = lax.broadcasted_iota(jnp.int32, (L2, L2), 1)
    base = (row // INV_BASE == col // INV_BASE) & (col < row)
    xb = [jnp.where(base, x, 0.0).astype(BF) for x in n_raw]
    tinv = [jnp.where(row == col, 1.0, 0.0) + x.astype(F32) for x in xb]
    x2 = [_dot(x, x).astype(BF) for x in xb]
    if wide:
        prod = [_dot(x2[p], jnp.concatenate([x2[p], tinv[p].astype(BF)], axis=1)) for p in rng]
        tinv = [tinv[p] + prod[p][:, L2:] for p in rng]
        x4 = [x[:, :L2].astype(BF) for x in prod]
    else:
        tinv = [tinv[p] + _dot(x2[p], tinv[p].astype(BF)) for p in rng]
        x4 = [_dot(x, x).astype(BF) for x in x2]
    tinv = [tinv[p] + _dot(x4[p], tinv[p].astype(BF)) for p in rng]
    blk = INV_BASE
    while blk < L:
        off = (row // (2 * blk) == col // (2 * blk)) & (row // blk == col // blk + 1)
        tb = [t.astype(BF) for t in tinv]
        pr = [_dot(jnp.where(off, n_raw[p], 0.0).astype(BF), tb[p]).astype(BF) for p in rng]
        tinv = [tinv[p] + _dot(tb[p], pr[p]) for p in rng]
        blk *= 2

    spb = [s.astype(BF) for s in sps]
    rhs = [keep_own(_dot_nt(la[p], spb[p])) + _dot(n_ak[p], vs[p]) for p in rng]
    ub = [_dot(tinv[p].astype(BF), rhs[p].astype(BF)).astype(BF) for p in rng]

    uv = [jnp.concatenate([ub[p], vs[p]], axis=0) for p in rng]
    y0 = [keep_own(_dot_nt(lr[p], spb[p])) for p in rng]
    if wide:
        ys = [y0[p] + _dot(m_bk[p], uv[p]) for p in rng]
    else:
        ys = [y0[p] + _dot(m_rb[p], ub[p]) + _dot(m_rk[p], vs[p]) for p in rng]
    decay = [jnp.exp(c_incl[p][L - 1:L, :]) for p in rng]
    if tall_state:
        upper = lax.broadcasted_iota(jnp.int32, (2 * hs, hs), 0) < hs
        decay = [jnp.where(upper, d[:, :hs], d[:, hs:]) for d in decay]
    sp_new = [(sps[p] + _dot_tn(uv[p], rbk[p])) * decay[p] for p in rng]

    outs = []
    for p in rng:
        r, _, k, v, _, _, gate = toks[p]
        r_k, lnx_g, lnx_b = vecs[p]
        y = ys[p][0:L, :] + ys[p][L:L2, :]
        yc = y - _seg_sum(y, first) * (1.0 / RWKV_HS)
        yv = _seg_sum(yc * yc, first) * (1.0 / RWKV_HS)
        yn = yc * lax.rsqrt(yv + RWKV_LN_EPS) * lnx_g + lnx_b
        outs.append(((yn + _seg_sum(r * k * r_k, first) * v) * gate, sp_new[p]))
    return outs


def _wkv_kernel(r_ref, lw_ref, k_ref, v_ref, kk_ref, a_ref, g_ref, s0_ref, rk_ref, lxg_ref, lxb_ref,
                y_out, s_out, sp_scr, *, L, nc, bb, group):
    c = pl.program_id(1)
    hs = RWKV_HS
    npair = RWKV_HEADS // 2
    L2 = 2 * L

    tall_state = nc == 1

    @pl.when(c == 0)
    def _():
        if tall_state:
            sp_scr[...] = s0_ref[...].reshape(sp_scr.shape)
        else:
            zero = jnp.zeros((hs, hs), F32)

            def init(i, carry):
                for p in range(npair):
                    top = jnp.concatenate([s0_ref[i, 2 * p], zero], axis=1)
                    bot = jnp.concatenate([zero, s0_ref[i, 2 * p + 1]], axis=1)
                    sp_scr[i, p] = jnp.concatenate([top, bot], axis=0)
                return carry

            lax.fori_loop(0, bb, init, 0)

    row = lax.broadcasted_iota(jnp.int32, (L, L), 0)
    col = lax.broadcasted_iota(jnp.int32, (L, L), 1)
    tri = jnp.where(col <= row, 1.0, 0.0).astype(BF)
    first = lax.broadcasted_iota(jnp.int32, (L, LANES), 1) < hs
    row2 = lax.broadcasted_iota(jnp.int32, (L2, L2), 0)
    col2 = lax.broadcasted_iota(jnp.int32, (L2, L2), 1)
    same = (row2 >= L) == (col2 >= L)
    own = ((lax.broadcasted_iota(jnp.int32, (L2, LANES), 0) >= L)
           == (lax.broadcasted_iota(jnp.int32, (L2, LANES), 1) >= hs))
    consts = (tri, first, same & (col2 < row2), same & (col2 <= row2), own)
    sls = [slice(p * LANES, (p + 1) * LANES) for p in range(npair)]

    def seqs(j, carry):
        ids = [(j * group + g, p) for g in range(group) for p in range(npair)]
        toks = [tuple(ref[i, :, sls[p]] for ref in (r_ref, lw_ref, k_ref, v_ref, kk_ref, a_ref, g_ref))
                for i, p in ids]
        vecs = [(rk_ref[:, sls[p]], lxg_ref[:, sls[p]], lxb_ref[:, sls[p]]) for _, p in ids]
        outs = _wkv_pair_chunks(toks, [sp_scr[i, p] for i, p in ids], vecs, consts, L, tall_state)
        for (i, p), (y, sp_new) in zip(ids, outs):
            y_out[i, :, sls[p]] = y
            sp_scr[i, p] = sp_new
        return carry

    lax.fori_loop(0, bb // group, seqs, 0)

    @pl.when(c == nc - 1)
    def _():
        if tall_state:
            s_out[...] = sp_scr[...].reshape(s_out.shape)
        else:
            def fin(i, carry):
                for p in range(npair):
                    sp = sp_scr[i, p]
                    s_out[i, 2 * p] = sp[0:hs, 0:hs]
                    s_out[i, 2 * p + 1] = sp[hs:2 * hs, hs:2 * hs]
                return carry

            lax.fori_loop(0, bb, fin, 0)


def wkv_recurrence(r, lw, k, v, kk, a, g, s0, P):
    B, S, _ = r.shape
    L = RWKV_CHUNK if S % RWKV_CHUNK == 0 else S
    assert L & (L - 1) == 0 and L % SUBLANES == 0
    nc = S // L
    bb, group = (2, 2) if nc > 1 else (min(B, 8), min(B, 8))
    assert B % bb == 0 and bb % group == 0
    npair = RWKV_HEADS // 2
    tok = pl.BlockSpec((bb, L, D_MODEL), lambda b, c: (b, c, 0))
    st = pl.BlockSpec((bb, RWKV_HEADS, RWKV_HS, RWKV_HS), lambda b, c: (b, 0, 0, 0))
    vec = pl.BlockSpec((1, D_MODEL), lambda b, c: (0, 0))
    return pl.pallas_call(
        functools.partial(_wkv_kernel, L=L, nc=nc, bb=bb, group=group),
        out_shape=(jax.ShapeDtypeStruct(r.shape, F32), jax.ShapeDtypeStruct(s0.shape, F32)),
        grid=(B // bb, nc),
        in_specs=[tok] * 7 + [st, vec, vec, vec],
        out_specs=(tok, st),
        scratch_shapes=[pltpu.VMEM((bb, npair, 2 * RWKV_HS, RWKV_HS if nc == 1 else 2 * RWKV_HS), F32)],
        compiler_params=_cparams(("parallel", "arbitrary")),
        name="wkv",
    )(r, lw, k, v, kk, a, g, s0, P['r_k'], P['lnx_g'], P['lnx_b'])


def _block_diag(w):
    G, n, _ = w.shape
    eye = jnp.eye(G, dtype=w.dtype)
    return (eye[:, None, :, None] * w[:, :, None, :]).reshape(G * n, G * n)


def _pad_cols(w, n):
    return jnp.pad(w, ((0, 0), (0, n - w.shape[1])))


def _pad_rows(w, n):
    return jnp.pad(w, ((0, n - w.shape[0]), (0, 0)))


def _prep_ab(j, p):
    w_in = p['ab_w_in'][j]
    n_main = 4 * MLSTM_WIDTH
    n_gate = 2 * MLSTM_HEADS
    w_all = jnp.concatenate([w_in[:, :n_main], w_in[:, n_main + n_gate:],
                             w_in[:, n_main:n_main + n_gate]], axis=1)
    return dict(
        w_in=_pad_cols(w_all, Z_COLS).astype(BF),
        b_gates=p['mlstm_b_gates'][j], norm_g=p['mlstm_norm_g'][j],
        conv_w=p['rg_conv_w'][j], conv_b=p['rg_conv_b'][j],
        wa=_block_diag(p['rg_w_a'][j]).astype(BF), wx=_block_diag(p['rg_w_x'][j]).astype(BF),
        b_a=p['rg_b_a'][j], b_x=p['rg_b_x'][j], lam=p['rg_lambda'][j],
        w_out_m=p['ab_w_out'][j][:MLSTM_WIDTH].astype(BF), w_out_r=p['ab_w_out'][j][MLSTM_WIDTH:].astype(BF),
    )


def _prep_rwkv(j, p):
    vec = lambda a: a.reshape(1, D_MODEL)
    return dict(
        mu=p['rw_mu'][j],
        wr=p['rw_wr'][j].astype(BF), wk=p['rw_wk'][j].astype(BF), wv=p['rw_wv'][j].astype(BF),
        w0=vec(p['rw_w0'][j]), w1=_pad_cols(p['rw_w1'][j], LORA_PAD).astype(BF),
        w2=_pad_rows(p['rw_w2'][j], LORA_PAD).astype(BF),
        a0=vec(p['rw_a0'][j]), a1=_pad_cols(p['rw_a1'][j], LORA_PAD).astype(BF),
        a2=_pad_rows(p['rw_a2'][j], LORA_PAD).astype(BF),
        g1=_pad_cols(p['rw_g1'][j], LORA_PAD).astype(BF), g2=_pad_rows(p['rw_g2'][j], LORA_PAD).astype(BF),
        k_k=vec(p['rw_k_k'][j]), k_a=vec(p['rw_k_a'][j]), r_k=vec(p['rw_r_k'][j]),
        lnx_g=vec(p['rw_lnx_g'][j]), lnx_b=vec(p['rw_lnx_b'][j]), wo=p['rw_wo'][j].astype(BF),
    )


def ab_mixer(x, mod, st, A):
    mC, mn, mm, rh, rconv = st
    z, gates = mod_matmul(x, mod, A['w_in'], 4 * MLSTM_WIDTH + 2 * RG_WIDTH, 2 * MLSTM_HEADS)
    hm, C1, n1, m1 = mlstm_mixer(z, gates, mC, mn, mm, A['b_gates'], A['norm_g'])
    hr, rh1, buf1 = rglru_mixer(z, rh, rconv, A['conv_w'], A['conv_b'], A['wa'], A['wx'],
                                A['b_a'], A['b_x'], A['lam'])
    return ((hm, 0), (hr, 0), A['w_out_m'], A['w_out_r']), (C1, n1, m1, rh1, buf1)


def rwkv_mixer(x, mod, st, R):
    wkv0, prev0 = st
    r, lw, k, v, kk, a, g, shift = rwkv_pre(x, mod, prev0, R)
    y, wkv1 = wkv_recurrence(r, lw, k, v, kk, a, g, wkv0, R)
    half = D_MODEL // 2
    return ((y, 0), (y, 1), R['wo'][:half], R['wo'][half:]), (wkv1, shift.reshape(shift.shape[0], D_MODEL))


def run_trunk(x, mods, states, W):
    mC, mn, mm, rh, rconv, wkv, shift = states
    new_ab, new_c = [], []
    for layer in range(DEPTH):
        mod = mods[layer]
        lg = lambda i: W['ln_g'][layer, i].reshape(1, D_MODEL)
        lb = lambda i: W['ln_b'][layer, i].reshape(1, D_MODEL)
        f = W['ffn']
        x = ffn_block(x, mod, 0, layer, 0, f[0], f[1], f[2], lg(0), lb(0))
        j = layer // 2
        if layer % 2 == 0:
            mixed, st = ab_mixer(x, mod, (mC[j], mn[j], mm[j], rh[j], rconv[j]), W['ab'][j])
            new_ab.append(st)
        else:
            mixed, st = rwkv_mixer(x, mod, (wkv[j], shift[j]), W['rwkv'][j])
            new_c.append(st)
        x = ffn_block(x, mod, 2, layer, 1, f[0], f[1], f[2], lg(2), lb(2), mixer=mixed + (lg(1), lb(1)))
    stk = lambda sts, i: sts[0][i][None] if len(sts) == 1 else jnp.stack([s[i] for s in sts], axis=0)
    return x, (stk(new_ab, 0), stk(new_ab, 1), stk(new_ab, 2), stk(new_ab, 3), stk(new_ab, 4),
               stk(new_c, 0), stk(new_c, 1))


def _zero_states(B):
    n_ab, n_c = (DEPTH + 1) // 2, DEPTH // 2
    return (jnp.zeros((n_ab, B, MLSTM_HEADS, MLSTM_HD, MLSTM_HD), F32),
            jnp.zeros((n_ab, B, MLSTM_HEADS, MLSTM_HD), F32),
            jnp.zeros((n_ab, B, MLSTM_HEADS), F32),
            jnp.zeros((n_ab, B, RG_WIDTH), F32),
            jnp.zeros((n_ab, B, RG_CONV - 1, RG_WIDTH), F32),
            jnp.zeros((n_c, B, RWKV_HEADS, RWKV_HS, RWKV_HS), F32),
            jnp.zeros((n_c, B, D_MODEL), F32))


def kernel(x_prompt, x_sample, c_prompt, c_sample, state_mlstm_C, state_mlstm_n, state_mlstm_m, state_rglru_h, state_rglru_conv, state_rwkv_wkv, state_rwkv_shift, ada_w, ada_b, ln_g, ln_b, ffn_w1, ffn_w3, ffn_w2, ab_w_in, mlstm_b_gates, mlstm_norm_g, rg_conv_w, rg_conv_b, rg_w_a, rg_b_a, rg_w_x, rg_b_x, rg_lambda, ab_w_out, rw_mu, rw_wr, rw_wk, rw_wv, rw_w0, rw_w1, rw_w2, rw_a0, rw_a1, rw_a2, rw_g1, rw_g2, rw_k_k, rw_k_a, rw_r_k, rw_lnx_g, rw_lnx_b, rw_wo):
    p = dict(ab_w_in=ab_w_in, mlstm_b_gates=mlstm_b_gates, mlstm_norm_g=mlstm_norm_g,
             rg_conv_w=rg_conv_w, rg_conv_b=rg_conv_b, rg_w_a=rg_w_a, rg_b_a=rg_b_a, rg_w_x=rg_w_x,
             rg_b_x=rg_b_x, rg_lambda=rg_lambda, ab_w_out=ab_w_out, rw_mu=rw_mu, rw_wr=rw_wr,
             rw_wk=rw_wk, rw_wv=rw_wv, rw_w0=rw_w0, rw_w1=rw_w1, rw_w2=rw_w2, rw_a0=rw_a0,
             rw_a1=rw_a1, rw_a2=rw_a2, rw_g1=rw_g1, rw_g2=rw_g2, rw_k_k=rw_k_k, rw_k_a=rw_k_a,
             rw_r_k=rw_r_k, rw_lnx_g=rw_lnx_g, rw_lnx_b=rw_lnx_b, rw_wo=rw_wo)
    W = dict(
        ln_g=ln_g, ln_b=ln_b,
        ffn=(ffn_w1.astype(BF), ffn_w3.astype(BF), ffn_w2.astype(BF)),
        ab=[_prep_ab(j, p) for j in range((DEPTH + 1) // 2)],
        rwkv=[_prep_rwkv(j, p) for j in range(DEPTH // 2)],
    )
    Bp, Bs = x_prompt.shape[0], x_sample.shape[0]
    mod_all = adaln(jnp.concatenate([c_prompt, c_sample], axis=0), ada_w, ada_b)
    mods_p = [mod_all[l, :Bp] for l in range(DEPTH)]
    mods_s = [mod_all[l, Bp:] for l in range(DEPTH)]
    y_prompt, sp = run_trunk(x_prompt, mods_p, _zero_states(Bp), W)
    y_sample, ss = run_trunk(x_sample, mods_s,
                             (state_mlstm_C, state_mlstm_n, state_mlstm_m, state_rglru_h,
                              state_rglru_conv, state_rwkv_wkv, state_rwkv_shift), W)
    return (y_prompt, y_sample) + tuple(sp) + tuple(ss)
```

```python
import functools
import math

import jax
import jax.numpy as jnp
from jax import lax
from jax.experimental import pallas as pl
from jax.experimental.pallas import tpu as pltpu

D_MODEL = 1024
DEPTH = 2
N_SUB = 3
MLSTM_WIDTH = 512
MLSTM_HEADS = 4
MLSTM_HD = 128
MLSTM_CHUNK = 128
RG_WIDTH = 512
RG_BLOCKS = 8
RG_BD = 64
RG_CONV = 4
RG_C = 8.0
RWKV_HS = 64
RWKV_HEADS = 16
RWKV_LN_EPS = 64e-5
RWKV_CHUNK = 64
DECAY_SCALE = math.exp(-0.5)
INV_BASE = 8
D_FF = 2816
ALPHA = (2.0 * DEPTH) ** 0.25
LN_EPS = 1e-5
HEAD_NORM_EPS = 1e-6

LANES = 128
SUBLANES = 8
VMEM_LIMIT_BYTES = 56 * 1024 * 1024
Z_COLS = 3200
LORA_PAD = 128

BF = jnp.bfloat16
F32 = jnp.float32

NT_DIMS = (((1,), (1,)), ((), ()))
TN_DIMS = (((0,), (0,)), ((), ()))


def _cparams(sem):
    return pltpu.CompilerParams(dimension_semantics=sem, vmem_limit_bytes=VMEM_LIMIT_BYTES)


def _dot(a, b):
    return jnp.dot(a, b, preferred_element_type=F32)


def _dot_nt(a, b):
    return lax.dot_general(a, b, NT_DIMS, preferred_element_type=F32)


def _dot_tn(a, b):
    return lax.dot_general(a, b, TN_DIMS, preferred_element_type=F32)


def _layer_norm(z, g, b):
    mu = jnp.mean(z, axis=-1, keepdims=True)
    zc = z - mu
    var = jnp.mean(zc * zc, axis=-1, keepdims=True)
    return zc * lax.rsqrt(var + LN_EPS) * g + b


def _row_tile(B, S, rows):
    if S >= rows:
        assert S % rows == 0
        return 1, rows
    bb = min(B, rows // S)
    assert B % bb == 0
    return bb, S


def _split2(x):
    hi = x.astype(BF)
    lo = (x - hi.astype(F32)).astype(BF)
    return hi, lo


def _split3(x):
    hi = x.astype(BF)
    r1 = x - hi.astype(F32)
    mid = r1.astype(BF)
    lo = (r1 - mid.astype(F32)).astype(BF)
    return hi, mid, lo


def _adaln_kernel(c_ref, w_ref, b_ref, o_ref):
    h = jax.nn.silu(c_ref[...]).astype(BF)
    res = _dot(h, w_ref[...].astype(BF)) + b_ref[...]
    o_ref[...] = res.reshape(o_ref.shape)


def adaln(c_all, ada_w, ada_b):
    Bc = c_all.shape[0]
    n_chunk = ada_w.shape[-1] // D_MODEL
    return pl.pallas_call(
        _adaln_kernel,
        out_shape=jax.ShapeDtypeStruct((DEPTH, Bc, n_chunk, 1, D_MODEL), F32),
        grid=(DEPTH, n_chunk),
        in_specs=[
            pl.BlockSpec((Bc, D_MODEL), lambda l, j: (0, 0)),
            pl.BlockSpec((None, D_MODEL, D_MODEL), lambda l, j: (l, 0, j)),
            pl.BlockSpec((None, 1, D_MODEL), lambda l, j: (l, 0, j)),
        ],
        out_specs=pl.BlockSpec((None, Bc, None, 1, D_MODEL), lambda l, j: (l, 0, j, 0, 0)),
        compiler_params=_cparams(("parallel", "parallel")),
        name="adaln",
    )(c_all, ada_w, ada_b.reshape(DEPTH, 1, n_chunk * D_MODEL))


def _mod_spec(bb, k):
    return pl.BlockSpec((bb, None, 1, D_MODEL), lambda i, *_: (i, k, 0, 0))


FFN_ROW_SPLITS = 2


def _ffn_rows(x, mixer, sh, sc, gt, w1_ref, w3_ref, w2_ref, lg, lb, res_w):
    bb, ts, _ = x.shape
    if mixer is not None:
        mgt, m1, m2, mw1_ref, mw2_ref, mlg, mlb = mixer
        flat = lambda m: m.reshape(bb * ts, -1).astype(BF)
        ym = _dot(flat(m1), mw1_ref[...]) + _dot(flat(m2), mw2_ref[...])
        yield
        x = _layer_norm(ALPHA * x + (1.0 + mgt) * ym.reshape(bb, ts, D_MODEL), mlg, mlb)
    h = (x * (1.0 + sc) + sh).reshape(bb * ts, D_MODEL).astype(BF)
    a = _dot(h, w1_ref[...])
    b = _dot(h, w3_ref[...])
    yield
    g = (jax.nn.silu(a) * b).astype(BF)
    y = _dot(g, w2_ref[...])
    yield
    return _layer_norm(ALPHA * x + (res_w * (1.0 + gt)) * y.reshape(bb, ts, D_MODEL), lg, lb)


def _ffn_kernel(*refs, res_w, after_mixer):
    if after_mixer:
        (x_ref, mgt_ref, m1_ref, m2_ref, mw1_ref, mw2_ref, mlg_ref, mlb_ref,
         sh_ref, sc_ref, gt_ref, w1_ref, w3_ref, w2_ref, lg_ref, lb_ref, o_ref) = refs
    else:
        x_ref, sh_ref, sc_ref, gt_ref, w1_ref, w3_ref, w2_ref, lg_ref, lb_ref, o_ref = refs
    bb, ts, _ = x_ref.shape
    if bb > 1:
        n = bb // FFN_ROW_SPLITS
        tok = lambda ref, i: ref[i * n:(i + 1) * n]
        per_seq = tok
    else:
        n = ts // FFN_ROW_SPLITS
        tok = lambda ref, i: ref[:, i * n:(i + 1) * n, :]
        per_seq = lambda ref, i: ref[...]
    gens = []
    for i in range(FFN_ROW_SPLITS):
        mixer = None
        if after_mixer:
            mixer = (per_seq(mgt_ref, i), tok(m1_ref, i), tok(m2_ref, i), mw1_ref, mw2_ref, mlg_ref[...], mlb_ref[...])
        gens.append(_ffn_rows(tok(x_ref, i), mixer, per_seq(sh_ref, i), per_seq(sc_ref, i), per_seq(gt_ref, i),
                              w1_ref, w3_ref, w2_ref, lg_ref[...], lb_ref[...], res_w))
    for i, out in enumerate(_lockstep(gens)):
        if bb > 1:
            o_ref[i * n:(i + 1) * n] = out
        else:
            o_ref[:, i * n:(i + 1) * n, :] = out


def _resident(shape):
    return pl.BlockSpec(shape, lambda *_: (0,) * len(shape), pipeline_mode=pl.Buffered(1))


def ffn_block(x, mod, sub, layer, half, w1, w3, w2, lg, lb, mixer=None):
    B, S, _ = x.shape
    bb, ts = _row_tile(B, S, 512)
    xmap = lambda i, s: (i, s, 0)
    wspec = lambda r, c: pl.BlockSpec((None, None, r, c), lambda i, s: (layer, half, 0, 0),
                                      pipeline_mode=pl.Buffered(1))
    vec = _resident((1, D_MODEL))
    specs = [pl.BlockSpec((bb, ts, D_MODEL), xmap)]
    args = [x]
    if mixer is not None:
        (m1, blk1), (m2, blk2), mw1, mw2, mlg, mlb = mixer
        mspec = lambda w, blk: pl.BlockSpec((bb, ts, w.shape[0]), lambda i, s: (i, s, blk))
        specs += [_mod_spec(bb, 5), mspec(mw1, blk1), mspec(mw2, blk2),
                  _resident(mw1.shape), _resident(mw2.shape), vec, vec]
        args += [mod, m1, m2, mw1, mw2, mlg, mlb]
    specs += [_mod_spec(bb, 3 * sub), _mod_spec(bb, 3 * sub + 1), _mod_spec(bb, 3 * sub + 2),
              wspec(D_MODEL, D_FF), wspec(D_MODEL, D_FF), wspec(D_FF, D_MODEL), vec, vec]
    args += [mod, mod, mod, w1, w3, w2, lg, lb]
    return pl.pallas_call(
        functools.partial(_ffn_kernel, res_w=0.5, after_mixer=mixer is not None),
        out_shape=jax.ShapeDtypeStruct(x.shape, F32),
        grid=(B // bb, S // ts),
        in_specs=specs,
        out_specs=pl.BlockSpec((bb, ts, D_MODEL), xmap),
        compiler_params=_cparams(("parallel", "parallel")),
        name="ffn_block",
    )(*args)


def _modmm_kernel(x_ref, sh_ref, sc_ref, w_ref, o_ref, gate_ref):
    bb, ts, _ = x_ref.shape
    n_main = o_ref.shape[-1]
    n_gate = gate_ref.shape[-1]
    h = (x_ref[...] * (1.0 + sc_ref[...]) + sh_ref[...]).reshape(bb * ts, D_MODEL).astype(BF)
    o_ref[...] = _dot(h, w_ref[:, :n_main]).reshape(o_ref.shape)
    gate_ref[...] = _dot(h, w_ref[:, n_main:])[:, :n_gate].reshape(gate_ref.shape)


def mod_matmul(x, mod, w, n_main, n_gate):
    B, S, _ = x.shape
    bb, ts = _row_tile(B, S, 512)
    xmap = lambda i, s: (i, s, 0)
    return pl.pallas_call(
        _modmm_kernel,
        out_shape=(jax.ShapeDtypeStruct((B, S, n_main), F32), jax.ShapeDtypeStruct((B, S, n_gate), F32)),
        grid=(B // bb, S // ts),
        in_specs=[
            pl.BlockSpec((bb, ts, D_MODEL), xmap),
            _mod_spec(bb, 3), _mod_spec(bb, 4),
            _resident(w.shape),
        ],
        out_specs=(pl.BlockSpec((bb, ts, n_main), xmap), pl.BlockSpec((bb, ts, n_gate), xmap)),
        compiler_params=_cparams(("parallel", "parallel")),
        name="mod_matmul",
    )(x, mod, mod, w)


def _log_sigmoid(x):
    return jnp.minimum(x, 0.0) - jnp.log1p(jnp.exp(-jnp.abs(x)))


def _lockstep(gens):
    results = [None] * len(gens)
    live = list(range(len(gens)))
    while live:
        still = []
        for idx in live:
            try:
                next(gens[idx])
                still.append(idx)
            except StopIteration as stop:
                results[idx] = stop.value
        live = still
    return results


def _dot_exact(a, b, dims):
    return lax.dot_general(a, b, dims, precision=lax.Precision.HIGHEST, preferred_element_type=F32)


def _mlstm_gate_views(gl, L):
    row = lax.broadcasted_iota(jnp.int32, (L, L), 0)
    col = lax.broadcasted_iota(jnp.int32, (L, L), 1)
    eye = jnp.where(row == col, 1.0, 0.0)
    lower = jnp.where(col <= row, 1.0, 0.0)
    upper = jnp.where(row <= col, 1.0, 0.0)
    gl_t = _dot_exact(gl, eye, TN_DIMS)
    cum_c = _dot_exact(lower, gl, (((1,), (0,)), ((), ())))
    cum_r = _dot_exact(gl, upper, TN_DIMS)
    return gl_t, cum_c, cum_r


def _mlstm_head_chunk(q, k, v, o, ig_c, b_c, ig_r, b_r, C0, n0, m0, g, L):
    d = MLSTM_HD
    spread = lambda colv, n: jnp.broadcast_to(colv, (L, n))

    def both(colv):
        over_l = spread(colv, L)
        return over_l, (over_l if L == d else spread(colv, d))

    ones_l = jnp.ones((L, d), BF)
    ones_d = jnp.ones((d, d), BF)

    on_mxu = L >= d

    def row_sum(x, ones):
        if not on_mxu:
            return spread(jnp.sum(x, axis=1, keepdims=True), d), x.astype(BF)
        hi, lo = _split2(x)
        return _dot(hi, ones) + _dot(lo, ones), hi

    row = lax.broadcasted_iota(jnp.int32, (L, L), 0)
    col = lax.broadcasted_iota(jnp.int32, (L, L), 1)
    causal = col <= row
    bc_ll = spread(b_c, L)
    dmat = jnp.where(causal, bc_ll - b_r + ig_r, -jnp.inf)
    m_inter = b_c + m0
    m = jnp.maximum(m_inter, jnp.max(dmat, axis=1, keepdims=True))
    m_ll, m_ld = both(m)
    p = jnp.exp(dmat - m_ll)
    w_inter = jnp.exp(spread(m_inter, d) - m_ld)

    k = k * (d ** -0.5)
    qb = q.astype(BF)
    kb = k.astype(BF)
    if on_mxu:
        q_hi, q_lo = _split2(q)
        n_hi, n_lo = _split2(jnp.broadcast_to(n0, (d, d)))
    yield
    s_raw = _dot_nt(qb, kb)
    q_c0 = _dot_nt(qb, C0.astype(BF))
    if on_mxu:
        q_n0 = _dot_nt(q_hi, n_hi) + _dot_nt(q_lo, n_hi) + _dot_nt(q_hi, n_lo)
    else:
        q_n0 = spread(jnp.sum(q * n0, axis=1, keepdims=True), d)
    yield
    scores = s_raw * p
    score_sum, scores_b = row_sum(scores, ones_l)
    num = _dot(scores_b, v.astype(BF)) + w_inter * q_c0
    den = score_sum + w_inter * q_n0
    hh = num / jnp.maximum(jnp.abs(den), jnp.exp(-m_ld))

    m_end = m[L - 1:L, :]
    b_end = b_c[L - 1:L, :]
    w_state = jnp.exp(b_end + m0 - m_end)
    w_rows = spread(jnp.exp(b_end - b_c + ig_c - m_end), d)
    yield
    c_new = w_state * C0 + _dot_tn((w_rows * v).astype(BF), kb)
    n_new = w_state * n0 + jnp.sum(w_rows * k, axis=0, keepdims=True)
    mu = row_sum(hh, ones_d)[0] * (1.0 / d)
    yield
    hc = hh - mu
    var = row_sum(hc * hc, ones_d)[0] * (1.0 / d)
    hn = hc * lax.rsqrt(var + HEAD_NORM_EPS)
    return hn * g * jax.nn.sigmoid(o), c_new, n_new, m_end


def _mlstm_kernel(m0_ref, bg_ref, q_ref, k_ref, v_ref, o_ref, gc_ref, c0_ref, n0_ref, g_ref,
                  h_out, c_out, n_out, m_out, c_scr, n_scr, m_scr, *, L, nc, bb, group):
    ib = pl.program_id(0)
    c = pl.program_id(1)
    H = MLSTM_HEADS

    @pl.when(c == 0)
    def _():
        c_scr[...] = c0_ref[...]
        n_scr[...] = n0_ref[...]

        def init(i, carry):
            for h in range(H):
                m_scr[i, h] = jnp.full((1, LANES), m0_ref[(ib * bb + i) * H + h], F32)
            return carry

        lax.fori_loop(0, bb, init, 0)

    sls = [slice(h * MLSTM_HD, (h + 1) * MLSTM_HD) for h in range(H)]

    def seqs(j, carry):
        views = []
        for g in range(group):
            gates = gc_ref[j * group + g] + bg_ref[...]
            is_input = lax.broadcasted_iota(jnp.int32, gates.shape, 1) < H
            gl = jnp.where(is_input, gates, _log_sigmoid(gates))
            views.append((gl,) + _mlstm_gate_views(gl, L))
        ids = [(j * group + g, h) for g in range(group) for h in range(H)]
        gate_args = [(views[g][0][:, h:h + 1], views[g][2][:, H + h:H + h + 1],
                      views[g][1][h:h + 1, :], views[g][3][H + h:H + h + 1, :])
                     for g in range(group) for h in range(H)]
        heads = _lockstep([_mlstm_head_chunk(
            q_ref[i, :, sls[h]], k_ref[i, :, sls[h]], v_ref[i, :, sls[h]], o_ref[i, :, sls[h]], *gate_args[n],
            c_scr[i, h], n_scr[i, h], m_scr[i, h][:, 0:1], g_ref[:, sls[h]], L) for n, (i, h) in enumerate(ids)])
        for (i, h), (out, c_new, n_new, m_end) in zip(ids, heads):
            h_out[i, :, sls[h]] = out
            c_scr[i, h] = c_new
            n_scr[i, h] = n_new
            m_scr[i, h] = jnp.broadcast_to(m_end, (1, LANES))
        return carry

    lax.fori_loop(0, bb // group, seqs, 0)

    @pl.when(c == nc - 1)
    def _():
        c_out[...] = c_scr[...]
        n_out[...] = n_scr[...]
        m_out[...] = m_scr[...]


def mlstm_mixer(z, gates, C0, n0, m0, b_gates, norm_g):
    B, S, _ = z.shape
    H = MLSTM_HEADS
    W = MLSTM_WIDTH
    L = MLSTM_CHUNK if S % MLSTM_CHUNK == 0 else S
    nc = S // L
    bb, group = (1, 1) if nc > 1 else (min(B, 8), min(B, 4))
    assert B % bb == 0 and bb % group == 0
    zspec = lambda blk: pl.BlockSpec((bb, L, W), lambda b, c: (b, c, blk))
    smem = pl.BlockSpec(memory_space=pltpu.SMEM)
    hd_spec = pl.BlockSpec((bb, H, 1, MLSTM_HD), lambda b, c: (b, 0, 0, 0))
    c_spec = pl.BlockSpec((bb, H, MLSTM_HD, MLSTM_HD), lambda b, c: (b, 0, 0, 0))
    hm, C1, n1, m1 = pl.pallas_call(
        functools.partial(_mlstm_kernel, L=L, nc=nc, bb=bb, group=group),
        out_shape=(
            jax.ShapeDtypeStruct((B, S, W), F32),
            jax.ShapeDtypeStruct((B, H, MLSTM_HD, MLSTM_HD), F32),
            jax.ShapeDtypeStruct((B, H, 1, MLSTM_HD), F32),
            jax.ShapeDtypeStruct((B, H, 1, LANES), F32),
        ),
        grid=(B // bb, nc),
        in_specs=[
            smem, pl.BlockSpec((1, 2 * H), lambda b, c: (0, 0)),
            zspec(0), zspec(1), zspec(2), zspec(3),
            pl.BlockSpec((bb, L, 2 * H), lambda b, c: (b, c, 0)),
            c_spec, hd_spec,
            pl.BlockSpec((1, W), lambda b, c: (0, 0)),
        ],
        out_specs=(pl.BlockSpec((bb, L, W), lambda b, c: (b, c, 0)), c_spec, hd_spec, hd_spec),
        scratch_shapes=[pltpu.VMEM((bb, H, MLSTM_HD, MLSTM_HD), F32), pltpu.VMEM((bb, H, 1, MLSTM_HD), F32),
                        pltpu.VMEM((bb, H, 1, LANES), F32)],
        compiler_params=_cparams(("parallel", "arbitrary")),
        name="mlstm",
    )(m0.reshape(B * H), b_gates.reshape(1, 2 * H), z, z, z, z, gates,
      C0, n0.reshape(B, H, 1, MLSTM_HD), norm_g.reshape(1, W))
    return hm, C1, n1.reshape(B, H, MLSTM_HD), m1[:, :, 0, 0]


def _expm1(y):
    u = jnp.exp(y)
    small = jnp.where(u == 1.0, y, (u - 1.0) * y / jnp.log(u))
    return jnp.where(jnp.abs(y) > 0.5, u - 1.0, small)


def _rglru_kernel(xr_ref, gr_ref, buf_ref, h0_ref, cw_ref, cb_ref, wa_ref, wx_ref, ba_ref, bx_ref, lam_ref,
                  hr_out, h_out, buf_out, prev_scr, h_scr, a_scr, u_scr, c_scr, *, ns, chained):
    s = pl.program_id(1)
    bb, ts, W = xr_ref.shape
    nb = bb * ts // SUBLANES
    x = xr_ref[...].reshape(nb, SUBLANES, W)

    if chained:
        @pl.when(s == 0)
        def _():
            prev_scr[...] = buf_ref[0]
            h_scr[...] = h0_ref[0]

        prev = jnp.concatenate([prev_scr[...][None], x[:nb - 1]], axis=0)
        prev_scr[...] = x[nb - 1]
    else:
        prev = buf_ref[...]

    t_idx = lax.broadcasted_iota(jnp.int32, x.shape, 1)
    cw = cw_ref[...]
    xc = cb_ref[...] + cw[RG_CONV - 1:RG_CONV, :] * x
    for d in range(1, RG_CONV):
        back = jnp.where(t_idx >= d, pltpu.roll(x, d, 1), pltpu.roll(prev, d, 1))
        xc = xc + cw[RG_CONV - 1 - d:RG_CONV - d, :] * back

    xc2 = xc.reshape(nb * SUBLANES, W)
    xb = xc2.astype(BF)
    r = jax.nn.sigmoid(_dot(xb, wa_ref[...]) + ba_ref[...])
    i = jax.nn.sigmoid(_dot(xb, wx_ref[...]) + bx_ref[...])
    lam = lam_ref[...]
    softplus_neg = jnp.maximum(-lam, 0.0) + jnp.log1p(jnp.exp(-jnp.abs(lam)))
    log_a = (-RG_C * softplus_neg) * r
    a = jnp.exp(log_a).reshape(nb, SUBLANES, W)
    u = (jnp.sqrt(-_expm1(2.0 * log_a)) * i * xc2).reshape(nb, SUBLANES, W)

    d = 1
    while d < SUBLANES:
        inside = t_idx >= d
        u = jnp.where(inside, a * pltpu.roll(u, d, 1) + u, u)
        a = jnp.where(inside, a * pltpu.roll(a, d, 1), a)
        d *= 2

    if chained:
        a_scr[...] = a
        u_scr[...] = u

        def carry_in(k, h):
            c_scr[k] = jnp.broadcast_to(h, (SUBLANES, W))
            return a_scr[k, SUBLANES - 1:SUBLANES, :] * h + u_scr[k, SUBLANES - 1:SUBLANES, :]

        h_last = lax.fori_loop(0, nb, carry_in, h_scr[...], unroll=8)
        h_scr[...] = h_last
        hs = a * c_scr[...] + u
    else:
        hs = a * h0_ref[...] + u
        h_last = hs[:, SUBLANES - 1:SUBLANES, :]
    hr_out[...] = (hs * jax.nn.gelu(gr_ref[...].reshape(nb, SUBLANES, W))).reshape(bb, ts, W)

    tail = SUBLANES - (RG_CONV - 1)
    if chained:
        @pl.when(s == ns - 1)
        def _():
            h_out[0] = h_last
            buf_out[0] = x[nb - 1, tail:, :]
    else:
        h_out[...] = h_last
        buf_out[...] = x[:, tail:, :]


def rglru_mixer(z, h0, buf0, conv_w, conv_b, wa_bd, wx_bd, b_a, b_x, lam):
    B, S, _ = z.shape
    W = RG_WIDTH
    rows = 512
    chained = S > SUBLANES
    if chained:
        bb, ts = 1, min(S, rows)
        assert S % ts == 0 and ts % SUBLANES == 0
    else:
        bb, ts = min(B, rows // SUBLANES), S
        assert S == SUBLANES and B % bb == 0
    ns = S // ts
    nb = bb * ts // SUBLANES
    xr_blk = (4 * MLSTM_WIDTH) // W
    row = lambda a: a.reshape(1, W)
    buf8 = jnp.pad(buf0, ((0, 0), (SUBLANES - (RG_CONV - 1), 0), (0, 0)))
    st = lambda n: pl.BlockSpec((bb, n, W), lambda b, s: (b, 0, 0))
    hr, h1, buf1 = pl.pallas_call(
        functools.partial(_rglru_kernel, ns=ns, chained=chained),
        out_shape=(
            jax.ShapeDtypeStruct((B, S, W), F32),
            jax.ShapeDtypeStruct((B, 1, W), F32),
            jax.ShapeDtypeStruct((B, RG_CONV - 1, W), F32),
        ),
        grid=(B // bb, ns),
        in_specs=[
            pl.BlockSpec((bb, ts, W), lambda b, s: (b, s, xr_blk)),
            pl.BlockSpec((bb, ts, W), lambda b, s: (b, s, xr_blk + 1)),
            st(SUBLANES), st(1),
            _resident((RG_CONV, W)), _resident((1, W)), _resident((W, W)), _resident((W, W)),
            _resident((1, W)), _resident((1, W)), _resident((1, W)),
        ],
        out_specs=(pl.BlockSpec((bb, ts, W), lambda b, s: (b, s, 0)), st(1), st(RG_CONV - 1)),
        scratch_shapes=[pltpu.VMEM((SUBLANES, W), F32), pltpu.VMEM((1, W), F32),
                        pltpu.VMEM((nb, SUBLANES, W), F32), pltpu.VMEM((nb, SUBLANES, W), F32),
                        pltpu.VMEM((nb, SUBLANES, W), F32)],
        compiler_params=_cparams(("parallel", "arbitrary")),
        name="rglru",
    )(z, z, buf8, h0.reshape(B, 1, W), conv_w, row(conv_b), wa_bd, wx_bd, row(b_a), row(b_x), row(lam))
    return hr, h1.reshape(B, W), buf1


def _rwkv_pre_kernel(x_ref, sh_ref, sc_ref, prev0_ref, mu_ref, wr_ref, wk_ref, wv_ref, w0_ref, w1_ref,
                     w2_ref, a0_ref, a1_ref, a2_ref, g1_ref, g2_ref, kk_ref, ka_ref,
                     r_out, lw_out, k_out, v_out, kk_out, a_out, g_out, shift_out, carry_scr, *, ns):
    s = pl.program_id(1)
    bb, ts, _ = x_ref.shape
    tm = bb * ts

    @pl.when(s == 0)
    def _():
        carry_scr[...] = prev0_ref[...]

    h = x_ref[...] * (1.0 + sc_ref[...]) + sh_ref[...]
    t_idx = lax.broadcasted_iota(jnp.int32, h.shape, 1)
    h_prev = jnp.where(t_idx == 0, carry_scr[...], pltpu.roll(h, 1, 1))
    last = h[:, ts - 1:ts, :]
    carry_scr[...] = last
    dx = (h_prev - h).reshape(tm, D_MODEL)
    h2 = h.reshape(tm, D_MODEL)
    mu = mu_ref[...]
    mix = lambda j: (h2 + dx * mu[j:j + 1, :]).astype(BF)
    xr, xw, xk, xv, xa, xg = [mix(j) for j in range(6)]

    w_mid = _dot(xw, w1_ref[...])
    a_mid = _dot(xa, a1_ref[...])
    g_mid = _dot(xg, g1_ref[...])
    r = _dot(xr, wr_ref[...])
    k = _dot(xk, wk_ref[...])
    v = _dot(xv, wv_ref[...])
    wl = w0_ref[...] + _dot(jnp.tanh(w_mid).astype(BF), w2_ref[...])
    log_decay = -DECAY_SCALE * jax.nn.sigmoid(wl)
    a = jax.nn.sigmoid(a0_ref[...] + _dot(a_mid.astype(BF), a2_ref[...]))
    g = _dot(jax.nn.sigmoid(g_mid).astype(BF), g2_ref[...])
    kk = k * kk_ref[...]
    k = k * (1.0 + (a - 1.0) * ka_ref[...])

    shp = x_ref.shape
    r_out[...] = r.reshape(shp)
    lw_out[...] = log_decay.reshape(shp)
    k_out[...] = k.reshape(shp)
    v_out[...] = v.reshape(shp)
    kk_out[...] = kk.reshape(shp)
    a_out[...] = a.reshape(shp)
    g_out[...] = g.reshape(shp)

    @pl.when(s == ns - 1)
    def _():
        shift_out[...] = last


def rwkv_pre(x, mod, prev0, P):
    B, S, _ = x.shape
    bb, ts = _row_tile(B, S, 512)
    ns = S // ts
    xmap = lambda i, s: (i, s, 0)
    full = _resident
    tok = pl.BlockSpec((bb, ts, D_MODEL), xmap)
    st = pl.BlockSpec((bb, 1, D_MODEL), lambda i, s: (i, 0, 0))
    sq, lo_in, lo_out, vec = (D_MODEL, D_MODEL), (D_MODEL, LORA_PAD), (LORA_PAD, D_MODEL), (1, D_MODEL)
    outs = pl.pallas_call(
        functools.partial(_rwkv_pre_kernel, ns=ns),
        out_shape=tuple([jax.ShapeDtypeStruct(x.shape, F32)] * 7
                        + [jax.ShapeDtypeStruct((B, 1, D_MODEL), F32)]),
        grid=(B // bb, ns),
        in_specs=[
            tok, _mod_spec(bb, 3), _mod_spec(bb, 4), st, full((6, D_MODEL)),
            full(sq), full(sq), full(sq), full(vec), full(lo_in), full(lo_out),
            full(vec), full(lo_in), full(lo_out), full(lo_in), full(lo_out),
            full(vec), full(vec),
        ],
        out_specs=tuple([tok] * 7 + [st]),
        scratch_shapes=[pltpu.VMEM((bb, 1, D_MODEL), F32)],
        compiler_params=_cparams(("parallel", "arbitrary")),
        name="rwkv_pre",
    )(x, mod, mod, prev0.reshape(B, 1, D_MODEL), P['mu'], P['wr'], P['wk'], P['wv'], P['w0'], P['w1'],
      P['w2'], P['a0'], P['a1'], P['a2'], P['g1'], P['g2'], P['k_k'], P['k_a'])
    return outs


def _stack_heads(x, first):
    return jnp.concatenate([jnp.where(first, x, 0.0), jnp.where(first, 0.0, x)], axis=0)


def _seg_sum(x, first):
    s0 = jnp.sum(jnp.where(first, x, 0.0), axis=1, keepdims=True)
    s1 = jnp.sum(jnp.where(first, 0.0, x), axis=1, keepdims=True)
    return jnp.where(first, s0, s1)


def _wkv_pair_chunks(toks, sps, vecs, consts, L, tall_state):
    tri, first, strict, incl, own = consts
    L2 = 2 * L
    hs = RWKV_HS
    rng = range(len(toks))
    wide = L2 % LANES == 0
    if tall_state:
        tall = lambda x: jnp.concatenate([x[:, :hs], x[:, hs:]], axis=0)
        keep_own = lambda x: jnp.where(own, x, 0.0)
    else:
        tall = lambda x: _stack_heads(x, first)
        keep_own = lambda x: x
    splits = [_split3(t[1]) for t in toks]
    c_incl = [_dot(tri, s[0]) + _dot(tri, s[1]) + _dot(tri, s[2]) for s in splits]
    e_inv = [jnp.exp(-c) for c in c_incl]
    kn = [t[4] / jnp.maximum(jnp.sqrt(_seg_sum(t[4] * t[4], first)), 1e-12) for t in toks]
    la = [tall(-kn[p] * jnp.exp(c_incl[p] - toks[p][1])).astype(BF) for p in rng]
    lr = [tall(toks[p][0] * jnp.exp(c_incl[p])).astype(BF) for p in rng]
    rb = [tall(kn[p] * toks[p][5] * e_inv[p]).astype(BF) for p in rng]
    rk = [tall(toks[p][2] * e_inv[p]).astype(BF) for p in rng]
    vs = [_stack_heads(toks[p][3], first).astype(BF) for p in rng]
    rbk = [jnp.concatenate([rb[p], rk[p]], axis=0) for p in rng]

    if wide:
        nn = [_dot_nt(la[p], rbk[p]) for p in rng]
        mm = [_dot_nt(lr[p], rbk[p]) for p in rng]
        n_raw = [x[:, :L2] for x in nn]
        n_ak = [jnp.where(strict, x[:, L2:], 0.0).astype(BF) for x in nn]
        m_bk = [jnp.concatenate([jnp.where(incl, x[:, :L2], 0.0), jnp.where(incl, x[:, L2:], 0.0)],
                                axis=1).astype(BF) for x in mm]
    else:
        n_raw = [_dot_nt(la[p], rb[p]) for p in rng]
        n_ak = [jnp.where(strict, _dot_nt(la[p], rk[p]), 0.0).astype(BF) for p in rng]
        m_rb = [jnp.where(incl, _dot_nt(lr[p], rb[p]), 0.0).astype(BF) for p in rng]
        m_rk = [jnp.where(incl, _dot_nt(lr[p], rk[p]), 0.0).astype(BF) for p in rng]

    row = lax.broadcasted_iota(jnp.int32, (L2, L2), 0)
    col = lax.broadcasted_iota(jnp.int32, (L2, L2), 1)
    base = (row // INV_BASE == col // INV_BASE) & (col < row)
    xb = [jnp.where(base, x, 0.0).astype(BF) for x in n_raw]
    tinv = [jnp.where(row == col, 1.0, 0.0) + x.astype(F32) for x in xb]
    x2 = [_dot(x, x).astype(BF) for x in xb]
    if wide:
        prod = [_dot(x2[p], jnp.concatenate([x2[p], tinv[p].astype(BF)], axis=1)) for p in rng]
        tinv = [tinv[p] + prod[p][:, L2:] for p in rng]
        x4 = [x[:, :L2].astype(BF) for x in prod]
    else:
        tinv = [tinv[p] + _dot(x2[p], tinv[p].astype(BF)) for p in rng]
        x4 = [_dot(x, x).astype(BF) for x in x2]
    tinv = [tinv[p] + _dot(x4[p], tinv[p].astype(BF)) for p in rng]
    blk = INV_BASE
    while blk < L:
        off = (row // (2 * blk) == col // (2 * blk)) & (row // blk == col // blk + 1)
        tb = [t.astype(BF) for t in tinv]
        pr = [_dot(jnp.where(off, n_raw[p], 0.0).astype(BF), tb[p]).astype(BF) for p in rng]
        tinv = [tinv[p] + _dot(tb[p], pr[p]) for p in rng]
        blk *= 2

    spb = [s.astype(BF) for s in sps]
    rhs = [keep_own(_dot_nt(la[p], spb[p])) + _dot(n_ak[p], vs[p]) for p in rng]
    ub = [_dot(tinv[p].astype(BF), rhs[p].astype(BF)).astype(BF) for p in rng]

    uv = [jnp.concatenate([ub[p], vs[p]], axis=0) for p in rng]
    y0 = [keep_own(_dot_nt(lr[p], spb[p])) for p in rng]
    if wide:
        ys = [y0[p] + _dot(m_bk[p], uv[p]) for p in rng]
    else:
        ys = [y0[p] + _dot(m_rb[p], ub[p]) + _dot(m_rk[p], vs[p]) for p in rng]
    decay = [jnp.exp(c_incl[p][L - 1:L, :]) for p in rng]
    if tall_state:
        upper = lax.broadcasted_iota(jnp.int32, (2 * hs, hs), 0) < hs
        decay = [jnp.where(upper, d[:, :hs], d[:, hs:]) for d in decay]
    sp_new = [(sps[p] + _dot_tn(uv[p], rbk[p])) * decay[p] for p in rng]

    outs = []
    for p in rng:
        r, _, k, v, _, _, gate = toks[p]
        r_k, lnx_g, lnx_b = vecs[p]
        y = ys[p][0:L, :] + ys[p][L:L2, :]
        yc = y - _seg_sum(y, first) * (1.0 / RWKV_HS)
        yv = _seg_sum(yc * yc, first) * (1.0 / RWKV_HS)
        yn = yc * lax.rsqrt(yv + RWKV_LN_EPS) * lnx_g + lnx_b
        outs.append(((yn + _seg_sum(r * k * r_k, first) * v) * gate, sp_new[p]))
    return outs


def _wkv_kernel(r_ref, lw_ref, k_ref, v_ref, kk_ref, a_ref, g_ref, s0_ref, rk_ref, lxg_ref, lxb_ref,
                y_out, s_out, sp_scr, *, L, nc, bb, group):
    c = pl.program_id(1)
    hs = RWKV_HS
    npair = RWKV_HEADS // 2
    L2 = 2 * L

    tall_state = nc == 1

    @pl.when(c == 0)
    def _():
        if tall_state:
            sp_scr[...] = s0_ref[...].reshape(sp_scr.shape)
        else:
            zero = jnp.zeros((hs, hs), F32)

            def init(i, carry):
                for p in range(npair):
                    top = jnp.concatenate([s0_ref[i, 2 * p], zero], axis=1)
                    bot = jnp.concatenate([zero, s0_ref[i, 2 * p + 1]], axis=1)
                    sp_scr[i, p] = jnp.concatenate([top, bot], axis=0)
                return carry

            lax.fori_loop(0, bb, init, 0)

    row = lax.broadcasted_iota(jnp.int32, (L, L), 0)
    col = lax.broadcasted_iota(jnp.int32, (L, L), 1)
    tri = jnp.where(col <= row, 1.0, 0.0).astype(BF)
    first = lax.broadcasted_iota(jnp.int32, (L, LANES), 1) < hs
    row2 = lax.broadcasted_iota(jnp.int32, (L2, L2), 0)
    col2 = lax.broadcasted_iota(jnp.int32, (L2, L2), 1)
    same = (row2 >= L) == (col2 >= L)
    own = ((lax.broadcasted_iota(jnp.int32, (L2, LANES), 0) >= L)
           == (lax.broadcasted_iota(jnp.int32, (L2, LANES), 1) >= hs))
    consts = (tri, first, same & (col2 < row2), same & (col2 <= row2), own)
    sls = [slice(p * LANES, (p + 1) * LANES) for p in range(npair)]

    def seqs(j, carry):
        ids = [(j * group + g, p) for g in range(group) for p in range(npair)]
        toks = [tuple(ref[i, :, sls[p]] for ref in (r_ref, lw_ref, k_ref, v_ref, kk_ref, a_ref, g_ref))
                for i, p in ids]
        vecs = [(rk_ref[:, sls[p]], lxg_ref[:, sls[p]], lxb_ref[:, sls[p]]) for _, p in ids]
        outs = _wkv_pair_chunks(toks, [sp_scr[i, p] for i, p in ids], vecs, consts, L, tall_state)
        for (i, p), (y, sp_new) in zip(ids, outs):
            y_out[i, :, sls[p]] = y
            sp_scr[i, p] = sp_new
        return carry

    lax.fori_loop(0, bb // group, seqs, 0)

    @pl.when(c == nc - 1)
    def _():
        if tall_state:
            s_out[...] = sp_scr[...].reshape(s_out.shape)
        else:
            def fin(i, carry):
                for p in range(npair):
                    sp = sp_scr[i, p]
                    s_out[i, 2 * p] = sp[0:hs, 0:hs]
                    s_out[i, 2 * p + 1] = sp[hs:2 * hs, hs:2 * hs]
                return carry

            lax.fori_loop(0, bb, fin, 0)


def wkv_recurrence(r, lw, k, v, kk, a, g, s0, P):
    B, S, _ = r.shape
    L = RWKV_CHUNK if S % RWKV_CHUNK == 0 else S
    assert L & (L - 1) == 0 and L % SUBLANES == 0
    nc = S // L
    bb, group = (min(B, 4), min(B, 4)) if nc > 1 else (min(B, 8), min(B, 8))
    assert B % bb == 0 and bb % group == 0
    npair = RWKV_HEADS // 2
    tok = pl.BlockSpec((bb, L, D_MODEL), lambda b, c: (b, c, 0))
    st = pl.BlockSpec((bb, RWKV_HEADS, RWKV_HS, RWKV_HS), lambda b, c: (b, 0, 0, 0))
    vec = pl.BlockSpec((1, D_MODEL), lambda b, c: (0, 0))
    return pl.pallas_call(
        functools.partial(_wkv_kernel, L=L, nc=nc, bb=bb, group=group),
        out_shape=(jax.ShapeDtypeStruct(r.shape, F32), jax.ShapeDtypeStruct(s0.shape, F32)),
        grid=(B // bb, nc),
        in_specs=[tok] * 7 + [st, vec, vec, vec],
        out_specs=(tok, st),
        scratch_shapes=[pltpu.VMEM((bb, npair, 2 * RWKV_HS, RWKV_HS if nc == 1 else 2 * RWKV_HS), F32)],
        compiler_params=_cparams(("parallel", "arbitrary")),
        name="wkv",
    )(r, lw, k, v, kk, a, g, s0, P['r_k'], P['lnx_g'], P['lnx_b'])


def _block_diag(w):
    G, n, _ = w.shape
    eye = jnp.eye(G, dtype=w.dtype)
    return (eye[:, None, :, None] * w[:, :, None, :]).reshape(G * n, G * n)


def _pad_cols(w, n):
    return jnp.pad(w, ((0, 0), (0, n - w.shape[1])))


def _pad_rows(w, n):
    return jnp.pad(w, ((0, n - w.shape[0]), (0, 0)))


def _prep_ab(j, p):
    w_in = p['ab_w_in'][j]
    n_main = 4 * MLSTM_WIDTH
    n_gate = 2 * MLSTM_HEADS
    w_all = jnp.concatenate([w_in[:, :n_main], w_in[:, n_main + n_gate:],
                             w_in[:, n_main:n_main + n_gate]], axis=1)
    return dict(
        w_in=_pad_cols(w_all, Z_COLS).astype(BF),
        b_gates=p['mlstm_b_gates'][j], norm_g=p['mlstm_norm_g'][j],
        conv_w=p['rg_conv_w'][j], conv_b=p['rg_conv_b'][j],
        wa=_block_diag(p['rg_w_a'][j]).astype(BF), wx=_block_diag(p['rg_w_x'][j]).astype(BF),
        b_a=p['rg_b_a'][j], b_x=p['rg_b_x'][j], lam=p['rg_lambda'][j],
        w_out_m=p['ab_w_out'][j][:MLSTM_WIDTH].astype(BF), w_out_r=p['ab_w_out'][j][MLSTM_WIDTH:].astype(BF),
    )


def _prep_rwkv(j, p):
    vec = lambda a: a.reshape(1, D_MODEL)
    return dict(
        mu=p['rw_mu'][j],
        wr=p['rw_wr'][j].astype(BF), wk=p['rw_wk'][j].astype(BF), wv=p['rw_wv'][j].astype(BF),
        w0=vec(p['rw_w0'][j]), w1=_pad_cols(p['rw_w1'][j], LORA_PAD).astype(BF),
        w2=_pad_rows(p['rw_w2'][j], LORA_PAD).astype(BF),
        a0=vec(p['rw_a0'][j]), a1=_pad_cols(p['rw_a1'][j], LORA_PAD).astype(BF),
        a2=_pad_rows(p['rw_a2'][j], LORA_PAD).astype(BF),
        g1=_pad_cols(p['rw_g1'][j], LORA_PAD).astype(BF), g2=_pad_rows(p['rw_g2'][j], LORA_PAD).astype(BF),
        k_k=vec(p['rw_k_k'][j]), k_a=vec(p['rw_k_a'][j]), r_k=vec(p['rw_r_k'][j]),
        lnx_g=vec(p['rw_lnx_g'][j]), lnx_b=vec(p['rw_lnx_b'][j]), wo=p['rw_wo'][j].astype(BF),
    )


def ab_mixer(x, mod, st, A):
    mC, mn, mm, rh, rconv = st
    z, gates = mod_matmul(x, mod, A['w_in'], 4 * MLSTM_WIDTH + 2 * RG_WIDTH, 2 * MLSTM_HEADS)
    hm, C1, n1, m1 = mlstm_mixer(z, gates, mC, mn, mm, A['b_gates'], A['norm_g'])
    hr, rh1, buf1 = rglru_mixer(z, rh, rconv, A['conv_w'], A['conv_b'], A['wa'], A['wx'],
                                A['b_a'], A['b_x'], A['lam'])
    return ((hm, 0), (hr, 0), A['w_out_m'], A['w_out_r']), (C1, n1, m1, rh1, buf1)


def rwkv_mixer(x, mod, st, R):
    wkv0, prev0 = st
    r, lw, k, v, kk, a, g, shift = rwkv_pre(x, mod, prev0, R)
    y, wkv1 = wkv_recurrence(r, lw, k, v, kk, a, g, wkv0, R)
    half = D_MODEL // 2
    return ((y, 0), (y, 1), R['wo'][:half], R['wo'][half:]), (wkv1, shift.reshape(shift.shape[0], D_MODEL))


def run_trunk(x, mods, states, W):
    mC, mn, mm, rh, rconv, wkv, shift = states
    new_ab, new_c = [], []
    for layer in range(DEPTH):
        mod = mods[layer]
        lg = lambda i: W['ln_g'][layer, i].reshape(1, D_MODEL)
        lb = lambda i: W['ln_b'][layer, i].reshape(1, D_MODEL)
        f = W['ffn']
        x = ffn_block(x, mod, 0, layer, 0, f[0], f[1], f[2], lg(0), lb(0))
        j = layer // 2
        if layer % 2 == 0:
            mixed, st = ab_mixer(x, mod, (mC[j], mn[j], mm[j], rh[j], rconv[j]), W['ab'][j])
            new_ab.append(st)
        else:
            mixed, st = rwkv_mixer(x, mod, (wkv[j], shift[j]), W['rwkv'][j])
            new_c.append(st)
        x = ffn_block(x, mod, 2, layer, 1, f[0], f[1], f[2], lg(2), lb(2), mixer=mixed + (lg(1), lb(1)))
    stk = lambda sts, i: sts[0][i][None] if len(sts) == 1 else jnp.stack([s[i] for s in sts], axis=0)
    return x, (stk(new_ab, 0), stk(new_ab, 1), stk(new_ab, 2), stk(new_ab, 3), stk(new_ab, 4),
               stk(new_c, 0), stk(new_c, 1))


def _zero_states(B):
    n_ab, n_c = (DEPTH + 1) // 2, DEPTH // 2
    return (jnp.zeros((n_ab, B, MLSTM_HEADS, MLSTM_HD, MLSTM_HD), F32),
            jnp.zeros((n_ab, B, MLSTM_HEADS, MLSTM_HD), F32),
            jnp.zeros((n_ab, B, MLSTM_HEADS), F32),
            jnp.zeros((n_ab, B, RG_WIDTH), F32),
            jnp.zeros((n_ab, B, RG_CONV - 1, RG_WIDTH), F32),
            jnp.zeros((n_c, B, RWKV_HEADS, RWKV_HS, RWKV_HS), F32),
            jnp.zeros((n_c, B, D_MODEL), F32))


def kernel(x_prompt, x_sample, c_prompt, c_sample, state_mlstm_C, state_mlstm_n, state_mlstm_m, state_rglru_h, state_rglru_conv, state_rwkv_wkv, state_rwkv_shift, ada_w, ada_b, ln_g, ln_b, ffn_w1, ffn_w3, ffn_w2, ab_w_in, mlstm_b_gates, mlstm_norm_g, rg_conv_w, rg_conv_b, rg_w_a, rg_b_a, rg_w_x, rg_b_x, rg_lambda, ab_w_out, rw_mu, rw_wr, rw_wk, rw_wv, rw_w0, rw_w1, rw_w2, rw_a0, rw_a1, rw_a2, rw_g1, rw_g2, rw_k_k, rw_k_a, rw_r_k, rw_lnx_g, rw_lnx_b, rw_wo):
    p = dict(ab_w_in=ab_w_in, mlstm_b_gates=mlstm_b_gates, mlstm_norm_g=mlstm_norm_g,
             rg_conv_w=rg_conv_w, rg_conv_b=rg_conv_b, rg_w_a=rg_w_a, rg_b_a=rg_b_a, rg_w_x=rg_w_x,
             rg_b_x=rg_b_x, rg_lambda=rg_lambda, ab_w_out=ab_w_out, rw_mu=rw_mu, rw_wr=rw_wr,
             rw_wk=rw_wk, rw_wv=rw_wv, rw_w0=rw_w0, rw_w1=rw_w1, rw_w2=rw_w2, rw_a0=rw_a0,
             rw_a1=rw_a1, rw_a2=rw_a2, rw_g1=rw_g1, rw_g2=rw_g2, rw_k_k=rw_k_k, rw_k_a=rw_k_a,
             rw_r_k=rw_r_k, rw_lnx_g=rw_lnx_g, rw_lnx_b=rw_lnx_b, rw_wo=rw_wo)
    W = dict(
        ln_g=ln_g, ln_b=ln_b,
        ffn=(ffn_w1.astype(BF), ffn_w3.astype(BF), ffn_w2.astype(BF)),
        ab=[_prep_ab(j, p) for j in range((DEPTH + 1) // 2)],
        rwkv=[_prep_rwkv(j, p) for j in range(DEPTH // 2)],
    )
    Bp, Bs = x_prompt.shape[0], x_sample.shape[0]
    mod_all = adaln(jnp.concatenate([c_prompt, c_sample], axis=0), ada_w, ada_b)
    mods_p = [mod_all[l, :Bp] for l in range(DEPTH)]
    mods_s = [mod_all[l, Bp:] for l in range(DEPTH)]
    y_prompt, sp = run_trunk(x_prompt, mods_p, _zero_states(Bp), W)
    y_sample, ss = run_trunk(x_sample, mods_s,
                             (state_mlstm_C, state_mlstm_n, state_mlstm_m, state_rglru_h,
                              state_rglru_conv, state_rwkv_wkv, state_rwkv_shift), W)
    return (y_prompt, y_sample) + tuple(sp) + tuple(ss)
```

```python
import functools
import math

import jax
import jax.numpy as jnp
from jax import lax
from jax.experimental import pallas as pl
from jax.experimental.pallas import tpu as pltpu

D_MODEL = 1024
DEPTH = 2
N_SUB = 3
MLSTM_WIDTH = 512
MLSTM_HEADS = 4
MLSTM_HD = 128
MLSTM_CHUNK = 128
RG_WIDTH = 512
RG_BLOCKS = 8
RG_BD = 64
RG_CONV = 4
RG_C = 8.0
RWKV_HS = 64
RWKV_HEADS = 16
RWKV_LN_EPS = 64e-5
RWKV_CHUNK = 64
DECAY_SCALE = math.exp(-0.5)
INV_BASE = 8
D_FF = 2816
ALPHA = (2.0 * DEPTH) ** 0.25
LN_EPS = 1e-5
HEAD_NORM_EPS = 1e-6

LANES = 128
SUBLANES = 8
VMEM_LIMIT_BYTES = 56 * 1024 * 1024
Z_COLS = 3200
LORA_PAD = 128

BF = jnp.bfloat16
F32 = jnp.float32

NT_DIMS = (((1,), (1,)), ((), ()))
TN_DIMS = (((0,), (0,)), ((), ()))


def _cparams(sem):
    return pltpu.CompilerParams(dimension_semantics=sem, vmem_limit_bytes=VMEM_LIMIT_BYTES)


def _dot(a, b):
    return jnp.dot(a, b, preferred_element_type=F32)


def _dot_nt(a, b):
    return lax.dot_general(a, b, NT_DIMS, preferred_element_type=F32)


def _dot_tn(a, b):
    return lax.dot_general(a, b, TN_DIMS, preferred_element_type=F32)


def _layer_norm(z, g, b):
    mu = jnp.mean(z, axis=-1, keepdims=True)
    zc = z - mu
    var = jnp.mean(zc * zc, axis=-1, keepdims=True)
    return zc * lax.rsqrt(var + LN_EPS) * g + b


def _row_tile(B, S, rows):
    if S >= rows:
        assert S % rows == 0
        return 1, rows
    bb = min(B, rows // S)
    assert B % bb == 0
    return bb, S


def _split2(x):
    hi = x.astype(BF)
    lo = (x - hi.astype(F32)).astype(BF)
    return hi, lo


def _split3(x):
    hi = x.astype(BF)
    r1 = x - hi.astype(F32)
    mid = r1.astype(BF)
    lo = (r1 - mid.astype(F32)).astype(BF)
    return hi, mid, lo


def _adaln_kernel(c_ref, w_ref, b_ref, o_ref):
    h = jax.nn.silu(c_ref[...]).astype(BF)
    res = _dot(h, w_ref[...].astype(BF)) + b_ref[...]
    o_ref[...] = res.reshape(o_ref.shape)


def adaln(c_all, ada_w, ada_b):
    Bc = c_all.shape[0]
    n_chunk = ada_w.shape[-1] // D_MODEL
    return pl.pallas_call(
        _adaln_kernel,
        out_shape=jax.ShapeDtypeStruct((DEPTH, Bc, n_chunk, 1, D_MODEL), F32),
        grid=(DEPTH, n_chunk),
        in_specs=[
            pl.BlockSpec((Bc, D_MODEL), lambda l, j: (0, 0)),
            pl.BlockSpec((None, D_MODEL, D_MODEL), lambda l, j: (l, 0, j)),
            pl.BlockSpec((None, 1, D_MODEL), lambda l, j: (l, 0, j)),
        ],
        out_specs=pl.BlockSpec((None, Bc, None, 1, D_MODEL), lambda l, j: (l, 0, j, 0, 0)),
        compiler_params=_cparams(("parallel", "parallel")),
        name="adaln",
    )(c_all, ada_w, ada_b.reshape(DEPTH, 1, n_chunk * D_MODEL))


def _mod_spec(bb, k):
    return pl.BlockSpec((bb, None, 1, D_MODEL), lambda i, *_: (i, k, 0, 0))


FFN_ROW_SPLITS = 2


def _ffn_rows(x, mixer, sh, sc, gt, w1_ref, w3_ref, w2_ref, lg, lb, res_w):
    bb, ts, _ = x.shape
    if mixer is not None:
        mgt, m1, m2, mw1_ref, mw2_ref, mlg, mlb = mixer
        flat = lambda m: m.reshape(bb * ts, -1).astype(BF)
        ym = _dot(flat(m1), mw1_ref[...]) + _dot(flat(m2), mw2_ref[...])
        yield
        x = _layer_norm(ALPHA * x + (1.0 + mgt) * ym.reshape(bb, ts, D_MODEL), mlg, mlb)
    h = (x * (1.0 + sc) + sh).reshape(bb * ts, D_MODEL).astype(BF)
    a = _dot(h, w1_ref[...])
    b = _dot(h, w3_ref[...])
    yield
    g = (jax.nn.silu(a) * b).astype(BF)
    y = _dot(g, w2_ref[...])
    yield
    return _layer_norm(ALPHA * x + (res_w * (1.0 + gt)) * y.reshape(bb, ts, D_MODEL), lg, lb)


def _ffn_kernel(*refs, res_w, after_mixer):
    if after_mixer:
        (x_ref, mgt_ref, m1_ref, m2_ref, mw1_ref, mw2_ref, mlg_ref, mlb_ref,
         sh_ref, sc_ref, gt_ref, w1_ref, w3_ref, w2_ref, lg_ref, lb_ref, o_ref) = refs
    else:
        x_ref, sh_ref, sc_ref, gt_ref, w1_ref, w3_ref, w2_ref, lg_ref, lb_ref, o_ref = refs
    bb, ts, _ = x_ref.shape
    if bb > 1:
        n = bb // FFN_ROW_SPLITS
        tok = lambda ref, i: ref[i * n:(i + 1) * n]
        per_seq = tok
    else:
        n = ts // FFN_ROW_SPLITS
        tok = lambda ref, i: ref[:, i * n:(i + 1) * n, :]
        per_seq = lambda ref, i: ref[...]
    gens = []
    for i in range(FFN_ROW_SPLITS):
        mixer = None
        if after_mixer:
            mixer = (per_seq(mgt_ref, i), tok(m1_ref, i), tok(m2_ref, i), mw1_ref, mw2_ref, mlg_ref[...], mlb_ref[...])
        gens.append(_ffn_rows(tok(x_ref, i), mixer, per_seq(sh_ref, i), per_seq(sc_ref, i), per_seq(gt_ref, i),
                              w1_ref, w3_ref, w2_ref, lg_ref[...], lb_ref[...], res_w))
    for i, out in enumerate(_lockstep(gens)):
        if bb > 1:
            o_ref[i * n:(i + 1) * n] = out
        else:
            o_ref[:, i * n:(i + 1) * n, :] = out


def _resident(shape):
    return pl.BlockSpec(shape, lambda *_: (0,) * len(shape), pipeline_mode=pl.Buffered(1))


def ffn_block(x, mod, sub, layer, half, w1, w3, w2, lg, lb, mixer=None):
    B, S, _ = x.shape
    bb, ts = _row_tile(B, S, 1024 if mixer is None else 512)
    xmap = lambda i, s: (i, s, 0)
    wspec = lambda r, c: pl.BlockSpec((None, None, r, c), lambda i, s: (layer, half, 0, 0),
                                      pipeline_mode=pl.Buffered(1))
    vec = _resident((1, D_MODEL))
    specs = [pl.BlockSpec((bb, ts, D_MODEL), xmap)]
    args = [x]
    if mixer is not None:
        (m1, blk1), (m2, blk2), mw1, mw2, mlg, mlb = mixer
        mspec = lambda w, blk: pl.BlockSpec((bb, ts, w.shape[0]), lambda i, s: (i, s, blk))
        specs += [_mod_spec(bb, 5), mspec(mw1, blk1), mspec(mw2, blk2),
                  _resident(mw1.shape), _resident(mw2.shape), vec, vec]
        args += [mod, m1, m2, mw1, mw2, mlg, mlb]
    specs += [_mod_spec(bb, 3 * sub), _mod_spec(bb, 3 * sub + 1), _mod_spec(bb, 3 * sub + 2),
              wspec(D_MODEL, D_FF), wspec(D_MODEL, D_FF), wspec(D_FF, D_MODEL), vec, vec]
    args += [mod, mod, mod, w1, w3, w2, lg, lb]
    return pl.pallas_call(
        functools.partial(_ffn_kernel, res_w=0.5, after_mixer=mixer is not None),
        out_shape=jax.ShapeDtypeStruct(x.shape, F32),
        grid=(B // bb, S // ts),
        in_specs=specs,
        out_specs=pl.BlockSpec((bb, ts, D_MODEL), xmap),
        compiler_params=_cparams(("parallel", "parallel")),
        name="ffn_block",
    )(*args)


def _modmm_kernel(x_ref, sh_ref, sc_ref, w_ref, o_ref, gate_ref):
    bb, ts, _ = x_ref.shape
    n_main = o_ref.shape[-1]
    n_gate = gate_ref.shape[-1]
    h = (x_ref[...] * (1.0 + sc_ref[...]) + sh_ref[...]).reshape(bb * ts, D_MODEL).astype(BF)
    o_ref[...] = _dot(h, w_ref[:, :n_main]).reshape(o_ref.shape)
    gate_ref[...] = _dot(h, w_ref[:, n_main:])[:, :n_gate].reshape(gate_ref.shape)


def mod_matmul(x, mod, w, n_main, n_gate):
    B, S, _ = x.shape
    bb, ts = _row_tile(B, S, 1024)
    xmap = lambda i, s: (i, s, 0)
    return pl.pallas_call(
        _modmm_kernel,
        out_shape=(jax.ShapeDtypeStruct((B, S, n_main), F32), jax.ShapeDtypeStruct((B, S, n_gate), F32)),
        grid=(B // bb, S // ts),
        in_specs=[
            pl.BlockSpec((bb, ts, D_MODEL), xmap),
            _mod_spec(bb, 3), _mod_spec(bb, 4),
            _resident(w.shape),
        ],
        out_specs=(pl.BlockSpec((bb, ts, n_main), xmap), pl.BlockSpec((bb, ts, n_gate), xmap)),
        compiler_params=_cparams(("parallel", "parallel")),
        name="mod_matmul",
    )(x, mod, mod, w)


def _log_sigmoid(x):
    return jnp.minimum(x, 0.0) - jnp.log1p(jnp.exp(-jnp.abs(x)))


def _lockstep(gens):
    results = [None] * len(gens)
    live = list(range(len(gens)))
    while live:
        still = []
        for idx in live:
            try:
                next(gens[idx])
                still.append(idx)
            except StopIteration as stop:
                results[idx] = stop.value
        live = still
    return results


def _dot_exact(a, b, dims):
    return lax.dot_general(a, b, dims, precision=lax.Precision.HIGHEST, preferred_element_type=F32)


def _mlstm_gate_views(gl, L):
    row = lax.broadcasted_iota(jnp.int32, (L, L), 0)
    col = lax.broadcasted_iota(jnp.int32, (L, L), 1)
    eye = jnp.where(row == col, 1.0, 0.0)
    lower = jnp.where(col <= row, 1.0, 0.0)
    upper = jnp.where(row <= col, 1.0, 0.0)
    gl_t = _dot_exact(gl, eye, TN_DIMS)
    cum_c = _dot_exact(lower, gl, (((1,), (0,)), ((), ())))
    cum_r = _dot_exact(gl, upper, TN_DIMS)
    return gl_t, cum_c, cum_r


def _mlstm_head_chunk(q, k, v, o, ig_c, b_c, ig_r, b_r, C0, n0, m0, g, L):
    d = MLSTM_HD
    spread = lambda colv, n: jnp.broadcast_to(colv, (L, n))

    def both(colv):
        over_l = spread(colv, L)
        return over_l, (over_l if L == d else spread(colv, d))

    ones_l = jnp.ones((L, d), BF)
    ones_d = jnp.ones((d, d), BF)

    on_mxu = L >= d

    def row_sum(x, ones):
        if not on_mxu:
            return spread(jnp.sum(x, axis=1, keepdims=True), d), x.astype(BF)
        hi, lo = _split2(x)
        return _dot(hi, ones) + _dot(lo, ones), hi

    row = lax.broadcasted_iota(jnp.int32, (L, L), 0)
    col = lax.broadcasted_iota(jnp.int32, (L, L), 1)
    causal = col <= row
    bc_ll = spread(b_c, L)
    dmat = jnp.where(causal, bc_ll - b_r + ig_r, -jnp.inf)
    m_inter = b_c + m0
    m = jnp.maximum(m_inter, jnp.max(dmat, axis=1, keepdims=True))
    m_ll, m_ld = both(m)
    p = jnp.exp(dmat - m_ll)
    w_inter = jnp.exp(spread(m_inter, d) - m_ld)

    k = k * (d ** -0.5)
    qb = q.astype(BF)
    kb = k.astype(BF)
    if on_mxu:
        q_hi, q_lo = _split2(q)
        n_hi, n_lo = _split2(jnp.broadcast_to(n0, (d, d)))
    yield
    s_raw = _dot_nt(qb, kb)
    q_c0 = _dot_nt(qb, C0.astype(BF))
    if on_mxu:
        q_n0 = _dot_nt(q_hi, n_hi) + _dot_nt(q_lo, n_hi) + _dot_nt(q_hi, n_lo)
    else:
        q_n0 = spread(jnp.sum(q * n0, axis=1, keepdims=True), d)
    yield
    scores = s_raw * p
    score_sum, scores_b = row_sum(scores, ones_l)
    num = _dot(scores_b, v.astype(BF)) + w_inter * q_c0
    den = score_sum + w_inter * q_n0
    hh = num / jnp.maximum(jnp.abs(den), jnp.exp(-m_ld))

    m_end = m[L - 1:L, :]
    b_end = b_c[L - 1:L, :]
    w_state = jnp.exp(b_end + m0 - m_end)
    w_rows = spread(jnp.exp(b_end - b_c + ig_c - m_end), d)
    yield
    c_new = w_state * C0 + _dot_tn((w_rows * v).astype(BF), kb)
    n_new = w_state * n0 + jnp.sum(w_rows * k, axis=0, keepdims=True)
    mu = row_sum(hh, ones_d)[0] * (1.0 / d)
    yield
    hc = hh - mu
    var = row_sum(hc * hc, ones_d)[0] * (1.0 / d)
    hn = hc * lax.rsqrt(var + HEAD_NORM_EPS)
    return hn * g * jax.nn.sigmoid(o), c_new, n_new, m_end


def _mlstm_kernel(m0_ref, bg_ref, q_ref, k_ref, v_ref, o_ref, gc_ref, c0_ref, n0_ref, g_ref,
                  h_out, c_out, n_out, m_out, c_scr, n_scr, m_scr, *, L, nc, bb, group):
    ib = pl.program_id(0)
    c = pl.program_id(1)
    H = MLSTM_HEADS

    @pl.when(c == 0)
    def _():
        c_scr[...] = c0_ref[...]
        n_scr[...] = n0_ref[...]

        def init(i, carry):
            for h in range(H):
                m_scr[i, h] = jnp.full((1, LANES), m0_ref[(ib * bb + i) * H + h], F32)
            return carry

        lax.fori_loop(0, bb, init, 0)

    sls = [slice(h * MLSTM_HD, (h + 1) * MLSTM_HD) for h in range(H)]

    def seqs(j, carry):
        views = []
        for g in range(group):
            gates = gc_ref[j * group + g] + bg_ref[...]
            is_input = lax.broadcasted_iota(jnp.int32, gates.shape, 1) < H
            gl = jnp.where(is_input, gates, _log_sigmoid(gates))
            views.append((gl,) + _mlstm_gate_views(gl, L))
        ids = [(j * group + g, h) for g in range(group) for h in range(H)]
        gate_args = [(views[g][0][:, h:h + 1], views[g][2][:, H + h:H + h + 1],
                      views[g][1][h:h + 1, :], views[g][3][H + h:H + h + 1, :])
                     for g in range(group) for h in range(H)]
        heads = _lockstep([_mlstm_head_chunk(
            q_ref[i, :, sls[h]], k_ref[i, :, sls[h]], v_ref[i, :, sls[h]], o_ref[i, :, sls[h]], *gate_args[n],
            c_scr[i, h], n_scr[i, h], m_scr[i, h][:, 0:1], g_ref[:, sls[h]], L) for n, (i, h) in enumerate(ids)])
        for (i, h), (out, c_new, n_new, m_end) in zip(ids, heads):
            h_out[i, :, sls[h]] = out
            c_scr[i, h] = c_new
            n_scr[i, h] = n_new
            m_scr[i, h] = jnp.broadcast_to(m_end, (1, LANES))
        return carry

    lax.fori_loop(0, bb // group, seqs, 0)

    @pl.when(c == nc - 1)
    def _():
        c_out[...] = c_scr[...]
        n_out[...] = n_scr[...]
        m_out[...] = m_scr[...]


def mlstm_mixer(z, gates, C0, n0, m0, b_gates, norm_g):
    B, S, _ = z.shape
    H = MLSTM_HEADS
    W = MLSTM_WIDTH
    L = MLSTM_CHUNK if S % MLSTM_CHUNK == 0 else S
    nc = S // L
    bb, group = (1, 1) if nc > 1 else (min(B, 8), min(B, 4))
    assert B % bb == 0 and bb % group == 0
    zspec = lambda blk: pl.BlockSpec((bb, L, W), lambda b, c: (b, c, blk))
    smem = pl.BlockSpec(memory_space=pltpu.SMEM)
    hd_spec = pl.BlockSpec((bb, H, 1, MLSTM_HD), lambda b, c: (b, 0, 0, 0))
    c_spec = pl.BlockSpec((bb, H, MLSTM_HD, MLSTM_HD), lambda b, c: (b, 0, 0, 0))
    hm, C1, n1, m1 = pl.pallas_call(
        functools.partial(_mlstm_kernel, L=L, nc=nc, bb=bb, group=group),
        out_shape=(
            jax.ShapeDtypeStruct((B, S, W), F32),
            jax.ShapeDtypeStruct((B, H, MLSTM_HD, MLSTM_HD), F32),
            jax.ShapeDtypeStruct((B, H, 1, MLSTM_HD), F32),
            jax.ShapeDtypeStruct((B, H, 1, LANES), F32),
        ),
        grid=(B // bb, nc),
        in_specs=[
            smem, pl.BlockSpec((1, 2 * H), lambda b, c: (0, 0)),
            zspec(0), zspec(1), zspec(2), zspec(3),
            pl.BlockSpec((bb, L, 2 * H), lambda b, c: (b, c, 0)),
            c_spec, hd_spec,
            pl.BlockSpec((1, W), lambda b, c: (0, 0)),
        ],
        out_specs=(pl.BlockSpec((bb, L, W), lambda b, c: (b, c, 0)), c_spec, hd_spec, hd_spec),
        scratch_shapes=[pltpu.VMEM((bb, H, MLSTM_HD, MLSTM_HD), F32), pltpu.VMEM((bb, H, 1, MLSTM_HD), F32),
                        pltpu.VMEM((bb, H, 1, LANES), F32)],
        compiler_params=_cparams(("parallel", "arbitrary")),
        name="mlstm",
    )(m0.reshape(B * H), b_gates.reshape(1, 2 * H), z, z, z, z, gates,
      C0, n0.reshape(B, H, 1, MLSTM_HD), norm_g.reshape(1, W))
    return hm, C1, n1.reshape(B, H, MLSTM_HD), m1[:, :, 0, 0]


def _expm1(y):
    u = jnp.exp(y)
    small = jnp.where(u == 1.0, y, (u - 1.0) * y / jnp.log(u))
    return jnp.where(jnp.abs(y) > 0.5, u - 1.0, small)


def _rglru_kernel(xr_ref, gr_ref, buf_ref, h0_ref, cw_ref, cb_ref, wa_ref, wx_ref, ba_ref, bx_ref, lam_ref,
                  hr_out, h_out, buf_out, prev_scr, h_scr, a_scr, u_scr, c_scr, *, ns, chained):
    s = pl.program_id(1)
    bb, ts, W = xr_ref.shape
    nb = bb * ts // SUBLANES
    x = xr_ref[...].reshape(nb, SUBLANES, W)

    if chained:
        @pl.when(s == 0)
        def _():
            prev_scr[...] = buf_ref[0]
            h_scr[...] = h0_ref[0]

        prev = jnp.concatenate([prev_scr[...][None], x[:nb - 1]], axis=0)
        prev_scr[...] = x[nb - 1]
    else:
        prev = buf_ref[...]

    t_idx = lax.broadcasted_iota(jnp.int32, x.shape, 1)
    cw = cw_ref[...]
    xc = cb_ref[...] + cw[RG_CONV - 1:RG_CONV, :] * x
    for d in range(1, RG_CONV):
        back = jnp.where(t_idx >= d, pltpu.roll(x, d, 1), pltpu.roll(prev, d, 1))
        xc = xc + cw[RG_CONV - 1 - d:RG_CONV - d, :] * back

    xc2 = xc.reshape(nb * SUBLANES, W)
    xb = xc2.astype(BF)
    r = jax.nn.sigmoid(_dot(xb, wa_ref[...]) + ba_ref[...])
    i = jax.nn.sigmoid(_dot(xb, wx_ref[...]) + bx_ref[...])
    lam = lam_ref[...]
    softplus_neg = jnp.maximum(-lam, 0.0) + jnp.log1p(jnp.exp(-jnp.abs(lam)))
    log_a = (-RG_C * softplus_neg) * r
    a = jnp.exp(log_a).reshape(nb, SUBLANES, W)
    u = (jnp.sqrt(-_expm1(2.0 * log_a)) * i * xc2).reshape(nb, SUBLANES, W)

    d = 1
    while d < SUBLANES:
        inside = t_idx >= d
        u = jnp.where(inside, a * pltpu.roll(u, d, 1) + u, u)
        a = jnp.where(inside, a * pltpu.roll(a, d, 1), a)
        d *= 2

    if chained:
        a_scr[...] = a
        u_scr[...] = u

        def carry_in(k, h):
            c_scr[k] = jnp.broadcast_to(h, (SUBLANES, W))
            return a_scr[k, SUBLANES - 1:SUBLANES, :] * h + u_scr[k, SUBLANES - 1:SUBLANES, :]

        h_last = lax.fori_loop(0, nb, carry_in, h_scr[...], unroll=8)
        h_scr[...] = h_last
        hs = a * c_scr[...] + u
    else:
        hs = a * h0_ref[...] + u
        h_last = hs[:, SUBLANES - 1:SUBLANES, :]
    hr_out[...] = (hs * jax.nn.gelu(gr_ref[...].reshape(nb, SUBLANES, W))).reshape(bb, ts, W)

    tail = SUBLANES - (RG_CONV - 1)
    if chained:
        @pl.when(s == ns - 1)
        def _():
            h_out[0] = h_last
            buf_out[0] = x[nb - 1, tail:, :]
    else:
        h_out[...] = h_last
        buf_out[...] = x[:, tail:, :]


def rglru_mixer(z, h0, buf0, conv_w, conv_b, wa_bd, wx_bd, b_a, b_x, lam):
    B, S, _ = z.shape
    W = RG_WIDTH
    rows = 512
    chained = S > SUBLANES
    if chained:
        bb, ts = 1, min(S, rows)
        assert S % ts == 0 and ts % SUBLANES == 0
    else:
        bb, ts = min(B, rows // SUBLANES), S
        assert S == SUBLANES and B % bb == 0
    ns = S // ts
    nb = bb * ts // SUBLANES
    xr_blk = (4 * MLSTM_WIDTH) // W
    row = lambda a: a.reshape(1, W)
    buf8 = jnp.pad(buf0, ((0, 0), (SUBLANES - (RG_CONV - 1), 0), (0, 0)))
    st = lambda n: pl.BlockSpec((bb, n, W), lambda b, s: (b, 0, 0))
    hr, h1, buf1 = pl.pallas_call(
        functools.partial(_rglru_kernel, ns=ns, chained=chained),
        out_shape=(
            jax.ShapeDtypeStruct((B, S, W), F32),
            jax.ShapeDtypeStruct((B, 1, W), F32),
            jax.ShapeDtypeStruct((B, RG_CONV - 1, W), F32),
        ),
        grid=(B // bb, ns),
        in_specs=[
            pl.BlockSpec((bb, ts, W), lambda b, s: (b, s, xr_blk)),
            pl.BlockSpec((bb, ts, W), lambda b, s: (b, s, xr_blk + 1)),
            st(SUBLANES), st(1),
            _resident((RG_CONV, W)), _resident((1, W)), _resident((W, W)), _resident((W, W)),
            _resident((1, W)), _resident((1, W)), _resident((1, W)),
        ],
        out_specs=(pl.BlockSpec((bb, ts, W), lambda b, s: (b, s, 0)), st(1), st(RG_CONV - 1)),
        scratch_shapes=[pltpu.VMEM((SUBLANES, W), F32), pltpu.VMEM((1, W), F32),
                        pltpu.VMEM((nb, SUBLANES, W), F32), pltpu.VMEM((nb, SUBLANES, W), F32),
                        pltpu.VMEM((nb, SUBLANES, W), F32)],
        compiler_params=_cparams(("parallel", "arbitrary")),
        name="rglru",
    )(z, z, buf8, h0.reshape(B, 1, W), conv_w, row(conv_b), wa_bd, wx_bd, row(b_a), row(b_x), row(lam))
    return hr, h1.reshape(B, W), buf1


def _rwkv_pre_kernel(x_ref, sh_ref, sc_ref, prev0_ref, mu_ref, wr_ref, wk_ref, wv_ref, w0_ref, w1_ref,
                     w2_ref, a0_ref, a1_ref, a2_ref, g1_ref, g2_ref, kk_ref, ka_ref,
                     r_out, lw_out, k_out, v_out, kk_out, a_out, g_out, shift_out, carry_scr, *, ns):
    s = pl.program_id(1)
    bb, ts, _ = x_ref.shape
    tm = bb * ts

    @pl.when(s == 0)
    def _():
        carry_scr[...] = prev0_ref[...]

    h = x_ref[...] * (1.0 + sc_ref[...]) + sh_ref[...]
    t_idx = lax.broadcasted_iota(jnp.int32, h.shape, 1)
    h_prev = jnp.where(t_idx == 0, carry_scr[...], pltpu.roll(h, 1, 1))
    last = h[:, ts - 1:ts, :]
    carry_scr[...] = last
    dx = (h_prev - h).reshape(tm, D_MODEL)
    h2 = h.reshape(tm, D_MODEL)
    mu = mu_ref[...]
    mix = lambda j: (h2 + dx * mu[j:j + 1, :]).astype(BF)
    xr, xw, xk, xv, xa, xg = [mix(j) for j in range(6)]

    w_mid = _dot(xw, w1_ref[...])
    a_mid = _dot(xa, a1_ref[...])
    g_mid = _dot(xg, g1_ref[...])
    r = _dot(xr, wr_ref[...])
    k = _dot(xk, wk_ref[...])
    v = _dot(xv, wv_ref[...])
    wl = w0_ref[...] + _dot(jnp.tanh(w_mid).astype(BF), w2_ref[...])
    log_decay = -DECAY_SCALE * jax.nn.sigmoid(wl)
    a = jax.nn.sigmoid(a0_ref[...] + _dot(a_mid.astype(BF), a2_ref[...]))
    g = _dot(jax.nn.sigmoid(g_mid).astype(BF), g2_ref[...])
    kk = k * kk_ref[...]
    k = k * (1.0 + (a - 1.0) * ka_ref[...])

    shp = x_ref.shape
    r_out[...] = r.reshape(shp)
    lw_out[...] = log_decay.reshape(shp)
    k_out[...] = k.reshape(shp)
    v_out[...] = v.reshape(shp)
    kk_out[...] = kk.reshape(shp)
    a_out[...] = a.reshape(shp)
    g_out[...] = g.reshape(shp)

    @pl.when(s == ns - 1)
    def _():
        shift_out[...] = last


def rwkv_pre(x, mod, prev0, P):
    B, S, _ = x.shape
    bb, ts = _row_tile(B, S, 512)
    ns = S // ts
    xmap = lambda i, s: (i, s, 0)
    full = _resident
    tok = pl.BlockSpec((bb, ts, D_MODEL), xmap)
    st = pl.BlockSpec((bb, 1, D_MODEL), lambda i, s: (i, 0, 0))
    sq, lo_in, lo_out, vec = (D_MODEL, D_MODEL), (D_MODEL, LORA_PAD), (LORA_PAD, D_MODEL), (1, D_MODEL)
    outs = pl.pallas_call(
        functools.partial(_rwkv_pre_kernel, ns=ns),
        out_shape=tuple([jax.ShapeDtypeStruct(x.shape, F32)] * 7
                        + [jax.ShapeDtypeStruct((B, 1, D_MODEL), F32)]),
        grid=(B // bb, ns),
        in_specs=[
            tok, _mod_spec(bb, 3), _mod_spec(bb, 4), st, full((6, D_MODEL)),
            full(sq), full(sq), full(sq), full(vec), full(lo_in), full(lo_out),
            full(vec), full(lo_in), full(lo_out), full(lo_in), full(lo_out),
            full(vec), full(vec),
        ],
        out_specs=tuple([tok] * 7 + [st]),
        scratch_shapes=[pltpu.VMEM((bb, 1, D_MODEL), F32)],
        compiler_params=_cparams(("parallel", "arbitrary")),
        name="rwkv_pre",
    )(x, mod, mod, prev0.reshape(B, 1, D_MODEL), P['mu'], P['wr'], P['wk'], P['wv'], P['w0'], P['w1'],
      P['w2'], P['a0'], P['a1'], P['a2'], P['g1'], P['g2'], P['k_k'], P['k_a'])
    return outs


def _stack_heads(x, first):
    return jnp.concatenate([jnp.where(first, x, 0.0), jnp.where(first, 0.0, x)], axis=0)


def _seg_sum(x, first):
    s0 = jnp.sum(jnp.where(first, x, 0.0), axis=1, keepdims=True)
    s1 = jnp.sum(jnp.where(first, 0.0, x), axis=1, keepdims=True)
    return jnp.where(first, s0, s1)


def _wkv_pair_chunks(toks, sps, vecs, consts, L, tall_state):
    tri, first, strict, incl, own = consts
    L2 = 2 * L
    hs = RWKV_HS
    rng = range(len(toks))
    wide = L2 % LANES == 0
    if tall_state:
        tall = lambda x: jnp.concatenate([x[:, :hs], x[:, hs:]], axis=0)
        keep_own = lambda x: jnp.where(own, x, 0.0)
    else:
        tall = lambda x: _stack_heads(x, first)
        keep_own = lambda x: x
    splits = [_split3(t[1]) for t in toks]
    c_incl = [_dot(tri, s[0]) + _dot(tri, s[1]) + _dot(tri, s[2]) for s in splits]
    e_inv = [jnp.exp(-c) for c in c_incl]
    kn = [t[4] / jnp.maximum(jnp.sqrt(_seg_sum(t[4] * t[4], first)), 1e-12) for t in toks]
    la = [tall(-kn[p] * jnp.exp(c_incl[p] - toks[p][1])).astype(BF) for p in rng]
    lr = [tall(toks[p][0] * jnp.exp(c_incl[p])).astype(BF) for p in rng]
    rb = [tall(kn[p] * toks[p][5] * e_inv[p]).astype(BF) for p in rng]
    rk = [tall(toks[p][2] * e_inv[p]).astype(BF) for p in rng]
    vs = [_stack_heads(toks[p][3], first).astype(BF) for p in rng]
    rbk = [jnp.concatenate([rb[p], rk[p]], axis=0) for p in rng]

    if wide:
        nn = [_dot_nt(la[p], rbk[p]) for p in rng]
        mm = [_dot_nt(lr[p], rbk[p]) for p in rng]
        n_raw = [x[:, :L2] for x in nn]
        n_ak = [jnp.where(strict, x[:, L2:], 0.0).astype(BF) for x in nn]
        m_bk = [jnp.concatenate([jnp.where(incl, x[:, :L2], 0.0), jnp.where(incl, x[:, L2:], 0.0)],
                                axis=1).astype(BF) for x in mm]
    else:
        n_raw = [_dot_nt(la[p], rb[p]) for p in rng]
        n_ak = [jnp.where(strict, _dot_nt(la[p], rk[p]), 0.0).astype(BF) for p in rng]
        m_rb = [jnp.where(incl, _dot_nt(lr[p], rb[p]), 0.0).astype(BF) for p in rng]
        m_rk = [jnp.where(incl, _dot_nt(lr[p], rk[p]), 0.0).astype(BF) for p in rng]

    row = lax.broadcasted_iota(jnp.int32, (L2, L2), 0)
    col = lax.broadcasted_iota(jnp.int32, (L2, L2), 1)
    base = (row // INV_BASE == col // INV_BASE) & (col < row)
    xb = [jnp.where(base, x, 0.0).astype(BF) for x in n_raw]
    tinv = [jnp.where(row == col, 1.0, 0.0) + x.astype(F32) for x in xb]
    x2 = [_dot(x, x).astype(BF) for x in xb]
    if wide:
        prod = [_dot(x2[p], jnp.concatenate([x2[p], tinv[p].astype(BF)], axis=1)) for p in rng]
        tinv = [tinv[p] + prod[p][:, L2:] for p in rng]
        x4 = [x[:, :L2].astype(BF) for x in prod]
    else:
        tinv = [tinv[p] + _dot(x2[p], tinv[p].astype(BF)) for p in rng]
        x4 = [_dot(x, x).astype(BF) for x in x2]
    tinv = [tinv[p] + _dot(x4[p], tinv[p].astype(BF)) for p in rng]
    blk = INV_BASE
    while blk < L:
        off = (row // (2 * blk) == col // (2 * blk)) & (row // blk == col // blk + 1)
        tb = [t.astype(BF) for t in tinv]
        pr = [_dot(jnp.where(off, n_raw[p], 0.0).astype(BF), tb[p]).astype(BF) for p in rng]
        tinv = [tinv[p] + _dot(tb[p], pr[p]) for p in rng]
        blk *= 2

    spb = [s.astype(BF) for s in sps]
    rhs = [keep_own(_dot_nt(la[p], spb[p])) + _dot(n_ak[p], vs[p]) for p in rng]
    ub = [_dot(tinv[p].astype(BF), rhs[p].astype(BF)).astype(BF) for p in rng]

    uv = [jnp.concatenate([ub[p], vs[p]], axis=0) for p in rng]
    y0 = [keep_own(_dot_nt(lr[p], spb[p])) for p in rng]
    if wide:
        ys = [y0[p] + _dot(m_bk[p], uv[p]) for p in rng]
    else:
        ys = [y0[p] + _dot(m_rb[p], ub[p]) + _dot(m_rk[p], vs[p]) for p in rng]
    decay = [jnp.exp(c_incl[p][L - 1:L, :]) for p in rng]
    if tall_state:
        upper = lax.broadcasted_iota(jnp.int32, (2 * hs, hs), 0) < hs
        decay = [jnp.where(upper, d[:, :hs], d[:, hs:]) for d in decay]
    sp_new = [(sps[p] + _dot_tn(uv[p], rbk[p])) * decay[p] for p in rng]

    outs = []
    for p in rng:
        r, _, k, v, _, _, gate = toks[p]
        r_k, lnx_g, lnx_b = vecs[p]
        y = ys[p][0:L, :] + ys[p][L:L2, :]
        yc = y - _seg_sum(y, first) * (1.0 / RWKV_HS)
        yv = _seg_sum(yc * yc, first) * (1.0 / RWKV_HS)
        yn = yc * lax.rsqrt(yv + RWKV_LN_EPS) * lnx_g + lnx_b
        outs.append(((yn + _seg_sum(r * k * r_k, first) * v) * gate, sp_new[p]))
    return outs


def _wkv_kernel(r_ref, lw_ref, k_ref, v_ref, kk_ref, a_ref, g_ref, s0_ref, rk_ref, lxg_ref, lxb_ref,
                y_out, s_out, sp_scr, *, L, nc, bb, group):
    c = pl.program_id(1)
    hs = RWKV_HS
    npair = RWKV_HEADS // 2
    L2 = 2 * L

    tall_state = nc == 1

    @pl.when(c == 0)
    def _():
        if tall_state:
            sp_scr[...] = s0_ref[...].reshape(sp_scr.shape)
        else:
            zero = jnp.zeros((hs, hs), F32)

            def init(i, carry):
                for p in range(npair):
                    top = jnp.concatenate([s0_ref[i, 2 * p], zero], axis=1)
                    bot = jnp.concatenate([zero, s0_ref[i, 2 * p + 1]], axis=1)
                    sp_scr[i, p] = jnp.concatenate([top, bot], axis=0)
                return carry

            lax.fori_loop(0, bb, init, 0)

    row = lax.broadcasted_iota(jnp.int32, (L, L), 0)
    col = lax.broadcasted_iota(jnp.int32, (L, L), 1)
    tri = jnp.where(col <= row, 1.0, 0.0).astype(BF)
    first = lax.broadcasted_iota(jnp.int32, (L, LANES), 1) < hs
    row2 = lax.broadcasted_iota(jnp.int32, (L2, L2), 0)
    col2 = lax.broadcasted_iota(jnp.int32, (L2, L2), 1)
    same = (row2 >= L) == (col2 >= L)
    own = ((lax.broadcasted_iota(jnp.int32, (L2, LANES), 0) >= L)
           == (lax.broadcasted_iota(jnp.int32, (L2, LANES), 1) >= hs))
    consts = (tri, first, same & (col2 < row2), same & (col2 <= row2), own)
    sls = [slice(p * LANES, (p + 1) * LANES) for p in range(npair)]

    def seqs(j, carry):
        ids = [(j * group + g, p) for g in range(group) for p in range(npair)]
        toks = [tuple(ref[i, :, sls[p]] for ref in (r_ref, lw_ref, k_ref, v_ref, kk_ref, a_ref, g_ref))
                for i, p in ids]
        vecs = [(rk_ref[:, sls[p]], lxg_ref[:, sls[p]], lxb_ref[:, sls[p]]) for _, p in ids]
        outs = _wkv_pair_chunks(toks, [sp_scr[i, p] for i, p in ids], vecs, consts, L, tall_state)
        for (i, p), (y, sp_new) in zip(ids, outs):
            y_out[i, :, sls[p]] = y
            sp_scr[i, p] = sp_new
        return carry

    lax.fori_loop(0, bb // group, seqs, 0)

    @pl.when(c == nc - 1)
    def _():
        if tall_state:
            s_out[...] = sp_scr[...].reshape(s_out.shape)
        else:
            def fin(i, carry):
                for p in range(npair):
                    sp = sp_scr[i, p]
                    s_out[i, 2 * p] = sp[0:hs, 0:hs]
                    s_out[i, 2 * p + 1] = sp[hs:2 * hs, hs:2 * hs]
                return carry

            lax.fori_loop(0, bb, fin, 0)


def wkv_recurrence(r, lw, k, v, kk, a, g, s0, P):
    B, S, _ = r.shape
    L = RWKV_CHUNK if S % RWKV_CHUNK == 0 else S
    assert L & (L - 1) == 0 and L % SUBLANES == 0
    nc = S // L
    bb, group = (min(B, 4), min(B, 4)) if nc > 1 else (min(B, 8), min(B, 8))
    assert B % bb == 0 and bb % group == 0
    npair = RWKV_HEADS // 2
    tok = pl.BlockSpec((bb, L, D_MODEL), lambda b, c: (b, c, 0))
    st = pl.BlockSpec((bb, RWKV_HEADS, RWKV_HS, RWKV_HS), lambda b, c: (b, 0, 0, 0))
    vec = pl.BlockSpec((1, D_MODEL), lambda b, c: (0, 0))
    return pl.pallas_call(
        functools.partial(_wkv_kernel, L=L, nc=nc, bb=bb, group=group),
        out_shape=(jax.ShapeDtypeStruct(r.shape, F32), jax.ShapeDtypeStruct(s0.shape, F32)),
        grid=(B // bb, nc),
        in_specs=[tok] * 7 + [st, vec, vec, vec],
        out_specs=(tok, st),
        scratch_shapes=[pltpu.VMEM((bb, npair, 2 * RWKV_HS, RWKV_HS if nc == 1 else 2 * RWKV_HS), F32)],
        compiler_params=_cparams(("parallel", "arbitrary")),
        name="wkv",
    )(r, lw, k, v, kk, a, g, s0, P['r_k'], P['lnx_g'], P['lnx_b'])


def _block_diag(w):
    G, n, _ = w.shape
    eye = jnp.eye(G, dtype=w.dtype)
    return (eye[:, None, :, None] * w[:, :, None, :]).reshape(G * n, G * n)


def _pad_cols(w, n):
    return jnp.pad(w, ((0, 0), (0, n - w.shape[1])))


def _pad_rows(w, n):
    return jnp.pad(w, ((0, n - w.shape[0]), (0, 0)))


def _prep_ab(j, p):
    w_in = p['ab_w_in'][j]
    n_main = 4 * MLSTM_WIDTH
    n_gate = 2 * MLSTM_HEADS
    w_all = jnp.concatenate([w_in[:, :n_main], w_in[:, n_main + n_gate:],
                             w_in[:, n_main:n_main + n_gate]], axis=1)
    return dict(
        w_in=_pad_cols(w_all, Z_COLS).astype(BF),
        b_gates=p['mlstm_b_gates'][j], norm_g=p['mlstm_norm_g'][j],
        conv_w=p['rg_conv_w'][j], conv_b=p['rg_conv_b'][j],
        wa=_block_diag(p['rg_w_a'][j]).astype(BF), wx=_block_diag(p['rg_w_x'][j]).astype(BF),
        b_a=p['rg_b_a'][j], b_x=p['rg_b_x'][j], lam=p['rg_lambda'][j],
        w_out_m=p['ab_w_out'][j][:MLSTM_WIDTH].astype(BF), w_out_r=p['ab_w_out'][j][MLSTM_WIDTH:].astype(BF),
    )


def _prep_rwkv(j, p):
    vec = lambda a: a.reshape(1, D_MODEL)
    return dict(
        mu=p['rw_mu'][j],
        wr=p['rw_wr'][j].astype(BF), wk=p['rw_wk'][j].astype(BF), wv=p['rw_wv'][j].astype(BF),
        w0=vec(p['rw_w0'][j]), w1=_pad_cols(p['rw_w1'][j], LORA_PAD).astype(BF),
        w2=_pad_rows(p['rw_w2'][j], LORA_PAD).astype(BF),
        a0=vec(p['rw_a0'][j]), a1=_pad_cols(p['rw_a1'][j], LORA_PAD).astype(BF),
        a2=_pad_rows(p['rw_a2'][j], LORA_PAD).astype(BF),
        g1=_pad_cols(p['rw_g1'][j], LORA_PAD).astype(BF), g2=_pad_rows(p['rw_g2'][j], LORA_PAD).astype(BF),
        k_k=vec(p['rw_k_k'][j]), k_a=vec(p['rw_k_a'][j]), r_k=vec(p['rw_r_k'][j]),
        lnx_g=vec(p['rw_lnx_g'][j]), lnx_b=vec(p['rw_lnx_b'][j]), wo=p['rw_wo'][j].astype(BF),
    )


def ab_mixer(x, mod, st, A):
    mC, mn, mm, rh, rconv = st
    z, gates = mod_matmul(x, mod, A['w_in'], 4 * MLSTM_WIDTH + 2 * RG_WIDTH, 2 * MLSTM_HEADS)
    hm, C1, n1, m1 = mlstm_mixer(z, gates, mC, mn, mm, A['b_gates'], A['norm_g'])
    hr, rh1, buf1 = rglru_mixer(z, rh, rconv, A['conv_w'], A['conv_b'], A['wa'], A['wx'],
                                A['b_a'], A['b_x'], A['lam'])
    return ((hm, 0), (hr, 0), A['w_out_m'], A['w_out_r']), (C1, n1, m1, rh1, buf1)


def rwkv_mixer(x, mod, st, R):
    wkv0, prev0 = st
    r, lw, k, v, kk, a, g, shift = rwkv_pre(x, mod, prev0, R)
    y, wkv1 = wkv_recurrence(r, lw, k, v, kk, a, g, wkv0, R)
    half = D_MODEL // 2
    return ((y, 0), (y, 1), R['wo'][:half], R['wo'][half:]), (wkv1, shift.reshape(shift.shape[0], D_MODEL))


def run_trunk(x, mods, states, W):
    mC, mn, mm, rh, rconv, wkv, shift = states
    new_ab, new_c = [], []
    for layer in range(DEPTH):
        mod = mods[layer]
        lg = lambda i: W['ln_g'][layer, i].reshape(1, D_MODEL)
        lb = lambda i: W['ln_b'][layer, i].reshape(1, D_MODEL)
        f = W['ffn']
        x = ffn_block(x, mod, 0, layer, 0, f[0], f[1], f[2], lg(0), lb(0))
        j = layer // 2
        if layer % 2 == 0:
            mixed, st = ab_mixer(x, mod, (mC[j], mn[j], mm[j], rh[j], rconv[j]), W['ab'][j])
            new_ab.append(st)
        else:
            mixed, st = rwkv_mixer(x, mod, (wkv[j], shift[j]), W['rwkv'][j])
            new_c.append(st)
        x = ffn_block(x, mod, 2, layer, 1, f[0], f[1], f[2], lg(2), lb(2), mixer=mixed + (lg(1), lb(1)))
    stk = lambda sts, i: sts[0][i][None] if len(sts) == 1 else jnp.stack([s[i] for s in sts], axis=0)
    return x, (stk(new_ab, 0), stk(new_ab, 1), stk(new_ab, 2), stk(new_ab, 3), stk(new_ab, 4),
               stk(new_c, 0), stk(new_c, 1))


def _zero_states(B):
    n_ab, n_c = (DEPTH + 1) // 2, DEPTH // 2
    return (jnp.zeros((n_ab, B, MLSTM_HEADS, MLSTM_HD, MLSTM_HD), F32),
            jnp.zeros((n_ab, B, MLSTM_HEADS, MLSTM_HD), F32),
            jnp.zeros((n_ab, B, MLSTM_HEADS), F32),
            jnp.zeros((n_ab, B, RG_WIDTH), F32),
            jnp.zeros((n_ab, B, RG_CONV - 1, RG_WIDTH), F32),
            jnp.zeros((n_c, B, RWKV_HEADS, RWKV_HS, RWKV_HS), F32),
            jnp.zeros((n_c, B, D_MODEL), F32))


def kernel(x_prompt, x_sample, c_prompt, c_sample, state_mlstm_C, state_mlstm_n, state_mlstm_m, state_rglru_h, state_rglru_conv, state_rwkv_wkv, state_rwkv_shift, ada_w, ada_b, ln_g, ln_b, ffn_w1, ffn_w3, ffn_w2, ab_w_in, mlstm_b_gates, mlstm_norm_g, rg_conv_w, rg_conv_b, rg_w_a, rg_b_a, rg_w_x, rg_b_x, rg_lambda, ab_w_out, rw_mu, rw_wr, rw_wk, rw_wv, rw_w0, rw_w1, rw_w2, rw_a0, rw_a1, rw_a2, rw_g1, rw_g2, rw_k_k, rw_k_a, rw_r_k, rw_lnx_g, rw_lnx_b, rw_wo):
    p = dict(ab_w_in=ab_w_in, mlstm_b_gates=mlstm_b_gates, mlstm_norm_g=mlstm_norm_g,
             rg_conv_w=rg_conv_w, rg_conv_b=rg_conv_b, rg_w_a=rg_w_a, rg_b_a=rg_b_a, rg_w_x=rg_w_x,
             rg_b_x=rg_b_x, rg_lambda=rg_lambda, ab_w_out=ab_w_out, rw_mu=rw_mu, rw_wr=rw_wr,
             rw_wk=rw_wk, rw_wv=rw_wv, rw_w0=rw_w0, rw_w1=rw_w1, rw_w2=rw_w2, rw_a0=rw_a0,
             rw_a1=rw_a1, rw_a2=rw_a2, rw_g1=rw_g1, rw_g2=rw_g2, rw_k_k=rw_k_k, rw_k_a=rw_k_a,
             rw_r_k=rw_r_k, rw_lnx_g=rw_lnx_g, rw_lnx_b=rw_lnx_b, rw_wo=rw_wo)
    W = dict(
        ln_g=ln_g, ln_b=ln_b,
        ffn=(ffn_w1.astype(BF), ffn_w3.astype(BF), ffn_w2.astype(BF)),
        ab=[_prep_ab(j, p) for j in range((DEPTH + 1) // 2)],
        rwkv=[_prep_rwkv(j, p) for j in range(DEPTH // 2)],
    )
    Bp, Bs = x_prompt.shape[0], x_sample.shape[0]
    mod_all = adaln(jnp.concatenate([c_prompt, c_sample], axis=0), ada_w, ada_b)
    mods_p = [mod_all[l, :Bp] for l in range(DEPTH)]
    mods_s = [mod_all[l, Bp:] for l in range(DEPTH)]
    y_prompt, sp = run_trunk(x_prompt, mods_p, _zero_states(Bp), W)
    y_sample, ss = run_trunk(x_sample, mods_s,
                             (state_mlstm_C, state_mlstm_n, state_mlstm_m, state_rglru_h,
                              state_rglru_conv, state_rwkv_wkv, state_rwkv_shift), W)
    return (y_prompt, y_sample) + tuple(sp) + tuple(ss)
```

```python
import functools
import math

import jax
import jax.numpy as jnp
from jax import lax
from jax.experimental import pallas as pl
from jax.experimental.pallas import tpu as pltpu

D_MODEL = 1024
DEPTH = 2
N_SUB = 3
MLSTM_WIDTH = 512
MLSTM_HEADS = 4
MLSTM_HD = 128
MLSTM_CHUNK = 128
RG_WIDTH = 512
RG_BLOCKS = 8
RG_BD = 64
RG_CONV = 4
RG_C = 8.0
RWKV_HS = 64
RWKV_HEADS = 16
RWKV_LN_EPS = 64e-5
RWKV_CHUNK = 64
DECAY_SCALE = math.exp(-0.5)
INV_BASE = 8
D_FF = 2816
ALPHA = (2.0 * DEPTH) ** 0.25
LN_EPS = 1e-5
HEAD_NORM_EPS = 1e-6

LANES = 128
SUBLANES = 8
VMEM_LIMIT_BYTES = 56 * 1024 * 1024
Z_COLS = 3200
LORA_PAD = 128

BF = jnp.bfloat16
F32 = jnp.float32

NT_DIMS = (((1,), (1,)), ((), ()))
TN_DIMS = (((0,), (0,)), ((), ()))


def _cparams(sem):
    return pltpu.CompilerParams(dimension_semantics=sem, vmem_limit_bytes=VMEM_LIMIT_BYTES)


def _dot(a, b):
    return jnp.dot(a, b, preferred_element_type=F32)


def _dot_nt(a, b):
    return lax.dot_general(a, b, NT_DIMS, preferred_element_type=F32)


def _dot_tn(a, b):
    return lax.dot_general(a, b, TN_DIMS, preferred_element_type=F32)


def _layer_norm(z, g, b):
    mu = jnp.mean(z, axis=-1, keepdims=True)
    zc = z - mu
    var = jnp.mean(zc * zc, axis=-1, keepdims=True)
    return zc * lax.rsqrt(var + LN_EPS) * g + b


def _row_tile(B, S, rows):
    if S >= rows:
        assert S % rows == 0
        return 1, rows
    bb = min(B, rows // S)
    assert B % bb == 0
    return bb, S


def _split2(x):
    hi = x.astype(BF)
    lo = (x - hi.astype(F32)).astype(BF)
    return hi, lo


def _split3(x):
    hi = x.astype(BF)
    r1 = x - hi.astype(F32)
    mid = r1.astype(BF)
    lo = (r1 - mid.astype(F32)).astype(BF)
    return hi, mid, lo


def _adaln_kernel(c_ref, w_ref, b_ref, o_ref):
    h = jax.nn.silu(c_ref[...]).astype(BF)
    res = _dot(h, w_ref[...].astype(BF)) + b_ref[...]
    o_ref[...] = res.reshape(o_ref.shape)


def adaln(c_all, ada_w, ada_b):
    Bc = c_all.shape[0]
    n_chunk = ada_w.shape[-1] // D_MODEL
    return pl.pallas_call(
        _adaln_kernel,
        out_shape=jax.ShapeDtypeStruct((DEPTH, Bc, n_chunk, 1, D_MODEL), F32),
        grid=(DEPTH, n_chunk),
        in_specs=[
            pl.BlockSpec((Bc, D_MODEL), lambda l, j: (0, 0)),
            pl.BlockSpec((None, D_MODEL, D_MODEL), lambda l, j: (l, 0, j)),
            pl.BlockSpec((None, 1, D_MODEL), lambda l, j: (l, 0, j)),
        ],
        out_specs=pl.BlockSpec((None, Bc, None, 1, D_MODEL), lambda l, j: (l, 0, j, 0, 0)),
        compiler_params=_cparams(("parallel", "parallel")),
        name="adaln",
    )(c_all, ada_w, ada_b.reshape(DEPTH, 1, n_chunk * D_MODEL))


def _mod_spec(bb, k):
    return pl.BlockSpec((bb, None, 1, D_MODEL), lambda i, *_: (i, k, 0, 0))


FFN_ROW_SPLITS = 2


def _ffn_rows(x, mixer, sh, sc, gt, w1_ref, w3_ref, w2_ref, lg, lb, res_w):
    bb, ts, _ = x.shape
    if mixer is not None:
        mgt, m1, m2, mw1_ref, mw2_ref, mlg, mlb = mixer
        flat = lambda m: m.reshape(bb * ts, -1).astype(BF)
        ym = _dot(flat(m1), mw1_ref[...]) + _dot(flat(m2), mw2_ref[...])
        yield
        x = _layer_norm(ALPHA * x + (1.0 + mgt) * ym.reshape(bb, ts, D_MODEL), mlg, mlb)
    h = (x * (1.0 + sc) + sh).reshape(bb * ts, D_MODEL).astype(BF)
    a = _dot(h, w1_ref[...])
    b = _dot(h, w3_ref[...])
    yield
    g = (jax.nn.silu(a) * b).astype(BF)
    y = _dot(g, w2_ref[...])
    yield
    return _layer_norm(ALPHA * x + (res_w * (1.0 + gt)) * y.reshape(bb, ts, D_MODEL), lg, lb)


def _ffn_kernel(*refs, res_w, after_mixer):
    if after_mixer:
        (x_ref, mgt_ref, m1_ref, m2_ref, mw1_ref, mw2_ref, mlg_ref, mlb_ref,
         sh_ref, sc_ref, gt_ref, w1_ref, w3_ref, w2_ref, lg_ref, lb_ref, o_ref) = refs
    else:
        x_ref, sh_ref, sc_ref, gt_ref, w1_ref, w3_ref, w2_ref, lg_ref, lb_ref, o_ref = refs
    bb, ts, _ = x_ref.shape
    if bb > 1:
        n = bb // FFN_ROW_SPLITS
        tok = lambda ref, i: ref[i * n:(i + 1) * n]
        per_seq = tok
    else:
        n = ts // FFN_ROW_SPLITS
        tok = lambda ref, i: ref[:, i * n:(i + 1) * n, :]
        per_seq = lambda ref, i: ref[...]
    gens = []
    for i in range(FFN_ROW_SPLITS):
        mixer = None
        if after_mixer:
            mixer = (per_seq(mgt_ref, i), tok(m1_ref, i), tok(m2_ref, i), mw1_ref, mw2_ref, mlg_ref[...], mlb_ref[...])
        gens.append(_ffn_rows(tok(x_ref, i), mixer, per_seq(sh_ref, i), per_seq(sc_ref, i), per_seq(gt_ref, i),
                              w1_ref, w3_ref, w2_ref, lg_ref[...], lb_ref[...], res_w))
    for i, out in enumerate(_lockstep(gens)):
        if bb > 1:
            o_ref[i * n:(i + 1) * n] = out
        else:
            o_ref[:, i * n:(i + 1) * n, :] = out


def _resident(shape):
    return pl.BlockSpec(shape, lambda *_: (0,) * len(shape), pipeline_mode=pl.Buffered(1))


def ffn_block(x, mod, sub, layer, half, w1, w3, w2, lg, lb, mixer=None):
    B, S, _ = x.shape
    bb, ts = _row_tile(B, S, 512)
    xmap = lambda i, s: (i, s, 0)
    wspec = lambda r, c: pl.BlockSpec((None, None, r, c), lambda i, s: (layer, half, 0, 0),
                                      pipeline_mode=pl.Buffered(1))
    vec = _resident((1, D_MODEL))
    specs = [pl.BlockSpec((bb, ts, D_MODEL), xmap)]
    args = [x]
    if mixer is not None:
        (m1, blk1), (m2, blk2), mw1, mw2, mlg, mlb = mixer
        mspec = lambda w, blk: pl.BlockSpec((bb, ts, w.shape[0]), lambda i, s: (i, s, blk))
        specs += [_mod_spec(bb, 5), mspec(mw1, blk1), mspec(mw2, blk2),
                  _resident(mw1.shape), _resident(mw2.shape), vec, vec]
        args += [mod, m1, m2, mw1, mw2, mlg, mlb]
    specs += [_mod_spec(bb, 3 * sub), _mod_spec(bb, 3 * sub + 1), _mod_spec(bb, 3 * sub + 2),
              wspec(D_MODEL, D_FF), wspec(D_MODEL, D_FF), wspec(D_FF, D_MODEL), vec, vec]
    args += [mod, mod, mod, w1, w3, w2, lg, lb]
    return pl.pallas_call(
        functools.partial(_ffn_kernel, res_w=0.5, after_mixer=mixer is not None),
        out_shape=jax.ShapeDtypeStruct(x.shape, F32),
        grid=(B // bb, S // ts),
        in_specs=specs,
        out_specs=pl.BlockSpec((bb, ts, D_MODEL), xmap),
        compiler_params=_cparams(("parallel", "parallel")),
        name="ffn_block",
    )(*args)


def _modmm_kernel(x_ref, sh_ref, sc_ref, w_ref, o_ref, gate_ref):
    bb, ts, _ = x_ref.shape
    n_main = o_ref.shape[-1]
    n_gate = gate_ref.shape[-1]
    h = (x_ref[...] * (1.0 + sc_ref[...]) + sh_ref[...]).reshape(bb * ts, D_MODEL).astype(BF)
    o_ref[...] = _dot(h, w_ref[:, :n_main]).reshape(o_ref.shape)
    gate_ref[...] = _dot(h, w_ref[:, n_main:])[:, :n_gate].reshape(gate_ref.shape)


def mod_matmul(x, mod, w, n_main, n_gate):
    B, S, _ = x.shape
    bb, ts = _row_tile(B, S, 512)
    xmap = lambda i, s: (i, s, 0)
    return pl.pallas_call(
        _modmm_kernel,
        out_shape=(jax.ShapeDtypeStruct((B, S, n_main), F32), jax.ShapeDtypeStruct((B, S, n_gate), F32)),
        grid=(B // bb, S // ts),
        in_specs=[
            pl.BlockSpec((bb, ts, D_MODEL), xmap),
            _mod_spec(bb, 3), _mod_spec(bb, 4),
            _resident(w.shape),
        ],
        out_specs=(pl.BlockSpec((bb, ts, n_main), xmap), pl.BlockSpec((bb, ts, n_gate), xmap)),
        compiler_params=_cparams(("parallel", "parallel")),
        name="mod_matmul",
    )(x, mod, mod, w)


def _log_sigmoid(x):
    return jnp.minimum(x, 0.0) - jnp.log1p(jnp.exp(-jnp.abs(x)))


def _lockstep(gens):
    results = [None] * len(gens)
    live = list(range(len(gens)))
    while live:
        still = []
        for idx in live:
            try:
                next(gens[idx])
                still.append(idx)
            except StopIteration as stop:
                results[idx] = stop.value
        live = still
    return results


def _dot_exact(a, b, dims):
    return lax.dot_general(a, b, dims, precision=lax.Precision.HIGHEST, preferred_element_type=F32)


def _mlstm_gate_views(gl, L):
    row = lax.broadcasted_iota(jnp.int32, (L, L), 0)
    col = lax.broadcasted_iota(jnp.int32, (L, L), 1)
    eye = jnp.where(row == col, 1.0, 0.0)
    lower = jnp.where(col <= row, 1.0, 0.0)
    upper = jnp.where(row <= col, 1.0, 0.0)
    gl_t = _dot_exact(gl, eye, TN_DIMS)
    cum_c = _dot_exact(lower, gl, (((1,), (0,)), ((), ())))
    cum_r = _dot_exact(gl, upper, TN_DIMS)
    return gl_t, cum_c, cum_r


def _mlstm_head_chunk(q, k, v, o, ig_c, b_c, ig_r, b_r, C0, n0, m0, g, L):
    d = MLSTM_HD
    spread = lambda colv, n: jnp.broadcast_to(colv, (L, n))

    def both(colv):
        over_l = spread(colv, L)
        return over_l, (over_l if L == d else spread(colv, d))

    ones_l = jnp.ones((L, d), BF)
    ones_d = jnp.ones((d, d), BF)

    on_mxu = L >= d

    def row_sum(x, ones):
        if not on_mxu:
            return spread(jnp.sum(x, axis=1, keepdims=True), d), x.astype(BF)
        hi, lo = _split2(x)
        return _dot(hi, ones) + _dot(lo, ones), hi

    row = lax.broadcasted_iota(jnp.int32, (L, L), 0)
    col = lax.broadcasted_iota(jnp.int32, (L, L), 1)
    causal = col <= row
    bc_ll = spread(b_c, L)
    dmat = jnp.where(causal, bc_ll - b_r + ig_r, -jnp.inf)
    m_inter = b_c + m0
    m = jnp.maximum(m_inter, jnp.max(dmat, axis=1, keepdims=True))
    m_ll, m_ld = both(m)
    p = jnp.exp(dmat - m_ll)
    w_inter = jnp.exp(spread(m_inter, d) - m_ld)

    k = k * (d ** -0.5)
    qb = q.astype(BF)
    kb = k.astype(BF)
    if on_mxu:
        q_hi, q_lo = _split2(q)
        n_hi, n_lo = _split2(jnp.broadcast_to(n0, (d, d)))
    yield
    s_raw = _dot_nt(qb, kb)
    q_c0 = _dot_nt(qb, C0.astype(BF))
    if on_mxu:
        q_n0 = _dot_nt(q_hi, n_hi) + _dot_nt(q_lo, n_hi) + _dot_nt(q_hi, n_lo)
    else:
        q_n0 = spread(jnp.sum(q * n0, axis=1, keepdims=True), d)
    yield
    scores = s_raw * p
    score_sum, scores_b = row_sum(scores, ones_l)
    num = _dot(scores_b, v.astype(BF)) + w_inter * q_c0
    den = score_sum + w_inter * q_n0
    hh = num / jnp.maximum(jnp.abs(den), jnp.exp(-m_ld))

    m_end = m[L - 1:L, :]
    b_end = b_c[L - 1:L, :]
    w_state = jnp.exp(b_end + m0 - m_end)
    w_rows = spread(jnp.exp(b_end - b_c + ig_c - m_end), d)
    yield
    c_new = w_state * C0 + _dot_tn((w_rows * v).astype(BF), kb)
    n_new = w_state * n0 + jnp.sum(w_rows * k, axis=0, keepdims=True)
    mu = row_sum(hh, ones_d)[0] * (1.0 / d)
    yield
    hc = hh - mu
    var = row_sum(hc * hc, ones_d)[0] * (1.0 / d)
    hn = hc * lax.rsqrt(var + HEAD_NORM_EPS)
    return hn * g * jax.nn.sigmoid(o), c_new, n_new, m_end


def _mlstm_kernel(m0_ref, bg_ref, q_ref, k_ref, v_ref, o_ref, gc_ref, c0_ref, n0_ref, g_ref,
                  h_out, c_out, n_out, m_out, c_scr, n_scr, m_scr, *, L, nc, bb, group):
    ib = pl.program_id(0)
    c = pl.program_id(1)
    H = MLSTM_HEADS

    @pl.when(c == 0)
    def _():
        c_scr[...] = c0_ref[...]
        n_scr[...] = n0_ref[...]

        def init(i, carry):
            for h in range(H):
                m_scr[i, h] = jnp.full((1, LANES), m0_ref[(ib * bb + i) * H + h], F32)
            return carry

        lax.fori_loop(0, bb, init, 0)

    sls = [slice(h * MLSTM_HD, (h + 1) * MLSTM_HD) for h in range(H)]

    def seqs(j, carry):
        views = []
        for g in range(group):
            gates = gc_ref[j * group + g] + bg_ref[...]
            is_input = lax.broadcasted_iota(jnp.int32, gates.shape, 1) < H
            gl = jnp.where(is_input, gates, _log_sigmoid(gates))
            views.append((gl,) + _mlstm_gate_views(gl, L))
        ids = [(j * group + g, h) for g in range(group) for h in range(H)]
        gate_args = [(views[g][0][:, h:h + 1], views[g][2][:, H + h:H + h + 1],
                      views[g][1][h:h + 1, :], views[g][3][H + h:H + h + 1, :])
                     for g in range(group) for h in range(H)]
        heads = _lockstep([_mlstm_head_chunk(
            q_ref[i, :, sls[h]], k_ref[i, :, sls[h]], v_ref[i, :, sls[h]], o_ref[i, :, sls[h]], *gate_args[n],
            c_scr[i, h], n_scr[i, h], m_scr[i, h][:, 0:1], g_ref[:, sls[h]], L) for n, (i, h) in enumerate(ids)])
        for (i, h), (out, c_new, n_new, m_end) in zip(ids, heads):
            h_out[i, :, sls[h]] = out
            c_scr[i, h] = c_new
            n_scr[i, h] = n_new
            m_scr[i, h] = jnp.broadcast_to(m_end, (1, LANES))
        return carry

    lax.fori_loop(0, bb // group, seqs, 0)

    @pl.when(c == nc - 1)
    def _():
        c_out[...] = c_scr[...]
        n_out[...] = n_scr[...]
        m_out[...] = m_scr[...]


def mlstm_mixer(z, gates, C0, n0, m0, b_gates, norm_g):
    B, S, _ = z.shape
    H = MLSTM_HEADS
    W = MLSTM_WIDTH
    L = MLSTM_CHUNK if S % MLSTM_CHUNK == 0 else S
    nc = S // L
    bb, group = (2, 2) if nc > 1 else (min(B, 8), min(B, 4))
    assert B % bb == 0 and bb % group == 0
    zspec = lambda blk: pl.BlockSpec((bb, L, W), lambda b, c: (b, c, blk))
    smem = pl.BlockSpec(memory_space=pltpu.SMEM)
    hd_spec = pl.BlockSpec((bb, H, 1, MLSTM_HD), lambda b, c: (b, 0, 0, 0))
    c_spec = pl.BlockSpec((bb, H, MLSTM_HD, MLSTM_HD), lambda b, c: (b, 0, 0, 0))
    hm, C1, n1, m1 = pl.pallas_call(
        functools.partial(_mlstm_kernel, L=L, nc=nc, bb=bb, group=group),
        out_shape=(
            jax.ShapeDtypeStruct((B, S, W), F32),
            jax.ShapeDtypeStruct((B, H, MLSTM_HD, MLSTM_HD), F32),
            jax.ShapeDtypeStruct((B, H, 1, MLSTM_HD), F32),
            jax.ShapeDtypeStruct((B, H, 1, LANES), F32),
        ),
        grid=(B // bb, nc),
        in_specs=[
            smem, pl.BlockSpec((1, 2 * H), lambda b, c: (0, 0)),
            zspec(0), zspec(1), zspec(2), zspec(3),
            pl.BlockSpec((bb, L, 2 * H), lambda b, c: (b, c, 0)),
            c_spec, hd_spec,
            pl.BlockSpec((1, W), lambda b, c: (0, 0)),
        ],
        out_specs=(pl.BlockSpec((bb, L, W), lambda b, c: (b, c, 0)), c_spec, hd_spec, hd_spec),
        scratch_shapes=[pltpu.VMEM((bb, H, MLSTM_HD, MLSTM_HD), F32), pltpu.VMEM((bb, H, 1, MLSTM_HD), F32),
                        pltpu.VMEM((bb, H, 1, LANES), F32)],
        compiler_params=_cparams(("parallel", "arbitrary")),
        name="mlstm",
    )(m0.reshape(B * H), b_gates.reshape(1, 2 * H), z, z, z, z, gates,
      C0, n0.reshape(B, H, 1, MLSTM_HD), norm_g.reshape(1, W))
    return hm, C1, n1.reshape(B, H, MLSTM_HD), m1[:, :, 0, 0]


def _expm1(y):
    u = jnp.exp(y)
    small = jnp.where(u == 1.0, y, (u - 1.0) * y / jnp.log(u))
    return jnp.where(jnp.abs(y) > 0.5, u - 1.0, small)


def _rglru_kernel(xr_ref, gr_ref, buf_ref, h0_ref, cw_ref, cb_ref, wa_ref, wx_ref, ba_ref, bx_ref, lam_ref,
                  hr_out, h_out, buf_out, prev_scr, h_scr, a_scr, u_scr, c_scr, *, ns, chained):
    s = pl.program_id(1)
    bb, ts, W = xr_ref.shape
    nb = bb * ts // SUBLANES
    x = xr_ref[...].reshape(nb, SUBLANES, W)

    if chained:
        @pl.when(s == 0)
        def _():
            prev_scr[...] = buf_ref[0]
            h_scr[...] = h0_ref[0]

        prev = jnp.concatenate([prev_scr[...][None], x[:nb - 1]], axis=0)
        prev_scr[...] = x[nb - 1]
    else:
        prev = buf_ref[...]

    t_idx = lax.broadcasted_iota(jnp.int32, x.shape, 1)
    cw = cw_ref[...]
    xc = cb_ref[...] + cw[RG_CONV - 1:RG_CONV, :] * x
    for d in range(1, RG_CONV):
        back = jnp.where(t_idx >= d, pltpu.roll(x, d, 1), pltpu.roll(prev, d, 1))
        xc = xc + cw[RG_CONV - 1 - d:RG_CONV - d, :] * back

    xc2 = xc.reshape(nb * SUBLANES, W)
    xb = xc2.astype(BF)
    r = jax.nn.sigmoid(_dot(xb, wa_ref[...]) + ba_ref[...])
    i = jax.nn.sigmoid(_dot(xb, wx_ref[...]) + bx_ref[...])
    lam = lam_ref[...]
    softplus_neg = jnp.maximum(-lam, 0.0) + jnp.log1p(jnp.exp(-jnp.abs(lam)))
    log_a = (-RG_C * softplus_neg) * r
    a = jnp.exp(log_a).reshape(nb, SUBLANES, W)
    u = (jnp.sqrt(-_expm1(2.0 * log_a)) * i * xc2).reshape(nb, SUBLANES, W)

    d = 1
    while d < SUBLANES:
        inside = t_idx >= d
        u = jnp.where(inside, a * pltpu.roll(u, d, 1) + u, u)
        a = jnp.where(inside, a * pltpu.roll(a, d, 1), a)
        d *= 2

    if chained:
        a_scr[...] = a
        u_scr[...] = u

        def carry_in(k, h):
            c_scr[k] = jnp.broadcast_to(h, (SUBLANES, W))
            return a_scr[k, SUBLANES - 1:SUBLANES, :] * h + u_scr[k, SUBLANES - 1:SUBLANES, :]

        h_last = lax.fori_loop(0, nb, carry_in, h_scr[...], unroll=8)
        h_scr[...] = h_last
        hs = a * c_scr[...] + u
    else:
        hs = a * h0_ref[...] + u
        h_last = hs[:, SUBLANES - 1:SUBLANES, :]
    hr_out[...] = (hs * jax.nn.gelu(gr_ref[...].reshape(nb, SUBLANES, W))).reshape(bb, ts, W)

    tail = SUBLANES - (RG_CONV - 1)
    if chained:
        @pl.when(s == ns - 1)
        def _():
            h_out[0] = h_last
            buf_out[0] = x[nb - 1, tail:, :]
    else:
        h_out[...] = h_last
        buf_out[...] = x[:, tail:, :]


def rglru_mixer(z, h0, buf0, conv_w, conv_b, wa_bd, wx_bd, b_a, b_x, lam):
    B, S, _ = z.shape
    W = RG_WIDTH
    rows = 512
    chained = S > SUBLANES
    if chained:
        bb, ts = 1, min(S, rows)
        assert S % ts == 0 and ts % SUBLANES == 0
    else:
        bb, ts = min(B, rows // SUBLANES), S
        assert S == SUBLANES and B % bb == 0
    ns = S // ts
    nb = bb * ts // SUBLANES
    xr_blk = (4 * MLSTM_WIDTH) // W
    row = lambda a: a.reshape(1, W)
    buf8 = jnp.pad(buf0, ((0, 0), (SUBLANES - (RG_CONV - 1), 0), (0, 0)))
    st = lambda n: pl.BlockSpec((bb, n, W), lambda b, s: (b, 0, 0))
    hr, h1, buf1 = pl.pallas_call(
        functools.partial(_rglru_kernel, ns=ns, chained=chained),
        out_shape=(
            jax.ShapeDtypeStruct((B, S, W), F32),
            jax.ShapeDtypeStruct((B, 1, W), F32),
            jax.ShapeDtypeStruct((B, RG_CONV - 1, W), F32),
        ),
        grid=(B // bb, ns),
        in_specs=[
            pl.BlockSpec((bb, ts, W), lambda b, s: (b, s, xr_blk)),
            pl.BlockSpec((bb, ts, W), lambda b, s: (b, s, xr_blk + 1)),
            st(SUBLANES), st(1),
            _resident((RG_CONV, W)), _resident((1, W)), _resident((W, W)), _resident((W, W)),
            _resident((1, W)), _resident((1, W)), _resident((1, W)),
        ],
        out_specs=(pl.BlockSpec((bb, ts, W), lambda b, s: (b, s, 0)), st(1), st(RG_CONV - 1)),
        scratch_shapes=[pltpu.VMEM((SUBLANES, W), F32), pltpu.VMEM((1, W), F32),
                        pltpu.VMEM((nb, SUBLANES, W), F32), pltpu.VMEM((nb, SUBLANES, W), F32),
                        pltpu.VMEM((nb, SUBLANES, W), F32)],
        compiler_params=_cparams(("parallel", "arbitrary")),
        name="rglru",
    )(z, z, buf8, h0.reshape(B, 1, W), conv_w, row(conv_b), wa_bd, wx_bd, row(b_a), row(b_x), row(lam))
    return hr, h1.reshape(B, W), buf1


def _rwkv_pre_kernel(x_ref, sh_ref, sc_ref, prev0_ref, mu_ref, wr_ref, wk_ref, wv_ref, w0_ref, w1_ref,
                     w2_ref, a0_ref, a1_ref, a2_ref, g1_ref, g2_ref, kk_ref, ka_ref,
                     r_out, lw_out, k_out, v_out, kk_out, a_out, g_out, shift_out, carry_scr, *, ns):
    s = pl.program_id(1)
    bb, ts, _ = x_ref.shape
    tm = bb * ts

    @pl.when(s == 0)
    def _():
        carry_scr[...] = prev0_ref[...]

    h = x_ref[...] * (1.0 + sc_ref[...]) + sh_ref[...]
    t_idx = lax.broadcasted_iota(jnp.int32, h.shape, 1)
    h_prev = jnp.where(t_idx == 0, carry_scr[...], pltpu.roll(h, 1, 1))
    last = h[:, ts - 1:ts, :]
    carry_scr[...] = last
    dx = (h_prev - h).reshape(tm, D_MODEL)
    h2 = h.reshape(tm, D_MODEL)
    mu = mu_ref[...]
    mix = lambda j: (h2 + dx * mu[j:j + 1, :]).astype(BF)
    xr, xw, xk, xv, xa, xg = [mix(j) for j in range(6)]

    w_mid = _dot(xw, w1_ref[...])
    a_mid = _dot(xa, a1_ref[...])
    g_mid = _dot(xg, g1_ref[...])
    r = _dot(xr, wr_ref[...])
    k = _dot(xk, wk_ref[...])
    v = _dot(xv, wv_ref[...])
    wl = w0_ref[...] + _dot(jnp.tanh(w_mid).astype(BF), w2_ref[...])
    log_decay = -DECAY_SCALE * jax.nn.sigmoid(wl)
    a = jax.nn.sigmoid(a0_ref[...] + _dot(a_mid.astype(BF), a2_ref[...]))
    g = _dot(jax.nn.sigmoid(g_mid).astype(BF), g2_ref[...])
    kk = k * kk_ref[...]
    k = k * (1.0 + (a - 1.0) * ka_ref[...])

    shp = x_ref.shape
    r_out[...] = r.reshape(shp)
    lw_out[...] = log_decay.reshape(shp)
    k_out[...] = k.reshape(shp)
    v_out[...] = v.reshape(shp)
    kk_out[...] = kk.reshape(shp)
    a_out[...] = a.reshape(shp)
    g_out[...] = g.reshape(shp)

    @pl.when(s == ns - 1)
    def _():
        shift_out[...] = last


def rwkv_pre(x, mod, prev0, P):
    B, S, _ = x.shape
    bb, ts = _row_tile(B, S, 512)
    ns = S // ts
    xmap = lambda i, s: (i, s, 0)
    full = _resident
    tok = pl.BlockSpec((bb, ts, D_MODEL), xmap)
    st = pl.BlockSpec((bb, 1, D_MODEL), lambda i, s: (i, 0, 0))
    sq, lo_in, lo_out, vec = (D_MODEL, D_MODEL), (D_MODEL, LORA_PAD), (LORA_PAD, D_MODEL), (1, D_MODEL)
    outs = pl.pallas_call(
        functools.partial(_rwkv_pre_kernel, ns=ns),
        out_shape=tuple([jax.ShapeDtypeStruct(x.shape, F32)] * 7
                        + [jax.ShapeDtypeStruct((B, 1, D_MODEL), F32)]),
        grid=(B // bb, ns),
        in_specs=[
            tok, _mod_spec(bb, 3), _mod_spec(bb, 4), st, full((6, D_MODEL)),
            full(sq), full(sq), full(sq), full(vec), full(lo_in), full(lo_out),
            full(vec), full(lo_in), full(lo_out), full(lo_in), full(lo_out),
            full(vec), full(vec),
        ],
        out_specs=tuple([tok] * 7 + [st]),
        scratch_shapes=[pltpu.VMEM((bb, 1, D_MODEL), F32)],
        compiler_params=_cparams(("parallel", "arbitrary")),
        name="rwkv_pre",
    )(x, mod, mod, prev0.reshape(B, 1, D_MODEL), P['mu'], P['wr'], P['wk'], P['wv'], P['w0'], P['w1'],
      P['w2'], P['a0'], P['a1'], P['a2'], P['g1'], P['g2'], P['k_k'], P['k_a'])
    return outs


def _stack_heads(x, first):
    return jnp.concatenate([jnp.where(first, x, 0.0), jnp.where(first, 0.0, x)], axis=0)


def _seg_sum(x, first):
    s0 = jnp.sum(jnp.where(first, x, 0.0), axis=1, keepdims=True)
    s1 = jnp.sum(jnp.where(first, 0.0, x), axis=1, keepdims=True)
    return jnp.where(first, s0, s1)


def _wkv_pair_chunks(toks, sps, vecs, consts, L, tall_state):
    tri, first, strict, incl, own = consts
    L2 = 2 * L
    hs = RWKV_HS
    rng = range(len(toks))
    wide = L2 % LANES == 0
    if tall_state:
        tall = lambda x: jnp.concatenate([x[:, :hs], x[:, hs:]], axis=0)
        keep_own = lambda x: jnp.where(own, x, 0.0)
    else:
        tall = lambda x: _stack_heads(x, first)
        keep_own = lambda x: x
    splits = [_split3(t[1]) for t in toks]
    c_incl = [_dot(tri, s[0]) + _dot(tri, s[1]) + _dot(tri, s[2]) for s in splits]
    e_inv = [jnp.exp(-c) for c in c_incl]
    kn = [t[4] / jnp.maximum(jnp.sqrt(_seg_sum(t[4] * t[4], first)), 1e-12) for t in toks]
    la = [tall(-kn[p] * jnp.exp(c_incl[p] - toks[p][1])).astype(BF) for p in rng]
    lr = [tall(toks[p][0] * jnp.exp(c_incl[p])).astype(BF) for p in rng]
    rb = [tall(kn[p] * toks[p][5] * e_inv[p]).astype(BF) for p in rng]
    rk = [tall(toks[p][2] * e_inv[p]).astype(BF) for p in rng]
    vs = [_stack_heads(toks[p][3], first).astype(BF) for p in rng]
    rbk = [jnp.concatenate([rb[p], rk[p]], axis=0) for p in rng]

    if wide:
        nn = [_dot_nt(la[p], rbk[p]) for p in rng]
        mm = [_dot_nt(lr[p], rbk[p]) for p in rng]
        n_raw = [x[:, :L2] for x in nn]
        n_ak = [jnp.where(strict, x[:, L2:], 0.0).astype(BF) for x in nn]
        m_bk = [jnp.concatenate([jnp.where(incl, x[:, :L2], 0.0), jnp.where(incl, x[:, L2:], 0.0)],
                                axis=1).astype(BF) for x in mm]
    else:
        n_raw = [_dot_nt(la[p], rb[p]) for p in rng]
        n_ak = [jnp.where(strict, _dot_nt(la[p], rk[p]), 0.0).astype(BF) for p in rng]
        m_rb = [jnp.where(incl, _dot_nt(lr[p], rb[p]), 0.0).astype(BF) for p in rng]
        m_rk = [jnp.where(incl, _dot_nt(lr[p], rk[p]), 0.0).astype(BF) for p in rng]

    row = lax.broadcasted_iota(jnp.int32, (L2, L2), 0)
    col = lax.broadcasted_iota(jnp.int32, (L2, L2), 1)
    base = (row // INV_BASE == col // INV_BASE) & (col < row)
    xb = [jnp.where(base, x, 0.0).astype(BF) for x in n_raw]
    tinv = [jnp.where(row == col, 1.0, 0.0) + x.astype(F32) for x in xb]
    x2 = [_dot(x, x).astype(BF) for x in xb]
    if wide:
        prod = [_dot(x2[p], jnp.concatenate([x2[p], tinv[p].astype(BF)], axis=1)) for p in rng]
        tinv = [tinv[p] + prod[p][:, L2:] for p in rng]
        x4 = [x[:, :L2].astype(BF) for x in prod]
    else:
        tinv = [tinv[p] + _dot(x2[p], tinv[p].astype(BF)) for p in rng]
        x4 = [_dot(x, x).astype(BF) for x in x2]
    tinv = [tinv[p] + _dot(x4[p], tinv[p].astype(BF)) for p in rng]
    blk = INV_BASE
    while blk < L:
        off = (row // (2 * blk) == col // (2 * blk)) & (row // blk == col // blk + 1)
        tb = [t.astype(BF) for t in tinv]
        pr = [_dot(jnp.where(off, n_raw[p], 0.0).astype(BF), tb[p]).astype(BF) for p in rng]
        tinv = [tinv[p] + _dot(tb[p], pr[p]) for p in rng]
        blk *= 2

    spb = [s.astype(BF) for s in sps]
    rhs = [keep_own(_dot_nt(la[p], spb[p])) + _dot(n_ak[p], vs[p]) for p in rng]
    ub = [_dot(tinv[p].astype(BF), rhs[p].astype(BF)).astype(BF) for p in rng]

    uv = [jnp.concatenate([ub[p], vs[p]], axis=0) for p in rng]
    y0 = [keep_own(_dot_nt(lr[p], spb[p])) for p in rng]
    if wide:
        ys = [y0[p] + _dot(m_bk[p], uv[p]) for p in rng]
    else:
        ys = [y0[p] + _dot(m_rb[p], ub[p]) + _dot(m_rk[p], vs[p]) for p in rng]
    decay = [jnp.exp(c_incl[p][L - 1:L, :]) for p in rng]
    if tall_state:
        upper = lax.broadcasted_iota(jnp.int32, (2 * hs, hs), 0) < hs
        decay = [jnp.where(upper, d[:, :hs], d[:, hs:]) for d in decay]
    sp_new = [(sps[p] + _dot_tn(uv[p], rbk[p])) * decay[p] for p in rng]

    outs = []
    for p in rng:
        r, _, k, v, _, _, gate = toks[p]
        r_k, lnx_g, lnx_b = vecs[p]
        y = ys[p][0:L, :] + ys[p][L:L2, :]
        yc = y - _seg_sum(y, first) * (1.0 / RWKV_HS)
        yv = _seg_sum(yc * yc, first) * (1.0 / RWKV_HS)
        yn = yc * lax.rsqrt(yv + RWKV_LN_EPS) * lnx_g + lnx_b
        outs.append(((yn + _seg_sum(r * k * r_k, first) * v) * gate, sp_new[p]))
    return outs


def _wkv_kernel(r_ref, lw_ref, k_ref, v_ref, kk_ref, a_ref, g_ref, s0_ref, rk_ref, lxg_ref, lxb_ref,
                y_out, s_out, sp_scr, *, L, nc, bb, group):
    c = pl.program_id(1)
    hs = RWKV_HS
    npair = RWKV_HEADS // 2
    L2 = 2 * L

    tall_state = nc == 1

    @pl.when(c == 0)
    def _():
        if tall_state:
            sp_scr[...] = s0_ref[...].reshape(sp_scr.shape)
        else:
            zero = jnp.zeros((hs, hs), F32)

            def init(i, carry):
                for p in range(npair):
                    top = jnp.concatenate([s0_ref[i, 2 * p], zero], axis=1)
                    bot = jnp.concatenate([zero, s0_ref[i, 2 * p + 1]], axis=1)
                    sp_scr[i, p] = jnp.concatenate([top, bot], axis=0)
                return carry

            lax.fori_loop(0, bb, init, 0)

    row = lax.broadcasted_iota(jnp.int32, (L, L), 0)
    col = lax.broadcasted_iota(jnp.int32, (L, L), 1)
    tri = jnp.where(col <= row, 1.0, 0.0).astype(BF)
    first = lax.broadcasted_iota(jnp.int32, (L, LANES), 1) < hs
    row2 = lax.broadcasted_iota(jnp.int32, (L2, L2), 0)
    col2 = lax.broadcasted_iota(jnp.int32, (L2, L2), 1)
    same = (row2 >= L) == (col2 >= L)
    own = ((lax.broadcasted_iota(jnp.int32, (L2, LANES), 0) >= L)
           == (lax.broadcasted_iota(jnp.int32, (L2, LANES), 1) >= hs))
    consts = (tri, first, same & (col2 < row2), same & (col2 <= row2), own)
    sls = [slice(p * LANES, (p + 1) * LANES) for p in range(npair)]

    def seqs(j, carry):
        ids = [(j * group + g, p) for g in range(group) for p in range(npair)]
        toks = [tuple(ref[i, :, sls[p]] for ref in (r_ref, lw_ref, k_ref, v_ref, kk_ref, a_ref, g_ref))
                for i, p in ids]
        vecs = [(rk_ref[:, sls[p]], lxg_ref[:, sls[p]], lxb_ref[:, sls[p]]) for _, p in ids]
        outs = _wkv_pair_chunks(toks, [sp_scr[i, p] for i, p in ids], vecs, consts, L, tall_state)
        for (i, p), (y, sp_new) in zip(ids, outs):
            y_out[i, :, sls[p]] = y
            sp_scr[i, p] = sp_new
        return carry

    lax.fori_loop(0, bb // group, seqs, 0)

    @pl.when(c == nc - 1)
    def _():
        if tall_state:
            s_out[...] = sp_scr[...].reshape(s_out.shape)
        else:
            def fin(i, carry):
                for p in range(npair):
                    sp = sp_scr[i, p]
                    s_out[i, 2 * p] = sp[0:hs, 0:hs]
                    s_out[i, 2 * p + 1] = sp[hs:2 * hs, hs:2 * hs]
                return carry

            lax.fori_loop(0, bb, fin, 0)


def wkv_recurrence(r, lw, k, v, kk, a, g, s0, P):
    B, S, _ = r.shape
    L = RWKV_CHUNK if S % RWKV_CHUNK == 0 else S
    assert L & (L - 1) == 0 and L % SUBLANES == 0
    nc = S // L
    bb, group = (min(B, 4), min(B, 4)) if nc > 1 else (min(B, 8), min(B, 8))
    assert B % bb == 0 and bb % group == 0
    npair = RWKV_HEADS // 2
    tok = pl.BlockSpec((bb, L, D_MODEL), lambda b, c: (b, c, 0))
    st = pl.BlockSpec((bb, RWKV_HEADS, RWKV_HS, RWKV_HS), lambda b, c: (b, 0, 0, 0))
    vec = pl.BlockSpec((1, D_MODEL), lambda b, c: (0, 0))
    return pl.pallas_call(
        functools.partial(_wkv_kernel, L=L, nc=nc, bb=bb, group=group),
        out_shape=(jax.ShapeDtypeStruct(r.shape, F32), jax.ShapeDtypeStruct(s0.shape, F32)),
        grid=(B // bb, nc),
        in_specs=[tok] * 7 + [st, vec, vec, vec],
        out_specs=(tok, st),
        scratch_shapes=[pltpu.VMEM((bb, npair, 2 * RWKV_HS, RWKV_HS if nc == 1 else 2 * RWKV_HS), F32)],
        compiler_params=_cparams(("parallel", "arbitrary")),
        name="wkv",
    )(r, lw, k, v, kk, a, g, s0, P['r_k'], P['lnx_g'], P['lnx_b'])


def _block_diag(w):
    G, n, _ = w.shape
    eye = jnp.eye(G, dtype=w.dtype)
    return (eye[:, None, :, None] * w[:, :, None, :]).reshape(G * n, G * n)


def _pad_cols(w, n):
    return jnp.pad(w, ((0, 0), (0, n - w.shape[1])))


def _pad_rows(w, n):
    return jnp.pad(w, ((0, n - w.shape[0]), (0, 0)))


def _prep_ab(j, p):
    w_in = p['ab_w_in'][j]
    n_main = 4 * MLSTM_WIDTH
    n_gate = 2 * MLSTM_HEADS
    w_all = jnp.concatenate([w_in[:, :n_main], w_in[:, n_main + n_gate:],
                             w_in[:, n_main:n_main + n_gate]], axis=1)
    return dict(
        w_in=_pad_cols(w_all, Z_COLS).astype(BF),
        b_gates=p['mlstm_b_gates'][j], norm_g=p['mlstm_norm_g'][j],
        conv_w=p['rg_conv_w'][j], conv_b=p['rg_conv_b'][j],
        wa=_block_diag(p['rg_w_a'][j]).astype(BF), wx=_block_diag(p['rg_w_x'][j]).astype(BF),
        b_a=p['rg_b_a'][j], b_x=p['rg_b_x'][j], lam=p['rg_lambda'][j],
        w_out_m=p['ab_w_out'][j][:MLSTM_WIDTH].astype(BF), w_out_r=p['ab_w_out'][j][MLSTM_WIDTH:].astype(BF),
    )


def _prep_rwkv(j, p):
    vec = lambda a: a.reshape(1, D_MODEL)
    return dict(
        mu=p['rw_mu'][j],
        wr=p['rw_wr'][j].astype(BF), wk=p['rw_wk'][j].astype(BF), wv=p['rw_wv'][j].astype(BF),
        w0=vec(p['rw_w0'][j]), w1=_pad_cols(p['rw_w1'][j], LORA_PAD).astype(BF),
        w2=_pad_rows(p['rw_w2'][j], LORA_PAD).astype(BF),
        a0=vec(p['rw_a0'][j]), a1=_pad_cols(p['rw_a1'][j], LORA_PAD).astype(BF),
        a2=_pad_rows(p['rw_a2'][j], LORA_PAD).astype(BF),
        g1=_pad_cols(p['rw_g1'][j], LORA_PAD).astype(BF), g2=_pad_rows(p['rw_g2'][j], LORA_PAD).astype(BF),
        k_k=vec(p['rw_k_k'][j]), k_a=vec(p['rw_k_a'][j]), r_k=vec(p['rw_r_k'][j]),
        lnx_g=vec(p['rw_lnx_g'][j]), lnx_b=vec(p['rw_lnx_b'][j]), wo=p['rw_wo'][j].astype(BF),
    )


def ab_mixer(x, mod, st, A):
    mC, mn, mm, rh, rconv = st
    z, gates = mod_matmul(x, mod, A['w_in'], 4 * MLSTM_WIDTH + 2 * RG_WIDTH, 2 * MLSTM_HEADS)
    hm, C1, n1, m1 = mlstm_mixer(z, gates, mC, mn, mm, A['b_gates'], A['norm_g'])
    hr, rh1, buf1 = rglru_mixer(z, rh, rconv, A['conv_w'], A['conv_b'], A['wa'], A['wx'],
                                A['b_a'], A['b_x'], A['lam'])
    return ((hm, 0), (hr, 0), A['w_out_m'], A['w_out_r']), (C1, n1, m1, rh1, buf1)


def rwkv_mixer(x, mod, st, R):
    wkv0, prev0 = st
    r, lw, k, v, kk, a, g, shift = rwkv_pre(x, mod, prev0, R)
    y, wkv1 = wkv_recurrence(r, lw, k, v, kk, a, g, wkv0, R)
    half = D_MODEL // 2
    return ((y, 0), (y, 1), R['wo'][:half], R['wo'][half:]), (wkv1, shift.reshape(shift.shape[0], D_MODEL))


def run_trunk(x, mods, states, W):
    mC, mn, mm, rh, rconv, wkv, shift = states
    new_ab, new_c = [], []
    for layer in range(DEPTH):
        mod = mods[layer]
        lg = lambda i: W['ln_g'][layer, i].reshape(1, D_MODEL)
        lb = lambda i: W['ln_b'][layer, i].reshape(1, D_MODEL)
        f = W['ffn']
        x = ffn_block(x, mod, 0, layer, 0, f[0], f[1], f[2], lg(0), lb(0))
        j = layer // 2
        if layer % 2 == 0:
            mixed, st = ab_mixer(x, mod, (mC[j], mn[j], mm[j], rh[j], rconv[j]), W['ab'][j])
            new_ab.append(st)
        else:
            mixed, st = rwkv_mixer(x, mod, (wkv[j], shift[j]), W['rwkv'][j])
            new_c.append(st)
        x = ffn_block(x, mod, 2, layer, 1, f[0], f[1], f[2], lg(2), lb(2), mixer=mixed + (lg(1), lb(1)))
    stk = lambda sts, i: sts[0][i][None] if len(sts) == 1 else jnp.stack([s[i] for s in sts], axis=0)
    return x, (stk(new_ab, 0), stk(new_ab, 1), stk(new_ab, 2), stk(new_ab, 3), stk(new_ab, 4),
               stk(new_c, 0), stk(new_c, 1))


def _zero_states(B):
    n_ab, n_c = (DEPTH + 1) // 2, DEPTH // 2
    return (jnp.zeros((n_ab, B, MLSTM_HEADS, MLSTM_HD, MLSTM_HD), F32),
            jnp.zeros((n_ab, B, MLSTM_HEADS, MLSTM_HD), F32),
            jnp.zeros((n_ab, B, MLSTM_HEADS), F32),
            jnp.zeros((n_ab, B, RG_WIDTH), F32),
            jnp.zeros((n_ab, B, RG_CONV - 1, RG_WIDTH), F32),
            jnp.zeros((n_c, B, RWKV_HEADS, RWKV_HS, RWKV_HS), F32),
            jnp.zeros((n_c, B, D_MODEL), F32))


def kernel(x_prompt, x_sample, c_prompt, c_sample, state_mlstm_C, state_mlstm_n, state_mlstm_m, state_rglru_h, state_rglru_conv, state_rwkv_wkv, state_rwkv_shift, ada_w, ada_b, ln_g, ln_b, ffn_w1, ffn_w3, ffn_w2, ab_w_in, mlstm_b_gates, mlstm_norm_g, rg_conv_w, rg_conv_b, rg_w_a, rg_b_a, rg_w_x, rg_b_x, rg_lambda, ab_w_out, rw_mu, rw_wr, rw_wk, rw_wv, rw_w0, rw_w1, rw_w2, rw_a0, rw_a1, rw_a2, rw_g1, rw_g2, rw_k_k, rw_k_a, rw_r_k, rw_lnx_g, rw_lnx_b, rw_wo):
    p = dict(ab_w_in=ab_w_in, mlstm_b_gates=mlstm_b_gates, mlstm_norm_g=mlstm_norm_g,
             rg_conv_w=rg_conv_w, rg_conv_b=rg_conv_b, rg_w_a=rg_w_a, rg_b_a=rg_b_a, rg_w_x=rg_w_x,
             rg_b_x=rg_b_x, rg_lambda=rg_lambda, ab_w_out=ab_w_out, rw_mu=rw_mu, rw_wr=rw_wr,
             rw_wk=rw_wk, rw_wv=rw_wv, rw_w0=rw_w0, rw_w1=rw_w1, rw_w2=rw_w2, rw_a0=rw_a0,
             rw_a1=rw_a1, rw_a2=rw_a2, rw_g1=rw_g1, rw_g2=rw_g2, rw_k_k=rw_k_k, rw_k_a=rw_k_a,
             rw_r_k=rw_r_k, rw_lnx_g=rw_lnx_g, rw_lnx_b=rw_lnx_b, rw_wo=rw_wo)
    W = dict(
        ln_g=ln_g, ln_b=ln_b,
        ffn=(ffn_w1.astype(BF), ffn_w3.astype(BF), ffn_w2.astype(BF)),
        ab=[_prep_ab(j, p) for j in range((DEPTH + 1) // 2)],
        rwkv=[_prep_rwkv(j, p) for j in range(DEPTH // 2)],
    )
    Bp, Bs = x_prompt.shape[0], x_sample.shape[0]
    mod_all = adaln(jnp.concatenate([c_prompt, c_sample], axis=0), ada_w, ada_b)
    mods_p = [mod_all[l, :Bp] for l in range(DEPTH)]
    mods_s = [mod_all[l, Bp:] for l in range(DEPTH)]
    y_prompt, sp = run_trunk(x_prompt, mods_p, _zero_states(Bp), W)
    y_sample, ss = run_trunk(x_sample, mods_s,
                             (state_mlstm_C, state_mlstm_n, state_mlstm_m, state_rglru_h,
                              state_rglru_conv, state_rwkv_wkv, state_rwkv_shift), W)
    return (y_prompt, y_sample) + tuple(sp) + tuple(ss)
```
